```python
import jax
import jax.numpy as jnp
from jax import lax
import numpy as np

D_MODEL = 1024
BATCH = 8
SEQ = 4096
DEPTH = 2

GRID_W = 64
CTX_LEN = 256
HEAD_DIM = 64
RW_HEADS = 8
RW_DIM = RW_HEADS * HEAD_DIM
LORA_W = 64
LORA_A = 64
LORA_V = 32
LORA_G = 128
ATT_HEADS = 8
KV_HEADS = 2
GROUP = ATT_HEADS // KV_HEADS
ATT_DIM = ATT_HEADS * HEAD_DIM
KV_DIM = KV_HEADS * HEAD_DIM
WINDOW = 128
BLOCK_Q = 128
SPAN = BLOCK_Q + 2 * WINDOW
ATT_SCALE = HEAD_DIM ** -0.5
ROPE_THETA = 10000.0
RW_IN = 3 * RW_DIM + LORA_W + LORA_A + LORA_G
ATT_IN = ATT_DIM + 2 * KV_DIM
N_IN = RW_IN + ATT_IN + 2 * D_MODEL
RW_SPLITS = [RW_DIM, 2 * RW_DIM, 3 * RW_DIM, 3 * RW_DIM + LORA_W, 3 * RW_DIM + LORA_W + LORA_A]
D_FF = 2816
N_EXPERTS = 8
TOP_K = 2
D_FF_EXPERT = 3584
N_DENSE = (DEPTH + 1) // 2
N_MOE = DEPTH // 2
N_VRES = DEPTH - 1
RMS_EPS = 1e-6
GN_EPS = 64e-5
NEG_INF = -1e30

kernel_name = 'hybrid_rwkv7_swa_moe_dit_block'


def rms_norm(x, g, eps=RMS_EPS):
    xf = x.astype(jnp.float32)
    y = xf * lax.rsqrt(jnp.mean(xf * xf, axis=-1, keepdims=True) + eps)
    return (y * g.astype(jnp.float32)).astype(x.dtype)


def modulate(h, shift, scale):
    return h * (1 + scale) + shift


def token_shift(p, mu):
    prev = jnp.pad(p, ((0, 0), (1, 0), (0, 0)))[:, :-1]
    nxt = jnp.pad(p, ((0, 0), (0, 1), (0, 0)))[:, 1:]
    return p + mu[0] * (prev - p) + mu[1] * (nxt - p)


def axial_rope(n_tokens):
    rows = n_tokens // GRID_W
    row = jnp.repeat(jnp.arange(rows, dtype=jnp.float32), GRID_W)
    col = jnp.tile(jnp.arange(GRID_W, dtype=jnp.float32), rows)
    half = HEAD_DIM // 2
    inv_freq = ROPE_THETA ** (-jnp.arange(0, half, 2, dtype=jnp.float32) / half)
    ar = row[:, None] * inv_freq
    ac = col[:, None] * inv_freq
    ang = jnp.concatenate([ar, ar, ac, ac], axis=-1)
    return jnp.cos(ang)[:, None, :], jnp.sin(ang)[:, None, :]


def apply_rope(u, cos, sin):
    uf = u.astype(jnp.float32)
    r1, r2, c1, c2 = jnp.split(uf, 4, axis=-1)
    rot = jnp.concatenate([-r2, r1, -c2, c1], axis=-1)
    return (uf * cos + rot * sin).astype(u.dtype)


def orient(u):
    return jnp.stack([u[0], jnp.flip(u[1], axis=1)])


def rwkv_branch(p, h, v_first, state0, lp, vres, emit):
    f32 = jnp.float32
    B, T, _ = p.shape
    r, k, v, w_lo, a_lo, g_lo = jnp.split(p, RW_SPLITS, axis=-1)
    w_pre = lp['w0'][:, None, None, :] + jnp.einsum('btl,dlc->dbtc', jnp.tanh(w_lo), lp['w_decay_up'])
    decay = jnp.exp(-jnp.exp(-jax.nn.softplus(-w_pre.astype(f32)) - 0.5))
    iclr = jax.nn.sigmoid(lp['a0'][:, None, None, :] + jnp.einsum('btl,dlc->dbtc', a_lo, lp['w_iclr_up']))
    if vres is None:
        v_first = v
    else:
        v0, v_down, v_up = vres
        v = v + (v_first - v) * jax.nn.sigmoid(v0 + (h @ v_down) @ v_up)

    def heads(u):
        return u.reshape(u.shape[:-1] + (RW_HEADS, HEAD_DIM))

    kk = heads(k * lp['k_k']).astype(f32)
    kk = kk * lax.rsqrt(jnp.sum(kk * kk, axis=-1, keepdims=True) + 1e-12)
    k_h = heads(k[None] * (1 + (iclr - 1) * lp['k_a'])).astype(f32)
    r_h = heads(r).astype(f32)
    v_h = heads(v).astype(f32)

    def both(u):
        return jnp.broadcast_to(u[None], (2,) + u.shape)

    xs = (orient(both(r_h)), orient(heads(decay)), orient(k_h), orient(both(v_h)),
          orient(both(kk)), orient(heads(iclr).astype(f32)))
    xs = tuple(jnp.moveaxis(u, 2, 0) for u in xs)

    def step(S, inp):
        r_t, w_t, k_t, v_t, kk_t, a_t = inp
        s_kk = jnp.einsum('dbhij,dbhj->dbhi', S, kk_t)
        S = S * w_t[..., None, :] - s_kk[..., None] * (kk_t * a_t)[..., None, :] + v_t[..., None] * k_t[..., None, :]
        return S, (jnp.einsum('dbhij,dbhj->dbhi', S, r_t) if emit else None)

    state, ys = lax.scan(step, state0, xs)
    if not emit:
        return state, v_first, None
    ys = orient(jnp.moveaxis(ys, 0, 2))
    y = ys[0] + ys[1]
    mu = jnp.mean(y, axis=-1, keepdims=True)
    var = jnp.mean(jnp.square(y - mu), axis=-1, keepdims=True)
    yn = ((y - mu) * lax.rsqrt(var + GN_EPS)).reshape(B, T, RW_DIM)
    yn = yn * lp['lnx_g'].astype(f32) + lp['lnx_b'].astype(f32)
    coef = jnp.sum(r_h * (k_h[0] + k_h[1]) * lp['r_k'].astype(f32), axis=-1, keepdims=True)
    y = yn + (coef * v_h).reshape(B, T, RW_DIM)
    g = jax.nn.sigmoid(g_lo) @ lp['w_gate_up']
    return state, v_first, y.astype(p.dtype) * g


def window_attention(q, k, v, kc, vc, sink):
    B, T = q.shape[:2]
    n_blocks = T // BLOCK_Q
    kp = jnp.pad(k, ((0, 0), (WINDOW, WINDOW), (0, 0), (0, 0)))
    vp = jnp.pad(v, ((0, 0), (WINDOW, WINDOW), (0, 0), (0, 0)))
    qb = jnp.swapaxes(q.reshape(B, n_blocks, BLOCK_Q, KV_HEADS, GROUP, HEAD_DIM), 0, 1)
    rel = jnp.arange(SPAN)[None, :] - WINDOW - jnp.arange(BLOCK_Q)[:, None]
    in_window = jnp.abs(rel) <= WINDOW
    sink_col = jnp.broadcast_to(sink.reshape(1, KV_HEADS, GROUP, 1, 1).astype(jnp.float32),
                                (B, KV_HEADS, GROUP, BLOCK_Q, 1))

    def block(args):
        qblk, i = args
        start = i * BLOCK_Q
        kb = lax.dynamic_slice_in_dim(kp, start, SPAN, axis=1)
        vb = lax.dynamic_slice_in_dim(vp, start, SPAN, axis=1)
        kpos = start - WINDOW + jnp.arange(SPAN)
        ok = in_window & ((kpos >= 0) & (kpos < T))[None, :]
        s_loc = jnp.einsum('bqhgd,bkhd->bhgqk', qblk, kb).astype(jnp.float32) * ATT_SCALE
        s_loc = jnp.where(ok, s_loc, NEG_INF)
        s_ctx = jnp.einsum('bqhgd,bchd->bhgqc', qblk, kc).astype(jnp.float32) * ATT_SCALE
        pr = jax.nn.softmax(jnp.concatenate([s_loc, s_ctx, sink_col], axis=-1), axis=-1).astype(v.dtype)
        o = jnp.einsum('bhgqk,bkhd->bqhgd', pr[..., :SPAN], vb)
        return o + jnp.einsum('bhgqc,bchd->bqhgd', pr[..., SPAN:-1], vc)

    o = lax.map(block, (qb, jnp.arange(n_blocks)))
    return jnp.swapaxes(o, 0, 1).reshape(B, T, ATT_DIM)


def context_attention(qc, kc, vc, sink):
    B, C = qc.shape[:2]
    qg = qc.reshape(B, C, KV_HEADS, GROUP, HEAD_DIM)
    s = jnp.einsum('bqhgd,bkhd->bhgqk', qg, kc).astype(jnp.float32) * ATT_SCALE
    sink_col = jnp.broadcast_to(sink.reshape(1, KV_HEADS, GROUP, 1, 1).astype(jnp.float32), s.shape[:-1] + (1,))
    pr = jax.nn.softmax(jnp.concatenate([s, sink_col], axis=-1), axis=-1)[..., :C].astype(vc.dtype)
    return jnp.einsum('bhgqk,bkhd->bqhgd', pr, vc).reshape(B, C, ATT_DIM)


def merge(y_rw, y_att, gates, lp):
    g_rw, g_att = jnp.split(gates, 2, axis=-1)
    m = jax.nn.sigmoid(g_rw) * (y_rw @ lp['w_br_rwkv']) + jax.nn.sigmoid(g_att) * (y_att @ lp['w_br_attn'])
    return m @ lp['w_out']


def mixer(h, hc, vf_lat, vf_ctx, lp, vres, cos, sin, last):
    p = h @ lp['w_in']
    pc = hc @ lp['w_in']
    rw, att, gates = jnp.split(p, [RW_IN, RW_IN + ATT_IN], axis=-1)
    rwc, attc, gatesc = jnp.split(pc, [RW_IN, RW_IN + ATT_IN], axis=-1)
    rw = token_shift(rw, lp['tshift'])
    rwc = token_shift(rwc, lp['tshift'])
    state0 = jnp.zeros((2, hc.shape[0], RW_HEADS, HEAD_DIM, HEAD_DIM), jnp.float32)
    s_ctx, vf_ctx, y_rw_c = rwkv_branch(rwc, hc, vf_ctx, state0, lp, vres, emit=not last)
    _, vf_lat, y_rw = rwkv_branch(rw, h, vf_lat, s_ctx, lp, vres, emit=True)

    def split_qkv(a):
        B, T = a.shape[:2]
        q, k, v = jnp.split(a, [ATT_DIM, ATT_DIM + KV_DIM], axis=-1)
        return (q.reshape(B, T, ATT_HEADS, HEAD_DIM),
                rms_norm(k.reshape(B, T, KV_HEADS, HEAD_DIM), lp['k_norm']),
                v.reshape(B, T, KV_HEADS, HEAD_DIM))

    q, k, v = split_qkv(att)
    q = apply_rope(rms_norm(q, lp['q_norm']), cos, sin)
    k = apply_rope(k, cos, sin)
    qc, kc, vc = split_qkv(attc)
    y_att = window_attention(q, k, v, kc, vc, lp['sink'])
    y = merge(y_rw, y_att, gates, lp)
    if last:
        return y, None, vf_lat, vf_ctx
    y_att_c = context_attention(rms_norm(qc, lp['q_norm']), kc, vc, lp['sink'])
    yc = merge(y_rw_c, y_att_c, gatesc, lp)
    return y, yc, vf_lat, vf_ctx


def swiglu(h, wg, wu, wd):
    return (jax.nn.silu(h @ wg) * (h @ wu)) @ wd


def moe_swiglu(h, router, wg, wu, wd):
    logits = (h @ router).astype(jnp.float32)
    top_v, top_i = lax.top_k(logits, TOP_K)
    w = jax.nn.softmax(top_v, axis=-1).astype(h.dtype)
    out = jnp.zeros_like(h)
    for e in range(N_EXPERTS):
        we = jnp.sum(jnp.where(top_i == e, w, 0.0), axis=-1, keepdims=True)
        out = out + we * swiglu(h, wg[e], wu[e], wd[e])
    return out


def setup_inputs(seed: int = 0) -> dict:
    key = jax.random.key(seed)
    keys = iter(jax.random.split(key, 40))

    def nrm(shape, s):
        return jax.random.normal(next(keys), shape, jnp.float32) * s

    def uni(shape, lo, hi):
        return jax.random.uniform(next(keys), shape, jnp.float32, lo, hi)

    D = D_MODEL
    return {
        'x': nrm((BATCH, SEQ, D), 1.0),
        'c': nrm((BATCH, D), 1.0),
        'ctx': nrm((BATCH, CTX_LEN, D), 1.0),
        'c_ctx': nrm((D,), 1.0),
        'norm1': 1.0 + nrm((DEPTH, D), 0.02),
        'norm2': 1.0 + nrm((DEPTH, D), 0.02),
        'w_mod': nrm((DEPTH, D, 6 * D), 0.5 * D ** -0.5),
        'b_mod': nrm((DEPTH, 6 * D), 0.02),
        'w_in': nrm((DEPTH, D, N_IN), D ** -0.5),
        'tshift': uni((DEPTH, 2, RW_IN), 0.0, 0.5),
        'w0': uni((DEPTH, 2, RW_DIM), -6.0, -1.0),
        'w_decay_up': nrm((DEPTH, 2, LORA_W, RW_DIM), 0.1 * LORA_W ** -0.5),
        'a0': nrm((DEPTH, 2, RW_DIM), 0.1),
        'w_iclr_up': nrm((DEPTH, 2, LORA_A, RW_DIM), 0.3 * LORA_A ** -0.5),
        'v0': 1.0 + nrm((N_VRES, RW_DIM), 0.1),
        'w_vres_down': nrm((N_VRES, D, LORA_V), D ** -0.5),
        'w_vres_up': nrm((N_VRES, LORA_V, RW_DIM), 0.3 * LORA_V ** -0.5),
        'w_gate_up': nrm((DEPTH, LORA_G, RW_DIM), LORA_G ** -0.5),
        'k_k': 0.85 + nrm((DEPTH, RW_DIM), 0.02),
        'k_a': 1.0 + nrm((DEPTH, RW_DIM), 0.02),
        'r_k': nrm((DEPTH, RW_HEADS, HEAD_DIM), 0.1),
        'lnx_g': 1.0 + nrm((DEPTH, RW_DIM), 0.02),
        'lnx_b': nrm((DEPTH, RW_DIM), 0.02),
        'q_norm': 1.0 + nrm((DEPTH, HEAD_DIM), 0.02),
        'k_norm': 1.0 + nrm((DEPTH, HEAD_DIM), 0.02),
        'sink': nrm((DEPTH, ATT_HEADS), 0.5),
        'w_br_rwkv': nrm((DEPTH, RW_DIM, D), RW_DIM ** -0.5),
        'w_br_attn': nrm((DEPTH, ATT_DIM, D), ATT_DIM ** -0.5),
        'w_out': nrm((DEPTH, D, D), D ** -0.5),
        'ffn_gate': nrm((N_DENSE, D, D_FF), D ** -0.5),
        'ffn_up': nrm((N_DENSE, D, D_FF), D ** -0.5),
        'ffn_down': nrm((N_DENSE, D_FF, D), D_FF ** -0.5),
        'router': nrm((N_MOE, D, N_EXPERTS), D ** -0.5),
        'moe_gate': nrm((N_MOE, N_EXPERTS, D, D_FF_EXPERT), D ** -0.5),
        'moe_up': nrm((N_MOE, N_EXPERTS, D, D_FF_EXPERT), D ** -0.5),
        'moe_down': nrm((N_MOE, N_EXPERTS, D_FF_EXPERT, D), D_FF_EXPERT ** -0.5),
    }


def reference(x, c, ctx, c_ctx, norm1, norm2, w_mod, b_mod, w_in, tshift, w0, w_decay_up, a0, w_iclr_up,
              v0, w_vres_down, w_vres_up, w_gate_up, k_k, k_a, r_k, lnx_g, lnx_b, q_norm, k_norm, sink,
              w_br_rwkv, w_br_attn, w_out, ffn_gate, ffn_up, ffn_down, router, moe_gate, moe_up, moe_down):
    cos, sin = axial_rope(x.shape[1])
    xc = ctx
    vf_lat = None
    vf_ctx = None
    for layer in range(DEPTH):
        last = layer == DEPTH - 1
        mod_l = jax.nn.silu(c) @ w_mod[layer] + b_mod[layer]
        mod_c = jax.nn.silu(c_ctx) @ w_mod[layer] + b_mod[layer]
        sh1, sc1, g1, sh2, sc2, g2 = jnp.split(mod_l[:, None, :], 6, axis=-1)
        csh1, csc1, cg1, csh2, csc2, cg2 = jnp.split(mod_c, 6, axis=-1)
        lp = {
            'w_in': w_in[layer], 'tshift': tshift[layer], 'w0': w0[layer], 'w_decay_up': w_decay_up[layer],
            'a0': a0[layer], 'w_iclr_up': w_iclr_up[layer], 'w_gate_up': w_gate_up[layer],
            'k_k': k_k[layer], 'k_a': k_a[layer], 'r_k': r_k[layer], 'lnx_g': lnx_g[layer], 'lnx_b': lnx_b[layer],
            'q_norm': q_norm[layer], 'k_norm': k_norm[layer], 'sink': sink[layer],
            'w_br_rwkv': w_br_rwkv[layer], 'w_br_attn': w_br_attn[layer], 'w_out': w_out[layer],
        }
        vres = None if layer == 0 else (v0[layer - 1], w_vres_down[layer - 1], w_vres_up[layer - 1])
        h = modulate(rms_norm(x, norm1[layer]), sh1, sc1)
        hc = modulate(rms_norm(xc, norm1[layer]), csh1, csc1)
        y, yc, vf_lat, vf_ctx = mixer(h, hc, vf_lat, vf_ctx, lp, vres, cos, sin, last)
        x = x + g1 * y
        if not last:
            xc = xc + cg1 * yc
        if layer % 2 == 0:
            i = layer // 2
            ffn = lambda u, i=i: swiglu(u, ffn_gate[i], ffn_up[i], ffn_down[i])
        else:
            i = layer // 2
            ffn = lambda u, i=i: moe_swiglu(u, router[i], moe_gate[i], moe_up[i], moe_down[i])
        x = x + g2 * ffn(modulate(rms_norm(x, norm2[layer]), sh2, sc2))
        if not last:
            xc = xc + cg2 * ffn(modulate(rms_norm(xc, norm2[layer]), csh2, csc2))
    return x
```

```python
import functools

import jax
import jax.numpy as jnp
from jax import lax
from jax.experimental import pallas as pl
from jax.experimental.pallas import tpu as pltpu

F32 = jnp.float32
BF16 = jnp.bfloat16
HIGHEST = lax.Precision.HIGHEST

LANES = 128
SUBLANES = 8
HEAD_DIM = 64
RW_DIM = 512
ATT_DIM = 512
KV_DIM = 128
RW_IN = 1792
ATT_OFF = RW_IN
GATE_OFF = RW_IN + ATT_DIM + 2 * KV_DIM
N_EXPERTS = 8
CHUNK = 64
N_PAIRS = RW_DIM // LANES
ATT_SCALE = HEAD_DIM ** -0.5
ROPE_THETA = 10000.0
GRID_W = 64
RMS_EPS = 1e-6
GN_EPS = 64e-5
NEG_INF = -1e30
DECAY_SCALE = 0.6065306597126334
VMEM_LIMIT = 48 * 1024 * 1024


def _cparams(*sem):
    return pltpu.CompilerParams(dimension_semantics=sem, vmem_limit_bytes=VMEM_LIMIT)


def _row_tile(rows, cap):
    t = cap
    while rows % t:
        t //= 2
    return t


def _bdot(a, b):
    return jnp.dot(a.astype(BF16), b.astype(BF16), preferred_element_type=F32)


def _bdot_nt(a, b):
    return lax.dot_general(a.astype(BF16), b.astype(BF16), (((1,), (1,)), ((), ())),
                           preferred_element_type=F32)


def _bdot_tn(a, b):
    return lax.dot_general(a.astype(BF16), b.astype(BF16), (((0,), (0,)), ((), ())),
                           preferred_element_type=F32)


def _split2(x):
    hi = x.astype(BF16)
    lo = (x - hi.astype(F32)).astype(BF16)
    return hi, lo


def _split3(x):
    hi = x.astype(BF16)
    r = x - hi.astype(F32)
    mid = r.astype(BF16)
    lo = (r - mid.astype(F32)).astype(BF16)
    return hi, mid, lo


def _dot3(a, w_hi, w_lo):
    a_hi, a_lo = _split2(a)
    return (jnp.dot(a_hi, w_hi, preferred_element_type=F32)
            + (jnp.dot(a_lo, w_hi, preferred_element_type=F32)
               + jnp.dot(a_hi, w_lo, preferred_element_type=F32)))


def _segsum(x, e):
    outs = []
    for s in range(x.shape[1] // LANES):
        hi, lo = _split2(x[:, s * LANES:(s + 1) * LANES])
        outs.append(jnp.dot(hi, e, preferred_element_type=F32)
                    + jnp.dot(lo, e, preferred_element_type=F32))
    return outs[0] if len(outs) == 1 else jnp.concatenate(outs, axis=1)


def _norm_mod(x, gain, shift, scale):
    y = x * lax.rsqrt(jnp.mean(x * x, axis=-1, keepdims=True) + RMS_EPS) * gain
    return y * (1.0 + scale) + shift


def _mod_kernel(c_ref, w_ref, b_ref, o_ref):
    c = c_ref[...]
    s = c * jax.nn.sigmoid(c)
    o_ref[0] = jnp.dot(s, w_ref[0], precision=HIGHEST, preferred_element_type=F32) + b_ref[0]


def _modulation(c, c_ctx, w_mod, b_mod):
    depth, d, n6 = w_mod.shape
    nb = c.shape[0]
    rm = -(-(nb + 1) // SUBLANES) * SUBLANES
    cc = jnp.zeros((rm, d), F32).at[:nb].set(c).at[nb].set(c_ctx)
    tn = _row_tile(n6, 1536)
    return pl.pallas_call(
        _mod_kernel,
        grid=(depth, n6 // tn),
        in_specs=[pl.BlockSpec((rm, d), lambda l, n: (0, 0)),
                  pl.BlockSpec((1, d, tn), lambda l, n: (l, 0, n)),
                  pl.BlockSpec((1, 1, tn), lambda l, n: (l, 0, n))],
        out_specs=pl.BlockSpec((1, rm, tn), lambda l, n: (l, 0, n)),
        out_shape=jax.ShapeDtypeStruct((depth, rm, n6), F32),
        compiler_params=_cparams("parallel", "parallel"),
        name="adaln_mod",
    )(cc, w_mod, b_mod.reshape(depth, 1, n6))


def _inproj_kernel(*refs, vres):
    if vres:
        x_ref, g_ref, mod_ref, w_ref, vd_ref, o_ref, hv_ref, h_sc = refs
    else:
        x_ref, g_ref, mod_ref, w_ref, o_ref, h_sc = refs

    @pl.when(pl.program_id(1) == 0)
    def _():
        h = _norm_mod(x_ref[...], g_ref[...], mod_ref[0, 0:1, :], mod_ref[0, 1:2, :])
        hb = h.astype(BF16)
        h_sc[...] = hb
        if vres:
            hv_ref[...] = jnp.dot(hb, vd_ref[...], preferred_element_type=F32)

    o_ref[...] = jnp.dot(h_sc[...], w_ref[...], preferred_element_type=F32)


def _inproj(x2, gain, mod, w_b, vd_b, rows):
    m, d = x2.shape
    n = w_b.shape[1]
    tm = _row_tile(rows, 512)
    tn = _row_tile(n, 1536)
    per = rows // tm
    many = mod.shape[0] > 1
    vres = vd_b is not None
    in_specs = [pl.BlockSpec((tm, d), lambda i, j: (i, 0)),
                pl.BlockSpec((1, d), lambda i, j: (0, 0)),
                pl.BlockSpec((1, 6, d), lambda i, j: ((i // per) if many else 0, 0, 0)),
                pl.BlockSpec((d, tn), lambda i, j: (0, j))]
    args = [x2, gain.reshape(1, d), mod, w_b]
    out_specs = [pl.BlockSpec((tm, tn), lambda i, j: (i, j))]
    out_shape = [jax.ShapeDtypeStruct((m, n), F32)]
    if vres:
        in_specs.append(pl.BlockSpec((d, LANES), lambda i, j: (0, 0)))
        args.append(vd_b)
        out_specs.append(pl.BlockSpec((tm, LANES), lambda i, j: (i, 0)))
        out_shape.append(jax.ShapeDtypeStruct((m, LANES), F32))
    outs = pl.pallas_call(
        functools.partial(_inproj_kernel, vres=vres),
        grid=(m // tm, n // tn),
        in_specs=in_specs, out_specs=out_specs, out_shape=out_shape,
        scratch_shapes=[pltpu.VMEM((tm, d), BF16)],
        compiler_params=_cparams("parallel", "arbitrary"),
        name="in_proj",
    )(*args)
    return (outs[0], outs[1]) if vres else (outs[0], None)


def _qkprep_kernel(*refs, rope):
    if rope:
        q0_ref, q1_ref, k_ref, v_ref, cs_ref, qg_ref, kg_ref, e_ref, qn_ref, kn_ref, vb_ref = refs
    else:
        q0_ref, q1_ref, k_ref, v_ref, qg_ref, kg_ref, e_ref, qn_ref, kn_ref, vb_ref = refs
    e = e_ref[...]
    lane = lax.broadcasted_iota(jnp.int32, (1, LANES), 1)
    first = (lane & 31) < 16

    def norm_rope(u, gain):
        y = u * lax.rsqrt(_segsum(u * u, e) * (1.0 / HEAD_DIM) + RMS_EPS) * gain
        if rope:
            rot = jnp.where(first, -pltpu.roll(y, LANES - 16, 1), pltpu.roll(y, 16, 1))
            y = y * cs_ref[0] + rot * cs_ref[1]
        return y

    qg = qg_ref[...]
    for s, ref in enumerate((q0_ref, q1_ref)):
        for t in range(2):
            u = ref[:, t * LANES:(t + 1) * LANES]
            c0 = (2 * s + t) * LANES
            qn_ref[:, c0:c0 + LANES] = (norm_rope(u, qg) * ATT_SCALE).astype(BF16)
    kn_ref[...] = norm_rope(k_ref[...], kg_ref[...]).astype(BF16)
    vb_ref[...] = v_ref[...].astype(BF16)


def _qk_prep(p, cs, qg, kg, e128, rows, rope):
    m = p.shape[0]
    tm = _row_tile(rows, 512)
    per = rows // tm
    qb = ATT_OFF // 256
    kb = (ATT_OFF + ATT_DIM) // LANES
    in_specs = [pl.BlockSpec((tm, 256), lambda i: (i, qb)),
                pl.BlockSpec((tm, 256), lambda i: (i, qb + 1)),
                pl.BlockSpec((tm, LANES), lambda i: (i, kb)),
                pl.BlockSpec((tm, LANES), lambda i: (i, kb + 1))]
    args = [p, p, p, p]
    if rope:
        in_specs.append(pl.BlockSpec((2, tm, LANES), lambda i: (0, i % per, 0)))
        args.append(cs)
    in_specs += [pl.BlockSpec((1, LANES), lambda i: (0, 0)),
                 pl.BlockSpec((1, LANES), lambda i: (0, 0)),
                 pl.BlockSpec((LANES, LANES), lambda i: (0, 0))]
    args += [qg, kg, e128]
    return pl.pallas_call(
        functools.partial(_qkprep_kernel, rope=rope),
        grid=(m // tm,),
        in_specs=in_specs,
        out_specs=[pl.BlockSpec((tm, ATT_DIM), lambda i: (i, 0)),
                   pl.BlockSpec((tm, LANES), lambda i: (i, 0)),
                   pl.BlockSpec((tm, LANES), lambda i: (i, 0))],
        out_shape=[jax.ShapeDtypeStruct((m, ATT_DIM), BF16),
                   jax.ShapeDtypeStruct((m, KV_DIM), BF16),
                   jax.ShapeDtypeStruct((m, KV_DIM), BF16)],
        compiler_params=_cparams("parallel"),
        name="qk_prep",
    )(*args)


def _attn_kernel(*refs, nq, local):
    if local:
        sink_ref, q_ref, kp_ref, kc_ref, kn_ref, vp_ref, vc_ref, vn_ref, kx_ref, vx_ref, o_ref = refs
        k = jnp.concatenate([kp_ref[...], kc_ref[...], kn_ref[...], kx_ref[...]], axis=0).astype(F32)
        v = jnp.concatenate([vp_ref[...], vc_ref[...], vn_ref[...], vx_ref[...]], axis=0).astype(F32)
    else:
        sink_ref, q_ref, kx_ref, vx_ref, o_ref = refs
        k = kx_ref[...].astype(F32)
        v = vx_ref[...].astype(F32)
    i = pl.program_id(1)
    bq = q_ref.shape[0]
    low = lax.broadcasted_iota(jnp.int32, (1, LANES), 1) < HEAD_DIM
    ksw = pltpu.roll(k, HEAD_DIM, 1)
    vsw = pltpu.roll(v, HEAD_DIM, 1)
    row = lax.broadcasted_iota(jnp.int32, (4 * bq, 1), 0)
    if local:
        qo = lax.broadcasted_iota(jnp.int32, (4 * bq, 3 * bq), 0) & (bq - 1)
        kcol = lax.broadcasted_iota(jnp.int32, (4 * bq, 3 * bq), 1)
        ko = kcol & (bq - 1)
        kblk = kcol // bq
        ok = ((kblk == 1)
              | ((kblk == 0) & (ko >= qo) & (i > 0))
              | ((kblk == 2) & (ko <= qo) & (i < nq - 1)))
    for g in range(2):
        kb = (jnp.where(low, k, ksw) if g == 0 else jnp.where(low, ksw, k)).astype(BF16)
        vb = (jnp.where(low, v, vsw) if g == 0 else jnp.where(low, vsw, v)).astype(BF16)
        qa = q_ref[:, (2 * g) * LANES:(2 * g + 1) * LANES].astype(F32)
        qb = q_ref[:, (2 * g + 1) * LANES:(2 * g + 2) * LANES].astype(F32)
        qs = jnp.concatenate([jnp.where(low, qa, 0.0), jnp.where(low, 0.0, qa),
                              jnp.where(low, qb, 0.0), jnp.where(low, 0.0, qb)], axis=0).astype(BF16)
        s = lax.dot_general(qs, kb, (((1,), (1,)), ((), ())), preferred_element_type=F32)
        if local:
            s = jnp.concatenate([jnp.where(ok, s[:, :3 * bq], NEG_INF), s[:, 3 * bq:]], axis=1)
        sk = jnp.where(row < bq, sink_ref[0, 4 * g],
                       jnp.where(row < 2 * bq, sink_ref[0, 4 * g + 1],
                                 jnp.where(row < 3 * bq, sink_ref[0, 4 * g + 2], sink_ref[0, 4 * g + 3])))
        mx = jnp.maximum(jnp.max(s, axis=1, keepdims=True), sk)
        ex = jnp.exp(s - mx)
        den = jnp.sum(ex, axis=1, keepdims=True) + jnp.exp(sk - mx)
        pr = (ex / den).astype(BF16)
        o = jnp.dot(pr, vb, preferred_element_type=F32)
        o_ref[:, (2 * g) * LANES:(2 * g + 1) * LANES] = jnp.where(low, o[0:bq], o[bq:2 * bq]).astype(o_ref.dtype)
        o_ref[:, (2 * g + 1) * LANES:(2 * g + 2) * LANES] = jnp.where(
            low, o[2 * bq:3 * bq], o[3 * bq:4 * bq]).astype(o_ref.dtype)


def _attention(qn, kn, vb, kctx, vctx, sink, nb, rows, cn, local):
    m = qn.shape[0]
    bq = 128
    nq = rows // bq
    smem = pl.BlockSpec(memory_space=pltpu.SMEM)
    qspec = pl.BlockSpec((bq, ATT_DIM), lambda b, i: (b * nq + i, 0))
    cspec = pl.BlockSpec((cn, KV_DIM), lambda b, i: (b, 0))
    if local:
        prv = pl.BlockSpec((bq, KV_DIM), lambda b, i: (b * nq + jnp.maximum(i - 1, 0), 0))
        cur = pl.BlockSpec((bq, KV_DIM), lambda b, i: (b * nq + i, 0))
        nxt = pl.BlockSpec((bq, KV_DIM), lambda b, i: (b * nq + jnp.minimum(i + 1, nq - 1), 0))
        in_specs = [smem, qspec, prv, cur, nxt, prv, cur, nxt, cspec, cspec]
        args = [sink, qn, kn, kn, kn, vb, vb, vb, kctx, vctx]
    else:
        in_specs = [smem, qspec, cspec, cspec]
        args = [sink, qn, kctx, vctx]
    return pl.pallas_call(
        functools.partial(_attn_kernel, nq=nq, local=local),
        grid=(nb, nq),
        in_specs=in_specs,
        out_specs=pl.BlockSpec((bq, ATT_DIM), lambda b, i: (b * nq + i, 0)),
        out_shape=jax.ShapeDtypeStruct((m, ATT_DIM), BF16),
        compiler_params=_cparams("parallel", "parallel"),
        name="window_attn" if local else "ctx_attn",
    )(*args)


def _rwkv_kernel(*refs, nb, nt, tq, vres):
    it = iter(refs)
    p_ref, pv_ref, nx_ref = next(it), next(it), next(it)
    if vres:
        vf_ref, hv_ref = next(it), next(it)
    ts_ref, pd_ref, ps_ref, wlh_ref, wll_ref, wg_ref = (next(it) for _ in range(6))
    if vres:
        vuh_ref, vul_ref = next(it), next(it)
    e_ref, sin_ref = next(it), next(it)
    y_ref, aux_ref, sout_ref = next(it), next(it), next(it)
    ops_sc, s_sc = next(it), next(it)

    d = pl.program_id(0) // nb
    j = pl.program_id(1)
    sgn = 1 - 2 * d
    tile = j + d * (nt - 1 - 2 * j)
    nch = tq // CHUNK

    @pl.when(j == 0)
    def _():
        s_sc[...] = sin_ref[0, 0]

    rw = p_ref[...]
    row = lax.broadcasted_iota(jnp.int32, (tq, 1), 0)
    prev_row = jnp.where(tile > 0, pv_ref[SUBLANES - 1:SUBLANES, :], 0.0)
    next_row = jnp.where(tile < nt - 1, nx_ref[0:1, :], 0.0)
    prv = jnp.where(row == 0, prev_row, pltpu.roll(rw, 1, 0))
    nxt = jnp.where(row == tq - 1, next_row, pltpu.roll(rw, tq - 1, 0))
    xs = rw + ts_ref[0:1, :] * (prv - rw) + ts_ref[1:2, :] * (nxt - rw)

    e = e_ref[...]
    r = xs[:, 0:RW_DIM]
    k = xs[:, RW_DIM:2 * RW_DIM]
    v = xs[:, 2 * RW_DIM:3 * RW_DIM]
    la = xs[:, 3 * RW_DIM:3 * RW_DIM + LANES]
    glo = xs[:, 3 * RW_DIM + LANES:RW_IN]
    lane = lax.broadcasted_iota(jnp.int32, (1, LANES), 1)
    low = lane < HEAD_DIM
    la = jnp.where(low, jnp.tanh(la), la)
    lora = _dot3(la, wlh_ref[0], wll_ref[0])
    logw = -DECAY_SCALE * jax.nn.sigmoid(pd_ref[0, 0:1, :] + lora[:, 0:RW_DIM])
    iclr = jax.nn.sigmoid(pd_ref[0, 1:2, :] + lora[:, RW_DIM:2 * RW_DIM])
    if vres:
        mix = jax.nn.sigmoid(ps_ref[3:4, :] + _dot3(hv_ref[...], vuh_ref[...], vul_ref[...]))
        v = v + (vf_ref[0] - v) * mix
    kk = k * ps_ref[0:1, :]
    kk = kk * lax.rsqrt(_segsum(kk * kk, e) + 1e-12)
    kh = k * (1.0 + (iclr - 1.0) * ps_ref[1:2, :])
    bb = kk * iclr
    cv = _segsum(r * kh * ps_ref[2:3, :], e) * v
    gate = _bdot(jax.nn.sigmoid(glo), wg_ref[...])
    aux_ref[0, :, 0:RW_DIM] = cv
    aux_ref[0, :, RW_DIM:2 * RW_DIM] = jnp.where(d == 0, gate, v)

    ti = lax.broadcasted_iota(jnp.int32, (tq, tq), 0)
    si = lax.broadcasted_iota(jnp.int32, (tq, tq), 1)
    same = (ti // CHUNK) == (si // CHUNK)
    tri = (same & (((si - ti) * sgn) <= 0)).astype(BF16)
    ones = same.astype(BF16)
    cin = jnp.zeros_like(logw)
    ctot = jnp.zeros_like(logw)
    for part in _split3(logw):
        cin = cin + jnp.dot(tri, part, preferred_element_type=F32)
        ctot = ctot + jnp.dot(ones, part, preferred_element_type=F32)
    e_neg = jnp.exp(-cin)
    e_rem = jnp.exp(ctot - cin)
    ops_sc[0] = kk * jnp.exp(cin - logw)
    ops_sc[1] = r * jnp.exp(cin)
    ops_sc[2] = bb * e_neg
    ops_sc[3] = kh * e_neg
    ops_sc[4] = kh * e_rem
    ops_sc[5] = bb * e_rem
    ops_sc[6] = v
    ops_sc[7] = jnp.exp(ctot)

    r2 = lax.broadcasted_iota(jnp.int32, (LANES, LANES), 0)
    c2 = lax.broadcasted_iota(jnp.int32, (LANES, LANES), 1)
    rel = ((c2 & (CHUNK - 1)) - (r2 & (CHUNK - 1))) * sgn
    strict = rel < 0
    incl = rel <= 0
    eye = (r2 == c2).astype(F32)

    def stack(x):
        return jnp.concatenate([jnp.where(low, x, 0.0), jnp.where(low, 0.0, x)], axis=0)

    for ii in range(nch):
        ci = ii + d * (nch - 1 - 2 * ii)
        off = pl.multiple_of(ci * CHUNK, CHUNK)
        for pr in range(N_PAIRS):
            ls = slice(pr * LANES, (pr + 1) * LANES)
            kks, rs, bs, ks, kps, bps, vs = (stack(ops_sc[n, pl.ds(off, CHUNK), ls]) for n in range(7))
            wl = ops_sc[7, pl.ds(off, 1), ls]
            g = _bdot_nt(jnp.concatenate([kks, rs], axis=0), jnp.concatenate([bs, ks], axis=0))
            nn = jnp.where(strict, -g[0:LANES, 0:LANES], 0.0)
            avk = jnp.where(strict, g[0:LANES, LANES:], 0.0)
            arb = jnp.where(incl, g[LANES:, 0:LANES], 0.0)
            ark = jnp.where(incl, g[LANES:, LANES:], 0.0)
            tinv = eye + nn
            q = nn
            for _ in range(5):
                q = _bdot(q, q)
                tinv = tinv + _bdot(tinv, q)
            sm = s_sc[pr]
            us = -_bdot(tinv, _bdot_nt(kks, sm) + _bdot(avk, vs))
            ys = _bdot_nt(rs, sm) + _bdot(ark, vs) + _bdot(arb, us)
            y_ref[0, pl.ds(off, CHUNK), ls] = ys[0:CHUNK] + ys[CHUNK:]
            s_sc[pr] = sm * wl + _bdot_tn(vs, kps) + _bdot_tn(us, bps)

    @pl.when(j == nt - 1)
    def _():
        sout_ref[0, 0] = s_sc[...]


def _rwkv(p, aux_prev, hv, state_in, prm, nb, rows):
    m = p.shape[0]
    tq = _row_tile(rows, 256)
    nt = rows // tq
    rb = tq // SUBLANES
    vres = aux_prev is not None

    def tile(g, j):
        d = g // nb
        return (g % nb) * nt + j + d * (nt - 1 - 2 * j)

    const2 = lambda g, j: (0, 0)
    dir3 = lambda g, j: (g // nb, 0, 0)
    in_specs = [pl.BlockSpec((tq, RW_IN), lambda g, j: (tile(g, j), 0)),
                pl.BlockSpec((SUBLANES, RW_IN), lambda g, j: (jnp.maximum(tile(g, j) * rb - 1, 0), 0)),
                pl.BlockSpec((SUBLANES, RW_IN),
                             lambda g, j: (jnp.minimum((tile(g, j) + 1) * rb, m // SUBLANES - 1), 0))]
    args = [p, p, p]
    if vres:
        in_specs += [pl.BlockSpec((1, tq, RW_DIM), lambda g, j: (1, tile(g, j), 1)),
                     pl.BlockSpec((tq, LANES), lambda g, j: (tile(g, j), 0))]
        args += [aux_prev, hv]
    in_specs += [pl.BlockSpec((2, RW_IN), const2),
                 pl.BlockSpec((1, SUBLANES, RW_DIM), dir3),
                 pl.BlockSpec((SUBLANES, RW_DIM), const2),
                 pl.BlockSpec((1, LANES, 2 * RW_DIM), dir3),
                 pl.BlockSpec((1, LANES, 2 * RW_DIM), dir3),
                 pl.BlockSpec((LANES, RW_DIM), const2)]
    args += [prm["tshift"], prm["pd"], prm["ps"], prm["wla_hi"], prm["wla_lo"], prm["wg"]]
    if vres:
        in_specs += [pl.BlockSpec((LANES, RW_DIM), const2), pl.BlockSpec((LANES, RW_DIM), const2)]
        args += [prm["vu_hi"], prm["vu_lo"]]
    in_specs += [pl.BlockSpec((LANES, LANES), const2),
                 pl.BlockSpec((1, 1, N_PAIRS, LANES, LANES), lambda g, j: (g // nb, g % nb, 0, 0, 0))]
    args += [prm["e128"], state_in]
    return pl.pallas_call(
        functools.partial(_rwkv_kernel, nb=nb, nt=nt, tq=tq, vres=vres),
        grid=(2 * nb, nt),
        in_specs=in_specs,
        out_specs=[pl.BlockSpec((1, tq, RW_DIM), lambda g, j: (g // nb, tile(g, j), 0)),
                   pl.BlockSpec((1, tq, 2 * RW_DIM), lambda g, j: (g // nb, tile(g, j), 0)),
                   pl.BlockSpec((1, 1, N_PAIRS, LANES, LANES), lambda g, j: (g // nb, g % nb, 0, 0, 0))],
        out_shape=[jax.ShapeDtypeStruct((2, m, RW_DIM), F32),
                   jax.ShapeDtypeStruct((2, m, 2 * RW_DIM), F32),
                   jax.ShapeDtypeStruct((2, nb, N_PAIRS, LANES, LANES), F32)],
        scratch_shapes=[pltpu.VMEM((8, tq, RW_DIM), F32), pltpu.VMEM((N_PAIRS, LANES, LANES), F32)],
        compiler_params=_cparams("parallel", "arbitrary"),
        name="rwkv7_scan",
    )(*args)


def _merge_kernel(x_ref, y_ref, aux_ref, ya_ref, g0_ref, g1_ref, g2_ref, g3_ref, ln_ref, e_ref,
                  wbr_ref, wba_ref, wo_ref, mod_ref, o_ref):
    e = e_ref[...]
    y = y_ref[0] + y_ref[1]
    mu = _segsum(y, e) * (1.0 / HEAD_DIM)
    yc = y - mu
    var = _segsum(yc * yc, e) * (1.0 / HEAD_DIM)
    yn = yc * lax.rsqrt(var + GN_EPS) * ln_ref[0:1, :] + ln_ref[1:2, :]
    cv = aux_ref[0, :, 0:RW_DIM] + aux_ref[1, :, 0:RW_DIM]
    yrw = (yn + cv) * aux_ref[0, :, RW_DIM:2 * RW_DIM]
    a = _bdot(yrw, wbr_ref[...])
    b = jnp.dot(ya_ref[...], wba_ref[...], preferred_element_type=F32)
    h = a.shape[1] // 2
    mrg = jnp.concatenate(
        [jax.nn.sigmoid(g0_ref[...]) * a[:, :h] + jax.nn.sigmoid(g2_ref[...]) * b[:, :h],
         jax.nn.sigmoid(g1_ref[...]) * a[:, h:] + jax.nn.sigmoid(g3_ref[...]) * b[:, h:]], axis=1)
    o_ref[...] = x_ref[...] + mod_ref[0, 2:3, :] * _bdot(mrg, wo_ref[...])


def _merge(x2, y, aux, ya, p, ln, e128, wbr, wba, wo, mod, rows):
    m, d = x2.shape
    tm = _row_tile(rows, 512)
    per = rows // tm
    many = mod.shape[0] > 1
    gb = GATE_OFF // 512
    const2 = lambda i: (0, 0)
    in_specs = [pl.BlockSpec((tm, d), lambda i: (i, 0)),
                pl.BlockSpec((2, tm, RW_DIM), lambda i: (0, i, 0)),
                pl.BlockSpec((2, tm, 2 * RW_DIM), lambda i: (0, i, 0)),
                pl.BlockSpec((tm, ATT_DIM), lambda i: (i, 0))]
    in_specs += [pl.BlockSpec((tm, 512), (lambda i, c=c: (i, gb + c))) for c in range(4)]
    in_specs += [pl.BlockSpec((2, RW_DIM), const2),
                 pl.BlockSpec((LANES, LANES), const2),
                 pl.BlockSpec(wbr.shape, const2),
                 pl.BlockSpec(wba.shape, const2),
                 pl.BlockSpec(wo.shape, const2),
                 pl.BlockSpec((1, 6, d), lambda i: ((i // per) if many else 0, 0, 0))]
    return pl.pallas_call(
        _merge_kernel,
        grid=(m // tm,),
        in_specs=in_specs,
        out_specs=pl.BlockSpec((tm, d), lambda i: (i, 0)),
        out_shape=jax.ShapeDtypeStruct((m, d), F32),
        compiler_params=_cparams("parallel"),
        name="branch_merge",
    )(x2, y, aux, ya, p, p, p, p, ln, e128, wbr, wba, wo, mod)


def _ffn_kernel(x_ref, g_ref, mod_ref, wg_ref, wu_ref, wd_ref, o_ref, h_sc, acc_sc, *, nf):
    f = pl.program_id(1)

    @pl.when(f == 0)
    def _():
        h = _norm_mod(x_ref[...], g_ref[...], mod_ref[0, 3:4, :], mod_ref[0, 4:5, :])
        h_sc[...] = h.astype(BF16)
        acc_sc[...] = jnp.zeros_like(acc_sc)

    h = h_sc[...]
    a = jnp.dot(h, wg_ref[...], preferred_element_type=F32)
    u = jnp.dot(h, wu_ref[...], preferred_element_type=F32)
    act = (a * jax.nn.sigmoid(a)) * u
    acc_sc[...] += jnp.dot(act.astype(BF16), wd_ref[...], preferred_element_type=F32)

    @pl.when(f == nf - 1)
    def _():
        o_ref[...] = x_ref[...] + mod_ref[0, 5:6, :] * acc_sc[...]


def _ffn(x2, gain, mod, wg, wu, wd, rows):
    m, d = x2.shape
    dff = wg.shape[1]
    tm = _row_tile(rows, 512)
    per = rows // tm
    many = mod.shape[0] > 1
    tf = dff // 2 if (dff // 2) % LANES == 0 else dff
    nf = dff // tf
    return pl.pallas_call(
        functools.partial(_ffn_kernel, nf=nf),
        grid=(m // tm, nf),
        in_specs=[pl.BlockSpec((tm, d), lambda i, f: (i, 0)),
                  pl.BlockSpec((1, d), lambda i, f: (0, 0)),
                  pl.BlockSpec((1, 6, d), lambda i, f: ((i // per) if many else 0, 0, 0)),
                  pl.BlockSpec((d, tf), lambda i, f: (0, f)),
                  pl.BlockSpec((d, tf), lambda i, f: (0, f)),
                  pl.BlockSpec((tf, d), lambda i, f: (f, 0))],
        out_specs=pl.BlockSpec((tm, d), lambda i, f: (i, 0)),
        out_shape=jax.ShapeDtypeStruct((m, d), F32),
        scratch_shapes=[pltpu.VMEM((tm, d), BF16), pltpu.VMEM((tm, d), F32)],
        compiler_params=_cparams("parallel", "arbitrary"),
        name="ffn_swiglu",
    )(x2, gain.reshape(1, d), mod, wg, wu, wd)


def _moe_kernel(x_ref, g_ref, mod_ref, rt_ref, wg_ref, wu_ref, wd_ref, o_ref, h_sc, acc_sc, we_sc, *, ne, nf):
    ex = pl.program_id(1)
    f = pl.program_id(2)
    lane = lax.broadcasted_iota(jnp.int32, we_sc.shape, 1)

    @pl.when((ex == 0) & (f == 0))
    def _():
        h = _norm_mod(x_ref[...], g_ref[...], mod_ref[0, 3:4, :], mod_ref[0, 4:5, :])
        h_sc[...] = h.astype(BF16)
        acc_sc[...] = jnp.zeros_like(acc_sc)
        logits = jnp.dot(h, rt_ref[...], precision=HIGHEST, preferred_element_type=F32)
        lg = jnp.where(lane < ne, logits, NEG_INF)
        t1 = jnp.max(lg, axis=1, keepdims=True)
        i1 = jnp.min(jnp.where(lg == t1, lane, LANES), axis=1, keepdims=True)
        lg2 = jnp.where(lane == i1, NEG_INF, lg)
        t2 = jnp.max(lg2, axis=1, keepdims=True)
        i2 = jnp.min(jnp.where(lg2 == t2, lane, LANES), axis=1, keepdims=True)
        e2 = jnp.exp(t2 - t1)
        den = 1.0 + e2
        we_sc[...] = jnp.where(lane == i1, 1.0 / den, 0.0) + jnp.where(lane == i2, e2 / den, 0.0)

    h = h_sc[...]
    a = jnp.dot(h, wg_ref[0], preferred_element_type=F32)
    u = jnp.dot(h, wu_ref[0], preferred_element_type=F32)
    act = (a * jax.nn.sigmoid(a)) * u
    w_e = jnp.sum(jnp.where(lane == ex, we_sc[...], 0.0), axis=1, keepdims=True)
    acc_sc[...] += w_e * jnp.dot(act.astype(BF16), wd_ref[0], preferred_element_type=F32)

    @pl.when((ex == ne - 1) & (f == nf - 1))
    def _():
        o_ref[...] = x_ref[...] + mod_ref[0, 5:6, :] * acc_sc[...]


def _moe(x2, gain, mod, router_pad, wg, wu, wd, rows):
    m, d = x2.shape
    ne, _, dff = wg.shape
    tm = _row_tile(rows, 512)
    per = rows // tm
    many = mod.shape[0] > 1
    tf = dff // 2 if (dff // 2) % LANES == 0 else dff
    nf = dff // tf
    return pl.pallas_call(
        functools.partial(_moe_kernel, ne=ne, nf=nf),
        grid=(m // tm, ne, nf),
        in_specs=[pl.BlockSpec((tm, d), lambda i, e, f: (i, 0)),
                  pl.BlockSpec((1, d), lambda i, e, f: (0, 0)),
                  pl.BlockSpec((1, 6, d), lambda i, e, f: ((i // per) if many else 0, 0, 0)),
                  pl.BlockSpec((d, LANES), lambda i, e, f: (0, 0)),
                  pl.BlockSpec((1, d, tf), lambda i, e, f: (e, 0, f)),
                  pl.BlockSpec((1, d, tf), lambda i, e, f: (e, 0, f)),
                  pl.BlockSpec((1, tf, d), lambda i, e, f: (e, f, 0))],
        out_specs=pl.BlockSpec((tm, d), lambda i, e, f: (i, 0)),
        out_shape=jax.ShapeDtypeStruct((m, d), F32),
        scratch_shapes=[pltpu.VMEM((tm, d), BF16), pltpu.VMEM((tm, d), F32), pltpu.VMEM((tm, LANES), F32)],
        compiler_params=_cparams("parallel", "arbitrary", "arbitrary"),
        name="moe_swiglu",
    )(x2, gain.reshape(1, d), mod, router_pad, wg, wu, wd)


def _rope_table(t):
    rows = t // GRID_W
    row = jnp.repeat(jnp.arange(rows, dtype=F32), GRID_W)
    col = jnp.tile(jnp.arange(GRID_W, dtype=F32), rows)
    half = HEAD_DIM // 2
    inv_freq = ROPE_THETA ** (-jnp.arange(0, half, 2, dtype=F32) / half)
    ar = row[:, None] * inv_freq
    ac = col[:, None] * inv_freq
    ang = jnp.concatenate([ar, ar, ac, ac] * 2, axis=-1)
    return jnp.stack([jnp.cos(ang), jnp.sin(ang)])


def _pad_rows(a, rows):
    return jnp.zeros((rows,) + a.shape[1:], a.dtype).at[:a.shape[0]].set(a)


def _hi_lo(w):
    hi = w.astype(BF16)
    return hi, (w - hi.astype(F32)).astype(BF16)


def kernel(x, c, ctx, c_ctx, norm1, norm2, w_mod, b_mod, w_in, tshift, w0, w_decay_up, a0, w_iclr_up, v0, w_vres_down, w_vres_up, w_gate_up, k_k, k_a, r_k, lnx_g, lnx_b, q_norm, k_norm, sink, w_br_rwkv, w_br_attn, w_out, ffn_gate, ffn_up, ffn_down, router, moe_gate, moe_up, moe_down):
    nb, t, d = x.shape
    cn = ctx.shape[1]
    depth = w_in.shape[0]
    mods = _modulation(c, c_ctx, w_mod, b_mod)
    cs = _rope_table(t)
    e128 = jnp.kron(jnp.eye(2, dtype=F32), jnp.ones((HEAD_DIM, HEAD_DIM), F32)).astype(BF16)
    xl = x.reshape(nb * t, d)
    xc = ctx.reshape(nb * cn, d)
    aux_l = aux_c = None
    for layer in range(depth):
        last = layer == depth - 1
        mod_l = mods[layer, :nb].reshape(nb, 6, d)
        mod_c = mods[layer, nb:nb + 1].reshape(1, 6, d)
        w_in_b = w_in[layer].astype(BF16)
        vd_b = None
        prm = {"tshift": tshift[layer], "e128": e128, "wg": w_gate_up[layer].astype(BF16)}
        prm["pd"] = jnp.stack(
            [_pad_rows(jnp.stack([w0[layer, dd], a0[layer, dd]]), SUBLANES) for dd in range(2)])
        shared = [k_k[layer], k_a[layer], r_k[layer].reshape(RW_DIM)]
        if layer > 0:
            shared.append(v0[layer - 1])
            vd_b = _pad_rows(w_vres_down[layer - 1].T, LANES).T.astype(BF16)
            prm["vu_hi"], prm["vu_lo"] = _hi_lo(_pad_rows(w_vres_up[layer - 1], LANES))
        prm["ps"] = _pad_rows(jnp.stack(shared), SUBLANES)
        wla = jnp.zeros((2, LANES, 2 * RW_DIM), F32)
        wla = wla.at[:, :HEAD_DIM, :RW_DIM].set(w_decay_up[layer]).at[:, HEAD_DIM:, RW_DIM:].set(w_iclr_up[layer])
        prm["wla_hi"], prm["wla_lo"] = _hi_lo(wla)

        p_l, hv_l = _inproj(xl, norm1[layer], mod_l, w_in_b, vd_b, t)
        p_c, hv_c = _inproj(xc, norm1[layer], mod_c, w_in_b, vd_b, cn)

        s0 = jnp.zeros((2, nb, N_PAIRS, LANES, LANES), F32)
        y_c, aux_c_new, s_ctx = _rwkv(p_c, aux_c, hv_c, s0, prm, nb, cn)
        y_l, aux_l_new, _ = _rwkv(p_l, aux_l, hv_l, s_ctx, prm, nb, t)
        if layer == 0:
            aux_first_l, aux_first_c = aux_l_new, aux_c_new
        aux_l, aux_c = aux_first_l, aux_first_c

        qg = jnp.tile(q_norm[layer], 2).reshape(1, LANES)
        kg = jnp.tile(k_norm[layer], 2).reshape(1, LANES)
        qn_l, kn_l, vb_l = _qk_prep(p_l, cs, qg, kg, e128, t, True)
        qn_c, kn_c, vb_c = _qk_prep(p_c, None, qg, kg, e128, cn, False)
        sk = sink[layer].reshape(1, -1)
        ya_l = _attention(qn_l, kn_l, vb_l, kn_c, vb_c, sk, nb, t, cn, True)

        ln = jnp.stack([lnx_g[layer], lnx_b[layer]])
        wbr = w_br_rwkv[layer].astype(BF16)
        wba = w_br_attn[layer].astype(BF16)
        wo = w_out[layer].astype(BF16)
        xl = _merge(xl, y_l, aux_l_new, ya_l, p_l, ln, e128, wbr, wba, wo, mod_l, t)
        if not last:
            ya_c = _attention(qn_c, None, None, kn_c, vb_c, sk, nb, cn, cn, False)
            xc = _merge(xc, y_c, aux_c_new, ya_c, p_c, ln, e128, wbr, wba, wo, mod_c, cn)

        i = layer // 2
        if layer % 2 == 0:
            fw = (ffn_gate[i].astype(BF16), ffn_up[i].astype(BF16), ffn_down[i].astype(BF16))
            xl = _ffn(xl, norm2[layer], mod_l, *fw, t)
            if not last:
                xc = _ffn(xc, norm2[layer], mod_c, *fw, cn)
        else:
            rt = _pad_rows(router[i].T, LANES).T
            mw = (moe_gate[i].astype(BF16), moe_up[i].astype(BF16), moe_down[i].astype(BF16))
            xl = _moe(xl, norm2[layer], mod_l, rt, *mw, t)
            if not last:
                xc = _moe(xc, norm2[layer], mod_c, rt, *mw, cn)
    return xl.reshape(nb, t, d)
```

```python
import functools

import jax
import jax.numpy as jnp
from jax import lax
from jax.experimental import pallas as pl
from jax.experimental.pallas import tpu as pltpu

F32 = jnp.float32
BF16 = jnp.bfloat16
HIGHEST = lax.Precision.HIGHEST

LANES = 128
SUBLANES = 8
HEAD_DIM = 64
RW_DIM = 512
ATT_DIM = 512
KV_DIM = 128
RW_IN = 1792
ATT_OFF = RW_IN
GATE_OFF = RW_IN + ATT_DIM + 2 * KV_DIM
N_EXPERTS = 8
CHUNK = 64
N_PAIRS = RW_DIM // LANES
ATT_SCALE = HEAD_DIM ** -0.5
ROPE_THETA = 10000.0
GRID_W = 64
RMS_EPS = 1e-6
GN_EPS = 64e-5
NEG_INF = -1e30
DECAY_SCALE = 0.6065306597126334
VMEM_LIMIT = 48 * 1024 * 1024


def _cparams(*sem):
    return pltpu.CompilerParams(dimension_semantics=sem, vmem_limit_bytes=VMEM_LIMIT)


def _row_tile(rows, cap):
    t = cap
    while rows % t:
        t //= 2
    return t


def _bdot(a, b):
    return jnp.dot(a.astype(BF16), b.astype(BF16), preferred_element_type=F32)


def _bdot_nt(a, b):
    return lax.dot_general(a.astype(BF16), b.astype(BF16), (((1,), (1,)), ((), ())),
                           preferred_element_type=F32)


def _bdot_tn(a, b):
    return lax.dot_general(a.astype(BF16), b.astype(BF16), (((0,), (0,)), ((), ())),
                           preferred_element_type=F32)


def _split2(x):
    hi = x.astype(BF16)
    lo = (x - hi.astype(F32)).astype(BF16)
    return hi, lo


def _split3(x):
    hi = x.astype(BF16)
    r = x - hi.astype(F32)
    mid = r.astype(BF16)
    lo = (r - mid.astype(F32)).astype(BF16)
    return hi, mid, lo


def _dot3(a, w_hi, w_lo):
    a_hi, a_lo = _split2(a)
    return (jnp.dot(a_hi, w_hi, preferred_element_type=F32)
            + (jnp.dot(a_lo, w_hi, preferred_element_type=F32)
               + jnp.dot(a_hi, w_lo, preferred_element_type=F32)))


def _segsum(x, e):
    outs = []
    for s in range(x.shape[1] // LANES):
        hi, lo = _split2(x[:, s * LANES:(s + 1) * LANES])
        outs.append(jnp.dot(hi, e, preferred_element_type=F32)
                    + jnp.dot(lo, e, preferred_element_type=F32))
    return outs[0] if len(outs) == 1 else jnp.concatenate(outs, axis=1)


def _norm_mod(x, gain, shift, scale):
    y = x * lax.rsqrt(jnp.mean(x * x, axis=-1, keepdims=True) + RMS_EPS) * gain
    return y * (1.0 + scale) + shift


def _mod_kernel(c_ref, w_ref, b_ref, o_ref):
    c = c_ref[...]
    s = c * jax.nn.sigmoid(c)
    o_ref[0] = jnp.dot(s, w_ref[0], precision=HIGHEST, preferred_element_type=F32) + b_ref[0]


def _modulation(c, c_ctx, w_mod, b_mod):
    depth, d, n6 = w_mod.shape
    nb = c.shape[0]
    rm = -(-(nb + 1) // SUBLANES) * SUBLANES
    cc = jnp.zeros((rm, d), F32).at[:nb].set(c).at[nb].set(c_ctx)
    tn = _row_tile(n6, 1536)
    return pl.pallas_call(
        _mod_kernel,
        grid=(depth, n6 // tn),
        in_specs=[pl.BlockSpec((rm, d), lambda l, n: (0, 0)),
                  pl.BlockSpec((1, d, tn), lambda l, n: (l, 0, n)),
                  pl.BlockSpec((1, 1, tn), lambda l, n: (l, 0, n))],
        out_specs=pl.BlockSpec((1, rm, tn), lambda l, n: (l, 0, n)),
        out_shape=jax.ShapeDtypeStruct((depth, rm, n6), F32),
        compiler_params=_cparams("parallel", "parallel"),
        name="adaln_mod",
    )(cc, w_mod, b_mod.reshape(depth, 1, n6))


def _inproj_kernel(*refs, vres):
    if vres:
        x_ref, g_ref, mod_ref, w_ref, vd_ref, o_ref, hv_ref, h_sc = refs
    else:
        x_ref, g_ref, mod_ref, w_ref, o_ref, h_sc = refs

    @pl.when(pl.program_id(1) == 0)
    def _():
        h = _norm_mod(x_ref[...], g_ref[...], mod_ref[0, 0:1, :], mod_ref[0, 1:2, :])
        hb = h.astype(BF16)
        h_sc[...] = hb
        if vres:
            hv_ref[...] = jnp.dot(hb, vd_ref[...], preferred_element_type=F32)

    o_ref[...] = jnp.dot(h_sc[...], w_ref[...], preferred_element_type=F32)


def _inproj(x2, gain, mod, w_b, vd_b, rows):
    m, d = x2.shape
    n = w_b.shape[1]
    tm = _row_tile(rows, 512)
    tn = _row_tile(n, 1536)
    per = rows // tm
    many = mod.shape[0] > 1
    vres = vd_b is not None
    in_specs = [pl.BlockSpec((tm, d), lambda i, j: (i, 0)),
                pl.BlockSpec((1, d), lambda i, j: (0, 0)),
                pl.BlockSpec((1, 6, d), lambda i, j: ((i // per) if many else 0, 0, 0)),
                pl.BlockSpec((d, tn), lambda i, j: (0, j))]
    args = [x2, gain.reshape(1, d), mod, w_b]
    out_specs = [pl.BlockSpec((tm, tn), lambda i, j: (i, j))]
    out_shape = [jax.ShapeDtypeStruct((m, n), F32)]
    if vres:
        in_specs.append(pl.BlockSpec((d, LANES), lambda i, j: (0, 0)))
        args.append(vd_b)
        out_specs.append(pl.BlockSpec((tm, LANES), lambda i, j: (i, 0)))
        out_shape.append(jax.ShapeDtypeStruct((m, LANES), F32))
    outs = pl.pallas_call(
        functools.partial(_inproj_kernel, vres=vres),
        grid=(m // tm, n // tn),
        in_specs=in_specs, out_specs=out_specs, out_shape=out_shape,
        scratch_shapes=[pltpu.VMEM((tm, d), BF16)],
        compiler_params=_cparams("parallel", "arbitrary"),
        name="in_proj",
    )(*args)
    return (outs[0], outs[1]) if vres else (outs[0], None)


def _qkprep_kernel(*refs, rope):
    if rope:
        q0_ref, q1_ref, k_ref, v_ref, cs_ref, qg_ref, kg_ref, e_ref, qn_ref, kn_ref, vb_ref = refs
    else:
        q0_ref, q1_ref, k_ref, v_ref, qg_ref, kg_ref, e_ref, qn_ref, kn_ref, vb_ref = refs
    e = e_ref[...]
    lane = lax.broadcasted_iota(jnp.int32, (1, LANES), 1)
    first = (lane & 31) < 16

    def norm_rope(u, gain):
        y = u * lax.rsqrt(_segsum(u * u, e) * (1.0 / HEAD_DIM) + RMS_EPS) * gain
        if rope:
            rot = jnp.where(first, -pltpu.roll(y, LANES - 16, 1), pltpu.roll(y, 16, 1))
            y = y * cs_ref[0] + rot * cs_ref[1]
        return y

    qg = qg_ref[...]
    for s, ref in enumerate((q0_ref, q1_ref)):
        for t in range(2):
            u = ref[:, t * LANES:(t + 1) * LANES]
            c0 = (2 * s + t) * LANES
            qn_ref[:, c0:c0 + LANES] = (norm_rope(u, qg) * ATT_SCALE).astype(BF16)
    kn_ref[...] = norm_rope(k_ref[...], kg_ref[...]).astype(BF16)
    vb_ref[...] = v_ref[...].astype(BF16)


def _qk_prep(p, cs, qg, kg, e128, rows, rope):
    m = p.shape[0]
    tm = _row_tile(rows, 512)
    per = rows // tm
    qb = ATT_OFF // 256
    kb = (ATT_OFF + ATT_DIM) // LANES
    in_specs = [pl.BlockSpec((tm, 256), lambda i: (i, qb)),
                pl.BlockSpec((tm, 256), lambda i: (i, qb + 1)),
                pl.BlockSpec((tm, LANES), lambda i: (i, kb)),
                pl.BlockSpec((tm, LANES), lambda i: (i, kb + 1))]
    args = [p, p, p, p]
    if rope:
        in_specs.append(pl.BlockSpec((2, tm, LANES), lambda i: (0, i % per, 0)))
        args.append(cs)
    in_specs += [pl.BlockSpec((1, LANES), lambda i: (0, 0)),
                 pl.BlockSpec((1, LANES), lambda i: (0, 0)),
                 pl.BlockSpec((LANES, LANES), lambda i: (0, 0))]
    args += [qg, kg, e128]
    return pl.pallas_call(
        functools.partial(_qkprep_kernel, rope=rope),
        grid=(m // tm,),
        in_specs=in_specs,
        out_specs=[pl.BlockSpec((tm, ATT_DIM), lambda i: (i, 0)),
                   pl.BlockSpec((tm, LANES), lambda i: (i, 0)),
                   pl.BlockSpec((tm, LANES), lambda i: (i, 0))],
        out_shape=[jax.ShapeDtypeStruct((m, ATT_DIM), BF16),
                   jax.ShapeDtypeStruct((m, KV_DIM), BF16),
                   jax.ShapeDtypeStruct((m, KV_DIM), BF16)],
        compiler_params=_cparams("parallel"),
        name="qk_prep",
    )(*args)


def _attn_kernel(*refs, nq, local):
    if local:
        sink_ref, q_ref, kp_ref, kc_ref, kn_ref, vp_ref, vc_ref, vn_ref, kx_ref, vx_ref, o_ref = refs
        k = jnp.concatenate([kp_ref[...], kc_ref[...], kn_ref[...], kx_ref[...]], axis=0).astype(F32)
        v = jnp.concatenate([vp_ref[...], vc_ref[...], vn_ref[...], vx_ref[...]], axis=0).astype(F32)
    else:
        sink_ref, q_ref, kx_ref, vx_ref, o_ref = refs
        k = kx_ref[...].astype(F32)
        v = vx_ref[...].astype(F32)
    i = pl.program_id(1)
    bq = q_ref.shape[0]
    low = lax.broadcasted_iota(jnp.int32, (1, LANES), 1) < HEAD_DIM
    ksw = pltpu.roll(k, HEAD_DIM, 1)
    vsw = pltpu.roll(v, HEAD_DIM, 1)
    row = lax.broadcasted_iota(jnp.int32, (4 * bq, 1), 0)
    if local:
        qo = lax.broadcasted_iota(jnp.int32, (4 * bq, 3 * bq), 0) & (bq - 1)
        kcol = lax.broadcasted_iota(jnp.int32, (4 * bq, 3 * bq), 1)
        ko = kcol & (bq - 1)
        kblk = kcol // bq
        ok = ((kblk == 1)
              | ((kblk == 0) & (ko >= qo) & (i > 0))
              | ((kblk == 2) & (ko <= qo) & (i < nq - 1)))
    for g in range(2):
        kb = (jnp.where(low, k, ksw) if g == 0 else jnp.where(low, ksw, k)).astype(BF16)
        vb = (jnp.where(low, v, vsw) if g == 0 else jnp.where(low, vsw, v)).astype(BF16)
        qa = q_ref[:, (2 * g) * LANES:(2 * g + 1) * LANES].astype(F32)
        qb = q_ref[:, (2 * g + 1) * LANES:(2 * g + 2) * LANES].astype(F32)
        qs = jnp.concatenate([jnp.where(low, qa, 0.0), jnp.where(low, 0.0, qa),
                              jnp.where(low, qb, 0.0), jnp.where(low, 0.0, qb)], axis=0).astype(BF16)
        s = lax.dot_general(qs, kb, (((1,), (1,)), ((), ())), preferred_element_type=F32)
        if local:
            s = jnp.concatenate([jnp.where(ok, s[:, :3 * bq], NEG_INF), s[:, 3 * bq:]], axis=1)
        sk = jnp.where(row < bq, sink_ref[0, 4 * g],
                       jnp.where(row < 2 * bq, sink_ref[0, 4 * g + 1],
                                 jnp.where(row < 3 * bq, sink_ref[0, 4 * g + 2], sink_ref[0, 4 * g + 3])))
        mx = jnp.maximum(jnp.max(s, axis=1, keepdims=True), sk)
        ex = jnp.exp(s - mx)
        den = jnp.sum(ex, axis=1, keepdims=True) + jnp.exp(sk - mx)
        pr = (ex / den).astype(BF16)
        o = jnp.dot(pr, vb, preferred_element_type=F32)
        o_ref[:, (2 * g) * LANES:(2 * g + 1) * LANES] = jnp.where(low, o[0:bq], o[bq:2 * bq]).astype(o_ref.dtype)
        o_ref[:, (2 * g + 1) * LANES:(2 * g + 2) * LANES] = jnp.where(
            low, o[2 * bq:3 * bq], o[3 * bq:4 * bq]).astype(o_ref.dtype)


def _attention(qn, kn, vb, kctx, vctx, sink, nb, rows, cn, local):
    m = qn.shape[0]
    bq = 128
    nq = rows // bq
    smem = pl.BlockSpec(memory_space=pltpu.SMEM)
    qspec = pl.BlockSpec((bq, ATT_DIM), lambda b, i: (b * nq + i, 0))
    cspec = pl.BlockSpec((cn, KV_DIM), lambda b, i: (b, 0))
    if local:
        prv = pl.BlockSpec((bq, KV_DIM), lambda b, i: (b * nq + jnp.maximum(i - 1, 0), 0))
        cur = pl.BlockSpec((bq, KV_DIM), lambda b, i: (b * nq + i, 0))
        nxt = pl.BlockSpec((bq, KV_DIM), lambda b, i: (b * nq + jnp.minimum(i + 1, nq - 1), 0))
        in_specs = [smem, qspec, prv, cur, nxt, prv, cur, nxt, cspec, cspec]
        args = [sink, qn, kn, kn, kn, vb, vb, vb, kctx, vctx]
    else:
        in_specs = [smem, qspec, cspec, cspec]
        args = [sink, qn, kctx, vctx]
    return pl.pallas_call(
        functools.partial(_attn_kernel, nq=nq, local=local),
        grid=(nb, nq),
        in_specs=in_specs,
        out_specs=pl.BlockSpec((bq, ATT_DIM), lambda b, i: (b * nq + i, 0)),
        out_shape=jax.ShapeDtypeStruct((m, ATT_DIM), BF16),
        compiler_params=_cparams("parallel", "parallel"),
        name="window_attn" if local else "ctx_attn",
    )(*args)


def _rwkv_kernel(*refs, nt, tq, vres, drn):
    it = iter(refs)
    p_ref, pv_ref, nx_ref = next(it), next(it), next(it)
    if vres:
        vf_ref, hv_ref = next(it), next(it)
    ts_ref, pd_ref, ps_ref, wlh_ref, wll_ref, wg_ref = (next(it) for _ in range(6))
    if vres:
        vuh_ref, vul_ref = next(it), next(it)
    e_ref, sin_ref = next(it), next(it)
    y_ref, aux_ref, sout_ref = next(it), next(it), next(it)
    ops_sc, s_sc = next(it), next(it)

    j = pl.program_id(1)
    sgn = 1 - 2 * drn
    tile = j if drn == 0 else nt - 1 - j
    nch = tq // CHUNK

    @pl.when(j == 0)
    def _():
        s_sc[...] = sin_ref[0]

    rw = p_ref[...]
    row = lax.broadcasted_iota(jnp.int32, (tq, 1), 0)
    prev_row = jnp.where(tile > 0, pv_ref[SUBLANES - 1:SUBLANES, :], 0.0)
    next_row = jnp.where(tile < nt - 1, nx_ref[0:1, :], 0.0)
    prv = jnp.where(row == 0, prev_row, pltpu.roll(rw, 1, 0))
    nxt = jnp.where(row == tq - 1, next_row, pltpu.roll(rw, tq - 1, 0))
    xs = rw + ts_ref[0:1, :] * (prv - rw) + ts_ref[1:2, :] * (nxt - rw)

    e = e_ref[...]
    r = xs[:, 0:RW_DIM]
    k = xs[:, RW_DIM:2 * RW_DIM]
    v = xs[:, 2 * RW_DIM:3 * RW_DIM]
    la = xs[:, 3 * RW_DIM:3 * RW_DIM + LANES]
    lane = lax.broadcasted_iota(jnp.int32, (1, LANES), 1)
    low = lane < HEAD_DIM
    la = jnp.where(low, jnp.tanh(la), la)
    lora = _dot3(la, wlh_ref[0], wll_ref[0])
    logw = -DECAY_SCALE * jax.nn.sigmoid(pd_ref[0, 0:1, :] + lora[:, 0:RW_DIM])
    iclr = jax.nn.sigmoid(pd_ref[0, 1:2, :] + lora[:, RW_DIM:2 * RW_DIM])
    if vres:
        mix = jax.nn.sigmoid(ps_ref[3:4, :] + _dot3(hv_ref[...], vuh_ref[...], vul_ref[...]))
        v = v + (vf_ref[...] - v) * mix
    kk = k * ps_ref[0:1, :]
    kk = kk * lax.rsqrt(_segsum(kk * kk, e) + 1e-12)
    kh = k * (1.0 + (iclr - 1.0) * ps_ref[1:2, :])
    bb = kk * iclr
    aux_ref[:, 0:RW_DIM] = _segsum(r * kh * ps_ref[2:3, :], e) * v
    if drn == 0:
        glo = xs[:, 3 * RW_DIM + LANES:RW_IN]
        aux_ref[:, RW_DIM:2 * RW_DIM] = _bdot(jax.nn.sigmoid(glo), wg_ref[...])
    else:
        aux_ref[:, RW_DIM:2 * RW_DIM] = v

    ti = lax.broadcasted_iota(jnp.int32, (tq, tq), 0)
    si = lax.broadcasted_iota(jnp.int32, (tq, tq), 1)
    same = (ti // CHUNK) == (si // CHUNK)
    tri = (same & (((si - ti) * sgn) <= 0)).astype(BF16)
    ones = same.astype(BF16)
    cin = jnp.zeros_like(logw)
    ctot = jnp.zeros_like(logw)
    for part in _split3(logw):
        cin = cin + jnp.dot(tri, part, preferred_element_type=F32)
        ctot = ctot + jnp.dot(ones, part, preferred_element_type=F32)
    e_neg = jnp.exp(-cin)
    e_rem = jnp.exp(ctot - cin)
    ops_sc[0] = kk * jnp.exp(cin - logw)
    ops_sc[1] = r * jnp.exp(cin)
    ops_sc[2] = bb * e_neg
    ops_sc[3] = kh * e_neg
    ops_sc[4] = kh * e_rem
    ops_sc[5] = bb * e_rem
    ops_sc[6] = v
    ops_sc[7] = jnp.exp(ctot)

    r2 = lax.broadcasted_iota(jnp.int32, (LANES, LANES), 0)
    c2 = lax.broadcasted_iota(jnp.int32, (LANES, LANES), 1)
    rel = ((c2 & (CHUNK - 1)) - (r2 & (CHUNK - 1))) * sgn
    strict = rel < 0
    incl = rel <= 0
    eye = (r2 == c2).astype(F32)

    def stack(x):
        return jnp.concatenate([jnp.where(low, x, 0.0), jnp.where(low, 0.0, x)], axis=0)

    probs = []
    for ii in range(nch):
        off = (ii if drn == 0 else nch - 1 - ii) * CHUNK
        for pr in range(N_PAIRS):
            ls = slice(pr * LANES, (pr + 1) * LANES)
            q = {"off": off, "ls": ls, "pr": pr}
            for n, name in enumerate(("kks", "rs", "bs", "ks", "kps", "bps", "vs")):
                q[name] = stack(ops_sc[n, off:off + CHUNK, ls])
            q["wl"] = ops_sc[7, off:off + 1, ls]
            probs.append(q)
    for q in probs:
        g = _bdot_nt(jnp.concatenate([q["kks"], q["rs"]], axis=0), jnp.concatenate([q["bs"], q["ks"]], axis=0))
        q["nn"] = jnp.where(strict, -g[0:LANES, 0:LANES], 0.0)
        q["avk"] = jnp.where(strict, g[0:LANES, LANES:], 0.0)
        q["arb"] = jnp.where(incl, g[LANES:, 0:LANES], 0.0)
        q["ark"] = jnp.where(incl, g[LANES:, LANES:], 0.0)
        q["t"] = eye + q["nn"]
        q["q"] = q["nn"]
    for _ in range(5):
        for q in probs:
            q["q"] = _bdot(q["q"], q["q"])
        for q in probs:
            q["t"] = q["t"] + _bdot(q["t"], q["q"])
    for q in probs:
        q["tk"] = _bdot(q["t"], q["kks"])
        q["av"] = _bdot(q["avk"], q["vs"])
    for q in probs:
        q["tav"] = _bdot(q["t"], q["av"])
    for q in probs:
        q["mb"] = _bdot_tn(q["tk"], q["bps"])
        q["nm"] = _bdot_tn(q["vs"], q["kps"]) - _bdot_tn(q["tav"], q["bps"])
        q["r2"] = q["rs"] - _bdot(q["arb"], q["tk"])
        q["y0"] = _bdot(q["ark"], q["vs"]) - _bdot(q["arb"], q["tav"])

    st = [s_sc[pr] for pr in range(N_PAIRS)]
    outs = []
    for q in probs:
        sm = st[q["pr"]]
        ys = _bdot_nt(q["r2"], sm) + q["y0"]
        outs.append(ys[0:CHUNK] + ys[CHUNK:])
        st[q["pr"]] = sm * q["wl"] - _bdot(sm, q["mb"]) + q["nm"]
    for q, yo in zip(probs, outs):
        y_ref[q["off"]:q["off"] + CHUNK, q["ls"]] = yo
    for pr in range(N_PAIRS):
        s_sc[pr] = st[pr]

    @pl.when(j == nt - 1)
    def _():
        sout_ref[0] = s_sc[...]


def _rwkv(p, vf_src, hv, state_in, prm, nb, rows, drn):
    m = p.shape[0]
    tq = _row_tile(rows, 256)
    nt = rows // tq
    rb = tq // SUBLANES
    vres = vf_src is not None

    def tile(b, j):
        return b * nt + (j if drn == 0 else nt - 1 - j)

    const2 = lambda b, j: (0, 0)
    dir3 = lambda b, j: (drn, 0, 0)
    in_specs = [pl.BlockSpec((tq, RW_IN), lambda b, j: (tile(b, j), 0)),
                pl.BlockSpec((SUBLANES, RW_IN), lambda b, j: (jnp.maximum(tile(b, j) * rb - 1, 0), 0)),
                pl.BlockSpec((SUBLANES, RW_IN),
                             lambda b, j: (jnp.minimum((tile(b, j) + 1) * rb, m // SUBLANES - 1), 0))]
    args = [p, p, p]
    if vres:
        in_specs += [pl.BlockSpec((tq, RW_DIM), lambda b, j: (tile(b, j), 1)),
                     pl.BlockSpec((tq, LANES), lambda b, j: (tile(b, j), 0))]
        args += [vf_src, hv]
    in_specs += [pl.BlockSpec((2, RW_IN), const2),
                 pl.BlockSpec((1, SUBLANES, RW_DIM), dir3),
                 pl.BlockSpec((SUBLANES, RW_DIM), const2),
                 pl.BlockSpec((1, LANES, 2 * RW_DIM), dir3),
                 pl.BlockSpec((1, LANES, 2 * RW_DIM), dir3),
                 pl.BlockSpec((LANES, RW_DIM), const2)]
    args += [prm["tshift"], prm["pd"], prm["ps"], prm["wla_hi"], prm["wla_lo"], prm["wg"]]
    if vres:
        in_specs += [pl.BlockSpec((LANES, RW_DIM), const2), pl.BlockSpec((LANES, RW_DIM), const2)]
        args += [prm["vu_hi"], prm["vu_lo"]]
    sspec = pl.BlockSpec((1, N_PAIRS, LANES, LANES), lambda b, j: (b, 0, 0, 0))
    in_specs += [pl.BlockSpec((LANES, LANES), const2), sspec]
    args += [prm["e128"], state_in]
    return pl.pallas_call(
        functools.partial(_rwkv_kernel, nt=nt, tq=tq, vres=vres, drn=drn),
        grid=(nb, nt),
        in_specs=in_specs,
        out_specs=[pl.BlockSpec((tq, RW_DIM), lambda b, j: (tile(b, j), 0)),
                   pl.BlockSpec((tq, 2 * RW_DIM), lambda b, j: (tile(b, j), 0)),
                   sspec],
        out_shape=[jax.ShapeDtypeStruct((m, RW_DIM), F32),
                   jax.ShapeDtypeStruct((m, 2 * RW_DIM), F32),
                   jax.ShapeDtypeStruct((nb, N_PAIRS, LANES, LANES), F32)],
        scratch_shapes=[pltpu.VMEM((8, tq, RW_DIM), F32), pltpu.VMEM((N_PAIRS, LANES, LANES), F32)],
        compiler_params=_cparams("parallel", "arbitrary"),
        name="rwkv7_fwd" if drn == 0 else "rwkv7_bwd",
    )(*args)


def _rwkv_both(p, vf_src, hv, states, prm, nb, rows):
    outs = [_rwkv(p, vf_src, hv, states[d], prm, nb, rows, d) for d in range(2)]
    return [o[0] for o in outs], [o[1] for o in outs], [o[2] for o in outs]


def _merge_kernel(x_ref, y0_ref, y1_ref, a0_ref, a1_ref, ya_ref, g0_ref, g1_ref, g2_ref, g3_ref, ln_ref, e_ref,
                  wbr_ref, wba_ref, wo_ref, mod_ref, o_ref):
    e = e_ref[...]
    y = y0_ref[...] + y1_ref[...]
    mu = _segsum(y, e) * (1.0 / HEAD_DIM)
    yc = y - mu
    var = _segsum(yc * yc, e) * (1.0 / HEAD_DIM)
    yn = yc * lax.rsqrt(var + GN_EPS) * ln_ref[0:1, :] + ln_ref[1:2, :]
    cv = a0_ref[:, 0:RW_DIM] + a1_ref[:, 0:RW_DIM]
    yrw = (yn + cv) * a0_ref[:, RW_DIM:2 * RW_DIM]
    a = _bdot(yrw, wbr_ref[...])
    b = jnp.dot(ya_ref[...], wba_ref[...], preferred_element_type=F32)
    h = a.shape[1] // 2
    mrg = jnp.concatenate(
        [jax.nn.sigmoid(g0_ref[...]) * a[:, :h] + jax.nn.sigmoid(g2_ref[...]) * b[:, :h],
         jax.nn.sigmoid(g1_ref[...]) * a[:, h:] + jax.nn.sigmoid(g3_ref[...]) * b[:, h:]], axis=1)
    o_ref[...] = x_ref[...] + mod_ref[0, 2:3, :] * _bdot(mrg, wo_ref[...])


def _merge(x2, ys, auxs, ya, p, ln, e128, wbr, wba, wo, mod, rows):
    m, d = x2.shape
    tm = _row_tile(rows, 512)
    per = rows // tm
    many = mod.shape[0] > 1
    gb = GATE_OFF // 512
    const2 = lambda i: (0, 0)
    in_specs = [pl.BlockSpec((tm, d), lambda i: (i, 0)),
                pl.BlockSpec((tm, RW_DIM), lambda i: (i, 0)),
                pl.BlockSpec((tm, RW_DIM), lambda i: (i, 0)),
                pl.BlockSpec((tm, 2 * RW_DIM), lambda i: (i, 0)),
                pl.BlockSpec((tm, 2 * RW_DIM), lambda i: (i, 0)),
                pl.BlockSpec((tm, ATT_DIM), lambda i: (i, 0))]
    in_specs += [pl.BlockSpec((tm, 512), (lambda i, c=c: (i, gb + c))) for c in range(4)]
    in_specs += [pl.BlockSpec((2, RW_DIM), const2),
                 pl.BlockSpec((LANES, LANES), const2),
                 pl.BlockSpec(wbr.shape, const2),
                 pl.BlockSpec(wba.shape, const2),
                 pl.BlockSpec(wo.shape, const2),
                 pl.BlockSpec((1, 6, d), lambda i: ((i // per) if many else 0, 0, 0))]
    return pl.pallas_call(
        _merge_kernel,
        grid=(m // tm,),
        in_specs=in_specs,
        out_specs=pl.BlockSpec((tm, d), lambda i: (i, 0)),
        out_shape=jax.ShapeDtypeStruct((m, d), F32),
        compiler_params=_cparams("parallel"),
        name="branch_merge",
    )(x2, ys[0], ys[1], auxs[0], auxs[1], ya, p, p, p, p, ln, e128, wbr, wba, wo, mod)


def _ffn_kernel(x_ref, g_ref, mod_ref, wg_ref, wu_ref, wd_ref, o_ref, h_sc, acc_sc, *, nf):
    f = pl.program_id(1)

    @pl.when(f == 0)
    def _():
        h = _norm_mod(x_ref[...], g_ref[...], mod_ref[0, 3:4, :], mod_ref[0, 4:5, :])
        h_sc[...] = h.astype(BF16)
        acc_sc[...] = jnp.zeros_like(acc_sc)

    h = h_sc[...]
    a = jnp.dot(h, wg_ref[...], preferred_element_type=F32)
    u = jnp.dot(h, wu_ref[...], preferred_element_type=F32)
    act = (a * jax.nn.sigmoid(a)) * u
    acc_sc[...] += jnp.dot(act.astype(BF16), wd_ref[...], preferred_element_type=F32)

    @pl.when(f == nf - 1)
    def _():
        o_ref[...] = x_ref[...] + mod_ref[0, 5:6, :] * acc_sc[...]


def _ffn(x2, gain, mod, wg, wu, wd, rows):
    m, d = x2.shape
    dff = wg.shape[1]
    tm = _row_tile(rows, 512)
    per = rows // tm
    many = mod.shape[0] > 1
    tf = dff // 2 if (dff // 2) % LANES == 0 else dff
    nf = dff // tf
    return pl.pallas_call(
        functools.partial(_ffn_kernel, nf=nf),
        grid=(m // tm, nf),
        in_specs=[pl.BlockSpec((tm, d), lambda i, f: (i, 0)),
                  pl.BlockSpec((1, d), lambda i, f: (0, 0)),
                  pl.BlockSpec((1, 6, d), lambda i, f: ((i // per) if many else 0, 0, 0)),
                  pl.BlockSpec((d, tf), lambda i, f: (0, f)),
                  pl.BlockSpec((d, tf), lambda i, f: (0, f)),
                  pl.BlockSpec((tf, d), lambda i, f: (f, 0))],
        out_specs=pl.BlockSpec((tm, d), lambda i, f: (i, 0)),
        out_shape=jax.ShapeDtypeStruct((m, d), F32),
        scratch_shapes=[pltpu.VMEM((tm, d), BF16), pltpu.VMEM((tm, d), F32)],
        compiler_params=_cparams("parallel", "arbitrary"),
        name="ffn_swiglu",
    )(x2, gain.reshape(1, d), mod, wg, wu, wd)


def _moe_kernel(x_ref, g_ref, mod_ref, rt_ref, wg_ref, wu_ref, wd_ref, o_ref, h_sc, acc_sc, we_sc, *, ne, nf):
    ex = pl.program_id(1)
    f = pl.program_id(2)
    lane = lax.broadcasted_iota(jnp.int32, we_sc.shape, 1)

    @pl.when((ex == 0) & (f == 0))
    def _():
        h = _norm_mod(x_ref[...], g_ref[...], mod_ref[0, 3:4, :], mod_ref[0, 4:5, :])
        h_sc[...] = h.astype(BF16)
        acc_sc[...] = jnp.zeros_like(acc_sc)
        logits = jnp.dot(h, rt_ref[...], precision=HIGHEST, preferred_element_type=F32)
        lg = jnp.where(lane < ne, logits, NEG_INF)
        t1 = jnp.max(lg, axis=1, keepdims=True)
        i1 = jnp.min(jnp.where(lg == t1, lane, LANES), axis=1, keepdims=True)
        lg2 = jnp.where(lane == i1, NEG_INF, lg)
        t2 = jnp.max(lg2, axis=1, keepdims=True)
        i2 = jnp.min(jnp.where(lg2 == t2, lane, LANES), axis=1, keepdims=True)
        e2 = jnp.exp(t2 - t1)
        den = 1.0 + e2
        we_sc[...] = jnp.where(lane == i1, 1.0 / den, 0.0) + jnp.where(lane == i2, e2 / den, 0.0)

    h = h_sc[...]
    a = jnp.dot(h, wg_ref[0], preferred_element_type=F32)
    u = jnp.dot(h, wu_ref[0], preferred_element_type=F32)
    act = (a * jax.nn.sigmoid(a)) * u
    w_e = jnp.sum(jnp.where(lane == ex, we_sc[...], 0.0), axis=1, keepdims=True)
    acc_sc[...] += w_e * jnp.dot(act.astype(BF16), wd_ref[0], preferred_element_type=F32)

    @pl.when((ex == ne - 1) & (f == nf - 1))
    def _():
        o_ref[...] = x_ref[...] + mod_ref[0, 5:6, :] * acc_sc[...]


def _moe(x2, gain, mod, router_pad, wg, wu, wd, rows):
    m, d = x2.shape
    ne, _, dff = wg.shape
    tm = _row_tile(rows, 512)
    per = rows // tm
    many = mod.shape[0] > 1
    tf = dff // 2 if (dff // 2) % LANES == 0 else dff
    nf = dff // tf
    return pl.pallas_call(
        functools.partial(_moe_kernel, ne=ne, nf=nf),
        grid=(m // tm, ne, nf),
        in_specs=[pl.BlockSpec((tm, d), lambda i, e, f: (i, 0)),
                  pl.BlockSpec((1, d), lambda i, e, f: (0, 0)),
                  pl.BlockSpec((1, 6, d), lambda i, e, f: ((i // per) if many else 0, 0, 0)),
                  pl.BlockSpec((d, LANES), lambda i, e, f: (0, 0)),
                  pl.BlockSpec((1, d, tf), lambda i, e, f: (e, 0, f)),
                  pl.BlockSpec((1, d, tf), lambda i, e, f: (e, 0, f)),
                  pl.BlockSpec((1, tf, d), lambda i, e, f: (e, f, 0))],
        out_specs=pl.BlockSpec((tm, d), lambda i, e, f: (i, 0)),
        out_shape=jax.ShapeDtypeStruct((m, d), F32),
        scratch_shapes=[pltpu.VMEM((tm, d), BF16), pltpu.VMEM((tm, d), F32), pltpu.VMEM((tm, LANES), F32)],
        compiler_params=_cparams("parallel", "arbitrary", "arbitrary"),
        name="moe_swiglu",
    )(x2, gain.reshape(1, d), mod, router_pad, wg, wu, wd)


def _rope_table(t):
    rows = t // GRID_W
    row = jnp.repeat(jnp.arange(rows, dtype=F32), GRID_W)
    col = jnp.tile(jnp.arange(GRID_W, dtype=F32), rows)
    half = HEAD_DIM // 2
    inv_freq = ROPE_THETA ** (-jnp.arange(0, half, 2, dtype=F32) / half)
    ar = row[:, None] * inv_freq
    ac = col[:, None] * inv_freq
    ang = jnp.concatenate([ar, ar, ac, ac] * 2, axis=-1)
    return jnp.stack([jnp.cos(ang), jnp.sin(ang)])


def _pad_rows(a, rows):
    return jnp.zeros((rows,) + a.shape[1:], a.dtype).at[:a.shape[0]].set(a)


def _hi_lo(w):
    hi = w.astype(BF16)
    return hi, (w - hi.astype(F32)).astype(BF16)


def kernel(x, c, ctx, c_ctx, norm1, norm2, w_mod, b_mod, w_in, tshift, w0, w_decay_up, a0, w_iclr_up, v0, w_vres_down, w_vres_up, w_gate_up, k_k, k_a, r_k, lnx_g, lnx_b, q_norm, k_norm, sink, w_br_rwkv, w_br_attn, w_out, ffn_gate, ffn_up, ffn_down, router, moe_gate, moe_up, moe_down):
    nb, t, d = x.shape
    cn = ctx.shape[1]
    depth = w_in.shape[0]
    mods = _modulation(c, c_ctx, w_mod, b_mod)
    cs = _rope_table(t)
    e128 = jnp.kron(jnp.eye(2, dtype=F32), jnp.ones((HEAD_DIM, HEAD_DIM), F32)).astype(BF16)
    xl = x.reshape(nb * t, d)
    xc = ctx.reshape(nb * cn, d)
    vf_l = vf_c = None
    for layer in range(depth):
        last = layer == depth - 1
        mod_l = mods[layer, :nb].reshape(nb, 6, d)
        mod_c = mods[layer, nb:nb + 1].reshape(1, 6, d)
        w_in_b = w_in[layer].astype(BF16)
        vd_b = None
        prm = {"tshift": tshift[layer], "e128": e128, "wg": w_gate_up[layer].astype(BF16)}
        prm["pd"] = jnp.stack(
            [_pad_rows(jnp.stack([w0[layer, dd], a0[layer, dd]]), SUBLANES) for dd in range(2)])
        shared = [k_k[layer], k_a[layer], r_k[layer].reshape(RW_DIM)]
        if layer > 0:
            shared.append(v0[layer - 1])
            vd_b = _pad_rows(w_vres_down[layer - 1].T, LANES).T.astype(BF16)
            prm["vu_hi"], prm["vu_lo"] = _hi_lo(_pad_rows(w_vres_up[layer - 1], LANES))
        prm["ps"] = _pad_rows(jnp.stack(shared), SUBLANES)
        wla = jnp.zeros((2, LANES, 2 * RW_DIM), F32)
        wla = wla.at[:, :HEAD_DIM, :RW_DIM].set(w_decay_up[layer]).at[:, HEAD_DIM:, RW_DIM:].set(w_iclr_up[layer])
        prm["wla_hi"], prm["wla_lo"] = _hi_lo(wla)

        p_l, hv_l = _inproj(xl, norm1[layer], mod_l, w_in_b, vd_b, t)
        p_c, hv_c = _inproj(xc, norm1[layer], mod_c, w_in_b, vd_b, cn)

        s0 = jnp.zeros((nb, N_PAIRS, LANES, LANES), F32)
        y_c, aux_c, s_ctx = _rwkv_both(p_c, vf_c, hv_c, [s0, s0], prm, nb, cn)
        y_l, aux_l, _ = _rwkv_both(p_l, vf_l, hv_l, s_ctx, prm, nb, t)
        if layer == 0:
            vf_l, vf_c = aux_l[1], aux_c[1]

        qg = jnp.tile(q_norm[layer], 2).reshape(1, LANES)
        kg = jnp.tile(k_norm[layer], 2).reshape(1, LANES)
        qn_l, kn_l, vb_l = _qk_prep(p_l, cs, qg, kg, e128, t, True)
        qn_c, kn_c, vb_c = _qk_prep(p_c, None, qg, kg, e128, cn, False)
        sk = sink[layer].reshape(1, -1)
        ya_l = _attention(qn_l, kn_l, vb_l, kn_c, vb_c, sk, nb, t, cn, True)

        ln = jnp.stack([lnx_g[layer], lnx_b[layer]])
        wbr = w_br_rwkv[layer].astype(BF16)
        wba = w_br_attn[layer].astype(BF16)
        wo = w_out[layer].astype(BF16)
        xl = _merge(xl, y_l, aux_l, ya_l, p_l, ln, e128, wbr, wba, wo, mod_l, t)
        if not last:
            ya_c = _attention(qn_c, None, None, kn_c, vb_c, sk, nb, cn, cn, False)
            xc = _merge(xc, y_c, aux_c, ya_c, p_c, ln, e128, wbr, wba, wo, mod_c, cn)

        i = layer // 2
        if layer % 2 == 0:
            fw = (ffn_gate[i].astype(BF16), ffn_up[i].astype(BF16), ffn_down[i].astype(BF16))
            xl = _ffn(xl, norm2[layer], mod_l, *fw, t)
            if not last:
                xc = _ffn(xc, norm2[layer], mod_c, *fw, cn)
        else:
            rt = _pad_rows(router[i].T, LANES).T
            mw = (moe_gate[i].astype(BF16), moe_up[i].astype(BF16), moe_down[i].astype(BF16))
            xl = _moe(xl, norm2[layer], mod_l, rt, *mw, t)
            if not last:
                xc = _moe(xc, norm2[layer], mod_c, rt, *mw, cn)
    return xl.reshape(nb, t, d)
```

```python
import functools

import jax
import jax.numpy as jnp
from jax import lax
from jax.experimental import pallas as pl
from jax.experimental.pallas import tpu as pltpu

F32 = jnp.float32
BF16 = jnp.bfloat16
HIGHEST = lax.Precision.HIGHEST

LANES = 128
SUBLANES = 8
HEAD_DIM = 64
RW_DIM = 512
ATT_DIM = 512
KV_DIM = 128
RW_IN = 1792
ATT_OFF = RW_IN
GATE_OFF = RW_IN + ATT_DIM + 2 * KV_DIM
N_EXPERTS = 8
CHUNK = 64
N_PAIRS = RW_DIM // LANES
MOE_TILE = 1024
MOE_BLK = 128
ATT_SCALE = HEAD_DIM ** -0.5
ROPE_THETA = 10000.0
GRID_W = 64
RMS_EPS = 1e-6
GN_EPS = 64e-5
NEG_INF = -1e30
DECAY_SCALE = 0.6065306597126334
VMEM_LIMIT = 48 * 1024 * 1024


def _cparams(*sem):
    return pltpu.CompilerParams(dimension_semantics=sem, vmem_limit_bytes=VMEM_LIMIT)


def _row_tile(rows, cap):
    t = cap
    while rows % t:
        t //= 2
    return t


def _bdot(a, b):
    return jnp.dot(a.astype(BF16), b.astype(BF16), preferred_element_type=F32)


def _bdot_nt(a, b):
    return lax.dot_general(a.astype(BF16), b.astype(BF16), (((1,), (1,)), ((), ())),
                           preferred_element_type=F32)


def _bdot_tn(a, b):
    return lax.dot_general(a.astype(BF16), b.astype(BF16), (((0,), (0,)), ((), ())),
                           preferred_element_type=F32)


def _split2(x):
    hi = x.astype(BF16)
    lo = (x - hi.astype(F32)).astype(BF16)
    return hi, lo


def _split3(x):
    hi = x.astype(BF16)
    r = x - hi.astype(F32)
    mid = r.astype(BF16)
    lo = (r - mid.astype(F32)).astype(BF16)
    return hi, mid, lo


def _dot3(a, w_hi, w_lo):
    a_hi, a_lo = _split2(a)
    return (jnp.dot(a_hi, w_hi, preferred_element_type=F32)
            + (jnp.dot(a_lo, w_hi, preferred_element_type=F32)
               + jnp.dot(a_hi, w_lo, preferred_element_type=F32)))


def _segsum(x, e):
    outs = []
    for s in range(x.shape[1] // LANES):
        hi, lo = _split2(x[:, s * LANES:(s + 1) * LANES])
        outs.append(jnp.dot(hi, e, preferred_element_type=F32)
                    + jnp.dot(lo, e, preferred_element_type=F32))
    return outs[0] if len(outs) == 1 else jnp.concatenate(outs, axis=1)


def _norm_mod(x, gain, shift, scale):
    y = x * lax.rsqrt(jnp.mean(x * x, axis=-1, keepdims=True) + RMS_EPS) * gain
    return y * (1.0 + scale) + shift


def _mod_kernel(c_ref, w_ref, b_ref, o_ref):
    c = c_ref[...]
    s = c * jax.nn.sigmoid(c)
    o_ref[0] = jnp.dot(s, w_ref[0], precision=HIGHEST, preferred_element_type=F32) + b_ref[0]


def _modulation(c, c_ctx, w_mod, b_mod):
    depth, d, n6 = w_mod.shape
    nb = c.shape[0]
    rm = -(-(nb + 1) // SUBLANES) * SUBLANES
    cc = jnp.zeros((rm, d), F32).at[:nb].set(c).at[nb].set(c_ctx)
    tn = _row_tile(n6, 1536)
    return pl.pallas_call(
        _mod_kernel,
        grid=(depth, n6 // tn),
        in_specs=[pl.BlockSpec((rm, d), lambda l, n: (0, 0)),
                  pl.BlockSpec((1, d, tn), lambda l, n: (l, 0, n)),
                  pl.BlockSpec((1, 1, tn), lambda l, n: (l, 0, n))],
        out_specs=pl.BlockSpec((1, rm, tn), lambda l, n: (l, 0, n)),
        out_shape=jax.ShapeDtypeStruct((depth, rm, n6), F32),
        compiler_params=_cparams("parallel", "parallel"),
        name="adaln_mod",
    )(cc, w_mod, b_mod.reshape(depth, 1, n6))


def _inproj_kernel(*refs, vres):
    if vres:
        x_ref, g_ref, mod_ref, w_ref, vd_ref, o_ref, hv_ref, h_sc = refs
    else:
        x_ref, g_ref, mod_ref, w_ref, o_ref, h_sc = refs

    @pl.when(pl.program_id(1) == 0)
    def _():
        h = _norm_mod(x_ref[...], g_ref[...], mod_ref[0, 0:1, :], mod_ref[0, 1:2, :])
        hb = h.astype(BF16)
        h_sc[...] = hb
        if vres:
            hv_ref[...] = jnp.dot(hb, vd_ref[...], preferred_element_type=F32)

    o_ref[...] = jnp.dot(h_sc[...], w_ref[...], preferred_element_type=F32)


def _inproj(x2, gain, mod, w_b, vd_b, rows):
    m, d = x2.shape
    n = w_b.shape[1]
    tm = _row_tile(rows, 512)
    tn = _row_tile(n, 1536)
    per = rows // tm
    many = mod.shape[0] > 1
    vres = vd_b is not None
    in_specs = [pl.BlockSpec((tm, d), lambda i, j: (i, 0)),
                pl.BlockSpec((1, d), lambda i, j: (0, 0)),
                pl.BlockSpec((1, 6, d), lambda i, j: ((i // per) if many else 0, 0, 0)),
                pl.BlockSpec((d, tn), lambda i, j: (0, j))]
    args = [x2, gain.reshape(1, d), mod, w_b]
    out_specs = [pl.BlockSpec((tm, tn), lambda i, j: (i, j))]
    out_shape = [jax.ShapeDtypeStruct((m, n), F32)]
    if vres:
        in_specs.append(pl.BlockSpec((d, LANES), lambda i, j: (0, 0)))
        args.append(vd_b)
        out_specs.append(pl.BlockSpec((tm, LANES), lambda i, j: (i, 0)))
        out_shape.append(jax.ShapeDtypeStruct((m, LANES), F32))
    outs = pl.pallas_call(
        functools.partial(_inproj_kernel, vres=vres),
        grid=(m // tm, n // tn),
        in_specs=in_specs, out_specs=out_specs, out_shape=out_shape,
        scratch_shapes=[pltpu.VMEM((tm, d), BF16)],
        compiler_params=_cparams("parallel", "arbitrary"),
        name="in_proj",
    )(*args)
    return (outs[0], outs[1]) if vres else (outs[0], None)


def _qkprep_kernel(*refs, rope):
    if rope:
        q0_ref, q1_ref, k_ref, v_ref, cs_ref, qg_ref, kg_ref, e_ref, qn_ref, kn_ref, vb_ref = refs
    else:
        q0_ref, q1_ref, k_ref, v_ref, qg_ref, kg_ref, e_ref, qn_ref, kn_ref, vb_ref = refs
    e = e_ref[...]
    lane = lax.broadcasted_iota(jnp.int32, (1, LANES), 1)
    first = (lane & 31) < 16

    def norm_rope(u, gain):
        y = u * lax.rsqrt(_segsum(u * u, e) * (1.0 / HEAD_DIM) + RMS_EPS) * gain
        if rope:
            rot = jnp.where(first, -pltpu.roll(y, LANES - 16, 1), pltpu.roll(y, 16, 1))
            y = y * cs_ref[0] + rot * cs_ref[1]
        return y

    qg = qg_ref[...]
    for s, ref in enumerate((q0_ref, q1_ref)):
        for t in range(2):
            u = ref[:, t * LANES:(t + 1) * LANES]
            c0 = (2 * s + t) * LANES
            qn_ref[:, c0:c0 + LANES] = (norm_rope(u, qg) * ATT_SCALE).astype(BF16)
    kn_ref[...] = norm_rope(k_ref[...], kg_ref[...]).astype(BF16)
    vb_ref[...] = v_ref[...].astype(BF16)


def _qk_prep(p, cs, qg, kg, e128, rows, rope):
    m = p.shape[0]
    tm = _row_tile(rows, 512)
    per = rows // tm
    qb = ATT_OFF // 256
    kb = (ATT_OFF + ATT_DIM) // LANES
    in_specs = [pl.BlockSpec((tm, 256), lambda i: (i, qb)),
                pl.BlockSpec((tm, 256), lambda i: (i, qb + 1)),
                pl.BlockSpec((tm, LANES), lambda i: (i, kb)),
                pl.BlockSpec((tm, LANES), lambda i: (i, kb + 1))]
    args = [p, p, p, p]
    if rope:
        in_specs.append(pl.BlockSpec((2, tm, LANES), lambda i: (0, i % per, 0)))
        args.append(cs)
    in_specs += [pl.BlockSpec((1, LANES), lambda i: (0, 0)),
                 pl.BlockSpec((1, LANES), lambda i: (0, 0)),
                 pl.BlockSpec((LANES, LANES), lambda i: (0, 0))]
    args += [qg, kg, e128]
    return pl.pallas_call(
        functools.partial(_qkprep_kernel, rope=rope),
        grid=(m // tm,),
        in_specs=in_specs,
        out_specs=[pl.BlockSpec((tm, ATT_DIM), lambda i: (i, 0)),
                   pl.BlockSpec((tm, LANES), lambda i: (i, 0)),
                   pl.BlockSpec((tm, LANES), lambda i: (i, 0))],
        out_shape=[jax.ShapeDtypeStruct((m, ATT_DIM), BF16),
                   jax.ShapeDtypeStruct((m, KV_DIM), BF16),
                   jax.ShapeDtypeStruct((m, KV_DIM), BF16)],
        compiler_params=_cparams("parallel"),
        name="qk_prep",
    )(*args)


def _attn_kernel(*refs, nq, local):
    if local:
        sink_ref, q_ref, kp_ref, kc_ref, kn_ref, vp_ref, vc_ref, vn_ref, kx_ref, vx_ref, o_ref = refs
        k = jnp.concatenate([kp_ref[...], kc_ref[...], kn_ref[...], kx_ref[...]], axis=0).astype(F32)
        v = jnp.concatenate([vp_ref[...], vc_ref[...], vn_ref[...], vx_ref[...]], axis=0).astype(F32)
    else:
        sink_ref, q_ref, kx_ref, vx_ref, o_ref = refs
        k = kx_ref[...].astype(F32)
        v = vx_ref[...].astype(F32)
    i = pl.program_id(1)
    bq = q_ref.shape[0]
    low = lax.broadcasted_iota(jnp.int32, (1, LANES), 1) < HEAD_DIM
    ksw = pltpu.roll(k, HEAD_DIM, 1)
    vsw = pltpu.roll(v, HEAD_DIM, 1)
    row = lax.broadcasted_iota(jnp.int32, (4 * bq, 1), 0)
    if local:
        qo = lax.broadcasted_iota(jnp.int32, (4 * bq, 3 * bq), 0) & (bq - 1)
        kcol = lax.broadcasted_iota(jnp.int32, (4 * bq, 3 * bq), 1)
        ko = kcol & (bq - 1)
        kblk = kcol // bq
        ok = ((kblk == 1)
              | ((kblk == 0) & (ko >= qo) & (i > 0))
              | ((kblk == 2) & (ko <= qo) & (i < nq - 1)))
    for g in range(2):
        kb = (jnp.where(low, k, ksw) if g == 0 else jnp.where(low, ksw, k)).astype(BF16)
        vb = (jnp.where(low, v, vsw) if g == 0 else jnp.where(low, vsw, v)).astype(BF16)
        qa = q_ref[:, (2 * g) * LANES:(2 * g + 1) * LANES].astype(F32)
        qb = q_ref[:, (2 * g + 1) * LANES:(2 * g + 2) * LANES].astype(F32)
        qs = jnp.concatenate([jnp.where(low, qa, 0.0), jnp.where(low, 0.0, qa),
                              jnp.where(low, qb, 0.0), jnp.where(low, 0.0, qb)], axis=0).astype(BF16)
        s = lax.dot_general(qs, kb, (((1,), (1,)), ((), ())), preferred_element_type=F32)
        if local:
            s = jnp.concatenate([jnp.where(ok, s[:, :3 * bq], NEG_INF), s[:, 3 * bq:]], axis=1)
        sk = jnp.where(row < bq, sink_ref[0, 4 * g],
                       jnp.where(row < 2 * bq, sink_ref[0, 4 * g + 1],
                                 jnp.where(row < 3 * bq, sink_ref[0, 4 * g + 2], sink_ref[0, 4 * g + 3])))
        mx = jnp.maximum(jnp.max(s, axis=1, keepdims=True), sk)
        ex = jnp.exp(s - mx)
        den = jnp.sum(ex, axis=1, keepdims=True) + jnp.exp(sk - mx)
        pr = (ex / den).astype(BF16)
        o = jnp.dot(pr, vb, preferred_element_type=F32)
        o_ref[:, (2 * g) * LANES:(2 * g + 1) * LANES] = jnp.where(low, o[0:bq], o[bq:2 * bq]).astype(o_ref.dtype)
        o_ref[:, (2 * g + 1) * LANES:(2 * g + 2) * LANES] = jnp.where(
            low, o[2 * bq:3 * bq], o[3 * bq:4 * bq]).astype(o_ref.dtype)


def _attention(qn, kn, vb, kctx, vctx, sink, nb, rows, cn, local):
    m = qn.shape[0]
    bq = 128
    nq = rows // bq
    smem = pl.BlockSpec(memory_space=pltpu.SMEM)
    qspec = pl.BlockSpec((bq, ATT_DIM), lambda b, i: (b * nq + i, 0))
    cspec = pl.BlockSpec((cn, KV_DIM), lambda b, i: (b, 0))
    if local:
        prv = pl.BlockSpec((bq, KV_DIM), lambda b, i: (b * nq + jnp.maximum(i - 1, 0), 0))
        cur = pl.BlockSpec((bq, KV_DIM), lambda b, i: (b * nq + i, 0))
        nxt = pl.BlockSpec((bq, KV_DIM), lambda b, i: (b * nq + jnp.minimum(i + 1, nq - 1), 0))
        in_specs = [smem, qspec, prv, cur, nxt, prv, cur, nxt, cspec, cspec]
        args = [sink, qn, kn, kn, kn, vb, vb, vb, kctx, vctx]
    else:
        in_specs = [smem, qspec, cspec, cspec]
        args = [sink, qn, kctx, vctx]
    return pl.pallas_call(
        functools.partial(_attn_kernel, nq=nq, local=local),
        grid=(nb, nq),
        in_specs=in_specs,
        out_specs=pl.BlockSpec((bq, ATT_DIM), lambda b, i: (b * nq + i, 0)),
        out_shape=jax.ShapeDtypeStruct((m, ATT_DIM), BF16),
        compiler_params=_cparams("parallel", "parallel"),
        name="window_attn" if local else "ctx_attn",
    )(*args)


def _rwkv_kernel(*refs, nt, tq, vres, drn):
    it = iter(refs)
    p_ref, pv_ref, nx_ref = next(it), next(it), next(it)
    if vres:
        vf_ref, hv_ref = next(it), next(it)
    ts_ref, pd_ref, ps_ref, wlh_ref, wll_ref, wg_ref = (next(it) for _ in range(6))
    if vres:
        vuh_ref, vul_ref = next(it), next(it)
    e_ref, sin_ref = next(it), next(it)
    y_ref, aux_ref, sout_ref = next(it), next(it), next(it)
    ops_sc, s_sc = next(it), next(it)

    j = pl.program_id(1)
    sgn = 1 - 2 * drn
    tile = j if drn == 0 else nt - 1 - j
    nch = tq // CHUNK

    @pl.when(j == 0)
    def _():
        s_sc[...] = sin_ref[0]

    rw = p_ref[...]
    row = lax.broadcasted_iota(jnp.int32, (tq, 1), 0)
    prev_row = jnp.where(tile > 0, pv_ref[SUBLANES - 1:SUBLANES, :], 0.0)
    next_row = jnp.where(tile < nt - 1, nx_ref[0:1, :], 0.0)
    prv = jnp.where(row == 0, prev_row, pltpu.roll(rw, 1, 0))
    nxt = jnp.where(row == tq - 1, next_row, pltpu.roll(rw, tq - 1, 0))
    xs = rw + ts_ref[0:1, :] * (prv - rw) + ts_ref[1:2, :] * (nxt - rw)

    e = e_ref[...]
    r = xs[:, 0:RW_DIM]
    k = xs[:, RW_DIM:2 * RW_DIM]
    v = xs[:, 2 * RW_DIM:3 * RW_DIM]
    la = xs[:, 3 * RW_DIM:3 * RW_DIM + LANES]
    lane = lax.broadcasted_iota(jnp.int32, (1, LANES), 1)
    low = lane < HEAD_DIM
    la = jnp.where(low, jnp.tanh(la), la)
    lora = _dot3(la, wlh_ref[0], wll_ref[0])
    logw = -DECAY_SCALE * jax.nn.sigmoid(pd_ref[0, 0:1, :] + lora[:, 0:RW_DIM])
    iclr = jax.nn.sigmoid(pd_ref[0, 1:2, :] + lora[:, RW_DIM:2 * RW_DIM])
    if vres:
        mix = jax.nn.sigmoid(ps_ref[3:4, :] + _dot3(hv_ref[...], vuh_ref[...], vul_ref[...]))
        v = v + (vf_ref[...] - v) * mix
    kk = k * ps_ref[0:1, :]
    kk = kk * lax.rsqrt(_segsum(kk * kk, e) + 1e-12)
    kh = k * (1.0 + (iclr - 1.0) * ps_ref[1:2, :])
    bb = kk * iclr
    aux_ref[:, 0:RW_DIM] = _segsum(r * kh * ps_ref[2:3, :], e) * v
    if drn == 0:
        glo = xs[:, 3 * RW_DIM + LANES:RW_IN]
        aux_ref[:, RW_DIM:2 * RW_DIM] = _bdot(jax.nn.sigmoid(glo), wg_ref[...])
    else:
        aux_ref[:, RW_DIM:2 * RW_DIM] = v

    ti = lax.broadcasted_iota(jnp.int32, (tq, tq), 0)
    si = lax.broadcasted_iota(jnp.int32, (tq, tq), 1)
    same = (ti // CHUNK) == (si // CHUNK)
    tri = (same & (((si - ti) * sgn) <= 0)).astype(BF16)
    ones = same.astype(BF16)
    cin = jnp.zeros_like(logw)
    ctot = jnp.zeros_like(logw)
    for part in _split3(logw):
        cin = cin + jnp.dot(tri, part, preferred_element_type=F32)
        ctot = ctot + jnp.dot(ones, part, preferred_element_type=F32)
    e_neg = jnp.exp(-cin)
    e_rem = jnp.exp(ctot - cin)
    ops_sc[0] = kk * jnp.exp(cin - logw)
    ops_sc[1] = r * jnp.exp(cin)
    ops_sc[2] = bb * e_neg
    ops_sc[3] = kh * e_neg
    ops_sc[4] = kh * e_rem
    ops_sc[5] = bb * e_rem
    ops_sc[6] = v
    ops_sc[7] = jnp.exp(ctot)

    r2 = lax.broadcasted_iota(jnp.int32, (LANES, LANES), 0)
    c2 = lax.broadcasted_iota(jnp.int32, (LANES, LANES), 1)
    rel = ((c2 & (CHUNK - 1)) - (r2 & (CHUNK - 1))) * sgn
    strict = rel < 0
    incl = rel <= 0
    eye = (r2 == c2).astype(F32)

    def stack(x):
        return jnp.concatenate([jnp.where(low, x, 0.0), jnp.where(low, 0.0, x)], axis=0)

    probs = []
    for ii in range(nch):
        off = (ii if drn == 0 else nch - 1 - ii) * CHUNK
        for pr in range(N_PAIRS):
            ls = slice(pr * LANES, (pr + 1) * LANES)
            q = {"off": off, "ls": ls, "pr": pr}
            for n, name in enumerate(("kks", "rs", "bs", "ks", "kps", "bps", "vs")):
                q[name] = stack(ops_sc[n, off:off + CHUNK, ls])
            q["wl"] = ops_sc[7, off:off + 1, ls]
            probs.append(q)
    for q in probs:
        g = _bdot_nt(jnp.concatenate([q["kks"], q["rs"]], axis=0), jnp.concatenate([q["bs"], q["ks"]], axis=0))
        q["nn"] = jnp.where(strict, -g[0:LANES, 0:LANES], 0.0)
        q["avk"] = jnp.where(strict, g[0:LANES, LANES:], 0.0)
        q["arb"] = jnp.where(incl, g[LANES:, 0:LANES], 0.0)
        q["ark"] = jnp.where(incl, g[LANES:, LANES:], 0.0)
        q["t"] = eye + q["nn"]
        q["q"] = q["nn"]
    for _ in range(5):
        for q in probs:
            q["q"] = _bdot(q["q"], q["q"])
        for q in probs:
            q["t"] = q["t"] + _bdot(q["t"], q["q"])
    for q in probs:
        q["tk"] = _bdot(q["t"], q["kks"])
        q["av"] = _bdot(q["avk"], q["vs"])
    for q in probs:
        q["tav"] = _bdot(q["t"], q["av"])
    for q in probs:
        q["mb"] = _bdot_tn(q["tk"], q["bps"])
        q["nm"] = _bdot_tn(q["vs"], q["kps"]) - _bdot_tn(q["tav"], q["bps"])
        q["r2"] = q["rs"] - _bdot(q["arb"], q["tk"])
        q["y0"] = _bdot(q["ark"], q["vs"]) - _bdot(q["arb"], q["tav"])

    st = [s_sc[pr] for pr in range(N_PAIRS)]
    outs = []
    for q in probs:
        sm = st[q["pr"]]
        ys = _bdot_nt(q["r2"], sm) + q["y0"]
        outs.append(ys[0:CHUNK] + ys[CHUNK:])
        st[q["pr"]] = sm * q["wl"] - _bdot(sm, q["mb"]) + q["nm"]
    for q, yo in zip(probs, outs):
        y_ref[q["off"]:q["off"] + CHUNK, q["ls"]] = yo
    for pr in range(N_PAIRS):
        s_sc[pr] = st[pr]

    @pl.when(j == nt - 1)
    def _():
        sout_ref[0] = s_sc[...]


def _rwkv(p, vf_src, hv, state_in, prm, nb, rows, drn):
    m = p.shape[0]
    tq = _row_tile(rows, 256)
    nt = rows // tq
    rb = tq // SUBLANES
    vres = vf_src is not None

    def tile(b, j):
        return b * nt + (j if drn == 0 else nt - 1 - j)

    const2 = lambda b, j: (0, 0)
    dir3 = lambda b, j: (drn, 0, 0)
    in_specs = [pl.BlockSpec((tq, RW_IN), lambda b, j: (tile(b, j), 0)),
                pl.BlockSpec((SUBLANES, RW_IN), lambda b, j: (jnp.maximum(tile(b, j) * rb - 1, 0), 0)),
                pl.BlockSpec((SUBLANES, RW_IN),
                             lambda b, j: (jnp.minimum((tile(b, j) + 1) * rb, m // SUBLANES - 1), 0))]
    args = [p, p, p]
    if vres:
        in_specs += [pl.BlockSpec((tq, RW_DIM), lambda b, j: (tile(b, j), 1)),
                     pl.BlockSpec((tq, LANES), lambda b, j: (tile(b, j), 0))]
        args += [vf_src, hv]
    in_specs += [pl.BlockSpec((2, RW_IN), const2),
                 pl.BlockSpec((1, SUBLANES, RW_DIM), dir3),
                 pl.BlockSpec((SUBLANES, RW_DIM), const2),
                 pl.BlockSpec((1, LANES, 2 * RW_DIM), dir3),
                 pl.BlockSpec((1, LANES, 2 * RW_DIM), dir3),
                 pl.BlockSpec((LANES, RW_DIM), const2)]
    args += [prm["tshift"], prm["pd"], prm["ps"], prm["wla_hi"], prm["wla_lo"], prm["wg"]]
    if vres:
        in_specs += [pl.BlockSpec((LANES, RW_DIM), const2), pl.BlockSpec((LANES, RW_DIM), const2)]
        args += [prm["vu_hi"], prm["vu_lo"]]
    sspec = pl.BlockSpec((1, N_PAIRS, LANES, LANES), lambda b, j: (b, 0, 0, 0))
    in_specs += [pl.BlockSpec((LANES, LANES), const2), sspec]
    args += [prm["e128"], state_in]
    return pl.pallas_call(
        functools.partial(_rwkv_kernel, nt=nt, tq=tq, vres=vres, drn=drn),
        grid=(nb, nt),
        in_specs=in_specs,
        out_specs=[pl.BlockSpec((tq, RW_DIM), lambda b, j: (tile(b, j), 0)),
                   pl.BlockSpec((tq, 2 * RW_DIM), lambda b, j: (tile(b, j), 0)),
                   sspec],
        out_shape=[jax.ShapeDtypeStruct((m, RW_DIM), F32),
                   jax.ShapeDtypeStruct((m, 2 * RW_DIM), F32),
                   jax.ShapeDtypeStruct((nb, N_PAIRS, LANES, LANES), F32)],
        scratch_shapes=[pltpu.VMEM((8, tq, RW_DIM), F32), pltpu.VMEM((N_PAIRS, LANES, LANES), F32)],
        compiler_params=_cparams("parallel", "arbitrary"),
        name="rwkv7_fwd" if drn == 0 else "rwkv7_bwd",
    )(*args)


def _rwkv_both(p, vf_src, hv, states, prm, nb, rows):
    outs = [_rwkv(p, vf_src, hv, states[d], prm, nb, rows, d) for d in range(2)]
    return [o[0] for o in outs], [o[1] for o in outs], [o[2] for o in outs]


def _merge_kernel(x_ref, y0_ref, y1_ref, a0_ref, a1_ref, ya_ref, g0_ref, g1_ref, g2_ref, g3_ref, ln_ref, e_ref,
                  wbr_ref, wba_ref, wo_ref, mod_ref, o_ref):
    e = e_ref[...]
    y = y0_ref[...] + y1_ref[...]
    mu = _segsum(y, e) * (1.0 / HEAD_DIM)
    yc = y - mu
    var = _segsum(yc * yc, e) * (1.0 / HEAD_DIM)
    yn = yc * lax.rsqrt(var + GN_EPS) * ln_ref[0:1, :] + ln_ref[1:2, :]
    cv = a0_ref[:, 0:RW_DIM] + a1_ref[:, 0:RW_DIM]
    yrw = (yn + cv) * a0_ref[:, RW_DIM:2 * RW_DIM]
    a = _bdot(yrw, wbr_ref[...])
    b = jnp.dot(ya_ref[...], wba_ref[...], preferred_element_type=F32)
    h = a.shape[1] // 2
    mrg = jnp.concatenate(
        [jax.nn.sigmoid(g0_ref[...]) * a[:, :h] + jax.nn.sigmoid(g2_ref[...]) * b[:, :h],
         jax.nn.sigmoid(g1_ref[...]) * a[:, h:] + jax.nn.sigmoid(g3_ref[...]) * b[:, h:]], axis=1)
    o_ref[...] = x_ref[...] + mod_ref[0, 2:3, :] * _bdot(mrg, wo_ref[...])


def _merge(x2, ys, auxs, ya, p, ln, e128, wbr, wba, wo, mod, rows):
    m, d = x2.shape
    tm = _row_tile(rows, 512)
    per = rows // tm
    many = mod.shape[0] > 1
    gb = GATE_OFF // 512
    const2 = lambda i: (0, 0)
    in_specs = [pl.BlockSpec((tm, d), lambda i: (i, 0)),
                pl.BlockSpec((tm, RW_DIM), lambda i: (i, 0)),
                pl.BlockSpec((tm, RW_DIM), lambda i: (i, 0)),
                pl.BlockSpec((tm, 2 * RW_DIM), lambda i: (i, 0)),
                pl.BlockSpec((tm, 2 * RW_DIM), lambda i: (i, 0)),
                pl.BlockSpec((tm, ATT_DIM), lambda i: (i, 0))]
    in_specs += [pl.BlockSpec((tm, 512), (lambda i, c=c: (i, gb + c))) for c in range(4)]
    in_specs += [pl.BlockSpec((2, RW_DIM), const2),
                 pl.BlockSpec((LANES, LANES), const2),
                 pl.BlockSpec(wbr.shape, const2),
                 pl.BlockSpec(wba.shape, const2),
                 pl.BlockSpec(wo.shape, const2),
                 pl.BlockSpec((1, 6, d), lambda i: ((i // per) if many else 0, 0, 0))]
    return pl.pallas_call(
        _merge_kernel,
        grid=(m // tm,),
        in_specs=in_specs,
        out_specs=pl.BlockSpec((tm, d), lambda i: (i, 0)),
        out_shape=jax.ShapeDtypeStruct((m, d), F32),
        compiler_params=_cparams("parallel"),
        name="branch_merge",
    )(x2, ys[0], ys[1], auxs[0], auxs[1], ya, p, p, p, p, ln, e128, wbr, wba, wo, mod)


def _ffn_kernel(x_ref, g_ref, mod_ref, wg_ref, wu_ref, wd_ref, o_ref, h_sc, acc_sc, *, nf):
    f = pl.program_id(1)

    @pl.when(f == 0)
    def _():
        h = _norm_mod(x_ref[...], g_ref[...], mod_ref[0, 3:4, :], mod_ref[0, 4:5, :])
        h_sc[...] = h.astype(BF16)
        acc_sc[...] = jnp.zeros_like(acc_sc)

    h = h_sc[...]
    a = jnp.dot(h, wg_ref[...], preferred_element_type=F32)
    u = jnp.dot(h, wu_ref[...], preferred_element_type=F32)
    act = (a * jax.nn.sigmoid(a)) * u
    acc_sc[...] += jnp.dot(act.astype(BF16), wd_ref[...], preferred_element_type=F32)

    @pl.when(f == nf - 1)
    def _():
        o_ref[...] = x_ref[...] + mod_ref[0, 5:6, :] * acc_sc[...]


def _ffn(x2, gain, mod, wg, wu, wd, rows):
    m, d = x2.shape
    dff = wg.shape[1]
    tm = _row_tile(rows, 512)
    per = rows // tm
    many = mod.shape[0] > 1
    tf = dff // 2 if (dff // 2) % LANES == 0 else dff
    nf = dff // tf
    return pl.pallas_call(
        functools.partial(_ffn_kernel, nf=nf),
        grid=(m // tm, nf),
        in_specs=[pl.BlockSpec((tm, d), lambda i, f: (i, 0)),
                  pl.BlockSpec((1, d), lambda i, f: (0, 0)),
                  pl.BlockSpec((1, 6, d), lambda i, f: ((i // per) if many else 0, 0, 0)),
                  pl.BlockSpec((d, tf), lambda i, f: (0, f)),
                  pl.BlockSpec((d, tf), lambda i, f: (0, f)),
                  pl.BlockSpec((tf, d), lambda i, f: (f, 0))],
        out_specs=pl.BlockSpec((tm, d), lambda i, f: (i, 0)),
        out_shape=jax.ShapeDtypeStruct((m, d), F32),
        scratch_shapes=[pltpu.VMEM((tm, d), BF16), pltpu.VMEM((tm, d), F32)],
        compiler_params=_cparams("parallel", "arbitrary"),
        name="ffn_swiglu",
    )(x2, gain.reshape(1, d), mod, wg, wu, wd)


def _moe_kernel(nblk_ref, x_ref, h_ref, we_ref, rank_ref, rankt_ref, mod_ref, wg_ref, wu_ref, wd_ref, o_ref,
                acc_sc, hs_sc, os_sc, *, ne, nf):
    i = pl.program_id(0)
    ex = pl.program_id(1)
    f = pl.program_id(2)
    tt = h_ref.shape[0]
    nblk = nblk_ref[i * ne + ex]
    lane = lax.broadcasted_iota(jnp.int32, (tt, LANES), 1)

    @pl.when((ex == 0) & (f == 0))
    def _():
        acc_sc[...] = jnp.zeros_like(acc_sc)

    @pl.when(f == 0)
    def _():
        rrow = rankt_ref[0, pl.ds(ex, 1), :]
        slot = lax.broadcasted_iota(jnp.int32, (MOE_BLK, tt), 0).astype(F32)

        def gather(b, carry):
            sel = ((rrow - (b * MOE_BLK).astype(F32)) == slot).astype(BF16)
            r0 = pl.multiple_of(b * MOE_BLK, MOE_BLK)
            hs_sc[pl.ds(r0, MOE_BLK), :] = jnp.dot(sel, h_ref[...], preferred_element_type=F32).astype(BF16)
            return carry

        lax.fori_loop(0, nblk, gather, 0)

    def ffn(b, carry):
        r0 = pl.multiple_of(b * MOE_BLK, MOE_BLK)
        hb = hs_sc[pl.ds(r0, MOE_BLK), :]
        a = jnp.dot(hb, wg_ref[0], preferred_element_type=F32)
        u = jnp.dot(hb, wu_ref[0], preferred_element_type=F32)
        act = ((a * jax.nn.sigmoid(a)) * u).astype(BF16)
        part = jnp.dot(act, wd_ref[0], preferred_element_type=F32)

        @pl.when(f == 0)
        def _():
            os_sc[pl.ds(r0, MOE_BLK), :] = part

        @pl.when(f > 0)
        def _():
            os_sc[pl.ds(r0, MOE_BLK), :] += part

        return carry

    lax.fori_loop(0, nblk, ffn, 0)

    @pl.when(f == nf - 1)
    def _():
        rcol = jnp.sum(jnp.where(lane == ex, rank_ref[...], 0.0), axis=1, keepdims=True)
        wcol = jnp.sum(jnp.where(lane == ex, we_ref[...], 0.0), axis=1, keepdims=True)
        slot = lane.astype(F32)

        def scatter(b, carry):
            sel = ((rcol - (b * MOE_BLK).astype(F32)) == slot).astype(BF16)
            r0 = pl.multiple_of(b * MOE_BLK, MOE_BLK)
            ob = os_sc[pl.ds(r0, MOE_BLK), :].astype(BF16)
            acc_sc[...] += wcol * jnp.dot(sel, ob, preferred_element_type=F32)
            return carry

        lax.fori_loop(0, nblk, scatter, 0)

    @pl.when((ex == ne - 1) & (f == nf - 1))
    def _():
        o_ref[...] = x_ref[...] + mod_ref[0, 5:6, :] * acc_sc[...]


def _route_kernel(x_ref, g_ref, mod_ref, rt_ref, h_ref, we_ref, rank_ref, rankt_ref, cnt_ref, *, ne):
    tt = x_ref.shape[0]
    h = _norm_mod(x_ref[...], g_ref[...], mod_ref[0, 3:4, :], mod_ref[0, 4:5, :])
    h_ref[...] = h.astype(BF16)
    lane = lax.broadcasted_iota(jnp.int32, (tt, LANES), 1)
    logits = jnp.dot(h, rt_ref[...], precision=HIGHEST, preferred_element_type=F32)
    lg = jnp.where(lane < ne, logits, NEG_INF)
    t1 = jnp.max(lg, axis=1, keepdims=True)
    i1 = jnp.min(jnp.where(lg == t1, lane, LANES), axis=1, keepdims=True)
    lg2 = jnp.where(lane == i1, NEG_INF, lg)
    t2 = jnp.max(lg2, axis=1, keepdims=True)
    i2 = jnp.min(jnp.where(lg2 == t2, lane, LANES), axis=1, keepdims=True)
    e2 = jnp.exp(t2 - t1)
    den = 1.0 + e2
    we_ref[...] = jnp.where(lane == i1, 1.0 / den, 0.0) + jnp.where(lane == i2, e2 / den, 0.0)
    sel = (lane == i1) | (lane == i2)
    self32 = sel.astype(F32)
    before = (lax.broadcasted_iota(jnp.int32, (tt, tt), 1) < lax.broadcasted_iota(jnp.int32, (tt, tt), 0))
    rank = jnp.dot(before.astype(BF16), sel.astype(BF16), preferred_element_type=F32)
    rank = jnp.where(sel, rank, -1.0)
    rank_ref[...] = rank
    rankt_ref[0] = rank.T[0:SUBLANES, :]
    cnt_ref[0] = jnp.broadcast_to(jnp.sum(self32, axis=0, keepdims=True), (SUBLANES, LANES))


def _moe(x2, gain, mod, router_pad, wg, wu, wd, rows):
    m, d = x2.shape
    ne, _, dff = wg.shape
    assert ne <= SUBLANES
    tt = _row_tile(rows, MOE_TILE)
    nti = m // tt
    per = rows // tt
    many = mod.shape[0] > 1
    modspec = lambda nargs: pl.BlockSpec(
        (1, 6, d), (lambda i: ((i // per) if many else 0, 0, 0)) if nargs == 1
        else (lambda i, e, f, nb: ((i // per) if many else 0, 0, 0)))
    h, we, rank, rankt, cnt = pl.pallas_call(
        functools.partial(_route_kernel, ne=ne),
        grid=(nti,),
        in_specs=[pl.BlockSpec((tt, d), lambda i: (i, 0)),
                  pl.BlockSpec((1, d), lambda i: (0, 0)),
                  modspec(1),
                  pl.BlockSpec((d, LANES), lambda i: (0, 0))],
        out_specs=[pl.BlockSpec((tt, d), lambda i: (i, 0)),
                   pl.BlockSpec((tt, LANES), lambda i: (i, 0)),
                   pl.BlockSpec((tt, LANES), lambda i: (i, 0)),
                   pl.BlockSpec((1, SUBLANES, tt), lambda i: (i, 0, 0)),
                   pl.BlockSpec((1, SUBLANES, LANES), lambda i: (i, 0, 0))],
        out_shape=[jax.ShapeDtypeStruct((m, d), BF16),
                   jax.ShapeDtypeStruct((m, LANES), F32),
                   jax.ShapeDtypeStruct((m, LANES), F32),
                   jax.ShapeDtypeStruct((nti, SUBLANES, tt), F32),
                   jax.ShapeDtypeStruct((nti, SUBLANES, LANES), F32)],
        compiler_params=_cparams("parallel"),
        name="moe_route",
    )(x2, gain.reshape(1, d), mod, router_pad)
    nblk = ((cnt[:, 0, :ne].astype(jnp.int32) + (MOE_BLK - 1)) // MOE_BLK).reshape(-1)

    tf = dff // 4 if (dff // 4) % LANES == 0 else dff
    nf = dff // tf
    grid_spec = pltpu.PrefetchScalarGridSpec(
        num_scalar_prefetch=1,
        grid=(nti, ne, nf),
        in_specs=[pl.BlockSpec((tt, d), lambda i, e, f, nb: (i, 0)),
                  pl.BlockSpec((tt, d), lambda i, e, f, nb: (i, 0)),
                  pl.BlockSpec((tt, LANES), lambda i, e, f, nb: (i, 0)),
                  pl.BlockSpec((tt, LANES), lambda i, e, f, nb: (i, 0)),
                  pl.BlockSpec((1, SUBLANES, tt), lambda i, e, f, nb: (i, 0, 0)),
                  modspec(4),
                  pl.BlockSpec((1, d, tf), lambda i, e, f, nb: (e, 0, f)),
                  pl.BlockSpec((1, d, tf), lambda i, e, f, nb: (e, 0, f)),
                  pl.BlockSpec((1, tf, d), lambda i, e, f, nb: (e, f, 0))],
        out_specs=pl.BlockSpec((tt, d), lambda i, e, f, nb: (i, 0)),
        scratch_shapes=[pltpu.VMEM((tt, d), F32), pltpu.VMEM((tt, d), BF16), pltpu.VMEM((tt, d), F32)])
    return pl.pallas_call(
        functools.partial(_moe_kernel, ne=ne, nf=nf),
        grid_spec=grid_spec,
        out_shape=jax.ShapeDtypeStruct((m, d), F32),
        compiler_params=_cparams("parallel", "arbitrary", "arbitrary"),
        name="moe_experts",
    )(nblk, x2, h, we, rank, rankt, mod, wg, wu, wd)


def _rope_table(t):
    rows = t // GRID_W
    row = jnp.repeat(jnp.arange(rows, dtype=F32), GRID_W)
    col = jnp.tile(jnp.arange(GRID_W, dtype=F32), rows)
    half = HEAD_DIM // 2
    inv_freq = ROPE_THETA ** (-jnp.arange(0, half, 2, dtype=F32) / half)
    ar = row[:, None] * inv_freq
    ac = col[:, None] * inv_freq
    ang = jnp.concatenate([ar, ar, ac, ac] * 2, axis=-1)
    return jnp.stack([jnp.cos(ang), jnp.sin(ang)])


def _pad_rows(a, rows):
    return jnp.zeros((rows,) + a.shape[1:], a.dtype).at[:a.shape[0]].set(a)


def _hi_lo(w):
    hi = w.astype(BF16)
    return hi, (w - hi.astype(F32)).astype(BF16)


def kernel(x, c, ctx, c_ctx, norm1, norm2, w_mod, b_mod, w_in, tshift, w0, w_decay_up, a0, w_iclr_up, v0, w_vres_down, w_vres_up, w_gate_up, k_k, k_a, r_k, lnx_g, lnx_b, q_norm, k_norm, sink, w_br_rwkv, w_br_attn, w_out, ffn_gate, ffn_up, ffn_down, router, moe_gate, moe_up, moe_down):
    nb, t, d = x.shape
    cn = ctx.shape[1]
    depth = w_in.shape[0]
    mods = _modulation(c, c_ctx, w_mod, b_mod)
    cs = _rope_table(t)
    e128 = jnp.kron(jnp.eye(2, dtype=F32), jnp.ones((HEAD_DIM, HEAD_DIM), F32)).astype(BF16)
    xl = x.reshape(nb * t, d)
    xc = ctx.reshape(nb * cn, d)
    vf_l = vf_c = None
    for layer in range(depth):
        last = layer == depth - 1
        mod_l = mods[layer, :nb].reshape(nb, 6, d)
        mod_c = mods[layer, nb:nb + 1].reshape(1, 6, d)
        w_in_b = w_in[layer].astype(BF16)
        vd_b = None
        prm = {"tshift": tshift[layer], "e128": e128, "wg": w_gate_up[layer].astype(BF16)}
        prm["pd"] = jnp.stack(
            [_pad_rows(jnp.stack([w0[layer, dd], a0[layer, dd]]), SUBLANES) for dd in range(2)])
        shared = [k_k[layer], k_a[layer], r_k[layer].reshape(RW_DIM)]
        if layer > 0:
            shared.append(v0[layer - 1])
            vd_b = _pad_rows(w_vres_down[layer - 1].T, LANES).T.astype(BF16)
            prm["vu_hi"], prm["vu_lo"] = _hi_lo(_pad_rows(w_vres_up[layer - 1], LANES))
        prm["ps"] = _pad_rows(jnp.stack(shared), SUBLANES)
        wla = jnp.zeros((2, LANES, 2 * RW_DIM), F32)
        wla = wla.at[:, :HEAD_DIM, :RW_DIM].set(w_decay_up[layer]).at[:, HEAD_DIM:, RW_DIM:].set(w_iclr_up[layer])
        prm["wla_hi"], prm["wla_lo"] = _hi_lo(wla)

        p_l, hv_l = _inproj(xl, norm1[layer], mod_l, w_in_b, vd_b, t)
        p_c, hv_c = _inproj(xc, norm1[layer], mod_c, w_in_b, vd_b, cn)

        s0 = jnp.zeros((nb, N_PAIRS, LANES, LANES), F32)
        y_c, aux_c, s_ctx = _rwkv_both(p_c, vf_c, hv_c, [s0, s0], prm, nb, cn)
        y_l, aux_l, _ = _rwkv_both(p_l, vf_l, hv_l, s_ctx, prm, nb, t)
        if layer == 0:
            vf_l, vf_c = aux_l[1], aux_c[1]

        qg = jnp.tile(q_norm[layer], 2).reshape(1, LANES)
        kg = jnp.tile(k_norm[layer], 2).reshape(1, LANES)
        qn_l, kn_l, vb_l = _qk_prep(p_l, cs, qg, kg, e128, t, True)
        qn_c, kn_c, vb_c = _qk_prep(p_c, None, qg, kg, e128, cn, False)
        sk = sink[layer].reshape(1, -1)
        ya_l = _attention(qn_l, kn_l, vb_l, kn_c, vb_c, sk, nb, t, cn, True)

        ln = jnp.stack([lnx_g[layer], lnx_b[layer]])
        wbr = w_br_rwkv[layer].astype(BF16)
        wba = w_br_attn[layer].astype(BF16)
        wo = w_out[layer].astype(BF16)
        xl = _merge(xl, y_l, aux_l, ya_l, p_l, ln, e128, wbr, wba, wo, mod_l, t)
        if not last:
            ya_c = _attention(qn_c, None, None, kn_c, vb_c, sk, nb, cn, cn, False)
            xc = _merge(xc, y_c, aux_c, ya_c, p_c, ln, e128, wbr, wba, wo, mod_c, cn)

        i = layer // 2
        if layer % 2 == 0:
            fw = (ffn_gate[i].astype(BF16), ffn_up[i].astype(BF16), ffn_down[i].astype(BF16))
            xl = _ffn(xl, norm2[layer], mod_l, *fw, t)
            if not last:
                xc = _ffn(xc, norm2[layer], mod_c, *fw, cn)
        else:
            rt = _pad_rows(router[i].T, LANES).T
            mw = (moe_gate[i].astype(BF16), moe_up[i].astype(BF16), moe_down[i].astype(BF16))
            xl = _moe(xl, norm2[layer], mod_l, rt, *mw, t)
            if not last:
                xc = _moe(xc, norm2[layer], mod_c, rt, *mw, cn)
    return xl.reshape(nb, t, d)
```

```python
import functools

import jax
import jax.numpy as jnp
from jax import lax
from jax.experimental import pallas as pl
from jax.experimental.pallas import tpu as pltpu

F32 = jnp.float32
BF16 = jnp.bfloat16
HIGHEST = lax.Precision.HIGHEST

LANES = 128
SUBLANES = 8
PACK = 16
ACT = jnp.bfloat16
HEAD_DIM = 64
RW_DIM = 512
ATT_DIM = 512
KV_DIM = 128
RW_IN = 1792
ATT_OFF = RW_IN
GATE_OFF = RW_IN + ATT_DIM + 2 * KV_DIM
N_EXPERTS = 8
CHUNK = 64
N_PAIRS = RW_DIM // LANES
MOE_TILE = 1024
MOE_BLK = 128
ATT_SCALE = HEAD_DIM ** -0.5
ROPE_THETA = 10000.0
GRID_W = 64
RMS_EPS = 1e-6
GN_EPS = 64e-5
NEG_INF = -1e30
DECAY_SCALE = 0.6065306597126334
VMEM_LIMIT = 48 * 1024 * 1024


def _cparams(*sem):
    return pltpu.CompilerParams(dimension_semantics=sem, vmem_limit_bytes=VMEM_LIMIT)


def _row_tile(rows, cap):
    t = cap
    while rows % t:
        t //= 2
    return t


def _bdot(a, b):
    return jnp.dot(a.astype(BF16), b.astype(BF16), preferred_element_type=F32)


def _bdot_nt(a, b):
    return lax.dot_general(a.astype(BF16), b.astype(BF16), (((1,), (1,)), ((), ())),
                           preferred_element_type=F32)


def _bdot_tn(a, b):
    return lax.dot_general(a.astype(BF16), b.astype(BF16), (((0,), (0,)), ((), ())),
                           preferred_element_type=F32)


def _split2(x):
    hi = x.astype(BF16)
    lo = (x - hi.astype(F32)).astype(BF16)
    return hi, lo


def _split3(x):
    hi = x.astype(BF16)
    r = x - hi.astype(F32)
    mid = r.astype(BF16)
    lo = (r - mid.astype(F32)).astype(BF16)
    return hi, mid, lo


def _dot3(a, w_hi, w_lo):
    a_hi, a_lo = _split2(a)
    return (jnp.dot(a_hi, w_hi, preferred_element_type=F32)
            + (jnp.dot(a_lo, w_hi, preferred_element_type=F32)
               + jnp.dot(a_hi, w_lo, preferred_element_type=F32)))


def _segsum(x, e):
    outs = []
    for s in range(x.shape[1] // LANES):
        hi, lo = _split2(x[:, s * LANES:(s + 1) * LANES])
        outs.append(jnp.dot(hi, e, preferred_element_type=F32)
                    + jnp.dot(lo, e, preferred_element_type=F32))
    return outs[0] if len(outs) == 1 else jnp.concatenate(outs, axis=1)


def _norm_mod(x, gain, shift, scale):
    y = x * lax.rsqrt(jnp.mean(x * x, axis=-1, keepdims=True) + RMS_EPS) * gain
    return y * (1.0 + scale) + shift


def _mod_kernel(c_ref, w_ref, b_ref, o_ref):
    c = c_ref[...]
    s = c * jax.nn.sigmoid(c)
    o_ref[0] = jnp.dot(s, w_ref[0], precision=HIGHEST, preferred_element_type=F32) + b_ref[0]


def _modulation(c, c_ctx, w_mod, b_mod):
    depth, d, n6 = w_mod.shape
    nb = c.shape[0]
    rm = -(-(nb + 1) // SUBLANES) * SUBLANES
    cc = jnp.zeros((rm, d), F32).at[:nb].set(c).at[nb].set(c_ctx)
    tn = _row_tile(n6, 1536)
    return pl.pallas_call(
        _mod_kernel,
        grid=(depth, n6 // tn),
        in_specs=[pl.BlockSpec((rm, d), lambda l, n: (0, 0)),
                  pl.BlockSpec((1, d, tn), lambda l, n: (l, 0, n)),
                  pl.BlockSpec((1, 1, tn), lambda l, n: (l, 0, n))],
        out_specs=pl.BlockSpec((1, rm, tn), lambda l, n: (l, 0, n)),
        out_shape=jax.ShapeDtypeStruct((depth, rm, n6), F32),
        compiler_params=_cparams("parallel", "parallel"),
        name="adaln_mod",
    )(cc, w_mod, b_mod.reshape(depth, 1, n6))


def _inproj_kernel(*refs, vres):
    if vres:
        x_ref, g_ref, mod_ref, w_ref, vd_ref, o_ref, hv_ref, h_sc = refs
    else:
        x_ref, g_ref, mod_ref, w_ref, o_ref, h_sc = refs

    @pl.when(pl.program_id(1) == 0)
    def _():
        h = _norm_mod(x_ref[...], g_ref[...], mod_ref[0, 0:1, :], mod_ref[0, 1:2, :])
        hb = h.astype(BF16)
        h_sc[...] = hb
        if vres:
            hv_ref[...] = jnp.dot(hb, vd_ref[...], preferred_element_type=F32)

    o_ref[...] = jnp.dot(h_sc[...], w_ref[...], preferred_element_type=F32).astype(o_ref.dtype)


def _inproj(x2, gain, mod, w_b, vd_b, rows):
    m, d = x2.shape
    n = w_b.shape[1]
    tm = _row_tile(rows, 1024)
    tn = _row_tile(n, 1536)
    per = rows // tm
    many = mod.shape[0] > 1
    vres = vd_b is not None
    in_specs = [pl.BlockSpec((tm, d), lambda i, j: (i, 0)),
                pl.BlockSpec((1, d), lambda i, j: (0, 0)),
                pl.BlockSpec((1, 6, d), lambda i, j: ((i // per) if many else 0, 0, 0)),
                pl.BlockSpec((d, tn), lambda i, j: (0, j))]
    args = [x2, gain.reshape(1, d), mod, w_b]
    out_specs = [pl.BlockSpec((tm, tn), lambda i, j: (i, j))]
    out_shape = [jax.ShapeDtypeStruct((m, n), ACT)]
    if vres:
        in_specs.append(pl.BlockSpec((d, LANES), lambda i, j: (0, 0)))
        args.append(vd_b)
        out_specs.append(pl.BlockSpec((tm, LANES), lambda i, j: (i, 0)))
        out_shape.append(jax.ShapeDtypeStruct((m, LANES), F32))
    outs = pl.pallas_call(
        functools.partial(_inproj_kernel, vres=vres),
        grid=(m // tm, n // tn),
        in_specs=in_specs, out_specs=out_specs, out_shape=out_shape,
        scratch_shapes=[pltpu.VMEM((tm, d), BF16)],
        compiler_params=_cparams("parallel", "arbitrary"),
        name="in_proj",
    )(*args)
    return (outs[0], outs[1]) if vres else (outs[0], None)


def _qkprep_kernel(*refs, rope):
    if rope:
        q0_ref, q1_ref, k_ref, v_ref, cs_ref, qg_ref, kg_ref, e_ref, qn_ref, kn_ref, vb_ref = refs
    else:
        q0_ref, q1_ref, k_ref, v_ref, qg_ref, kg_ref, e_ref, qn_ref, kn_ref, vb_ref = refs
    e = e_ref[...]
    lane = lax.broadcasted_iota(jnp.int32, (1, LANES), 1)
    first = (lane & 31) < 16

    def norm_rope(u, gain):
        y = u * lax.rsqrt(_segsum(u * u, e) * (1.0 / HEAD_DIM) + RMS_EPS) * gain
        if rope:
            rot = jnp.where(first, -pltpu.roll(y, LANES - 16, 1), pltpu.roll(y, 16, 1))
            y = y * cs_ref[0] + rot * cs_ref[1]
        return y

    qg = qg_ref[...]
    for s, ref in enumerate((q0_ref, q1_ref)):
        for t in range(2):
            u = ref[:, t * LANES:(t + 1) * LANES].astype(F32)
            c0 = (2 * s + t) * LANES
            qn_ref[:, c0:c0 + LANES] = (norm_rope(u, qg) * ATT_SCALE).astype(BF16)
    kn_ref[...] = norm_rope(k_ref[...].astype(F32), kg_ref[...]).astype(BF16)
    vb_ref[...] = v_ref[...].astype(BF16)


def _qk_prep(p, cs, qg, kg, e128, rows, rope):
    m = p.shape[0]
    tm = _row_tile(rows, 512)
    per = rows // tm
    qb = ATT_OFF // 256
    kb = (ATT_OFF + ATT_DIM) // LANES
    in_specs = [pl.BlockSpec((tm, 256), lambda i: (i, qb)),
                pl.BlockSpec((tm, 256), lambda i: (i, qb + 1)),
                pl.BlockSpec((tm, LANES), lambda i: (i, kb)),
                pl.BlockSpec((tm, LANES), lambda i: (i, kb + 1))]
    args = [p, p, p, p]
    if rope:
        in_specs.append(pl.BlockSpec((2, tm, LANES), lambda i: (0, i % per, 0)))
        args.append(cs)
    in_specs += [pl.BlockSpec((1, LANES), lambda i: (0, 0)),
                 pl.BlockSpec((1, LANES), lambda i: (0, 0)),
                 pl.BlockSpec((LANES, LANES), lambda i: (0, 0))]
    args += [qg, kg, e128]
    return pl.pallas_call(
        functools.partial(_qkprep_kernel, rope=rope),
        grid=(m // tm,),
        in_specs=in_specs,
        out_specs=[pl.BlockSpec((tm, ATT_DIM), lambda i: (i, 0)),
                   pl.BlockSpec((tm, LANES), lambda i: (i, 0)),
                   pl.BlockSpec((tm, LANES), lambda i: (i, 0))],
        out_shape=[jax.ShapeDtypeStruct((m, ATT_DIM), BF16),
                   jax.ShapeDtypeStruct((m, KV_DIM), BF16),
                   jax.ShapeDtypeStruct((m, KV_DIM), BF16)],
        compiler_params=_cparams("parallel"),
        name="qk_prep",
    )(*args)


def _attn_kernel(*refs, nq, local):
    if local:
        sink_ref, q_ref, kp_ref, kc_ref, kn_ref, vp_ref, vc_ref, vn_ref, kx_ref, vx_ref, o_ref = refs
        k = jnp.concatenate([kp_ref[...], kc_ref[...], kn_ref[...], kx_ref[...]], axis=0).astype(F32)
        v = jnp.concatenate([vp_ref[...], vc_ref[...], vn_ref[...], vx_ref[...]], axis=0).astype(F32)
    else:
        sink_ref, q_ref, kx_ref, vx_ref, o_ref = refs
        k = kx_ref[...].astype(F32)
        v = vx_ref[...].astype(F32)
    i = pl.program_id(1)
    bq = q_ref.shape[0]
    low = lax.broadcasted_iota(jnp.int32, (1, LANES), 1) < HEAD_DIM
    ksw = pltpu.roll(k, HEAD_DIM, 1)
    vsw = pltpu.roll(v, HEAD_DIM, 1)
    row = lax.broadcasted_iota(jnp.int32, (4 * bq, 1), 0)
    if local:
        qo = lax.broadcasted_iota(jnp.int32, (4 * bq, 3 * bq), 0) & (bq - 1)
        kcol = lax.broadcasted_iota(jnp.int32, (4 * bq, 3 * bq), 1)
        ko = kcol & (bq - 1)
        kblk = kcol // bq
        ok = ((kblk == 1)
              | ((kblk == 0) & (ko >= qo) & (i > 0))
              | ((kblk == 2) & (ko <= qo) & (i < nq - 1)))
    for g in range(2):
        kb = (jnp.where(low, k, ksw) if g == 0 else jnp.where(low, ksw, k)).astype(BF16)
        vb = (jnp.where(low, v, vsw) if g == 0 else jnp.where(low, vsw, v)).astype(BF16)
        qa = q_ref[:, (2 * g) * LANES:(2 * g + 1) * LANES].astype(F32)
        qb = q_ref[:, (2 * g + 1) * LANES:(2 * g + 2) * LANES].astype(F32)
        qs = jnp.concatenate([jnp.where(low, qa, 0.0), jnp.where(low, 0.0, qa),
                              jnp.where(low, qb, 0.0), jnp.where(low, 0.0, qb)], axis=0).astype(BF16)
        s = lax.dot_general(qs, kb, (((1,), (1,)), ((), ())), preferred_element_type=F32)
        if local:
            s = jnp.concatenate([jnp.where(ok, s[:, :3 * bq], NEG_INF), s[:, 3 * bq:]], axis=1)
        sk = jnp.where(row < bq, sink_ref[0, 4 * g],
                       jnp.where(row < 2 * bq, sink_ref[0, 4 * g + 1],
                                 jnp.where(row < 3 * bq, sink_ref[0, 4 * g + 2], sink_ref[0, 4 * g + 3])))
        mx = jnp.maximum(jnp.max(s, axis=1, keepdims=True), sk)
        ex = jnp.exp(s - mx)
        den = jnp.sum(ex, axis=1, keepdims=True) + jnp.exp(sk - mx)
        pr = (ex / den).astype(BF16)
        o = jnp.dot(pr, vb, preferred_element_type=F32)
        o_ref[:, (2 * g) * LANES:(2 * g + 1) * LANES] = jnp.where(low, o[0:bq], o[bq:2 * bq]).astype(o_ref.dtype)
        o_ref[:, (2 * g + 1) * LANES:(2 * g + 2) * LANES] = jnp.where(
            low, o[2 * bq:3 * bq], o[3 * bq:4 * bq]).astype(o_ref.dtype)


def _attention(qn, kn, vb, kctx, vctx, sink, nb, rows, cn, local):
    m = qn.shape[0]
    bq = 128
    nq = rows // bq
    smem = pl.BlockSpec(memory_space=pltpu.SMEM)
    qspec = pl.BlockSpec((bq, ATT_DIM), lambda b, i: (b * nq + i, 0))
    cspec = pl.BlockSpec((cn, KV_DIM), lambda b, i: (b, 0))
    if local:
        prv = pl.BlockSpec((bq, KV_DIM), lambda b, i: (b * nq + jnp.maximum(i - 1, 0), 0))
        cur = pl.BlockSpec((bq, KV_DIM), lambda b, i: (b * nq + i, 0))
        nxt = pl.BlockSpec((bq, KV_DIM), lambda b, i: (b * nq + jnp.minimum(i + 1, nq - 1), 0))
        in_specs = [smem, qspec, prv, cur, nxt, prv, cur, nxt, cspec, cspec]
        args = [sink, qn, kn, kn, kn, vb, vb, vb, kctx, vctx]
    else:
        in_specs = [smem, qspec, cspec, cspec]
        args = [sink, qn, kctx, vctx]
    return pl.pallas_call(
        functools.partial(_attn_kernel, nq=nq, local=local),
        grid=(nb, nq),
        in_specs=in_specs,
        out_specs=pl.BlockSpec((bq, ATT_DIM), lambda b, i: (b * nq + i, 0)),
        out_shape=jax.ShapeDtypeStruct((m, ATT_DIM), BF16),
        compiler_params=_cparams("parallel", "parallel"),
        name="window_attn" if local else "ctx_attn",
    )(*args)


def _rwkv_kernel(*refs, nt, tq, vres, drn):
    it = iter(refs)
    p_ref, pv_ref, nx_ref = next(it), next(it), next(it)
    if vres:
        vf_ref, hv_ref = next(it), next(it)
    ts_ref, pd_ref, ps_ref, wlh_ref, wll_ref, wg_ref = (next(it) for _ in range(6))
    if vres:
        vuh_ref, vul_ref = next(it), next(it)
    e_ref, sin_ref = next(it), next(it)
    y_ref, aux_ref, sout_ref = next(it), next(it), next(it)
    ops_sc, s_sc = next(it), next(it)

    j = pl.program_id(1)
    sgn = 1 - 2 * drn
    tile = j if drn == 0 else nt - 1 - j
    nch = tq // CHUNK

    @pl.when(j == 0)
    def _():
        s_sc[...] = sin_ref[0]

    rw = p_ref[...].astype(F32)
    row = lax.broadcasted_iota(jnp.int32, (tq, 1), 0)
    prev_row = jnp.where(tile > 0, pv_ref[PACK - 1:PACK, :].astype(F32), 0.0)
    next_row = jnp.where(tile < nt - 1, nx_ref[0:1, :].astype(F32), 0.0)
    prv = jnp.where(row == 0, prev_row, pltpu.roll(rw, 1, 0))
    nxt = jnp.where(row == tq - 1, next_row, pltpu.roll(rw, tq - 1, 0))
    xs = rw + ts_ref[0:1, :] * (prv - rw) + ts_ref[1:2, :] * (nxt - rw)

    e = e_ref[...]
    r = xs[:, 0:RW_DIM]
    k = xs[:, RW_DIM:2 * RW_DIM]
    v = xs[:, 2 * RW_DIM:3 * RW_DIM]
    la = xs[:, 3 * RW_DIM:3 * RW_DIM + LANES]
    lane = lax.broadcasted_iota(jnp.int32, (1, LANES), 1)
    low = lane < HEAD_DIM
    la = jnp.where(low, jnp.tanh(la), la)
    lora = _dot3(la, wlh_ref[0], wll_ref[0])
    logw = -DECAY_SCALE * jax.nn.sigmoid(pd_ref[0, 0:1, :] + lora[:, 0:RW_DIM])
    iclr = jax.nn.sigmoid(pd_ref[0, 1:2, :] + lora[:, RW_DIM:2 * RW_DIM])
    if vres:
        mix = jax.nn.sigmoid(ps_ref[3:4, :] + _dot3(hv_ref[...], vuh_ref[...], vul_ref[...]))
        v = v + (vf_ref[...].astype(F32) - v) * mix
    kk = k * ps_ref[0:1, :]
    kk = kk * lax.rsqrt(_segsum(kk * kk, e) + 1e-12)
    kh = k * (1.0 + (iclr - 1.0) * ps_ref[1:2, :])
    bb = kk * iclr
    aux_ref[:, 0:RW_DIM] = (_segsum(r * kh * ps_ref[2:3, :], e) * v).astype(aux_ref.dtype)
    if drn == 0:
        glo = xs[:, 3 * RW_DIM + LANES:RW_IN]
        aux_ref[:, RW_DIM:2 * RW_DIM] = _bdot(jax.nn.sigmoid(glo), wg_ref[...]).astype(aux_ref.dtype)
    else:
        aux_ref[:, RW_DIM:2 * RW_DIM] = v.astype(aux_ref.dtype)

    ti = lax.broadcasted_iota(jnp.int32, (tq, tq), 0)
    si = lax.broadcasted_iota(jnp.int32, (tq, tq), 1)
    same = (ti // CHUNK) == (si // CHUNK)
    tri = (same & (((si - ti) * sgn) <= 0)).astype(BF16)
    ones = same.astype(BF16)
    cin = jnp.zeros_like(logw)
    ctot = jnp.zeros_like(logw)
    for part in _split3(logw):
        cin = cin + jnp.dot(tri, part, preferred_element_type=F32)
        ctot = ctot + jnp.dot(ones, part, preferred_element_type=F32)
    e_neg = jnp.exp(-cin)
    e_rem = jnp.exp(ctot - cin)
    ops_sc[0] = kk * jnp.exp(cin - logw)
    ops_sc[1] = r * jnp.exp(cin)
    ops_sc[2] = bb * e_neg
    ops_sc[3] = kh * e_neg
    ops_sc[4] = kh * e_rem
    ops_sc[5] = bb * e_rem
    ops_sc[6] = v
    ops_sc[7] = jnp.exp(ctot)

    r2 = lax.broadcasted_iota(jnp.int32, (LANES, LANES), 0)
    c2 = lax.broadcasted_iota(jnp.int32, (LANES, LANES), 1)
    rel = ((c2 & (CHUNK - 1)) - (r2 & (CHUNK - 1))) * sgn
    strict = rel < 0
    incl = rel <= 0
    eye = (r2 == c2).astype(F32)

    def stack(x):
        return jnp.concatenate([jnp.where(low, x, 0.0), jnp.where(low, 0.0, x)], axis=0)

    probs = []
    for ii in range(nch):
        off = (ii if drn == 0 else nch - 1 - ii) * CHUNK
        for pr in range(N_PAIRS):
            ls = slice(pr * LANES, (pr + 1) * LANES)
            q = {"off": off, "ls": ls, "pr": pr}
            for n, name in enumerate(("kks", "rs", "bs", "ks", "kps", "bps", "vs")):
                q[name] = stack(ops_sc[n, off:off + CHUNK, ls])
            q["wl"] = ops_sc[7, off:off + 1, ls]
            probs.append(q)
    for q in probs:
        g = _bdot_nt(jnp.concatenate([q["kks"], q["rs"]], axis=0), jnp.concatenate([q["bs"], q["ks"]], axis=0))
        q["nn"] = jnp.where(strict, -g[0:LANES, 0:LANES], 0.0)
        q["avk"] = jnp.where(strict, g[0:LANES, LANES:], 0.0)
        q["arb"] = jnp.where(incl, g[LANES:, 0:LANES], 0.0)
        q["ark"] = jnp.where(incl, g[LANES:, LANES:], 0.0)
        q["t"] = eye + q["nn"]
        q["q"] = q["nn"]
    for _ in range(5):
        for q in probs:
            q["q"] = _bdot(q["q"], q["q"])
        for q in probs:
            q["t"] = q["t"] + _bdot(q["t"], q["q"])
    for q in probs:
        q["tk"] = _bdot(q["t"], q["kks"])
        q["av"] = _bdot(q["avk"], q["vs"])
    for q in probs:
        q["tav"] = _bdot(q["t"], q["av"])
    for q in probs:
        q["mb"] = _bdot_tn(q["tk"], q["bps"])
        q["nm"] = _bdot_tn(q["vs"], q["kps"]) - _bdot_tn(q["tav"], q["bps"])
        q["r2"] = q["rs"] - _bdot(q["arb"], q["tk"])
        q["y0"] = _bdot(q["ark"], q["vs"]) - _bdot(q["arb"], q["tav"])

    st = [s_sc[pr] for pr in range(N_PAIRS)]
    outs = []
    for q in probs:
        sm = st[q["pr"]]
        ys = _bdot_nt(q["r2"], sm) + q["y0"]
        outs.append(ys[0:CHUNK] + ys[CHUNK:])
        st[q["pr"]] = sm * q["wl"] - _bdot(sm, q["mb"]) + q["nm"]
    for q, yo in zip(probs, outs):
        y_ref[q["off"]:q["off"] + CHUNK, q["ls"]] = yo.astype(y_ref.dtype)
    for pr in range(N_PAIRS):
        s_sc[pr] = st[pr]

    @pl.when(j == nt - 1)
    def _():
        sout_ref[0] = s_sc[...]


def _rwkv(p, vf_src, hv, state_in, prm, nb, rows, drn):
    m = p.shape[0]
    tq = _row_tile(rows, 256)
    nt = rows // tq
    rb = tq // PACK
    vres = vf_src is not None

    def tile(b, j):
        return b * nt + (j if drn == 0 else nt - 1 - j)

    const2 = lambda b, j: (0, 0)
    dir3 = lambda b, j: (drn, 0, 0)
    in_specs = [pl.BlockSpec((tq, RW_IN), lambda b, j: (tile(b, j), 0)),
                pl.BlockSpec((PACK, RW_IN), lambda b, j: (jnp.maximum(tile(b, j) * rb - 1, 0), 0)),
                pl.BlockSpec((PACK, RW_IN),
                             lambda b, j: (jnp.minimum((tile(b, j) + 1) * rb, m // PACK - 1), 0))]
    args = [p, p, p]
    if vres:
        in_specs += [pl.BlockSpec((tq, RW_DIM), lambda b, j: (tile(b, j), 1)),
                     pl.BlockSpec((tq, LANES), lambda b, j: (tile(b, j), 0))]
        args += [vf_src, hv]
    in_specs += [pl.BlockSpec((2, RW_IN), const2),
                 pl.BlockSpec((1, SUBLANES, RW_DIM), dir3),
                 pl.BlockSpec((SUBLANES, RW_DIM), const2),
                 pl.BlockSpec((1, LANES, 2 * RW_DIM), dir3),
                 pl.BlockSpec((1, LANES, 2 * RW_DIM), dir3),
                 pl.BlockSpec((LANES, RW_DIM), const2)]
    args += [prm["tshift"], prm["pd"], prm["ps"], prm["wla_hi"], prm["wla_lo"], prm["wg"]]
    if vres:
        in_specs += [pl.BlockSpec((LANES, RW_DIM), const2), pl.BlockSpec((LANES, RW_DIM), const2)]
        args += [prm["vu_hi"], prm["vu_lo"]]
    sspec = pl.BlockSpec((1, N_PAIRS, LANES, LANES), lambda b, j: (b, 0, 0, 0))
    in_specs += [pl.BlockSpec((LANES, LANES), const2), sspec]
    args += [prm["e128"], state_in]
    return pl.pallas_call(
        functools.partial(_rwkv_kernel, nt=nt, tq=tq, vres=vres, drn=drn),
        grid=(nb, nt),
        in_specs=in_specs,
        out_specs=[pl.BlockSpec((tq, RW_DIM), lambda b, j: (tile(b, j), 0)),
                   pl.BlockSpec((tq, 2 * RW_DIM), lambda b, j: (tile(b, j), 0)),
                   sspec],
        out_shape=[jax.ShapeDtypeStruct((m, RW_DIM), ACT),
                   jax.ShapeDtypeStruct((m, 2 * RW_DIM), ACT),
                   jax.ShapeDtypeStruct((nb, N_PAIRS, LANES, LANES), F32)],
        scratch_shapes=[pltpu.VMEM((8, tq, RW_DIM), F32), pltpu.VMEM((N_PAIRS, LANES, LANES), F32)],
        compiler_params=_cparams("parallel", "arbitrary"),
        name="rwkv7_fwd" if drn == 0 else "rwkv7_bwd",
    )(*args)


def _rwkv_both(p, vf_src, hv, states, prm, nb, rows):
    outs = [_rwkv(p, vf_src, hv, states[d], prm, nb, rows, d) for d in range(2)]
    return [o[0] for o in outs], [o[1] for o in outs], [o[2] for o in outs]


def _merge_kernel(x_ref, y0_ref, y1_ref, a0_ref, a1_ref, ya_ref, g0_ref, g1_ref, g2_ref, g3_ref, ln_ref, e_ref,
                  wbr_ref, wba_ref, wo_ref, mod_ref, o_ref):
    e = e_ref[...]
    y = y0_ref[...].astype(F32) + y1_ref[...].astype(F32)
    mu = _segsum(y, e) * (1.0 / HEAD_DIM)
    yc = y - mu
    var = _segsum(yc * yc, e) * (1.0 / HEAD_DIM)
    yn = yc * lax.rsqrt(var + GN_EPS) * ln_ref[0:1, :] + ln_ref[1:2, :]
    cv = a0_ref[:, 0:RW_DIM].astype(F32) + a1_ref[:, 0:RW_DIM].astype(F32)
    yrw = (yn + cv) * a0_ref[:, RW_DIM:2 * RW_DIM].astype(F32)
    a = _bdot(yrw, wbr_ref[...])
    b = jnp.dot(ya_ref[...], wba_ref[...], preferred_element_type=F32)
    h = a.shape[1] // 2
    mrg = jnp.concatenate(
        [jax.nn.sigmoid(g0_ref[...].astype(F32)) * a[:, :h] + jax.nn.sigmoid(g2_ref[...].astype(F32)) * b[:, :h],
         jax.nn.sigmoid(g1_ref[...].astype(F32)) * a[:, h:] + jax.nn.sigmoid(g3_ref[...].astype(F32)) * b[:, h:]],
        axis=1)
    o_ref[...] = x_ref[...] + mod_ref[0, 2:3, :] * _bdot(mrg, wo_ref[...])


def _merge(x2, ys, auxs, ya, p, ln, e128, wbr, wba, wo, mod, rows):
    m, d = x2.shape
    tm = _row_tile(rows, 512)
    per = rows // tm
    many = mod.shape[0] > 1
    gb = GATE_OFF // 512
    const2 = lambda i: (0, 0)
    in_specs = [pl.BlockSpec((tm, d), lambda i: (i, 0)),
                pl.BlockSpec((tm, RW_DIM), lambda i: (i, 0)),
                pl.BlockSpec((tm, RW_DIM), lambda i: (i, 0)),
                pl.BlockSpec((tm, 2 * RW_DIM), lambda i: (i, 0)),
                pl.BlockSpec((tm, 2 * RW_DIM), lambda i: (i, 0)),
                pl.BlockSpec((tm, ATT_DIM), lambda i: (i, 0))]
    in_specs += [pl.BlockSpec((tm, 512), (lambda i, c=c: (i, gb + c))) for c in range(4)]
    in_specs += [pl.BlockSpec((2, RW_DIM), const2),
                 pl.BlockSpec((LANES, LANES), const2),
                 pl.BlockSpec(wbr.shape, const2),
                 pl.BlockSpec(wba.shape, const2),
                 pl.BlockSpec(wo.shape, const2),
                 pl.BlockSpec((1, 6, d), lambda i: ((i // per) if many else 0, 0, 0))]
    return pl.pallas_call(
        _merge_kernel,
        grid=(m // tm,),
        in_specs=in_specs,
        out_specs=pl.BlockSpec((tm, d), lambda i: (i, 0)),
        out_shape=jax.ShapeDtypeStruct((m, d), F32),
        compiler_params=_cparams("parallel"),
        name="branch_merge",
    )(x2, ys[0], ys[1], auxs[0], auxs[1], ya, p, p, p, p, ln, e128, wbr, wba, wo, mod)


def _ffn_kernel(x_ref, g_ref, mod_ref, wg_ref, wu_ref, wd_ref, o_ref, h_sc, acc_sc, *, nf):
    f = pl.program_id(1)

    @pl.when(f == 0)
    def _():
        h = _norm_mod(x_ref[...], g_ref[...], mod_ref[0, 3:4, :], mod_ref[0, 4:5, :])
        h_sc[...] = h.astype(BF16)
        acc_sc[...] = jnp.zeros_like(acc_sc)

    h = h_sc[...]
    a = jnp.dot(h, wg_ref[...], preferred_element_type=F32)
    u = jnp.dot(h, wu_ref[...], preferred_element_type=F32)
    act = (a * jax.nn.sigmoid(a)) * u
    acc_sc[...] += jnp.dot(act.astype(BF16), wd_ref[...], preferred_element_type=F32)

    @pl.when(f == nf - 1)
    def _():
        o_ref[...] = x_ref[...] + mod_ref[0, 5:6, :] * acc_sc[...]


def _ffn(x2, gain, mod, wg, wu, wd, rows):
    m, d = x2.shape
    dff = wg.shape[1]
    tm = _row_tile(rows, 512)
    per = rows // tm
    many = mod.shape[0] > 1
    tf = dff // 2 if (dff // 2) % LANES == 0 else dff
    nf = dff // tf
    return pl.pallas_call(
        functools.partial(_ffn_kernel, nf=nf),
        grid=(m // tm, nf),
        in_specs=[pl.BlockSpec((tm, d), lambda i, f: (i, 0)),
                  pl.BlockSpec((1, d), lambda i, f: (0, 0)),
                  pl.BlockSpec((1, 6, d), lambda i, f: ((i // per) if many else 0, 0, 0)),
                  pl.BlockSpec((d, tf), lambda i, f: (0, f)),
                  pl.BlockSpec((d, tf), lambda i, f: (0, f)),
                  pl.BlockSpec((tf, d), lambda i, f: (f, 0))],
        out_specs=pl.BlockSpec((tm, d), lambda i, f: (i, 0)),
        out_shape=jax.ShapeDtypeStruct((m, d), F32),
        scratch_shapes=[pltpu.VMEM((tm, d), BF16), pltpu.VMEM((tm, d), F32)],
        compiler_params=_cparams("parallel", "arbitrary"),
        name="ffn_swiglu",
    )(x2, gain.reshape(1, d), mod, wg, wu, wd)


def _moe_kernel(nblk_ref, x_ref, h_ref, we_ref, rank_ref, rankt_ref, mod_ref, wg_ref, wu_ref, wd_ref, o_ref,
                acc_sc, hs_sc, os_sc, *, ne, nf):
    i = pl.program_id(0)
    ex = pl.program_id(1)
    f = pl.program_id(2)
    tt = h_ref.shape[0]
    nblk = nblk_ref[i * ne + ex]
    lane = lax.broadcasted_iota(jnp.int32, (tt, LANES), 1)

    @pl.when((ex == 0) & (f == 0))
    def _():
        acc_sc[...] = jnp.zeros_like(acc_sc)

    @pl.when(f == 0)
    def _():
        rrow = rankt_ref[0, pl.ds(ex, 1), :]
        slot = lax.broadcasted_iota(jnp.int32, (MOE_BLK, tt), 0).astype(F32)

        def gather(b, carry):
            sel = ((rrow - (b * MOE_BLK).astype(F32)) == slot).astype(BF16)
            r0 = pl.multiple_of(b * MOE_BLK, MOE_BLK)
            hs_sc[pl.ds(r0, MOE_BLK), :] = jnp.dot(sel, h_ref[...], preferred_element_type=F32).astype(BF16)
            return carry

        lax.fori_loop(0, nblk, gather, 0)

    def ffn(b, carry):
        r0 = pl.multiple_of(b * MOE_BLK, MOE_BLK)
        hb = hs_sc[pl.ds(r0, MOE_BLK), :]
        a = jnp.dot(hb, wg_ref[0], preferred_element_type=F32)
        u = jnp.dot(hb, wu_ref[0], preferred_element_type=F32)
        act = ((a * jax.nn.sigmoid(a)) * u).astype(BF16)
        part = jnp.dot(act, wd_ref[0], preferred_element_type=F32)

        @pl.when(f == 0)
        def _():
            os_sc[pl.ds(r0, MOE_BLK), :] = part

        @pl.when(f > 0)
        def _():
            os_sc[pl.ds(r0, MOE_BLK), :] += part

        return carry

    lax.fori_loop(0, nblk, ffn, 0)

    @pl.when(f == nf - 1)
    def _():
        rcol = jnp.sum(jnp.where(lane == ex, rank_ref[...], 0.0), axis=1, keepdims=True)
        wcol = jnp.sum(jnp.where(lane == ex, we_ref[...], 0.0), axis=1, keepdims=True)
        slot = lane.astype(F32)

        def scatter(b, carry):
            sel = ((rcol - (b * MOE_BLK).astype(F32)) == slot).astype(BF16)
            r0 = pl.multiple_of(b * MOE_BLK, MOE_BLK)
            ob = os_sc[pl.ds(r0, MOE_BLK), :].astype(BF16)
            acc_sc[...] += wcol * jnp.dot(sel, ob, preferred_element_type=F32)
            return carry

        lax.fori_loop(0, nblk, scatter, 0)

    @pl.when((ex == ne - 1) & (f == nf - 1))
    def _():
        o_ref[...] = x_ref[...] + mod_ref[0, 5:6, :] * acc_sc[...]


def _route_kernel(x_ref, g_ref, mod_ref, rt_ref, h_ref, we_ref, rank_ref, rankt_ref, cnt_ref, *, ne):
    tt = x_ref.shape[0]
    h = _norm_mod(x_ref[...], g_ref[...], mod_ref[0, 3:4, :], mod_ref[0, 4:5, :])
    h_ref[...] = h.astype(BF16)
    lane = lax.broadcasted_iota(jnp.int32, (tt, LANES), 1)
    logits = jnp.dot(h, rt_ref[...], precision=HIGHEST, preferred_element_type=F32)
    lg = jnp.where(lane < ne, logits, NEG_INF)
    t1 = jnp.max(lg, axis=1, keepdims=True)
    i1 = jnp.min(jnp.where(lg == t1, lane, LANES), axis=1, keepdims=True)
    lg2 = jnp.where(lane == i1, NEG_INF, lg)
    t2 = jnp.max(lg2, axis=1, keepdims=True)
    i2 = jnp.min(jnp.where(lg2 == t2, lane, LANES), axis=1, keepdims=True)
    e2 = jnp.exp(t2 - t1)
    den = 1.0 + e2
    we_ref[...] = jnp.where(lane == i1, 1.0 / den, 0.0) + jnp.where(lane == i2, e2 / den, 0.0)
    sel = (lane == i1) | (lane == i2)
    self32 = sel.astype(F32)
    before = (lax.broadcasted_iota(jnp.int32, (tt, tt), 1) < lax.broadcasted_iota(jnp.int32, (tt, tt), 0))
    rank = jnp.dot(before.astype(BF16), sel.astype(BF16), preferred_element_type=F32)
    rank = jnp.where(sel, rank, -1.0)
    rank_ref[...] = rank
    rankt_ref[0] = rank.T[0:SUBLANES, :]
    cnt_ref[0] = jnp.broadcast_to(jnp.sum(self32, axis=0, keepdims=True), (SUBLANES, LANES))


def _moe(x2, gain, mod, router_pad, wg, wu, wd, rows):
    m, d = x2.shape
    ne, _, dff = wg.shape
    assert ne <= SUBLANES
    tt = _row_tile(rows, MOE_TILE)
    nti = m // tt
    per = rows // tt
    many = mod.shape[0] > 1
    modspec = lambda nargs: pl.BlockSpec(
        (1, 6, d), (lambda i: ((i // per) if many else 0, 0, 0)) if nargs == 1
        else (lambda i, e, f, nb: ((i // per) if many else 0, 0, 0)))
    h, we, rank, rankt, cnt = pl.pallas_call(
        functools.partial(_route_kernel, ne=ne),
        grid=(nti,),
        in_specs=[pl.BlockSpec((tt, d), lambda i: (i, 0)),
                  pl.BlockSpec((1, d), lambda i: (0, 0)),
                  modspec(1),
                  pl.BlockSpec((d, LANES), lambda i: (0, 0))],
        out_specs=[pl.BlockSpec((tt, d), lambda i: (i, 0)),
                   pl.BlockSpec((tt, LANES), lambda i: (i, 0)),
                   pl.BlockSpec((tt, LANES), lambda i: (i, 0)),
                   pl.BlockSpec((1, SUBLANES, tt), lambda i: (i, 0, 0)),
                   pl.BlockSpec((1, SUBLANES, LANES), lambda i: (i, 0, 0))],
        out_shape=[jax.ShapeDtypeStruct((m, d), BF16),
                   jax.ShapeDtypeStruct((m, LANES), F32),
                   jax.ShapeDtypeStruct((m, LANES), F32),
                   jax.ShapeDtypeStruct((nti, SUBLANES, tt), F32),
                   jax.ShapeDtypeStruct((nti, SUBLANES, LANES), F32)],
        compiler_params=_cparams("parallel"),
        name="moe_route",
    )(x2, gain.reshape(1, d), mod, router_pad)
    nblk = ((cnt[:, 0, :ne].astype(jnp.int32) + (MOE_BLK - 1)) // MOE_BLK).reshape(-1)

    tf = dff // 4 if (dff // 4) % LANES == 0 else dff
    nf = dff // tf
    grid_spec = pltpu.PrefetchScalarGridSpec(
        num_scalar_prefetch=1,
        grid=(nti, ne, nf),
        in_specs=[pl.BlockSpec((tt, d), lambda i, e, f, nb: (i, 0)),
                  pl.BlockSpec((tt, d), lambda i, e, f, nb: (i, 0)),
                  pl.BlockSpec((tt, LANES), lambda i, e, f, nb: (i, 0)),
                  pl.BlockSpec((tt, LANES), lambda i, e, f, nb: (i, 0)),
                  pl.BlockSpec((1, SUBLANES, tt), lambda i, e, f, nb: (i, 0, 0)),
                  modspec(4),
                  pl.BlockSpec((1, d, tf), lambda i, e, f, nb: (e, 0, f)),
                  pl.BlockSpec((1, d, tf), lambda i, e, f, nb: (e, 0, f)),
                  pl.BlockSpec((1, tf, d), lambda i, e, f, nb: (e, f, 0))],
        out_specs=pl.BlockSpec((tt, d), lambda i, e, f, nb: (i, 0)),
        scratch_shapes=[pltpu.VMEM((tt, d), F32), pltpu.VMEM((tt, d), BF16), pltpu.VMEM((tt, d), F32)])
    return pl.pallas_call(
        functools.partial(_moe_kernel, ne=ne, nf=nf),
        grid_spec=grid_spec,
        out_shape=jax.ShapeDtypeStruct((m, d), F32),
        compiler_params=_cparams("parallel", "arbitrary", "arbitrary"),
        name="moe_experts",
    )(nblk, x2, h, we, rank, rankt, mod, wg, wu, wd)


def _rope_table(t):
    rows = t // GRID_W
    row = jnp.repeat(jnp.arange(rows, dtype=F32), GRID_W)
    col = jnp.tile(jnp.arange(GRID_W, dtype=F32), rows)
    half = HEAD_DIM // 2
    inv_freq = ROPE_THETA ** (-jnp.arange(0, half, 2, dtype=F32) / half)
    ar = row[:, None] * inv_freq
    ac = col[:, None] * inv_freq
    ang = jnp.concatenate([ar, ar, ac, ac] * 2, axis=-1)
    return jnp.stack([jnp.cos(ang), jnp.sin(ang)])


def _pad_rows(a, rows):
    return jnp.zeros((rows,) + a.shape[1:], a.dtype).at[:a.shape[0]].set(a)


def _hi_lo(w):
    hi = w.astype(BF16)
    return hi, (w - hi.astype(F32)).astype(BF16)


def kernel(x, c, ctx, c_ctx, norm1, norm2, w_mod, b_mod, w_in, tshift, w0, w_decay_up, a0, w_iclr_up, v0, w_vres_down, w_vres_up, w_gate_up, k_k, k_a, r_k, lnx_g, lnx_b, q_norm, k_norm, sink, w_br_rwkv, w_br_attn, w_out, ffn_gate, ffn_up, ffn_down, router, moe_gate, moe_up, moe_down):
    nb, t, d = x.shape
    cn = ctx.shape[1]
    depth = w_in.shape[0]
    mods = _modulation(c, c_ctx, w_mod, b_mod)
    cs = _rope_table(t)
    e128 = jnp.kron(jnp.eye(2, dtype=F32), jnp.ones((HEAD_DIM, HEAD_DIM), F32)).astype(BF16)
    xl = x.reshape(nb * t, d)
    xc = ctx.reshape(nb * cn, d)
    vf_l = vf_c = None
    for layer in range(depth):
        last = layer == depth - 1
        mod_l = mods[layer, :nb].reshape(nb, 6, d)
        mod_c = mods[layer, nb:nb + 1].reshape(1, 6, d)
        w_in_b = w_in[layer].astype(BF16)
        vd_b = None
        prm = {"tshift": tshift[layer], "e128": e128, "wg": w_gate_up[layer].astype(BF16)}
        prm["pd"] = jnp.stack(
            [_pad_rows(jnp.stack([w0[layer, dd], a0[layer, dd]]), SUBLANES) for dd in range(2)])
        shared = [k_k[layer], k_a[layer], r_k[layer].reshape(RW_DIM)]
        if layer > 0:
            shared.append(v0[layer - 1])
            vd_b = _pad_rows(w_vres_down[layer - 1].T, LANES).T.astype(BF16)
            prm["vu_hi"], prm["vu_lo"] = _hi_lo(_pad_rows(w_vres_up[layer - 1], LANES))
        prm["ps"] = _pad_rows(jnp.stack(shared), SUBLANES)
        wla = jnp.zeros((2, LANES, 2 * RW_DIM), F32)
        wla = wla.at[:, :HEAD_DIM, :RW_DIM].set(w_decay_up[layer]).at[:, HEAD_DIM:, RW_DIM:].set(w_iclr_up[layer])
        prm["wla_hi"], prm["wla_lo"] = _hi_lo(wla)

        p_l, hv_l = _inproj(xl, norm1[layer], mod_l, w_in_b, vd_b, t)
        p_c, hv_c = _inproj(xc, norm1[layer], mod_c, w_in_b, vd_b, cn)

        s0 = jnp.zeros((nb, N_PAIRS, LANES, LANES), F32)
        y_c, aux_c, s_ctx = _rwkv_both(p_c, vf_c, hv_c, [s0, s0], prm, nb, cn)
        y_l, aux_l, _ = _rwkv_both(p_l, vf_l, hv_l, s_ctx, prm, nb, t)
        if layer == 0:
            vf_l, vf_c = aux_l[1], aux_c[1]

        qg = jnp.tile(q_norm[layer], 2).reshape(1, LANES)
        kg = jnp.tile(k_norm[layer], 2).reshape(1, LANES)
        qn_l, kn_l, vb_l = _qk_prep(p_l, cs, qg, kg, e128, t, True)
        qn_c, kn_c, vb_c = _qk_prep(p_c, None, qg, kg, e128, cn, False)
        sk = sink[layer].reshape(1, -1)
        ya_l = _attention(qn_l, kn_l, vb_l, kn_c, vb_c, sk, nb, t, cn, True)

        ln = jnp.stack([lnx_g[layer], lnx_b[layer]])
        wbr = w_br_rwkv[layer].astype(BF16)
        wba = w_br_attn[layer].astype(BF16)
        wo = w_out[layer].astype(BF16)
        xl = _merge(xl, y_l, aux_l, ya_l, p_l, ln, e128, wbr, wba, wo, mod_l, t)
        if not last:
            ya_c = _attention(qn_c, None, None, kn_c, vb_c, sk, nb, cn, cn, False)
            xc = _merge(xc, y_c, aux_c, ya_c, p_c, ln, e128, wbr, wba, wo, mod_c, cn)

        i = layer // 2
        if layer % 2 == 0:
            fw = (ffn_gate[i].astype(BF16), ffn_up[i].astype(BF16), ffn_down[i].astype(BF16))
            xl = _ffn(xl, norm2[layer], mod_l, *fw, t)
            if not last:
                xc = _ffn(xc, norm2[layer], mod_c, *fw, cn)
        else:
            rt = _pad_rows(router[i].T, LANES).T
            mw = (moe_gate[i].astype(BF16), moe_up[i].astype(BF16), moe_down[i].astype(BF16))
            xl = _moe(xl, norm2[layer], mod_l, rt, *mw, t)
            if not last:
                xc = _moe(xc, norm2[layer], mod_c, rt, *mw, cn)
    return xl.reshape(nb, t, d)
```

```python
import functools

import jax
import jax.numpy as jnp
from jax import lax
from jax.experimental import pallas as pl
from jax.experimental.pallas import tpu as pltpu

F32 = jnp.float32
BF16 = jnp.bfloat16
HIGHEST = lax.Precision.HIGHEST

LANES = 128
SUBLANES = 8
PACK = 16
ACT = jnp.bfloat16
HEAD_DIM = 64
RW_DIM = 512
ATT_DIM = 512
KV_DIM = 128
RW_IN = 1792
ATT_OFF = RW_IN
GATE_OFF = RW_IN + ATT_DIM + 2 * KV_DIM
N_EXPERTS = 8
CHUNK = 64
N_PAIRS = RW_DIM // LANES
MOE_TILE = 1024
MOE_BLK = 128
MOE_FT = 512
ATT_SCALE = HEAD_DIM ** -0.5
ROPE_THETA = 10000.0
GRID_W = 64
RMS_EPS = 1e-6
GN_EPS = 64e-5
NEG_INF = -1e30
DECAY_SCALE = 0.6065306597126334
VMEM_LIMIT = 48 * 1024 * 1024


def _cparams(*sem):
    return pltpu.CompilerParams(dimension_semantics=sem, vmem_limit_bytes=VMEM_LIMIT)


def _row_tile(rows, cap):
    t = cap
    while rows % t:
        t //= 2
    return t


def _bdot(a, b):
    return jnp.dot(a.astype(BF16), b.astype(BF16), preferred_element_type=F32)


def _bdot_nt(a, b):
    return lax.dot_general(a.astype(BF16), b.astype(BF16), (((1,), (1,)), ((), ())),
                           preferred_element_type=F32)


def _bdot_tn(a, b):
    return lax.dot_general(a.astype(BF16), b.astype(BF16), (((0,), (0,)), ((), ())),
                           preferred_element_type=F32)


def _split2(x):
    hi = x.astype(BF16)
    lo = (x - hi.astype(F32)).astype(BF16)
    return hi, lo


def _split3(x):
    hi = x.astype(BF16)
    r = x - hi.astype(F32)
    mid = r.astype(BF16)
    lo = (r - mid.astype(F32)).astype(BF16)
    return hi, mid, lo


def _dot3(a, w_hi, w_lo):
    a_hi, a_lo = _split2(a)
    return (jnp.dot(a_hi, w_hi, preferred_element_type=F32)
            + (jnp.dot(a_lo, w_hi, preferred_element_type=F32)
               + jnp.dot(a_hi, w_lo, preferred_element_type=F32)))


def _segsum(x, e):
    outs = []
    for s in range(x.shape[1] // LANES):
        hi, lo = _split2(x[:, s * LANES:(s + 1) * LANES])
        outs.append(jnp.dot(hi, e, preferred_element_type=F32)
                    + jnp.dot(lo, e, preferred_element_type=F32))
    return outs[0] if len(outs) == 1 else jnp.concatenate(outs, axis=1)


def _norm_mod(x, gain, shift, scale):
    y = x * lax.rsqrt(jnp.mean(x * x, axis=-1, keepdims=True) + RMS_EPS) * gain
    return y * (1.0 + scale) + shift


def _mod_kernel(c_ref, w_ref, b_ref, o_ref):
    c = c_ref[...]
    s = c * jax.nn.sigmoid(c)
    o_ref[0] = jnp.dot(s, w_ref[0], precision=HIGHEST, preferred_element_type=F32) + b_ref[0]


def _modulation(c, c_ctx, w_mod, b_mod):
    depth, d, n6 = w_mod.shape
    nb = c.shape[0]
    rm = -(-(nb + 1) // SUBLANES) * SUBLANES
    cc = jnp.zeros((rm, d), F32).at[:nb].set(c).at[nb].set(c_ctx)
    tn = _row_tile(n6, 1536)
    return pl.pallas_call(
        _mod_kernel,
        grid=(depth, n6 // tn),
        in_specs=[pl.BlockSpec((rm, d), lambda l, n: (0, 0)),
                  pl.BlockSpec((1, d, tn), lambda l, n: (l, 0, n)),
                  pl.BlockSpec((1, 1, tn), lambda l, n: (l, 0, n))],
        out_specs=pl.BlockSpec((1, rm, tn), lambda l, n: (l, 0, n)),
        out_shape=jax.ShapeDtypeStruct((depth, rm, n6), F32),
        compiler_params=_cparams("parallel", "parallel"),
        name="adaln_mod",
    )(cc, w_mod, b_mod.reshape(depth, 1, n6))


def _inproj_kernel(*refs, vres):
    if vres:
        x_ref, g_ref, mod_ref, w_ref, vd_ref, o_ref, hv_ref, h_sc = refs
    else:
        x_ref, g_ref, mod_ref, w_ref, o_ref, h_sc = refs

    @pl.when(pl.program_id(1) == 0)
    def _():
        h = _norm_mod(x_ref[...], g_ref[...], mod_ref[0, 0:1, :], mod_ref[0, 1:2, :])
        hb = h.astype(BF16)
        h_sc[...] = hb
        if vres:
            hv_ref[...] = jnp.dot(hb, vd_ref[...], preferred_element_type=F32)

    o_ref[...] = jnp.dot(h_sc[...], w_ref[...], preferred_element_type=F32).astype(o_ref.dtype)


def _inproj(x2, gain, mod, w_b, vd_b, rows):
    m, d = x2.shape
    n = w_b.shape[1]
    tm = _row_tile(rows, 1024)
    tn = _row_tile(n, 1536)
    per = rows // tm
    many = mod.shape[0] > 1
    vres = vd_b is not None
    in_specs = [pl.BlockSpec((tm, d), lambda i, j: (i, 0)),
                pl.BlockSpec((1, d), lambda i, j: (0, 0)),
                pl.BlockSpec((1, 6, d), lambda i, j: ((i // per) if many else 0, 0, 0)),
                pl.BlockSpec((d, tn), lambda i, j: (0, j))]
    args = [x2, gain.reshape(1, d), mod, w_b]
    out_specs = [pl.BlockSpec((tm, tn), lambda i, j: (i, j))]
    out_shape = [jax.ShapeDtypeStruct((m, n), ACT)]
    if vres:
        in_specs.append(pl.BlockSpec((d, LANES), lambda i, j: (0, 0)))
        args.append(vd_b)
        out_specs.append(pl.BlockSpec((tm, LANES), lambda i, j: (i, 0)))
        out_shape.append(jax.ShapeDtypeStruct((m, LANES), F32))
    outs = pl.pallas_call(
        functools.partial(_inproj_kernel, vres=vres),
        grid=(m // tm, n // tn),
        in_specs=in_specs, out_specs=out_specs, out_shape=out_shape,
        scratch_shapes=[pltpu.VMEM((tm, d), BF16)],
        compiler_params=_cparams("parallel", "arbitrary"),
        name="in_proj",
    )(*args)
    return (outs[0], outs[1]) if vres else (outs[0], None)


def _qkprep_kernel(*refs, rope):
    if rope:
        q0_ref, q1_ref, k_ref, v_ref, cs_ref, qg_ref, kg_ref, e_ref, qn_ref, kn_ref, vb_ref = refs
    else:
        q0_ref, q1_ref, k_ref, v_ref, qg_ref, kg_ref, e_ref, qn_ref, kn_ref, vb_ref = refs
    e = e_ref[...]
    lane = lax.broadcasted_iota(jnp.int32, (1, LANES), 1)
    first = (lane & 31) < 16

    def norm_rope(u, gain):
        y = u * lax.rsqrt(_segsum(u * u, e) * (1.0 / HEAD_DIM) + RMS_EPS) * gain
        if rope:
            rot = jnp.where(first, -pltpu.roll(y, LANES - 16, 1), pltpu.roll(y, 16, 1))
            y = y * cs_ref[0] + rot * cs_ref[1]
        return y

    qg = qg_ref[...]
    for s, ref in enumerate((q0_ref, q1_ref)):
        for t in range(2):
            u = ref[:, t * LANES:(t + 1) * LANES].astype(F32)
            c0 = (2 * s + t) * LANES
            qn_ref[:, c0:c0 + LANES] = (norm_rope(u, qg) * ATT_SCALE).astype(BF16)
    kn_ref[...] = norm_rope(k_ref[...].astype(F32), kg_ref[...]).astype(BF16)
    vb_ref[...] = v_ref[...].astype(BF16)


def _qk_prep(p, cs, qg, kg, e128, rows, rope):
    m = p.shape[0]
    tm = _row_tile(rows, 512)
    per = rows // tm
    qb = ATT_OFF // 256
    kb = (ATT_OFF + ATT_DIM) // LANES
    in_specs = [pl.BlockSpec((tm, 256), lambda i: (i, qb)),
                pl.BlockSpec((tm, 256), lambda i: (i, qb + 1)),
                pl.BlockSpec((tm, LANES), lambda i: (i, kb)),
                pl.BlockSpec((tm, LANES), lambda i: (i, kb + 1))]
    args = [p, p, p, p]
    if rope:
        in_specs.append(pl.BlockSpec((2, tm, LANES), lambda i: (0, i % per, 0)))
        args.append(cs)
    in_specs += [pl.BlockSpec((1, LANES), lambda i: (0, 0)),
                 pl.BlockSpec((1, LANES), lambda i: (0, 0)),
                 pl.BlockSpec((LANES, LANES), lambda i: (0, 0))]
    args += [qg, kg, e128]
    return pl.pallas_call(
        functools.partial(_qkprep_kernel, rope=rope),
        grid=(m // tm,),
        in_specs=in_specs,
        out_specs=[pl.BlockSpec((tm, ATT_DIM), lambda i: (i, 0)),
                   pl.BlockSpec((tm, LANES), lambda i: (i, 0)),
                   pl.BlockSpec((tm, LANES), lambda i: (i, 0))],
        out_shape=[jax.ShapeDtypeStruct((m, ATT_DIM), BF16),
                   jax.ShapeDtypeStruct((m, KV_DIM), BF16),
                   jax.ShapeDtypeStruct((m, KV_DIM), BF16)],
        compiler_params=_cparams("parallel"),
        name="qk_prep",
    )(*args)


def _attn_kernel(*refs, nq, local):
    if local:
        sink_ref, q_ref, kp_ref, kc_ref, kn_ref, vp_ref, vc_ref, vn_ref, kx_ref, vx_ref, o_ref = refs
        k = jnp.concatenate([kp_ref[...], kc_ref[...], kn_ref[...], kx_ref[...]], axis=0).astype(F32)
        v = jnp.concatenate([vp_ref[...], vc_ref[...], vn_ref[...], vx_ref[...]], axis=0).astype(F32)
    else:
        sink_ref, q_ref, kx_ref, vx_ref, o_ref = refs
        k = kx_ref[...].astype(F32)
        v = vx_ref[...].astype(F32)
    i = pl.program_id(1)
    bq = q_ref.shape[0]
    low = lax.broadcasted_iota(jnp.int32, (1, LANES), 1) < HEAD_DIM
    ksw = pltpu.roll(k, HEAD_DIM, 1)
    vsw = pltpu.roll(v, HEAD_DIM, 1)
    row = lax.broadcasted_iota(jnp.int32, (4 * bq, 1), 0)
    if local:
        qo = lax.broadcasted_iota(jnp.int32, (4 * bq, 3 * bq), 0) & (bq - 1)
        kcol = lax.broadcasted_iota(jnp.int32, (4 * bq, 3 * bq), 1)
        ko = kcol & (bq - 1)
        kblk = kcol // bq
        ok = ((kblk == 1)
              | ((kblk == 0) & (ko >= qo) & (i > 0))
              | ((kblk == 2) & (ko <= qo) & (i < nq - 1)))
    for g in range(2):
        kb = (jnp.where(low, k, ksw) if g == 0 else jnp.where(low, ksw, k)).astype(BF16)
        vb = (jnp.where(low, v, vsw) if g == 0 else jnp.where(low, vsw, v)).astype(BF16)
        qa = q_ref[:, (2 * g) * LANES:(2 * g + 1) * LANES].astype(F32)
        qb = q_ref[:, (2 * g + 1) * LANES:(2 * g + 2) * LANES].astype(F32)
        qs = jnp.concatenate([jnp.where(low, qa, 0.0), jnp.where(low, 0.0, qa),
                              jnp.where(low, qb, 0.0), jnp.where(low, 0.0, qb)], axis=0).astype(BF16)
        s = lax.dot_general(qs, kb, (((1,), (1,)), ((), ())), preferred_element_type=F32)
        if local:
            s = jnp.concatenate([jnp.where(ok, s[:, :3 * bq], NEG_INF), s[:, 3 * bq:]], axis=1)
        sk = jnp.where(row < bq, sink_ref[0, 4 * g],
                       jnp.where(row < 2 * bq, sink_ref[0, 4 * g + 1],
                                 jnp.where(row < 3 * bq, sink_ref[0, 4 * g + 2], sink_ref[0, 4 * g + 3])))
        mx = jnp.maximum(jnp.max(s, axis=1, keepdims=True), sk)
        ex = jnp.exp(s - mx)
        den = jnp.sum(ex, axis=1, keepdims=True) + jnp.exp(sk - mx)
        pr = (ex / den).astype(BF16)
        o = jnp.dot(pr, vb, preferred_element_type=F32)
        o_ref[:, (2 * g) * LANES:(2 * g + 1) * LANES] = jnp.where(low, o[0:bq], o[bq:2 * bq]).astype(o_ref.dtype)
        o_ref[:, (2 * g + 1) * LANES:(2 * g + 2) * LANES] = jnp.where(
            low, o[2 * bq:3 * bq], o[3 * bq:4 * bq]).astype(o_ref.dtype)


def _attention(qn, kn, vb, kctx, vctx, sink, nb, rows, cn, local):
    m = qn.shape[0]
    bq = 128
    nq = rows // bq
    smem = pl.BlockSpec(memory_space=pltpu.SMEM)
    qspec = pl.BlockSpec((bq, ATT_DIM), lambda b, i: (b * nq + i, 0))
    cspec = pl.BlockSpec((cn, KV_DIM), lambda b, i: (b, 0))
    if local:
        prv = pl.BlockSpec((bq, KV_DIM), lambda b, i: (b * nq + jnp.maximum(i - 1, 0), 0))
        cur = pl.BlockSpec((bq, KV_DIM), lambda b, i: (b * nq + i, 0))
        nxt = pl.BlockSpec((bq, KV_DIM), lambda b, i: (b * nq + jnp.minimum(i + 1, nq - 1), 0))
        in_specs = [smem, qspec, prv, cur, nxt, prv, cur, nxt, cspec, cspec]
        args = [sink, qn, kn, kn, kn, vb, vb, vb, kctx, vctx]
    else:
        in_specs = [smem, qspec, cspec, cspec]
        args = [sink, qn, kctx, vctx]
    return pl.pallas_call(
        functools.partial(_attn_kernel, nq=nq, local=local),
        grid=(nb, nq),
        in_specs=in_specs,
        out_specs=pl.BlockSpec((bq, ATT_DIM), lambda b, i: (b * nq + i, 0)),
        out_shape=jax.ShapeDtypeStruct((m, ATT_DIM), BF16),
        compiler_params=_cparams("parallel", "parallel"),
        name="window_attn" if local else "ctx_attn",
    )(*args)


def _rwkv_kernel(*refs, nt, tq, vres, drn):
    it = iter(refs)
    p_ref, pv_ref, nx_ref = next(it), next(it), next(it)
    if vres:
        vf_ref, hv_ref = next(it), next(it)
    ts_ref, pd_ref, ps_ref, wlh_ref, wll_ref, wg_ref = (next(it) for _ in range(6))
    if vres:
        vuh_ref, vul_ref = next(it), next(it)
    e_ref, sin_ref = next(it), next(it)
    y_ref, aux_ref, sout_ref = next(it), next(it), next(it)
    ops_sc, s_sc = next(it), next(it)

    j = pl.program_id(1)
    sgn = 1 - 2 * drn
    tile = j if drn == 0 else nt - 1 - j
    nch = tq // CHUNK

    @pl.when(j == 0)
    def _():
        s_sc[...] = sin_ref[0]

    rw = p_ref[...].astype(F32)
    row = lax.broadcasted_iota(jnp.int32, (tq, 1), 0)
    prev_row = jnp.where(tile > 0, pv_ref[PACK - 1:PACK, :].astype(F32), 0.0)
    next_row = jnp.where(tile < nt - 1, nx_ref[0:1, :].astype(F32), 0.0)
    prv = jnp.where(row == 0, prev_row, pltpu.roll(rw, 1, 0))
    nxt = jnp.where(row == tq - 1, next_row, pltpu.roll(rw, tq - 1, 0))
    xs = rw + ts_ref[0:1, :] * (prv - rw) + ts_ref[1:2, :] * (nxt - rw)

    e = e_ref[...]
    r = xs[:, 0:RW_DIM]
    k = xs[:, RW_DIM:2 * RW_DIM]
    v = xs[:, 2 * RW_DIM:3 * RW_DIM]
    la = xs[:, 3 * RW_DIM:3 * RW_DIM + LANES]
    lane = lax.broadcasted_iota(jnp.int32, (1, LANES), 1)
    low = lane < HEAD_DIM
    la = jnp.where(low, jnp.tanh(la), la)
    lora = _dot3(la, wlh_ref[0], wll_ref[0])
    logw = -DECAY_SCALE * jax.nn.sigmoid(pd_ref[0, 0:1, :] + lora[:, 0:RW_DIM])
    iclr = jax.nn.sigmoid(pd_ref[0, 1:2, :] + lora[:, RW_DIM:2 * RW_DIM])
    if vres:
        mix = jax.nn.sigmoid(ps_ref[3:4, :] + _dot3(hv_ref[...], vuh_ref[...], vul_ref[...]))
        v = v + (vf_ref[...].astype(F32) - v) * mix
    kk = k * ps_ref[0:1, :]
    kk = kk * lax.rsqrt(_segsum(kk * kk, e) + 1e-12)
    kh = k * (1.0 + (iclr - 1.0) * ps_ref[1:2, :])
    bb = kk * iclr
    aux_ref[:, 0:RW_DIM] = (_segsum(r * kh * ps_ref[2:3, :], e) * v).astype(aux_ref.dtype)
    if drn == 0:
        glo = xs[:, 3 * RW_DIM + LANES:RW_IN]
        aux_ref[:, RW_DIM:2 * RW_DIM] = _bdot(jax.nn.sigmoid(glo), wg_ref[...]).astype(aux_ref.dtype)
    else:
        aux_ref[:, RW_DIM:2 * RW_DIM] = v.astype(aux_ref.dtype)

    ti = lax.broadcasted_iota(jnp.int32, (tq, tq), 0)
    si = lax.broadcasted_iota(jnp.int32, (tq, tq), 1)
    same = (ti // CHUNK) == (si // CHUNK)
    tri = (same & (((si - ti) * sgn) <= 0)).astype(BF16)
    ones = same.astype(BF16)
    cin = jnp.zeros_like(logw)
    ctot = jnp.zeros_like(logw)
    for part in _split3(logw):
        cin = cin + jnp.dot(tri, part, preferred_element_type=F32)
        ctot = ctot + jnp.dot(ones, part, preferred_element_type=F32)
    e_neg = jnp.exp(-cin)
    e_rem = jnp.exp(ctot - cin)
    ops_sc[0] = kk * jnp.exp(cin - logw)
    ops_sc[1] = r * jnp.exp(cin)
    ops_sc[2] = bb * e_neg
    ops_sc[3] = kh * e_neg
    ops_sc[4] = kh * e_rem
    ops_sc[5] = bb * e_rem
    ops_sc[6] = v
    ops_sc[7] = jnp.exp(ctot)

    r2 = lax.broadcasted_iota(jnp.int32, (LANES, LANES), 0)
    c2 = lax.broadcasted_iota(jnp.int32, (LANES, LANES), 1)
    rel = ((c2 & (CHUNK - 1)) - (r2 & (CHUNK - 1))) * sgn
    strict = rel < 0
    incl = rel <= 0
    eye = (r2 == c2).astype(F32)

    def stack(x):
        return jnp.concatenate([jnp.where(low, x, 0.0), jnp.where(low, 0.0, x)], axis=0)

    probs = []
    for ii in range(nch):
        off = (ii if drn == 0 else nch - 1 - ii) * CHUNK
        for pr in range(N_PAIRS):
            ls = slice(pr * LANES, (pr + 1) * LANES)
            q = {"off": off, "ls": ls, "pr": pr}
            for n, name in enumerate(("kks", "rs", "bs", "ks", "kps", "bps", "vs")):
                q[name] = stack(ops_sc[n, off:off + CHUNK, ls])
            q["wl"] = ops_sc[7, off:off + 1, ls]
            probs.append(q)
    for q in probs:
        g = _bdot_nt(jnp.concatenate([q["kks"], q["rs"]], axis=0), jnp.concatenate([q["bs"], q["ks"]], axis=0))
        q["nn"] = jnp.where(strict, -g[0:LANES, 0:LANES], 0.0)
        q["avk"] = jnp.where(strict, g[0:LANES, LANES:], 0.0)
        q["arb"] = jnp.where(incl, g[LANES:, 0:LANES], 0.0)
        q["ark"] = jnp.where(incl, g[LANES:, LANES:], 0.0)
        q["t"] = eye + q["nn"]
        q["q"] = q["nn"]
    for _ in range(5):
        for q in probs:
            q["q"] = _bdot(q["q"], q["q"])
        for q in probs:
            q["t"] = q["t"] + _bdot(q["t"], q["q"])
    for q in probs:
        q["tk"] = _bdot(q["t"], q["kks"])
        q["av"] = _bdot(q["avk"], q["vs"])
    for q in probs:
        q["tav"] = _bdot(q["t"], q["av"])
    for q in probs:
        q["mb"] = _bdot_tn(q["tk"], q["bps"])
        q["nm"] = _bdot_tn(q["vs"], q["kps"]) - _bdot_tn(q["tav"], q["bps"])
        q["r2"] = q["rs"] - _bdot(q["arb"], q["tk"])
        q["y0"] = _bdot(q["ark"], q["vs"]) - _bdot(q["arb"], q["tav"])

    st = [s_sc[pr] for pr in range(N_PAIRS)]
    outs = []
    for q in probs:
        sm = st[q["pr"]]
        ys = _bdot_nt(q["r2"], sm) + q["y0"]
        outs.append(ys[0:CHUNK] + ys[CHUNK:])
        st[q["pr"]] = sm * q["wl"] - _bdot(sm, q["mb"]) + q["nm"]
    for q, yo in zip(probs, outs):
        y_ref[q["off"]:q["off"] + CHUNK, q["ls"]] = yo.astype(y_ref.dtype)
    for pr in range(N_PAIRS):
        s_sc[pr] = st[pr]

    @pl.when(j == nt - 1)
    def _():
        sout_ref[0] = s_sc[...]


def _rwkv(p, vf_src, hv, state_in, prm, nb, rows, drn):
    m = p.shape[0]
    tq = _row_tile(rows, 256)
    nt = rows // tq
    rb = tq // PACK
    vres = vf_src is not None

    def tile(b, j):
        return b * nt + (j if drn == 0 else nt - 1 - j)

    const2 = lambda b, j: (0, 0)
    dir3 = lambda b, j: (drn, 0, 0)
    in_specs = [pl.BlockSpec((tq, RW_IN), lambda b, j: (tile(b, j), 0)),
                pl.BlockSpec((PACK, RW_IN), lambda b, j: (jnp.maximum(tile(b, j) * rb - 1, 0), 0)),
                pl.BlockSpec((PACK, RW_IN),
                             lambda b, j: (jnp.minimum((tile(b, j) + 1) * rb, m // PACK - 1), 0))]
    args = [p, p, p]
    if vres:
        in_specs += [pl.BlockSpec((tq, RW_DIM), lambda b, j: (tile(b, j), 1)),
                     pl.BlockSpec((tq, LANES), lambda b, j: (tile(b, j), 0))]
        args += [vf_src, hv]
    in_specs += [pl.BlockSpec((2, RW_IN), const2),
                 pl.BlockSpec((1, SUBLANES, RW_DIM), dir3),
                 pl.BlockSpec((SUBLANES, RW_DIM), const2),
                 pl.BlockSpec((1, LANES, 2 * RW_DIM), dir3),
                 pl.BlockSpec((1, LANES, 2 * RW_DIM), dir3),
                 pl.BlockSpec((LANES, RW_DIM), const2)]
    args += [prm["tshift"], prm["pd"], prm["ps"], prm["wla_hi"], prm["wla_lo"], prm["wg"]]
    if vres:
        in_specs += [pl.BlockSpec((LANES, RW_DIM), const2), pl.BlockSpec((LANES, RW_DIM), const2)]
        args += [prm["vu_hi"], prm["vu_lo"]]
    sspec = pl.BlockSpec((1, N_PAIRS, LANES, LANES), lambda b, j: (b, 0, 0, 0))
    in_specs += [pl.BlockSpec((LANES, LANES), const2), sspec]
    args += [prm["e128"], state_in]
    return pl.pallas_call(
        functools.partial(_rwkv_kernel, nt=nt, tq=tq, vres=vres, drn=drn),
        grid=(nb, nt),
        in_specs=in_specs,
        out_specs=[pl.BlockSpec((tq, RW_DIM), lambda b, j: (tile(b, j), 0)),
                   pl.BlockSpec((tq, 2 * RW_DIM), lambda b, j: (tile(b, j), 0)),
                   sspec],
        out_shape=[jax.ShapeDtypeStruct((m, RW_DIM), ACT),
                   jax.ShapeDtypeStruct((m, 2 * RW_DIM), ACT),
                   jax.ShapeDtypeStruct((nb, N_PAIRS, LANES, LANES), F32)],
        scratch_shapes=[pltpu.VMEM((8, tq, RW_DIM), F32), pltpu.VMEM((N_PAIRS, LANES, LANES), F32)],
        compiler_params=_cparams("parallel", "arbitrary"),
        name="rwkv7_fwd" if drn == 0 else "rwkv7_bwd",
    )(*args)


def _rwkv_both(p, vf_src, hv, states, prm, nb, rows):
    outs = [_rwkv(p, vf_src, hv, states[d], prm, nb, rows, d) for d in range(2)]
    return [o[0] for o in outs], [o[1] for o in outs], [o[2] for o in outs]


def _merge_kernel(x_ref, y0_ref, y1_ref, a0_ref, a1_ref, ya_ref, g0_ref, g1_ref, g2_ref, g3_ref, ln_ref, e_ref,
                  wbr_ref, wba_ref, wo_ref, mod_ref, o_ref):
    e = e_ref[...]
    y = y0_ref[...].astype(F32) + y1_ref[...].astype(F32)
    mu = _segsum(y, e) * (1.0 / HEAD_DIM)
    yc = y - mu
    var = _segsum(yc * yc, e) * (1.0 / HEAD_DIM)
    yn = yc * lax.rsqrt(var + GN_EPS) * ln_ref[0:1, :] + ln_ref[1:2, :]
    cv = a0_ref[:, 0:RW_DIM].astype(F32) + a1_ref[:, 0:RW_DIM].astype(F32)
    yrw = (yn + cv) * a0_ref[:, RW_DIM:2 * RW_DIM].astype(F32)
    a = _bdot(yrw, wbr_ref[...])
    b = jnp.dot(ya_ref[...], wba_ref[...], preferred_element_type=F32)
    h = a.shape[1] // 2
    mrg = jnp.concatenate(
        [jax.nn.sigmoid(g0_ref[...].astype(F32)) * a[:, :h] + jax.nn.sigmoid(g2_ref[...].astype(F32)) * b[:, :h],
         jax.nn.sigmoid(g1_ref[...].astype(F32)) * a[:, h:] + jax.nn.sigmoid(g3_ref[...].astype(F32)) * b[:, h:]],
        axis=1)
    o_ref[...] = x_ref[...] + mod_ref[0, 2:3, :] * _bdot(mrg, wo_ref[...])


def _merge(x2, ys, auxs, ya, p, ln, e128, wbr, wba, wo, mod, rows):
    m, d = x2.shape
    tm = _row_tile(rows, 512)
    per = rows // tm
    many = mod.shape[0] > 1
    gb = GATE_OFF // 512
    const2 = lambda i: (0, 0)
    in_specs = [pl.BlockSpec((tm, d), lambda i: (i, 0)),
                pl.BlockSpec((tm, RW_DIM), lambda i: (i, 0)),
                pl.BlockSpec((tm, RW_DIM), lambda i: (i, 0)),
                pl.BlockSpec((tm, 2 * RW_DIM), lambda i: (i, 0)),
                pl.BlockSpec((tm, 2 * RW_DIM), lambda i: (i, 0)),
                pl.BlockSpec((tm, ATT_DIM), lambda i: (i, 0))]
    in_specs += [pl.BlockSpec((tm, 512), (lambda i, c=c: (i, gb + c))) for c in range(4)]
    in_specs += [pl.BlockSpec((2, RW_DIM), const2),
                 pl.BlockSpec((LANES, LANES), const2),
                 pl.BlockSpec(wbr.shape, const2),
                 pl.BlockSpec(wba.shape, const2),
                 pl.BlockSpec(wo.shape, const2),
                 pl.BlockSpec((1, 6, d), lambda i: ((i // per) if many else 0, 0, 0))]
    return pl.pallas_call(
        _merge_kernel,
        grid=(m // tm,),
        in_specs=in_specs,
        out_specs=pl.BlockSpec((tm, d), lambda i: (i, 0)),
        out_shape=jax.ShapeDtypeStruct((m, d), F32),
        compiler_params=_cparams("parallel"),
        name="branch_merge",
    )(x2, ys[0], ys[1], auxs[0], auxs[1], ya, p, p, p, p, ln, e128, wbr, wba, wo, mod)


def _ffn_kernel(x_ref, g_ref, mod_ref, wg_ref, wu_ref, wd_ref, o_ref, h_sc, acc_sc, *, nf):
    f = pl.program_id(1)

    @pl.when(f == 0)
    def _():
        h = _norm_mod(x_ref[...], g_ref[...], mod_ref[0, 3:4, :], mod_ref[0, 4:5, :])
        h_sc[...] = h.astype(BF16)
        acc_sc[...] = jnp.zeros_like(acc_sc)

    h = h_sc[...]
    a = jnp.dot(h, wg_ref[...], preferred_element_type=F32)
    u = jnp.dot(h, wu_ref[...], preferred_element_type=F32)
    act = (a * jax.nn.sigmoid(a)) * u
    acc_sc[...] += jnp.dot(act.astype(BF16), wd_ref[...], preferred_element_type=F32)

    @pl.when(f == nf - 1)
    def _():
        o_ref[...] = x_ref[...] + mod_ref[0, 5:6, :] * acc_sc[...]


def _ffn(x2, gain, mod, wg, wu, wd, rows):
    m, d = x2.shape
    dff = wg.shape[1]
    tm = _row_tile(rows, 512)
    per = rows // tm
    many = mod.shape[0] > 1
    tf = dff // 2 if (dff // 2) % LANES == 0 else dff
    nf = dff // tf
    return pl.pallas_call(
        functools.partial(_ffn_kernel, nf=nf),
        grid=(m // tm, nf),
        in_specs=[pl.BlockSpec((tm, d), lambda i, f: (i, 0)),
                  pl.BlockSpec((1, d), lambda i, f: (0, 0)),
                  pl.BlockSpec((1, 6, d), lambda i, f: ((i // per) if many else 0, 0, 0)),
                  pl.BlockSpec((d, tf), lambda i, f: (0, f)),
                  pl.BlockSpec((d, tf), lambda i, f: (0, f)),
                  pl.BlockSpec((tf, d), lambda i, f: (f, 0))],
        out_specs=pl.BlockSpec((tm, d), lambda i, f: (i, 0)),
        out_shape=jax.ShapeDtypeStruct((m, d), F32),
        scratch_shapes=[pltpu.VMEM((tm, d), BF16), pltpu.VMEM((tm, d), F32)],
        compiler_params=_cparams("parallel", "arbitrary"),
        name="ffn_swiglu",
    )(x2, gain.reshape(1, d), mod, wg, wu, wd)


def _dma_pieces(count, sizes, make):
    for k, size in enumerate(sizes):
        @pl.when((count & size) != 0)
        def _(k=k, size=size):
            make(k, pl.multiple_of(count & ~(2 * size - 1), PACK), size)


def _moe_gather_kernel(pos_ref, cnt_ref, h_ref, rankt_ref, zero_ref, out_ref, hs_sc, sem, *, ne, nsteps, sizes):
    del zero_ref
    i = pl.program_id(0)
    ex = pl.program_id(1)
    step = i * ne + ex
    slot = step % 2
    tt = h_ref.shape[0]

    def copies(st, sl, act):
        base = pos_ref[st]

        def make(k, off, size):
            act(pltpu.make_async_copy(hs_sc.at[sl, pl.ds(off, size)],
                                      out_ref.at[pl.ds(pl.multiple_of(base + off, PACK), size)], sem.at[sl, k]))

        _dma_pieces(cnt_ref[st], sizes, make)

    @pl.when(step >= 2)
    def _():
        copies(step - 2, slot, lambda cp: cp.wait())

    rrow = rankt_ref[0, pl.ds(ex, 1), :]
    rowid = lax.broadcasted_iota(jnp.int32, (MOE_BLK, tt), 0).astype(F32)

    def gather(b, carry):
        sel = ((rrow - (b * MOE_BLK).astype(F32)) == rowid).astype(BF16)
        r0 = pl.multiple_of(b * MOE_BLK, MOE_BLK)
        hs_sc[slot, pl.ds(r0, MOE_BLK), :] = jnp.dot(sel, h_ref[...], preferred_element_type=F32).astype(BF16)
        return carry

    lax.fori_loop(0, (cnt_ref[step] + MOE_BLK - 1) // MOE_BLK, gather, 0)
    copies(step, slot, lambda cp: cp.start())

    @pl.when(step == nsteps - 1)
    def _():
        if nsteps >= 2:
            copies(step - 1, 1 - slot, lambda cp: cp.wait())
        copies(step, slot, lambda cp: cp.wait())


def _moe_ffn_kernel(te_ref, tm_ref, tv_ref, hs_ref, wg_ref, wu_ref, wd_ref, os_ref, acc_sc, *, nf):
    del te_ref, tm_ref
    t = pl.program_id(0)
    f = pl.program_id(1)
    valid = tv_ref[t] != 0

    @pl.when(valid)
    def _():
        hb = hs_ref[...]
        a = jnp.dot(hb, wg_ref[0], preferred_element_type=F32)
        u = jnp.dot(hb, wu_ref[0], preferred_element_type=F32)
        act = ((a * jax.nn.sigmoid(a)) * u).astype(BF16)
        part = jnp.dot(act, wd_ref[0], preferred_element_type=F32)

        @pl.when(f == 0)
        def _():
            acc_sc[...] = part

        @pl.when(f > 0)
        def _():
            acc_sc[...] += part

    @pl.when(f == nf - 1)
    def _():
        os_ref[...] = jnp.where(valid, acc_sc[...], 0.0).astype(os_ref.dtype)


def _moe_combine_kernel(pos_ref, cnt_ref, x_ref, we_ref, rank_ref, mod_ref, os_ref, o_ref, acc_sc, ob_sc, sem,
                        *, ne, nsteps, sizes):
    i = pl.program_id(0)
    ex = pl.program_id(1)
    step = i * ne + ex
    slot = step % 2
    tt = x_ref.shape[0]
    lane = lax.broadcasted_iota(jnp.int32, (tt, LANES), 1)

    def copies(st, sl, act):
        base = pos_ref[st]

        def make(k, off, size):
            act(pltpu.make_async_copy(os_ref.at[pl.ds(pl.multiple_of(base + off, PACK), size)],
                                      ob_sc.at[sl, pl.ds(off, size)], sem.at[sl, k]))

        _dma_pieces(cnt_ref[st], sizes, make)

    @pl.when(step == 0)
    def _():
        ob_sc[...] = jnp.zeros_like(ob_sc)
        copies(0, 0, lambda cp: cp.start())

    @pl.when(step + 1 < nsteps)
    def _():
        copies(step + 1, 1 - slot, lambda cp: cp.start())

    copies(step, slot, lambda cp: cp.wait())

    @pl.when(ex == 0)
    def _():
        acc_sc[...] = jnp.zeros_like(acc_sc)

    rcol = jnp.sum(jnp.where(lane == ex, rank_ref[...], 0.0), axis=1, keepdims=True)
    wcol = jnp.sum(jnp.where(lane == ex, we_ref[...], 0.0), axis=1, keepdims=True)
    colid = lane.astype(F32)

    def scatter(b, carry):
        sel = ((rcol - (b * MOE_BLK).astype(F32)) == colid).astype(BF16)
        r0 = pl.multiple_of(b * MOE_BLK, MOE_BLK)
        acc_sc[...] += wcol * jnp.dot(sel, ob_sc[slot, pl.ds(r0, MOE_BLK), :], preferred_element_type=F32)
        return carry

    lax.fori_loop(0, (cnt_ref[step] + MOE_BLK - 1) // MOE_BLK, scatter, 0)

    @pl.when(ex == ne - 1)
    def _():
        o_ref[...] = x_ref[...] + mod_ref[0, 5:6, :] * acc_sc[...]


def _route_kernel(x_ref, g_ref, mod_ref, rt_ref, h_ref, we_ref, rank_ref, rankt_ref, cnt_ref, *, ne):
    tt = x_ref.shape[0]
    h = _norm_mod(x_ref[...], g_ref[...], mod_ref[0, 3:4, :], mod_ref[0, 4:5, :])
    h_ref[...] = h.astype(BF16)
    lane = lax.broadcasted_iota(jnp.int32, (tt, LANES), 1)
    logits = jnp.dot(h, rt_ref[...], precision=HIGHEST, preferred_element_type=F32)
    lg = jnp.where(lane < ne, logits, NEG_INF)
    t1 = jnp.max(lg, axis=1, keepdims=True)
    i1 = jnp.min(jnp.where(lg == t1, lane, LANES), axis=1, keepdims=True)
    lg2 = jnp.where(lane == i1, NEG_INF, lg)
    t2 = jnp.max(lg2, axis=1, keepdims=True)
    i2 = jnp.min(jnp.where(lg2 == t2, lane, LANES), axis=1, keepdims=True)
    e2 = jnp.exp(t2 - t1)
    den = 1.0 + e2
    we_ref[...] = jnp.where(lane == i1, 1.0 / den, 0.0) + jnp.where(lane == i2, e2 / den, 0.0)
    sel = (lane == i1) | (lane == i2)
    self32 = sel.astype(F32)
    before = (lax.broadcasted_iota(jnp.int32, (tt, tt), 1) < lax.broadcasted_iota(jnp.int32, (tt, tt), 0))
    rank = jnp.dot(before.astype(BF16), sel.astype(BF16), preferred_element_type=F32)
    rank = jnp.where(sel, rank, -1.0)
    rank_ref[...] = rank
    rankt_ref[0] = rank.T[0:SUBLANES, :]
    cnt_ref[0] = jnp.broadcast_to(jnp.sum(self32, axis=0, keepdims=True), (SUBLANES, LANES))


def _moe(x2, gain, mod, router_pad, wg, wu, wd, rows):
    m, d = x2.shape
    ne, _, dff = wg.shape
    assert ne <= SUBLANES
    tt = _row_tile(rows, MOE_TILE)
    nti = m // tt
    per = rows // tt
    many = mod.shape[0] > 1
    h, we, rank, rankt, cnt = pl.pallas_call(
        functools.partial(_route_kernel, ne=ne),
        grid=(nti,),
        in_specs=[pl.BlockSpec((tt, d), lambda i: (i, 0)),
                  pl.BlockSpec((1, d), lambda i: (0, 0)),
                  pl.BlockSpec((1, 6, d), lambda i: ((i // per) if many else 0, 0, 0)),
                  pl.BlockSpec((d, LANES), lambda i: (0, 0))],
        out_specs=[pl.BlockSpec((tt, d), lambda i: (i, 0)),
                   pl.BlockSpec((tt, LANES), lambda i: (i, 0)),
                   pl.BlockSpec((tt, LANES), lambda i: (i, 0)),
                   pl.BlockSpec((1, SUBLANES, tt), lambda i: (i, 0, 0)),
                   pl.BlockSpec((1, SUBLANES, LANES), lambda i: (i, 0, 0))],
        out_shape=[jax.ShapeDtypeStruct((m, d), BF16),
                   jax.ShapeDtypeStruct((m, LANES), F32),
                   jax.ShapeDtypeStruct((m, LANES), F32),
                   jax.ShapeDtypeStruct((nti, SUBLANES, tt), F32),
                   jax.ShapeDtypeStruct((nti, SUBLANES, LANES), F32)],
        compiler_params=_cparams("parallel"),
        name="moe_route",
    )(x2, gain.reshape(1, d), mod, router_pad)
    c16 = (cnt[:, 0, :ne].astype(jnp.int32) + (PACK - 1)) // PACK * PACK
    region = (jnp.sum(c16, axis=0) + (MOE_FT - 1)) // MOE_FT * MOE_FT
    e_off = jnp.cumsum(region) - region
    pos = (e_off[None, :] + jnp.cumsum(c16, axis=0) - c16).reshape(-1).astype(jnp.int32)
    c16 = c16.reshape(-1)
    rmax = -(-(2 * m + nti * ne * PACK + ne * MOE_FT) // MOE_FT) * MOE_FT
    nft = rmax // MOE_FT
    tiles_e = jnp.cumsum(region // MOE_FT)
    tid = jnp.arange(nft, dtype=jnp.int32)
    tile_valid = (tid < tiles_e[-1]).astype(jnp.int32)
    tile_map = jnp.minimum(tid, tiles_e[-1] - 1).astype(jnp.int32)
    tile_exp = jnp.minimum(jnp.searchsorted(tiles_e, tile_map, side="right"), ne - 1).astype(jnp.int32)

    sizes = tuple(tt >> k for k in range(tt.bit_length()) if (tt >> k) >= PACK)
    nsteps = nti * ne
    hs = pl.pallas_call(
        functools.partial(_moe_gather_kernel, ne=ne, nsteps=nsteps, sizes=sizes),
        grid_spec=pltpu.PrefetchScalarGridSpec(
            num_scalar_prefetch=2,
            grid=(nti, ne),
            in_specs=[pl.BlockSpec((tt, d), lambda i, e, p_, c_: (i, 0)),
                      pl.BlockSpec((1, SUBLANES, tt), lambda i, e, p_, c_: (i, 0, 0)),
                      pl.BlockSpec(memory_space=pl.ANY)],
            out_specs=pl.BlockSpec(memory_space=pl.ANY),
            scratch_shapes=[pltpu.VMEM((2, tt, d), BF16), pltpu.SemaphoreType.DMA((2, len(sizes)))]),
        out_shape=jax.ShapeDtypeStruct((rmax, d), BF16),
        input_output_aliases={4: 0},
        compiler_params=_cparams("arbitrary", "arbitrary"),
        name="moe_gather",
    )(pos, c16, h, rankt, jnp.zeros((rmax, d), BF16))

    tf = dff // 4 if (dff // 4) % LANES == 0 else dff
    nf = dff // tf
    os_sorted = pl.pallas_call(
        functools.partial(_moe_ffn_kernel, nf=nf),
        grid_spec=pltpu.PrefetchScalarGridSpec(
            num_scalar_prefetch=3,
            grid=(nft, nf),
            in_specs=[pl.BlockSpec((MOE_FT, d), lambda t, f, te, tm, tv: (tm[t], 0)),
                      pl.BlockSpec((1, d, tf), lambda t, f, te, tm, tv: (te[t], 0, f)),
                      pl.BlockSpec((1, d, tf), lambda t, f, te, tm, tv: (te[t], 0, f)),
                      pl.BlockSpec((1, tf, d), lambda t, f, te, tm, tv: (te[t], f, 0))],
            out_specs=pl.BlockSpec((MOE_FT, d), lambda t, f, te, tm, tv: (t, 0)),
            scratch_shapes=[pltpu.VMEM((MOE_FT, d), F32)]),
        out_shape=jax.ShapeDtypeStruct((rmax, d), BF16),
        compiler_params=_cparams("parallel", "arbitrary"),
        name="moe_experts",
    )(tile_exp, tile_map, tile_valid, hs, wg, wu, wd)

    return pl.pallas_call(
        functools.partial(_moe_combine_kernel, ne=ne, nsteps=nsteps, sizes=sizes),
        grid_spec=pltpu.PrefetchScalarGridSpec(
            num_scalar_prefetch=2,
            grid=(nti, ne),
            in_specs=[pl.BlockSpec((tt, d), lambda i, e, p_, c_: (i, 0)),
                      pl.BlockSpec((tt, LANES), lambda i, e, p_, c_: (i, 0)),
                      pl.BlockSpec((tt, LANES), lambda i, e, p_, c_: (i, 0)),
                      pl.BlockSpec((1, 6, d), lambda i, e, p_, c_: ((i // per) if many else 0, 0, 0)),
                      pl.BlockSpec(memory_space=pl.ANY)],
            out_specs=pl.BlockSpec((tt, d), lambda i, e, p_, c_: (i, 0)),
            scratch_shapes=[pltpu.VMEM((tt, d), F32), pltpu.VMEM((2, tt, d), BF16),
                            pltpu.SemaphoreType.DMA((2, len(sizes)))]),
        out_shape=jax.ShapeDtypeStruct((m, d), F32),
        compiler_params=_cparams("arbitrary", "arbitrary"),
        name="moe_combine",
    )(pos, c16, x2, we, rank, mod, os_sorted)


def _rope_table(t):
    rows = t // GRID_W
    row = jnp.repeat(jnp.arange(rows, dtype=F32), GRID_W)
    col = jnp.tile(jnp.arange(GRID_W, dtype=F32), rows)
    half = HEAD_DIM // 2
    inv_freq = ROPE_THETA ** (-jnp.arange(0, half, 2, dtype=F32) / half)
    ar = row[:, None] * inv_freq
    ac = col[:, None] * inv_freq
    ang = jnp.concatenate([ar, ar, ac, ac] * 2, axis=-1)
    return jnp.stack([jnp.cos(ang), jnp.sin(ang)])


def _pad_rows(a, rows):
    return jnp.zeros((rows,) + a.shape[1:], a.dtype).at[:a.shape[0]].set(a)


def _hi_lo(w):
    hi = w.astype(BF16)
    return hi, (w - hi.astype(F32)).astype(BF16)


def kernel(x, c, ctx, c_ctx, norm1, norm2, w_mod, b_mod, w_in, tshift, w0, w_decay_up, a0, w_iclr_up, v0, w_vres_down, w_vres_up, w_gate_up, k_k, k_a, r_k, lnx_g, lnx_b, q_norm, k_norm, sink, w_br_rwkv, w_br_attn, w_out, ffn_gate, ffn_up, ffn_down, router, moe_gate, moe_up, moe_down):
    nb, t, d = x.shape
    cn = ctx.shape[1]
    depth = w_in.shape[0]
    mods = _modulation(c, c_ctx, w_mod, b_mod)
    cs = _rope_table(t)
    e128 = jnp.kron(jnp.eye(2, dtype=F32), jnp.ones((HEAD_DIM, HEAD_DIM), F32)).astype(BF16)
    xl = x.reshape(nb * t, d)
    xc = ctx.reshape(nb * cn, d)
    vf_l = vf_c = None
    for layer in range(depth):
        last = layer == depth - 1
        mod_l = mods[layer, :nb].reshape(nb, 6, d)
        mod_c = mods[layer, nb:nb + 1].reshape(1, 6, d)
        w_in_b = w_in[layer].astype(BF16)
        vd_b = None
        prm = {"tshift": tshift[layer], "e128": e128, "wg": w_gate_up[layer].astype(BF16)}
        prm["pd"] = jnp.stack(
            [_pad_rows(jnp.stack([w0[layer, dd], a0[layer, dd]]), SUBLANES) for dd in range(2)])
        shared = [k_k[layer], k_a[layer], r_k[layer].reshape(RW_DIM)]
        if layer > 0:
            shared.append(v0[layer - 1])
            vd_b = _pad_rows(w_vres_down[layer - 1].T, LANES).T.astype(BF16)
            prm["vu_hi"], prm["vu_lo"] = _hi_lo(_pad_rows(w_vres_up[layer - 1], LANES))
        prm["ps"] = _pad_rows(jnp.stack(shared), SUBLANES)
        wla = jnp.zeros((2, LANES, 2 * RW_DIM), F32)
        wla = wla.at[:, :HEAD_DIM, :RW_DIM].set(w_decay_up[layer]).at[:, HEAD_DIM:, RW_DIM:].set(w_iclr_up[layer])
        prm["wla_hi"], prm["wla_lo"] = _hi_lo(wla)

        p_l, hv_l = _inproj(xl, norm1[layer], mod_l, w_in_b, vd_b, t)
        p_c, hv_c = _inproj(xc, norm1[layer], mod_c, w_in_b, vd_b, cn)

        s0 = jnp.zeros((nb, N_PAIRS, LANES, LANES), F32)
        y_c, aux_c, s_ctx = _rwkv_both(p_c, vf_c, hv_c, [s0, s0], prm, nb, cn)
        y_l, aux_l, _ = _rwkv_both(p_l, vf_l, hv_l, s_ctx, prm, nb, t)
        if layer == 0:
            vf_l, vf_c = aux_l[1], aux_c[1]

        qg = jnp.tile(q_norm[layer], 2).reshape(1, LANES)
        kg = jnp.tile(k_norm[layer], 2).reshape(1, LANES)
        qn_l, kn_l, vb_l = _qk_prep(p_l, cs, qg, kg, e128, t, True)
        qn_c, kn_c, vb_c = _qk_prep(p_c, None, qg, kg, e128, cn, False)
        sk = sink[layer].reshape(1, -1)
        ya_l = _attention(qn_l, kn_l, vb_l, kn_c, vb_c, sk, nb, t, cn, True)

        ln = jnp.stack([lnx_g[layer], lnx_b[layer]])
        wbr = w_br_rwkv[layer].astype(BF16)
        wba = w_br_attn[layer].astype(BF16)
        wo = w_out[layer].astype(BF16)
        xl = _merge(xl, y_l, aux_l, ya_l, p_l, ln, e128, wbr, wba, wo, mod_l, t)
        if not last:
            ya_c = _attention(qn_c, None, None, kn_c, vb_c, sk, nb, cn, cn, False)
            xc = _merge(xc, y_c, aux_c, ya_c, p_c, ln, e128, wbr, wba, wo, mod_c, cn)

        i = layer // 2
        if layer % 2 == 0:
            fw = (ffn_gate[i].astype(BF16), ffn_up[i].astype(BF16), ffn_down[i].astype(BF16))
            xl = _ffn(xl, norm2[layer], mod_l, *fw, t)
            if not last:
                xc = _ffn(xc, norm2[layer], mod_c, *fw, cn)
        else:
            rt = _pad_rows(router[i].T, LANES).T
            mw = (moe_gate[i].astype(BF16), moe_up[i].astype(BF16), moe_down[i].astype(BF16))
            xl = _moe(xl, norm2[layer], mod_l, rt, *mw, t)
            if not last:
                xc = _moe(xc, norm2[layer], mod_c, rt, *mw, cn)
    return xl.reshape(nb, t, d)
```

```python
import functools

import jax
import jax.numpy as jnp
from jax import lax
from jax.experimental import pallas as pl
from jax.experimental.pallas import tpu as pltpu

F32 = jnp.float32
BF16 = jnp.bfloat16
HIGHEST = lax.Precision.HIGHEST

LANES = 128
SUBLANES = 8
PACK = 16
ACT = jnp.bfloat16
HEAD_DIM = 64
RW_DIM = 512
ATT_DIM = 512
KV_DIM = 128
RW_IN = 1792
ATT_OFF = RW_IN
GATE_OFF = RW_IN + ATT_DIM + 2 * KV_DIM
N_EXPERTS = 8
CHUNK = 64
N_PAIRS = RW_DIM // LANES
MOE_TILE = 1024
MOE_BLK = 128
MOE_FT = 512
ATT_SCALE = HEAD_DIM ** -0.5
ROPE_THETA = 10000.0
GRID_W = 64
RMS_EPS = 1e-6
GN_EPS = 64e-5
NEG_INF = -1e30
DECAY_SCALE = 0.6065306597126334
VMEM_LIMIT = 48 * 1024 * 1024


def _cparams(*sem):
    return pltpu.CompilerParams(dimension_semantics=sem, vmem_limit_bytes=VMEM_LIMIT)


def _row_tile(rows, cap):
    t = cap
    while rows % t:
        t //= 2
    return t


def _bdot(a, b):
    return jnp.dot(a.astype(BF16), b.astype(BF16), preferred_element_type=F32)


def _bdot_nt(a, b):
    return lax.dot_general(a.astype(BF16), b.astype(BF16), (((1,), (1,)), ((), ())),
                           preferred_element_type=F32)


def _bdot_tn(a, b):
    return lax.dot_general(a.astype(BF16), b.astype(BF16), (((0,), (0,)), ((), ())),
                           preferred_element_type=F32)


def _split2(x):
    hi = x.astype(BF16)
    lo = (x - hi.astype(F32)).astype(BF16)
    return hi, lo


def _split3(x):
    hi = x.astype(BF16)
    r = x - hi.astype(F32)
    mid = r.astype(BF16)
    lo = (r - mid.astype(F32)).astype(BF16)
    return hi, mid, lo


def _dot3(a, w_hi, w_lo):
    a_hi, a_lo = _split2(a)
    return (jnp.dot(a_hi, w_hi, preferred_element_type=F32)
            + (jnp.dot(a_lo, w_hi, preferred_element_type=F32)
               + jnp.dot(a_hi, w_lo, preferred_element_type=F32)))


def _segsum(x, e):
    outs = []
    for s in range(x.shape[1] // LANES):
        hi, lo = _split2(x[:, s * LANES:(s + 1) * LANES])
        outs.append(jnp.dot(hi, e, preferred_element_type=F32)
                    + jnp.dot(lo, e, preferred_element_type=F32))
    return outs[0] if len(outs) == 1 else jnp.concatenate(outs, axis=1)


def _norm_mod(x, gain, shift, scale):
    y = x * lax.rsqrt(jnp.mean(x * x, axis=-1, keepdims=True) + RMS_EPS) * gain
    return y * (1.0 + scale) + shift


def _mod_kernel(c_ref, w_ref, b_ref, o_ref):
    c = c_ref[...]
    s = c * jax.nn.sigmoid(c)
    o_ref[0] = jnp.dot(s, w_ref[0], precision=HIGHEST, preferred_element_type=F32) + b_ref[0]


def _modulation(c, c_ctx, w_mod, b_mod):
    depth, d, n6 = w_mod.shape
    nb = c.shape[0]
    rm = -(-(nb + 1) // SUBLANES) * SUBLANES
    cc = jnp.zeros((rm, d), F32).at[:nb].set(c).at[nb].set(c_ctx)
    tn = _row_tile(n6, 1536)
    return pl.pallas_call(
        _mod_kernel,
        grid=(depth, n6 // tn),
        in_specs=[pl.BlockSpec((rm, d), lambda l, n: (0, 0)),
                  pl.BlockSpec((1, d, tn), lambda l, n: (l, 0, n)),
                  pl.BlockSpec((1, 1, tn), lambda l, n: (l, 0, n))],
        out_specs=pl.BlockSpec((1, rm, tn), lambda l, n: (l, 0, n)),
        out_shape=jax.ShapeDtypeStruct((depth, rm, n6), F32),
        compiler_params=_cparams("parallel", "parallel"),
        name="adaln_mod",
    )(cc, w_mod, b_mod.reshape(depth, 1, n6))


def _inproj_kernel(*refs, vres):
    if vres:
        x_ref, g_ref, mod_ref, w_ref, vd_ref, o_ref, hv_ref, h_sc = refs
    else:
        x_ref, g_ref, mod_ref, w_ref, o_ref, h_sc = refs

    @pl.when(pl.program_id(1) == 0)
    def _():
        h = _norm_mod(x_ref[...], g_ref[...], mod_ref[0, 0:1, :], mod_ref[0, 1:2, :])
        hb = h.astype(BF16)
        h_sc[...] = hb
        if vres:
            hv_ref[...] = jnp.dot(hb, vd_ref[...], preferred_element_type=F32)

    o_ref[...] = jnp.dot(h_sc[...], w_ref[...], preferred_element_type=F32).astype(o_ref.dtype)


def _inproj(x2, gain, mod, w_b, vd_b, rows):
    m, d = x2.shape
    n = w_b.shape[1]
    tm = _row_tile(rows, 1024)
    tn = _row_tile(n, 1536)
    per = rows // tm
    many = mod.shape[0] > 1
    vres = vd_b is not None
    in_specs = [pl.BlockSpec((tm, d), lambda i, j: (i, 0)),
                pl.BlockSpec((1, d), lambda i, j: (0, 0)),
                pl.BlockSpec((1, 6, d), lambda i, j: ((i // per) if many else 0, 0, 0)),
                pl.BlockSpec((d, tn), lambda i, j: (0, j))]
    args = [x2, gain.reshape(1, d), mod, w_b]
    out_specs = [pl.BlockSpec((tm, tn), lambda i, j: (i, j))]
    out_shape = [jax.ShapeDtypeStruct((m, n), ACT)]
    if vres:
        in_specs.append(pl.BlockSpec((d, LANES), lambda i, j: (0, 0)))
        args.append(vd_b)
        out_specs.append(pl.BlockSpec((tm, LANES), lambda i, j: (i, 0)))
        out_shape.append(jax.ShapeDtypeStruct((m, LANES), F32))
    outs = pl.pallas_call(
        functools.partial(_inproj_kernel, vres=vres),
        grid=(m // tm, n // tn),
        in_specs=in_specs, out_specs=out_specs, out_shape=out_shape,
        scratch_shapes=[pltpu.VMEM((tm, d), BF16)],
        compiler_params=_cparams("parallel", "arbitrary"),
        name="in_proj",
    )(*args)
    return (outs[0], outs[1]) if vres else (outs[0], None)


def _qkprep_kernel(*refs, rope):
    if rope:
        q0_ref, q1_ref, k_ref, v_ref, cs_ref, qg_ref, kg_ref, e_ref, qn_ref, kn_ref, vb_ref = refs
    else:
        q0_ref, q1_ref, k_ref, v_ref, qg_ref, kg_ref, e_ref, qn_ref, kn_ref, vb_ref = refs
    e = e_ref[...]
    lane = lax.broadcasted_iota(jnp.int32, (1, LANES), 1)
    first = (lane & 31) < 16

    def norm_rope(u, gain):
        y = u * lax.rsqrt(_segsum(u * u, e) * (1.0 / HEAD_DIM) + RMS_EPS) * gain
        if rope:
            rot = jnp.where(first, -pltpu.roll(y, LANES - 16, 1), pltpu.roll(y, 16, 1))
            y = y * cs_ref[0] + rot * cs_ref[1]
        return y

    qg = qg_ref[...]
    for s, ref in enumerate((q0_ref, q1_ref)):
        for t in range(2):
            u = ref[:, t * LANES:(t + 1) * LANES].astype(F32)
            c0 = (2 * s + t) * LANES
            qn_ref[:, c0:c0 + LANES] = (norm_rope(u, qg) * ATT_SCALE).astype(BF16)
    kn_ref[...] = norm_rope(k_ref[...].astype(F32), kg_ref[...]).astype(BF16)
    vb_ref[...] = v_ref[...].astype(BF16)


def _qk_prep(p, cs, qg, kg, e128, rows, rope):
    m = p.shape[0]
    tm = _row_tile(rows, 512)
    per = rows // tm
    qb = ATT_OFF // 256
    kb = (ATT_OFF + ATT_DIM) // LANES
    in_specs = [pl.BlockSpec((tm, 256), lambda i: (i, qb)),
                pl.BlockSpec((tm, 256), lambda i: (i, qb + 1)),
                pl.BlockSpec((tm, LANES), lambda i: (i, kb)),
                pl.BlockSpec((tm, LANES), lambda i: (i, kb + 1))]
    args = [p, p, p, p]
    if rope:
        in_specs.append(pl.BlockSpec((2, tm, LANES), lambda i: (0, i % per, 0)))
        args.append(cs)
    in_specs += [pl.BlockSpec((1, LANES), lambda i: (0, 0)),
                 pl.BlockSpec((1, LANES), lambda i: (0, 0)),
                 pl.BlockSpec((LANES, LANES), lambda i: (0, 0))]
    args += [qg, kg, e128]
    return pl.pallas_call(
        functools.partial(_qkprep_kernel, rope=rope),
        grid=(m // tm,),
        in_specs=in_specs,
        out_specs=[pl.BlockSpec((tm, ATT_DIM), lambda i: (i, 0)),
                   pl.BlockSpec((tm, LANES), lambda i: (i, 0)),
                   pl.BlockSpec((tm, LANES), lambda i: (i, 0))],
        out_shape=[jax.ShapeDtypeStruct((m, ATT_DIM), BF16),
                   jax.ShapeDtypeStruct((m, KV_DIM), BF16),
                   jax.ShapeDtypeStruct((m, KV_DIM), BF16)],
        compiler_params=_cparams("parallel"),
        name="qk_prep",
    )(*args)


def _attn_kernel(*refs, nq, local):
    if local:
        sink_ref, q_ref, kp_ref, kc_ref, kn_ref, vp_ref, vc_ref, vn_ref, kx_ref, vx_ref, o_ref = refs
        k = jnp.concatenate([kp_ref[...], kc_ref[...], kn_ref[...], kx_ref[...]], axis=0).astype(F32)
        v = jnp.concatenate([vp_ref[...], vc_ref[...], vn_ref[...], vx_ref[...]], axis=0).astype(F32)
    else:
        sink_ref, q_ref, kx_ref, vx_ref, o_ref = refs
        k = kx_ref[...].astype(F32)
        v = vx_ref[...].astype(F32)
    i = pl.program_id(1)
    bq = q_ref.shape[0]
    low = lax.broadcasted_iota(jnp.int32, (1, LANES), 1) < HEAD_DIM
    ksw = pltpu.roll(k, HEAD_DIM, 1)
    vsw = pltpu.roll(v, HEAD_DIM, 1)
    row = lax.broadcasted_iota(jnp.int32, (4 * bq, 1), 0)
    if local:
        qo = lax.broadcasted_iota(jnp.int32, (4 * bq, 3 * bq), 0) & (bq - 1)
        kcol = lax.broadcasted_iota(jnp.int32, (4 * bq, 3 * bq), 1)
        ko = kcol & (bq - 1)
        kblk = kcol // bq
        ok = ((kblk == 1)
              | ((kblk == 0) & (ko >= qo) & (i > 0))
              | ((kblk == 2) & (ko <= qo) & (i < nq - 1)))
    for g in range(2):
        kb = (jnp.where(low, k, ksw) if g == 0 else jnp.where(low, ksw, k)).astype(BF16)
        vb = (jnp.where(low, v, vsw) if g == 0 else jnp.where(low, vsw, v)).astype(BF16)
        qa = q_ref[:, (2 * g) * LANES:(2 * g + 1) * LANES].astype(F32)
        qb = q_ref[:, (2 * g + 1) * LANES:(2 * g + 2) * LANES].astype(F32)
        qs = jnp.concatenate([jnp.where(low, qa, 0.0), jnp.where(low, 0.0, qa),
                              jnp.where(low, qb, 0.0), jnp.where(low, 0.0, qb)], axis=0).astype(BF16)
        s = lax.dot_general(qs, kb, (((1,), (1,)), ((), ())), preferred_element_type=F32)
        if local:
            s = jnp.concatenate([jnp.where(ok, s[:, :3 * bq], NEG_INF), s[:, 3 * bq:]], axis=1)
        sk = jnp.where(row < bq, sink_ref[0, 4 * g],
                       jnp.where(row < 2 * bq, sink_ref[0, 4 * g + 1],
                                 jnp.where(row < 3 * bq, sink_ref[0, 4 * g + 2], sink_ref[0, 4 * g + 3])))
        mx = jnp.maximum(jnp.max(s, axis=1, keepdims=True), sk)
        ex = jnp.exp(s - mx)
        den = jnp.sum(ex, axis=1, keepdims=True) + jnp.exp(sk - mx)
        pr = (ex / den).astype(BF16)
        o = jnp.dot(pr, vb, preferred_element_type=F32)
        o_ref[:, (2 * g) * LANES:(2 * g + 1) * LANES] = jnp.where(low, o[0:bq], o[bq:2 * bq]).astype(o_ref.dtype)
        o_ref[:, (2 * g + 1) * LANES:(2 * g + 2) * LANES] = jnp.where(
            low, o[2 * bq:3 * bq], o[3 * bq:4 * bq]).astype(o_ref.dtype)


def _attention(qn, kn, vb, kctx, vctx, sink, nb, rows, cn, local):
    m = qn.shape[0]
    bq = 128
    nq = rows // bq
    smem = pl.BlockSpec(memory_space=pltpu.SMEM)
    qspec = pl.BlockSpec((bq, ATT_DIM), lambda b, i: (b * nq + i, 0))
    cspec = pl.BlockSpec((cn, KV_DIM), lambda b, i: (b, 0))
    if local:
        prv = pl.BlockSpec((bq, KV_DIM), lambda b, i: (b * nq + jnp.maximum(i - 1, 0), 0))
        cur = pl.BlockSpec((bq, KV_DIM), lambda b, i: (b * nq + i, 0))
        nxt = pl.BlockSpec((bq, KV_DIM), lambda b, i: (b * nq + jnp.minimum(i + 1, nq - 1), 0))
        in_specs = [smem, qspec, prv, cur, nxt, prv, cur, nxt, cspec, cspec]
        args = [sink, qn, kn, kn, kn, vb, vb, vb, kctx, vctx]
    else:
        in_specs = [smem, qspec, cspec, cspec]
        args = [sink, qn, kctx, vctx]
    return pl.pallas_call(
        functools.partial(_attn_kernel, nq=nq, local=local),
        grid=(nb, nq),
        in_specs=in_specs,
        out_specs=pl.BlockSpec((bq, ATT_DIM), lambda b, i: (b * nq + i, 0)),
        out_shape=jax.ShapeDtypeStruct((m, ATT_DIM), BF16),
        compiler_params=_cparams("parallel", "parallel"),
        name="window_attn" if local else "ctx_attn",
    )(*args)


def _rwkv_kernel(*refs, nt, tq, vres, drn):
    it = iter(refs)
    p_ref, pv_ref, nx_ref = next(it), next(it), next(it)
    if vres:
        vf_ref, hv_ref = next(it), next(it)
    ts_ref, pd_ref, ps_ref, wlh_ref, wll_ref, wg_ref = (next(it) for _ in range(6))
    if vres:
        vuh_ref, vul_ref = next(it), next(it)
    e_ref, sin_ref = next(it), next(it)
    y_ref, aux_ref, sout_ref = next(it), next(it), next(it)
    ops_sc, s_sc = next(it), next(it)

    j = pl.program_id(1)
    sgn = 1 - 2 * drn
    tile = j if drn == 0 else nt - 1 - j
    nch = tq // CHUNK

    @pl.when(j == 0)
    def _():
        s_sc[...] = sin_ref[0]

    rw = p_ref[...].astype(F32)
    row = lax.broadcasted_iota(jnp.int32, (tq, 1), 0)
    prev_row = jnp.where(tile > 0, pv_ref[PACK - 1:PACK, :].astype(F32), 0.0)
    next_row = jnp.where(tile < nt - 1, nx_ref[0:1, :].astype(F32), 0.0)
    prv = jnp.where(row == 0, prev_row, pltpu.roll(rw, 1, 0))
    nxt = jnp.where(row == tq - 1, next_row, pltpu.roll(rw, tq - 1, 0))
    xs = rw + ts_ref[0:1, :] * (prv - rw) + ts_ref[1:2, :] * (nxt - rw)

    e = e_ref[...]
    r = xs[:, 0:RW_DIM]
    k = xs[:, RW_DIM:2 * RW_DIM]
    v = xs[:, 2 * RW_DIM:3 * RW_DIM]
    la = xs[:, 3 * RW_DIM:3 * RW_DIM + LANES]
    lane = lax.broadcasted_iota(jnp.int32, (1, LANES), 1)
    low = lane < HEAD_DIM
    la = jnp.where(low, jnp.tanh(la), la)
    lora = _dot3(la, wlh_ref[0], wll_ref[0])
    logw = -DECAY_SCALE * jax.nn.sigmoid(pd_ref[0, 0:1, :] + lora[:, 0:RW_DIM])
    iclr = jax.nn.sigmoid(pd_ref[0, 1:2, :] + lora[:, RW_DIM:2 * RW_DIM])
    if vres:
        mix = jax.nn.sigmoid(ps_ref[3:4, :] + _dot3(hv_ref[...], vuh_ref[...], vul_ref[...]))
        v = v + (vf_ref[...].astype(F32) - v) * mix
    kk = k * ps_ref[0:1, :]
    kk = kk * lax.rsqrt(_segsum(kk * kk, e) + 1e-12)
    kh = k * (1.0 + (iclr - 1.0) * ps_ref[1:2, :])
    bb = kk * iclr
    aux_ref[:, 0:RW_DIM] = (_segsum(r * kh * ps_ref[2:3, :], e) * v).astype(aux_ref.dtype)
    if drn == 0:
        glo = xs[:, 3 * RW_DIM + LANES:RW_IN]
        aux_ref[:, RW_DIM:2 * RW_DIM] = _bdot(jax.nn.sigmoid(glo), wg_ref[...]).astype(aux_ref.dtype)
    else:
        aux_ref[:, RW_DIM:2 * RW_DIM] = v.astype(aux_ref.dtype)

    ti = lax.broadcasted_iota(jnp.int32, (tq, tq), 0)
    si = lax.broadcasted_iota(jnp.int32, (tq, tq), 1)
    same = (ti // CHUNK) == (si // CHUNK)
    tri = (same & (((si - ti) * sgn) <= 0)).astype(BF16)
    ones = same.astype(BF16)
    cin = jnp.zeros_like(logw)
    ctot = jnp.zeros_like(logw)
    for part in _split3(logw):
        cin = cin + jnp.dot(tri, part, preferred_element_type=F32)
        ctot = ctot + jnp.dot(ones, part, preferred_element_type=F32)
    e_neg = jnp.exp(-cin)
    e_rem = jnp.exp(ctot - cin)
    ops_sc[0] = kk * jnp.exp(cin - logw)
    ops_sc[1] = r * jnp.exp(cin)
    ops_sc[2] = bb * e_neg
    ops_sc[3] = kh * e_neg
    ops_sc[4] = kh * e_rem
    ops_sc[5] = bb * e_rem
    ops_sc[6] = v
    ops_sc[7] = jnp.exp(ctot)

    r2 = lax.broadcasted_iota(jnp.int32, (LANES, LANES), 0)
    c2 = lax.broadcasted_iota(jnp.int32, (LANES, LANES), 1)
    rel = ((c2 & (CHUNK - 1)) - (r2 & (CHUNK - 1))) * sgn
    strict = rel < 0
    incl = rel <= 0
    eye = (r2 == c2).astype(F32)

    def stack(x):
        return jnp.concatenate([jnp.where(low, x, 0.0), jnp.where(low, 0.0, x)], axis=0)

    probs = []
    for ii in range(nch):
        off = (ii if drn == 0 else nch - 1 - ii) * CHUNK
        for pr in range(N_PAIRS):
            ls = slice(pr * LANES, (pr + 1) * LANES)
            q = {"off": off, "ls": ls, "pr": pr}
            for n, name in enumerate(("kks", "rs", "bs", "ks", "kps", "bps", "vs")):
                q[name] = stack(ops_sc[n, off:off + CHUNK, ls])
            q["wl"] = ops_sc[7, off:off + 1, ls]
            probs.append(q)
    for q in probs:
        g = _bdot_nt(jnp.concatenate([q["kks"], q["rs"]], axis=0), jnp.concatenate([q["bs"], q["ks"]], axis=0))
        q["nn"] = jnp.where(strict, -g[0:LANES, 0:LANES], 0.0)
        q["avk"] = jnp.where(strict, g[0:LANES, LANES:], 0.0)
        q["arb"] = jnp.where(incl, g[LANES:, 0:LANES], 0.0)
        q["ark"] = jnp.where(incl, g[LANES:, LANES:], 0.0)
        q["t"] = eye + q["nn"]
        q["q"] = q["nn"]
    for _ in range(5):
        for q in probs:
            q["q"] = _bdot(q["q"], q["q"])
        for q in probs:
            q["t"] = q["t"] + _bdot(q["t"], q["q"])
    for q in probs:
        q["av"] = _bdot(q["avk"], q["vs"])
    for q in probs:
        tt2 = _bdot(q["t"], jnp.concatenate([q["kks"], q["av"]], axis=1))
        q["tk"], q["tav"] = tt2[:, :LANES], tt2[:, LANES:]
    for q in probs:
        q["mb"] = _bdot_tn(q["tk"], q["bps"])
        q["nm"] = _bdot_tn(q["vs"], q["kps"]) - _bdot_tn(q["tav"], q["bps"])
        z = _bdot(q["arb"], jnp.concatenate([q["tk"], q["tav"]], axis=1))
        q["r2"] = q["rs"] - z[:, :LANES]
        q["y0"] = _bdot(q["ark"], q["vs"]) - z[:, LANES:]

    st = [s_sc[pr] for pr in range(N_PAIRS)]
    outs = []
    for q in probs:
        sm = st[q["pr"]]
        ys = _bdot_nt(q["r2"], sm) + q["y0"]
        outs.append(ys[0:CHUNK] + ys[CHUNK:])
        st[q["pr"]] = sm * q["wl"] - _bdot(sm, q["mb"]) + q["nm"]
    for q, yo in zip(probs, outs):
        y_ref[q["off"]:q["off"] + CHUNK, q["ls"]] = yo.astype(y_ref.dtype)
    for pr in range(N_PAIRS):
        s_sc[pr] = st[pr]

    @pl.when(j == nt - 1)
    def _():
        sout_ref[0] = s_sc[...]


def _rwkv(p, vf_src, hv, state_in, prm, nb, rows, drn):
    m = p.shape[0]
    tq = _row_tile(rows, 256)
    nt = rows // tq
    rb = tq // PACK
    vres = vf_src is not None

    def tile(b, j):
        return b * nt + (j if drn == 0 else nt - 1 - j)

    const2 = lambda b, j: (0, 0)
    dir3 = lambda b, j: (drn, 0, 0)
    in_specs = [pl.BlockSpec((tq, RW_IN), lambda b, j: (tile(b, j), 0)),
                pl.BlockSpec((PACK, RW_IN), lambda b, j: (jnp.maximum(tile(b, j) * rb - 1, 0), 0)),
                pl.BlockSpec((PACK, RW_IN),
                             lambda b, j: (jnp.minimum((tile(b, j) + 1) * rb, m // PACK - 1), 0))]
    args = [p, p, p]
    if vres:
        in_specs += [pl.BlockSpec((tq, RW_DIM), lambda b, j: (tile(b, j), 1)),
                     pl.BlockSpec((tq, LANES), lambda b, j: (tile(b, j), 0))]
        args += [vf_src, hv]
    in_specs += [pl.BlockSpec((2, RW_IN), const2),
                 pl.BlockSpec((1, SUBLANES, RW_DIM), dir3),
                 pl.BlockSpec((SUBLANES, RW_DIM), const2),
                 pl.BlockSpec((1, LANES, 2 * RW_DIM), dir3),
                 pl.BlockSpec((1, LANES, 2 * RW_DIM), dir3),
                 pl.BlockSpec((LANES, RW_DIM), const2)]
    args += [prm["tshift"], prm["pd"], prm["ps"], prm["wla_hi"], prm["wla_lo"], prm["wg"]]
    if vres:
        in_specs += [pl.BlockSpec((LANES, RW_DIM), const2), pl.BlockSpec((LANES, RW_DIM), const2)]
        args += [prm["vu_hi"], prm["vu_lo"]]
    sspec = pl.BlockSpec((1, N_PAIRS, LANES, LANES), lambda b, j: (b, 0, 0, 0))
    in_specs += [pl.BlockSpec((LANES, LANES), const2), sspec]
    args += [prm["e128"], state_in]
    return pl.pallas_call(
        functools.partial(_rwkv_kernel, nt=nt, tq=tq, vres=vres, drn=drn),
        grid=(nb, nt),
        in_specs=in_specs,
        out_specs=[pl.BlockSpec((tq, RW_DIM), lambda b, j: (tile(b, j), 0)),
                   pl.BlockSpec((tq, 2 * RW_DIM), lambda b, j: (tile(b, j), 0)),
                   sspec],
        out_shape=[jax.ShapeDtypeStruct((m, RW_DIM), ACT),
                   jax.ShapeDtypeStruct((m, 2 * RW_DIM), ACT),
                   jax.ShapeDtypeStruct((nb, N_PAIRS, LANES, LANES), F32)],
        scratch_shapes=[pltpu.VMEM((8, tq, RW_DIM), F32), pltpu.VMEM((N_PAIRS, LANES, LANES), F32)],
        compiler_params=_cparams("parallel", "arbitrary"),
        name="rwkv7_fwd" if drn == 0 else "rwkv7_bwd",
    )(*args)


def _rwkv_both(p, vf_src, hv, states, prm, nb, rows):
    outs = [_rwkv(p, vf_src, hv, states[d], prm, nb, rows, d) for d in range(2)]
    return [o[0] for o in outs], [o[1] for o in outs], [o[2] for o in outs]


def _merge_kernel(x_ref, y0_ref, y1_ref, a0_ref, a1_ref, ya_ref, g0_ref, g1_ref, g2_ref, g3_ref, ln_ref, e_ref,
                  wbr_ref, wba_ref, wo_ref, mod_ref, o_ref):
    e = e_ref[...]
    y = y0_ref[...].astype(F32) + y1_ref[...].astype(F32)
    mu = _segsum(y, e) * (1.0 / HEAD_DIM)
    yc = y - mu
    var = _segsum(yc * yc, e) * (1.0 / HEAD_DIM)
    yn = yc * lax.rsqrt(var + GN_EPS) * ln_ref[0:1, :] + ln_ref[1:2, :]
    cv = a0_ref[:, 0:RW_DIM].astype(F32) + a1_ref[:, 0:RW_DIM].astype(F32)
    yrw = (yn + cv) * a0_ref[:, RW_DIM:2 * RW_DIM].astype(F32)
    a = _bdot(yrw, wbr_ref[...])
    b = jnp.dot(ya_ref[...], wba_ref[...], preferred_element_type=F32)
    h = a.shape[1] // 2
    mrg = jnp.concatenate(
        [jax.nn.sigmoid(g0_ref[...].astype(F32)) * a[:, :h] + jax.nn.sigmoid(g2_ref[...].astype(F32)) * b[:, :h],
         jax.nn.sigmoid(g1_ref[...].astype(F32)) * a[:, h:] + jax.nn.sigmoid(g3_ref[...].astype(F32)) * b[:, h:]],
        axis=1)
    o_ref[...] = x_ref[...] + mod_ref[0, 2:3, :] * _bdot(mrg, wo_ref[...])


def _merge(x2, ys, auxs, ya, p, ln, e128, wbr, wba, wo, mod, rows):
    m, d = x2.shape
    tm = _row_tile(rows, 512)
    per = rows // tm
    many = mod.shape[0] > 1
    gb = GATE_OFF // 512
    const2 = lambda i: (0, 0)
    in_specs = [pl.BlockSpec((tm, d), lambda i: (i, 0)),
                pl.BlockSpec((tm, RW_DIM), lambda i: (i, 0)),
                pl.BlockSpec((tm, RW_DIM), lambda i: (i, 0)),
                pl.BlockSpec((tm, 2 * RW_DIM), lambda i: (i, 0)),
                pl.BlockSpec((tm, 2 * RW_DIM), lambda i: (i, 0)),
                pl.BlockSpec((tm, ATT_DIM), lambda i: (i, 0))]
    in_specs += [pl.BlockSpec((tm, 512), (lambda i, c=c: (i, gb + c))) for c in range(4)]
    in_specs += [pl.BlockSpec((2, RW_DIM), const2),
                 pl.BlockSpec((LANES, LANES), const2),
                 pl.BlockSpec(wbr.shape, const2),
                 pl.BlockSpec(wba.shape, const2),
                 pl.BlockSpec(wo.shape, const2),
                 pl.BlockSpec((1, 6, d), lambda i: ((i // per) if many else 0, 0, 0))]
    return pl.pallas_call(
        _merge_kernel,
        grid=(m // tm,),
        in_specs=in_specs,
        out_specs=pl.BlockSpec((tm, d), lambda i: (i, 0)),
        out_shape=jax.ShapeDtypeStruct((m, d), F32),
        compiler_params=_cparams("parallel"),
        name="branch_merge",
    )(x2, ys[0], ys[1], auxs[0], auxs[1], ya, p, p, p, p, ln, e128, wbr, wba, wo, mod)


def _ffn_kernel(x_ref, g_ref, mod_ref, wg_ref, wu_ref, wd_ref, o_ref, h_sc, acc_sc, *, nf):
    f = pl.program_id(1)

    @pl.when(f == 0)
    def _():
        h = _norm_mod(x_ref[...], g_ref[...], mod_ref[0, 3:4, :], mod_ref[0, 4:5, :])
        h_sc[...] = h.astype(BF16)
        acc_sc[...] = jnp.zeros_like(acc_sc)

    h = h_sc[...]
    a = jnp.dot(h, wg_ref[...], preferred_element_type=F32)
    u = jnp.dot(h, wu_ref[...], preferred_element_type=F32)
    act = (a * jax.nn.sigmoid(a)) * u
    acc_sc[...] += jnp.dot(act.astype(BF16), wd_ref[...], preferred_element_type=F32)

    @pl.when(f == nf - 1)
    def _():
        o_ref[...] = x_ref[...] + mod_ref[0, 5:6, :] * acc_sc[...]


def _ffn(x2, gain, mod, wg, wu, wd, rows):
    m, d = x2.shape
    dff = wg.shape[1]
    tm = _row_tile(rows, 512)
    per = rows // tm
    many = mod.shape[0] > 1
    tf = dff // 2 if (dff // 2) % LANES == 0 else dff
    nf = dff // tf
    return pl.pallas_call(
        functools.partial(_ffn_kernel, nf=nf),
        grid=(m // tm, nf),
        in_specs=[pl.BlockSpec((tm, d), lambda i, f: (i, 0)),
                  pl.BlockSpec((1, d), lambda i, f: (0, 0)),
                  pl.BlockSpec((1, 6, d), lambda i, f: ((i // per) if many else 0, 0, 0)),
                  pl.BlockSpec((d, tf), lambda i, f: (0, f)),
                  pl.BlockSpec((d, tf), lambda i, f: (0, f)),
                  pl.BlockSpec((tf, d), lambda i, f: (f, 0))],
        out_specs=pl.BlockSpec((tm, d), lambda i, f: (i, 0)),
        out_shape=jax.ShapeDtypeStruct((m, d), F32),
        scratch_shapes=[pltpu.VMEM((tm, d), BF16), pltpu.VMEM((tm, d), F32)],
        compiler_params=_cparams("parallel", "arbitrary"),
        name="ffn_swiglu",
    )(x2, gain.reshape(1, d), mod, wg, wu, wd)


def _dma_pieces(count, sizes, make):
    for k, size in enumerate(sizes):
        @pl.when((count & size) != 0)
        def _(k=k, size=size):
            make(k, pl.multiple_of(count & ~(2 * size - 1), PACK), size)


def _moe_gather_kernel(pos_ref, cnt_ref, h_ref, rankt_ref, zero_ref, out_ref, hs_sc, sem, *, ne, nsteps, sizes):
    del zero_ref
    i = pl.program_id(0)
    ex = pl.program_id(1)
    step = i * ne + ex
    slot = step % 2
    tt = h_ref.shape[0]

    def copies(st, sl, act):
        base = pos_ref[st]

        def make(k, off, size):
            act(pltpu.make_async_copy(hs_sc.at[sl, pl.ds(off, size)],
                                      out_ref.at[pl.ds(pl.multiple_of(base + off, PACK), size)], sem.at[sl, k]))

        _dma_pieces(cnt_ref[st], sizes, make)

    @pl.when(step >= 2)
    def _():
        copies(step - 2, slot, lambda cp: cp.wait())

    rrow = rankt_ref[0, pl.ds(ex, 1), :]
    rowid = lax.broadcasted_iota(jnp.int32, (MOE_BLK, tt), 0).astype(F32)

    def gather(b, carry):
        sel = ((rrow - (b * MOE_BLK).astype(F32)) == rowid).astype(BF16)
        r0 = pl.multiple_of(b * MOE_BLK, MOE_BLK)
        hs_sc[slot, pl.ds(r0, MOE_BLK), :] = jnp.dot(sel, h_ref[...], preferred_element_type=F32).astype(BF16)
        return carry

    lax.fori_loop(0, (cnt_ref[step] + MOE_BLK - 1) // MOE_BLK, gather, 0)
    copies(step, slot, lambda cp: cp.start())

    @pl.when(step == nsteps - 1)
    def _():
        if nsteps >= 2:
            copies(step - 1, 1 - slot, lambda cp: cp.wait())
        copies(step, slot, lambda cp: cp.wait())


def _moe_ffn_kernel(te_ref, tm_ref, tv_ref, hs_ref, wg_ref, wu_ref, wd_ref, os_ref, acc_sc, *, nf):
    del te_ref, tm_ref
    t = pl.program_id(0)
    f = pl.program_id(1)
    valid = tv_ref[t] != 0

    @pl.when(valid)
    def _():
        hb = hs_ref[...]
        a = jnp.dot(hb, wg_ref[0], preferred_element_type=F32)
        u = jnp.dot(hb, wu_ref[0], preferred_element_type=F32)
        act = ((a * jax.nn.sigmoid(a)) * u).astype(BF16)
        part = jnp.dot(act, wd_ref[0], preferred_element_type=F32)

        @pl.when(f == 0)
        def _():
            acc_sc[...] = part

        @pl.when(f > 0)
        def _():
            acc_sc[...] += part

    @pl.when(f == nf - 1)
    def _():
        os_ref[...] = jnp.where(valid, acc_sc[...], 0.0).astype(os_ref.dtype)


def _moe_combine_kernel(pos_ref, cnt_ref, x_ref, we_ref, rank_ref, mod_ref, os_ref, o_ref, acc_sc, ob_sc, sem,
                        *, ne, nsteps, sizes):
    i = pl.program_id(0)
    ex = pl.program_id(1)
    step = i * ne + ex
    slot = step % 2
    tt = x_ref.shape[0]
    sblk = min(2 * MOE_BLK, tt)
    lane = lax.broadcasted_iota(jnp.int32, (tt, LANES), 1)

    def copies(st, sl, act):
        base = pos_ref[st]

        def make(k, off, size):
            act(pltpu.make_async_copy(os_ref.at[pl.ds(pl.multiple_of(base + off, PACK), size)],
                                      ob_sc.at[sl, pl.ds(off, size)], sem.at[sl, k]))

        _dma_pieces(cnt_ref[st], sizes, make)

    @pl.when(step == 0)
    def _():
        ob_sc[...] = jnp.zeros_like(ob_sc)
        copies(0, 0, lambda cp: cp.start())

    @pl.when(step + 1 < nsteps)
    def _():
        copies(step + 1, 1 - slot, lambda cp: cp.start())

    copies(step, slot, lambda cp: cp.wait())

    @pl.when(ex == 0)
    def _():
        acc_sc[...] = jnp.zeros_like(acc_sc)

    rcol = jnp.sum(jnp.where(lane == ex, rank_ref[...], 0.0), axis=1, keepdims=True)
    wcol = jnp.sum(jnp.where(lane == ex, we_ref[...], 0.0), axis=1, keepdims=True)
    colid = lax.broadcasted_iota(jnp.int32, (tt, sblk), 1).astype(F32)

    def scatter(b, carry):
        sel = ((rcol - (b * sblk).astype(F32)) == colid).astype(BF16)
        r0 = pl.multiple_of(b * sblk, sblk)
        acc_sc[...] += wcol * jnp.dot(sel, ob_sc[slot, pl.ds(r0, sblk), :], preferred_element_type=F32)
        return carry

    lax.fori_loop(0, (cnt_ref[step] + sblk - 1) // sblk, scatter, 0)

    @pl.when(ex == ne - 1)
    def _():
        o_ref[...] = x_ref[...] + mod_ref[0, 5:6, :] * acc_sc[...]


def _route_kernel(x_ref, g_ref, mod_ref, rt_ref, h_ref, we_ref, rank_ref, rankt_ref, cnt_ref, *, ne):
    tt = x_ref.shape[0]
    h = _norm_mod(x_ref[...], g_ref[...], mod_ref[0, 3:4, :], mod_ref[0, 4:5, :])
    h_ref[...] = h.astype(BF16)
    lane = lax.broadcasted_iota(jnp.int32, (tt, LANES), 1)
    logits = jnp.dot(h, rt_ref[...], precision=HIGHEST, preferred_element_type=F32)
    lg = jnp.where(lane < ne, logits, NEG_INF)
    t1 = jnp.max(lg, axis=1, keepdims=True)
    i1 = jnp.min(jnp.where(lg == t1, lane, LANES), axis=1, keepdims=True)
    lg2 = jnp.where(lane == i1, NEG_INF, lg)
    t2 = jnp.max(lg2, axis=1, keepdims=True)
    i2 = jnp.min(jnp.where(lg2 == t2, lane, LANES), axis=1, keepdims=True)
    e2 = jnp.exp(t2 - t1)
    den = 1.0 + e2
    we_ref[...] = jnp.where(lane == i1, 1.0 / den, 0.0) + jnp.where(lane == i2, e2 / den, 0.0)
    sel = (lane == i1) | (lane == i2)
    self32 = sel.astype(F32)
    before = (lax.broadcasted_iota(jnp.int32, (tt, tt), 1) < lax.broadcasted_iota(jnp.int32, (tt, tt), 0))
    rank = jnp.dot(before.astype(BF16), sel.astype(BF16), preferred_element_type=F32)
    rank = jnp.where(sel, rank, -1.0)
    rank_ref[...] = rank
    rankt_ref[0] = rank.T[0:SUBLANES, :]
    cnt_ref[0] = jnp.broadcast_to(jnp.sum(self32, axis=0, keepdims=True), (SUBLANES, LANES))


def _moe(x2, gain, mod, router_pad, wg, wu, wd, rows):
    m, d = x2.shape
    ne, _, dff = wg.shape
    assert ne <= SUBLANES
    tt = _row_tile(rows, MOE_TILE)
    nti = m // tt
    per = rows // tt
    many = mod.shape[0] > 1
    h, we, rank, rankt, cnt = pl.pallas_call(
        functools.partial(_route_kernel, ne=ne),
        grid=(nti,),
        in_specs=[pl.BlockSpec((tt, d), lambda i: (i, 0)),
                  pl.BlockSpec((1, d), lambda i: (0, 0)),
                  pl.BlockSpec((1, 6, d), lambda i: ((i // per) if many else 0, 0, 0)),
                  pl.BlockSpec((d, LANES), lambda i: (0, 0))],
        out_specs=[pl.BlockSpec((tt, d), lambda i: (i, 0)),
                   pl.BlockSpec((tt, LANES), lambda i: (i, 0)),
                   pl.BlockSpec((tt, LANES), lambda i: (i, 0)),
                   pl.BlockSpec((1, SUBLANES, tt), lambda i: (i, 0, 0)),
                   pl.BlockSpec((1, SUBLANES, LANES), lambda i: (i, 0, 0))],
        out_shape=[jax.ShapeDtypeStruct((m, d), BF16),
                   jax.ShapeDtypeStruct((m, LANES), F32),
                   jax.ShapeDtypeStruct((m, LANES), F32),
                   jax.ShapeDtypeStruct((nti, SUBLANES, tt), F32),
                   jax.ShapeDtypeStruct((nti, SUBLANES, LANES), F32)],
        compiler_params=_cparams("parallel"),
        name="moe_route",
    )(x2, gain.reshape(1, d), mod, router_pad)
    c16 = (cnt[:, 0, :ne].astype(jnp.int32) + (PACK - 1)) // PACK * PACK
    region = (jnp.sum(c16, axis=0) + (MOE_FT - 1)) // MOE_FT * MOE_FT
    e_off = jnp.cumsum(region) - region
    pos = (e_off[None, :] + jnp.cumsum(c16, axis=0) - c16).reshape(-1).astype(jnp.int32)
    c16 = c16.reshape(-1)
    rmax = -(-(2 * m + nti * ne * PACK + ne * MOE_FT) // MOE_FT) * MOE_FT
    nft = rmax // MOE_FT
    tiles_e = jnp.cumsum(region // MOE_FT)
    tid = jnp.arange(nft, dtype=jnp.int32)
    tile_valid = (tid < tiles_e[-1]).astype(jnp.int32)
    tile_map = jnp.minimum(tid, tiles_e[-1] - 1).astype(jnp.int32)
    tile_exp = jnp.minimum(jnp.searchsorted(tiles_e, tile_map, side="right"), ne - 1).astype(jnp.int32)

    sizes = tuple(tt >> k for k in range(tt.bit_length()) if (tt >> k) >= PACK)
    nsteps = nti * ne
    hs = pl.pallas_call(
        functools.partial(_moe_gather_kernel, ne=ne, nsteps=nsteps, sizes=sizes),
        grid_spec=pltpu.PrefetchScalarGridSpec(
            num_scalar_prefetch=2,
            grid=(nti, ne),
            in_specs=[pl.BlockSpec((tt, d), lambda i, e, p_, c_: (i, 0)),
                      pl.BlockSpec((1, SUBLANES, tt), lambda i, e, p_, c_: (i, 0, 0)),
                      pl.BlockSpec(memory_space=pl.ANY)],
            out_specs=pl.BlockSpec(memory_space=pl.ANY),
            scratch_shapes=[pltpu.VMEM((2, tt, d), BF16), pltpu.SemaphoreType.DMA((2, len(sizes)))]),
        out_shape=jax.ShapeDtypeStruct((rmax, d), BF16),
        input_output_aliases={4: 0},
        compiler_params=_cparams("arbitrary", "arbitrary"),
        name="moe_gather",
    )(pos, c16, h, rankt, jnp.zeros((rmax, d), BF16))

    tf = dff // 2 if (dff // 2) % LANES == 0 else dff
    nf = dff // tf
    os_sorted = pl.pallas_call(
        functools.partial(_moe_ffn_kernel, nf=nf),
        grid_spec=pltpu.PrefetchScalarGridSpec(
            num_scalar_prefetch=3,
            grid=(nft, nf),
            in_specs=[pl.BlockSpec((MOE_FT, d), lambda t, f, te, tm, tv: (tm[t], 0)),
                      pl.BlockSpec((1, d, tf), lambda t, f, te, tm, tv: (te[t], 0, f)),
                      pl.BlockSpec((1, d, tf), lambda t, f, te, tm, tv: (te[t], 0, f)),
                      pl.BlockSpec((1, tf, d), lambda t, f, te, tm, tv: (te[t], f, 0))],
            out_specs=pl.BlockSpec((MOE_FT, d), lambda t, f, te, tm, tv: (t, 0)),
            scratch_shapes=[pltpu.VMEM((MOE_FT, d), F32)]),
        out_shape=jax.ShapeDtypeStruct((rmax, d), BF16),
        compiler_params=_cparams("parallel", "arbitrary"),
        name="moe_experts",
    )(tile_exp, tile_map, tile_valid, hs, wg, wu, wd)

    return pl.pallas_call(
        functools.partial(_moe_combine_kernel, ne=ne, nsteps=nsteps, sizes=sizes),
        grid_spec=pltpu.PrefetchScalarGridSpec(
            num_scalar_prefetch=2,
            grid=(nti, ne),
            in_specs=[pl.BlockSpec((tt, d), lambda i, e, p_, c_: (i, 0)),
                      pl.BlockSpec((tt, LANES), lambda i, e, p_, c_: (i, 0)),
                      pl.BlockSpec((tt, LANES), lambda i, e, p_, c_: (i, 0)),
                      pl.BlockSpec((1, 6, d), lambda i, e, p_, c_: ((i // per) if many else 0, 0, 0)),
                      pl.BlockSpec(memory_space=pl.ANY)],
            out_specs=pl.BlockSpec((tt, d), lambda i, e, p_, c_: (i, 0)),
            scratch_shapes=[pltpu.VMEM((tt, d), F32), pltpu.VMEM((2, tt, d), BF16),
                            pltpu.SemaphoreType.DMA((2, len(sizes)))]),
        out_shape=jax.ShapeDtypeStruct((m, d), F32),
        compiler_params=_cparams("arbitrary", "arbitrary"),
        name="moe_combine",
    )(pos, c16, x2, we, rank, mod, os_sorted)


def _rope_table(t):
    rows = t // GRID_W
    row = jnp.repeat(jnp.arange(rows, dtype=F32), GRID_W)
    col = jnp.tile(jnp.arange(GRID_W, dtype=F32), rows)
    half = HEAD_DIM // 2
    inv_freq = ROPE_THETA ** (-jnp.arange(0, half, 2, dtype=F32) / half)
    ar = row[:, None] * inv_freq
    ac = col[:, None] * inv_freq
    ang = jnp.concatenate([ar, ar, ac, ac] * 2, axis=-1)
    return jnp.stack([jnp.cos(ang), jnp.sin(ang)])


def _pad_rows(a, rows):
    return jnp.zeros((rows,) + a.shape[1:], a.dtype).at[:a.shape[0]].set(a)


def _hi_lo(w):
    hi = w.astype(BF16)
    return hi, (w - hi.astype(F32)).astype(BF16)


def kernel(x, c, ctx, c_ctx, norm1, norm2, w_mod, b_mod, w_in, tshift, w0, w_decay_up, a0, w_iclr_up, v0, w_vres_down, w_vres_up, w_gate_up, k_k, k_a, r_k, lnx_g, lnx_b, q_norm, k_norm, sink, w_br_rwkv, w_br_attn, w_out, ffn_gate, ffn_up, ffn_down, router, moe_gate, moe_up, moe_down):
    nb, t, d = x.shape
    cn = ctx.shape[1]
    depth = w_in.shape[0]
    mods = _modulation(c, c_ctx, w_mod, b_mod)
    cs = _rope_table(t)
    e128 = jnp.kron(jnp.eye(2, dtype=F32), jnp.ones((HEAD_DIM, HEAD_DIM), F32)).astype(BF16)
    xl = x.reshape(nb * t, d)
    xc = ctx.reshape(nb * cn, d)
    vf_l = vf_c = None
    for layer in range(depth):
        last = layer == depth - 1
        mod_l = mods[layer, :nb].reshape(nb, 6, d)
        mod_c = mods[layer, nb:nb + 1].reshape(1, 6, d)
        w_in_b = w_in[layer].astype(BF16)
        vd_b = None
        prm = {"tshift": tshift[layer], "e128": e128, "wg": w_gate_up[layer].astype(BF16)}
        prm["pd"] = jnp.stack(
            [_pad_rows(jnp.stack([w0[layer, dd], a0[layer, dd]]), SUBLANES) for dd in range(2)])
        shared = [k_k[layer], k_a[layer], r_k[layer].reshape(RW_DIM)]
        if layer > 0:
            shared.append(v0[layer - 1])
            vd_b = _pad_rows(w_vres_down[layer - 1].T, LANES).T.astype(BF16)
            prm["vu_hi"], prm["vu_lo"] = _hi_lo(_pad_rows(w_vres_up[layer - 1], LANES))
        prm["ps"] = _pad_rows(jnp.stack(shared), SUBLANES)
        wla = jnp.zeros((2, LANES, 2 * RW_DIM), F32)
        wla = wla.at[:, :HEAD_DIM, :RW_DIM].set(w_decay_up[layer]).at[:, HEAD_DIM:, RW_DIM:].set(w_iclr_up[layer])
        prm["wla_hi"], prm["wla_lo"] = _hi_lo(wla)

        p_l, hv_l = _inproj(xl, norm1[layer], mod_l, w_in_b, vd_b, t)
        p_c, hv_c = _inproj(xc, norm1[layer], mod_c, w_in_b, vd_b, cn)

        s0 = jnp.zeros((nb, N_PAIRS, LANES, LANES), F32)
        y_c, aux_c, s_ctx = _rwkv_both(p_c, vf_c, hv_c, [s0, s0], prm, nb, cn)
        y_l, aux_l, _ = _rwkv_both(p_l, vf_l, hv_l, s_ctx, prm, nb, t)
        if layer == 0:
            vf_l, vf_c = aux_l[1], aux_c[1]

        qg = jnp.tile(q_norm[layer], 2).reshape(1, LANES)
        kg = jnp.tile(k_norm[layer], 2).reshape(1, LANES)
        qn_l, kn_l, vb_l = _qk_prep(p_l, cs, qg, kg, e128, t, True)
        qn_c, kn_c, vb_c = _qk_prep(p_c, None, qg, kg, e128, cn, False)
        sk = sink[layer].reshape(1, -1)
        ya_l = _attention(qn_l, kn_l, vb_l, kn_c, vb_c, sk, nb, t, cn, True)

        ln = jnp.stack([lnx_g[layer], lnx_b[layer]])
        wbr = w_br_rwkv[layer].astype(BF16)
        wba = w_br_attn[layer].astype(BF16)
        wo = w_out[layer].astype(BF16)
        xl = _merge(xl, y_l, aux_l, ya_l, p_l, ln, e128, wbr, wba, wo, mod_l, t)
        if not last:
            ya_c = _attention(qn_c, None, None, kn_c, vb_c, sk, nb, cn, cn, False)
            xc = _merge(xc, y_c, aux_c, ya_c, p_c, ln, e128, wbr, wba, wo, mod_c, cn)

        i = layer // 2
        if layer % 2 == 0:
            fw = (ffn_gate[i].astype(BF16), ffn_up[i].astype(BF16), ffn_down[i].astype(BF16))
            xl = _ffn(xl, norm2[layer], mod_l, *fw, t)
            if not last:
                xc = _ffn(xc, norm2[layer], mod_c, *fw, cn)
        else:
            rt = _pad_rows(router[i].T, LANES).T
            mw = (moe_gate[i].astype(BF16), moe_up[i].astype(BF16), moe_down[i].astype(BF16))
            xl = _moe(xl, norm2[layer], mod_l, rt, *mw, t)
            if not last:
                xc = _moe(xc, norm2[layer], mod_c, rt, *mw, cn)
    return xl.reshape(nb, t, d)
```

```python
import functools

import jax
import jax.numpy as jnp
from jax import lax
from jax.experimental import pallas as pl
from jax.experimental.pallas import tpu as pltpu

F32 = jnp.float32
BF16 = jnp.bfloat16
HIGHEST = lax.Precision.HIGHEST

LANES = 128
SUBLANES = 8
PACK = 16
ACT = jnp.bfloat16
HEAD_DIM = 64
RW_DIM = 512
ATT_DIM = 512
KV_DIM = 128
RW_IN = 1792
ATT_OFF = RW_IN
GATE_OFF = RW_IN + ATT_DIM + 2 * KV_DIM
N_EXPERTS = 8
CHUNK = 64
N_PAIRS = RW_DIM // LANES
MOE_TILE = 1024
MOE_BLK = 128
MOE_FT = 512
ATT_SCALE = HEAD_DIM ** -0.5
ROPE_THETA = 10000.0
GRID_W = 64
RMS_EPS = 1e-6
GN_EPS = 64e-5
NEG_INF = -1e30
DECAY_SCALE = 0.6065306597126334
VMEM_LIMIT = 48 * 1024 * 1024


def _cparams(*sem):
    return pltpu.CompilerParams(dimension_semantics=sem, vmem_limit_bytes=VMEM_LIMIT)


def _row_tile(rows, cap):
    t = cap
    while rows % t:
        t //= 2
    return t


def _bdot(a, b):
    return jnp.dot(a.astype(BF16), b.astype(BF16), preferred_element_type=F32)


def _bdot_nt(a, b):
    return lax.dot_general(a.astype(BF16), b.astype(BF16), (((1,), (1,)), ((), ())),
                           preferred_element_type=F32)


def _bdot_tn(a, b):
    return lax.dot_general(a.astype(BF16), b.astype(BF16), (((0,), (0,)), ((), ())),
                           preferred_element_type=F32)


def _split2(x):
    hi = x.astype(BF16)
    lo = (x - hi.astype(F32)).astype(BF16)
    return hi, lo


def _split3(x):
    hi = x.astype(BF16)
    r = x - hi.astype(F32)
    mid = r.astype(BF16)
    lo = (r - mid.astype(F32)).astype(BF16)
    return hi, mid, lo


def _dot3(a, w_hi, w_lo):
    a_hi, a_lo = _split2(a)
    return (jnp.dot(a_hi, w_hi, preferred_element_type=F32)
            + (jnp.dot(a_lo, w_hi, preferred_element_type=F32)
               + jnp.dot(a_hi, w_lo, preferred_element_type=F32)))


def _segsum(x, e):
    outs = []
    for s in range(x.shape[1] // LANES):
        hi, lo = _split2(x[:, s * LANES:(s + 1) * LANES])
        outs.append(jnp.dot(hi, e, preferred_element_type=F32)
                    + jnp.dot(lo, e, preferred_element_type=F32))
    return outs[0] if len(outs) == 1 else jnp.concatenate(outs, axis=1)


def _norm_mod(x, gain, shift, scale):
    y = x * lax.rsqrt(jnp.mean(x * x, axis=-1, keepdims=True) + RMS_EPS) * gain
    return y * (1.0 + scale) + shift


def _mod_kernel(c_ref, w_ref, b_ref, o_ref):
    c = c_ref[...]
    s = c * jax.nn.sigmoid(c)
    o_ref[0] = jnp.dot(s, w_ref[0], precision=HIGHEST, preferred_element_type=F32) + b_ref[0]


def _modulation(c, c_ctx, w_mod, b_mod):
    depth, d, n6 = w_mod.shape
    nb = c.shape[0]
    rm = -(-(nb + 1) // SUBLANES) * SUBLANES
    cc = jnp.zeros((rm, d), F32).at[:nb].set(c).at[nb].set(c_ctx)
    tn = _row_tile(n6, 1536)
    return pl.pallas_call(
        _mod_kernel,
        grid=(depth, n6 // tn),
        in_specs=[pl.BlockSpec((rm, d), lambda l, n: (0, 0)),
                  pl.BlockSpec((1, d, tn), lambda l, n: (l, 0, n)),
                  pl.BlockSpec((1, 1, tn), lambda l, n: (l, 0, n))],
        out_specs=pl.BlockSpec((1, rm, tn), lambda l, n: (l, 0, n)),
        out_shape=jax.ShapeDtypeStruct((depth, rm, n6), F32),
        compiler_params=_cparams("parallel", "parallel"),
        name="adaln_mod",
    )(cc, w_mod, b_mod.reshape(depth, 1, n6))


def _inproj_kernel(*refs, vres):
    if vres:
        x_ref, g_ref, mod_ref, w_ref, vd_ref, o_ref, hv_ref, h_sc = refs
    else:
        x_ref, g_ref, mod_ref, w_ref, o_ref, h_sc = refs

    @pl.when(pl.program_id(1) == 0)
    def _():
        h = _norm_mod(x_ref[...], g_ref[...], mod_ref[0, 0:1, :], mod_ref[0, 1:2, :])
        hb = h.astype(BF16)
        h_sc[...] = hb
        if vres:
            hv_ref[...] = jnp.dot(hb, vd_ref[...], preferred_element_type=F32)

    o_ref[...] = jnp.dot(h_sc[...], w_ref[...], preferred_element_type=F32).astype(o_ref.dtype)


def _inproj(x2, gain, mod, w_b, vd_b, rows):
    m, d = x2.shape
    n = w_b.shape[1]
    tm = _row_tile(rows, 1024)
    tn = _row_tile(n, 1536)
    per = rows // tm
    many = mod.shape[0] > 1
    vres = vd_b is not None
    in_specs = [pl.BlockSpec((tm, d), lambda i, j: (i, 0)),
                pl.BlockSpec((1, d), lambda i, j: (0, 0)),
                pl.BlockSpec((1, 6, d), lambda i, j: ((i // per) if many else 0, 0, 0)),
                pl.BlockSpec((d, tn), lambda i, j: (0, j))]
    args = [x2, gain.reshape(1, d), mod, w_b]
    out_specs = [pl.BlockSpec((tm, tn), lambda i, j: (i, j))]
    out_shape = [jax.ShapeDtypeStruct((m, n), ACT)]
    if vres:
        in_specs.append(pl.BlockSpec((d, LANES), lambda i, j: (0, 0)))
        args.append(vd_b)
        out_specs.append(pl.BlockSpec((tm, LANES), lambda i, j: (i, 0)))
        out_shape.append(jax.ShapeDtypeStruct((m, LANES), F32))
    outs = pl.pallas_call(
        functools.partial(_inproj_kernel, vres=vres),
        grid=(m // tm, n // tn),
        in_specs=in_specs, out_specs=out_specs, out_shape=out_shape,
        scratch_shapes=[pltpu.VMEM((tm, d), BF16)],
        compiler_params=_cparams("parallel", "arbitrary"),
        name="in_proj",
    )(*args)
    return (outs[0], outs[1]) if vres else (outs[0], None)


def _qkprep_kernel(*refs, rope):
    if rope:
        q0_ref, q1_ref, k_ref, v_ref, cs_ref, qg_ref, kg_ref, e_ref, qn_ref, kn_ref, vb_ref = refs
    else:
        q0_ref, q1_ref, k_ref, v_ref, qg_ref, kg_ref, e_ref, qn_ref, kn_ref, vb_ref = refs
    e = e_ref[...]
    lane = lax.broadcasted_iota(jnp.int32, (1, LANES), 1)
    first = (lane & 31) < 16

    def norm_rope(u, gain):
        y = u * lax.rsqrt(_segsum(u * u, e) * (1.0 / HEAD_DIM) + RMS_EPS) * gain
        if rope:
            rot = jnp.where(first, -pltpu.roll(y, LANES - 16, 1), pltpu.roll(y, 16, 1))
            y = y * cs_ref[0] + rot * cs_ref[1]
        return y

    qg = qg_ref[...]
    for s, ref in enumerate((q0_ref, q1_ref)):
        for t in range(2):
            u = ref[:, t * LANES:(t + 1) * LANES].astype(F32)
            c0 = (2 * s + t) * LANES
            qn_ref[:, c0:c0 + LANES] = (norm_rope(u, qg) * ATT_SCALE).astype(BF16)
    kn_ref[...] = norm_rope(k_ref[...].astype(F32), kg_ref[...]).astype(BF16)
    vb_ref[...] = v_ref[...].astype(BF16)


def _qk_prep(p, cs, qg, kg, e128, rows, rope):
    m = p.shape[0]
    tm = _row_tile(rows, 512)
    per = rows // tm
    qb = ATT_OFF // 256
    kb = (ATT_OFF + ATT_DIM) // LANES
    in_specs = [pl.BlockSpec((tm, 256), lambda i: (i, qb)),
                pl.BlockSpec((tm, 256), lambda i: (i, qb + 1)),
                pl.BlockSpec((tm, LANES), lambda i: (i, kb)),
                pl.BlockSpec((tm, LANES), lambda i: (i, kb + 1))]
    args = [p, p, p, p]
    if rope:
        in_specs.append(pl.BlockSpec((2, tm, LANES), lambda i: (0, i % per, 0)))
        args.append(cs)
    in_specs += [pl.BlockSpec((1, LANES), lambda i: (0, 0)),
                 pl.BlockSpec((1, LANES), lambda i: (0, 0)),
                 pl.BlockSpec((LANES, LANES), lambda i: (0, 0))]
    args += [qg, kg, e128]
    return pl.pallas_call(
        functools.partial(_qkprep_kernel, rope=rope),
        grid=(m // tm,),
        in_specs=in_specs,
        out_specs=[pl.BlockSpec((tm, ATT_DIM), lambda i: (i, 0)),
                   pl.BlockSpec((tm, LANES), lambda i: (i, 0)),
                   pl.BlockSpec((tm, LANES), lambda i: (i, 0))],
        out_shape=[jax.ShapeDtypeStruct((m, ATT_DIM), BF16),
                   jax.ShapeDtypeStruct((m, KV_DIM), BF16),
                   jax.ShapeDtypeStruct((m, KV_DIM), BF16)],
        compiler_params=_cparams("parallel"),
        name="qk_prep",
    )(*args)


def _attn_kernel(*refs, nq, local):
    if local:
        sink_ref, q_ref, kp_ref, kc_ref, kn_ref, vp_ref, vc_ref, vn_ref, kx_ref, vx_ref, o_ref = refs
        k = jnp.concatenate([kp_ref[...], kc_ref[...], kn_ref[...], kx_ref[...]], axis=0).astype(F32)
        v = jnp.concatenate([vp_ref[...], vc_ref[...], vn_ref[...], vx_ref[...]], axis=0).astype(F32)
    else:
        sink_ref, q_ref, kx_ref, vx_ref, o_ref = refs
        k = kx_ref[...].astype(F32)
        v = vx_ref[...].astype(F32)
    i = pl.program_id(1)
    bq = q_ref.shape[0]
    low = lax.broadcasted_iota(jnp.int32, (1, LANES), 1) < HEAD_DIM
    ksw = pltpu.roll(k, HEAD_DIM, 1)
    vsw = pltpu.roll(v, HEAD_DIM, 1)
    row = lax.broadcasted_iota(jnp.int32, (4 * bq, 1), 0)
    if local:
        qo = lax.broadcasted_iota(jnp.int32, (4 * bq, 3 * bq), 0) & (bq - 1)
        kcol = lax.broadcasted_iota(jnp.int32, (4 * bq, 3 * bq), 1)
        ko = kcol & (bq - 1)
        kblk = kcol // bq
        ok = ((kblk == 1)
              | ((kblk == 0) & (ko >= qo) & (i > 0))
              | ((kblk == 2) & (ko <= qo) & (i < nq - 1)))
    for g in range(2):
        kb = (jnp.where(low, k, ksw) if g == 0 else jnp.where(low, ksw, k)).astype(BF16)
        vb = (jnp.where(low, v, vsw) if g == 0 else jnp.where(low, vsw, v)).astype(BF16)
        qa = q_ref[:, (2 * g) * LANES:(2 * g + 1) * LANES].astype(F32)
        qb = q_ref[:, (2 * g + 1) * LANES:(2 * g + 2) * LANES].astype(F32)
        qs = jnp.concatenate([jnp.where(low, qa, 0.0), jnp.where(low, 0.0, qa),
                              jnp.where(low, qb, 0.0), jnp.where(low, 0.0, qb)], axis=0).astype(BF16)
        s = lax.dot_general(qs, kb, (((1,), (1,)), ((), ())), preferred_element_type=F32)
        if local:
            s = jnp.concatenate([jnp.where(ok, s[:, :3 * bq], NEG_INF), s[:, 3 * bq:]], axis=1)
        sk = jnp.where(row < bq, sink_ref[0, 4 * g],
                       jnp.where(row < 2 * bq, sink_ref[0, 4 * g + 1],
                                 jnp.where(row < 3 * bq, sink_ref[0, 4 * g + 2], sink_ref[0, 4 * g + 3])))
        mx = jnp.maximum(jnp.max(s, axis=1, keepdims=True), sk)
        ex = jnp.exp(s - mx)
        den = jnp.sum(ex, axis=1, keepdims=True) + jnp.exp(sk - mx)
        pr = (ex / den).astype(BF16)
        o = jnp.dot(pr, vb, preferred_element_type=F32)
        o_ref[:, (2 * g) * LANES:(2 * g + 1) * LANES] = jnp.where(low, o[0:bq], o[bq:2 * bq]).astype(o_ref.dtype)
        o_ref[:, (2 * g + 1) * LANES:(2 * g + 2) * LANES] = jnp.where(
            low, o[2 * bq:3 * bq], o[3 * bq:4 * bq]).astype(o_ref.dtype)


def _attention(qn, kn, vb, kctx, vctx, sink, nb, rows, cn, local):
    m = qn.shape[0]
    bq = 128
    nq = rows // bq
    smem = pl.BlockSpec(memory_space=pltpu.SMEM)
    qspec = pl.BlockSpec((bq, ATT_DIM), lambda b, i: (b * nq + i, 0))
    cspec = pl.BlockSpec((cn, KV_DIM), lambda b, i: (b, 0))
    if local:
        prv = pl.BlockSpec((bq, KV_DIM), lambda b, i: (b * nq + jnp.maximum(i - 1, 0), 0))
        cur = pl.BlockSpec((bq, KV_DIM), lambda b, i: (b * nq + i, 0))
        nxt = pl.BlockSpec((bq, KV_DIM), lambda b, i: (b * nq + jnp.minimum(i + 1, nq - 1), 0))
        in_specs = [smem, qspec, prv, cur, nxt, prv, cur, nxt, cspec, cspec]
        args = [sink, qn, kn, kn, kn, vb, vb, vb, kctx, vctx]
    else:
        in_specs = [smem, qspec, cspec, cspec]
        args = [sink, qn, kctx, vctx]
    return pl.pallas_call(
        functools.partial(_attn_kernel, nq=nq, local=local),
        grid=(nb, nq),
        in_specs=in_specs,
        out_specs=pl.BlockSpec((bq, ATT_DIM), lambda b, i: (b * nq + i, 0)),
        out_shape=jax.ShapeDtypeStruct((m, ATT_DIM), BF16),
        compiler_params=_cparams("parallel", "parallel"),
        name="window_attn" if local else "ctx_attn",
    )(*args)


def _rwkv_kernel(*refs, nt, tq, vres, drn):
    it = iter(refs)
    p_ref, pv_ref, nx_ref = next(it), next(it), next(it)
    if vres:
        vf_ref, hv_ref = next(it), next(it)
    ts_ref, pd_ref, ps_ref, wlh_ref, wll_ref, wg_ref = (next(it) for _ in range(6))
    if vres:
        vuh_ref, vul_ref = next(it), next(it)
    e_ref, sin_ref = next(it), next(it)
    y_ref, aux_ref, sout_ref = next(it), next(it), next(it)
    ops_sc, s_sc = next(it), next(it)

    j = pl.program_id(1)
    sgn = 1 - 2 * drn
    tile = j if drn == 0 else nt - 1 - j
    nch = tq // CHUNK

    @pl.when(j == 0)
    def _():
        s_sc[...] = sin_ref[0]

    rw = p_ref[...].astype(F32)
    row = lax.broadcasted_iota(jnp.int32, (tq, 1), 0)
    prev_row = jnp.where(tile > 0, pv_ref[PACK - 1:PACK, :].astype(F32), 0.0)
    next_row = jnp.where(tile < nt - 1, nx_ref[0:1, :].astype(F32), 0.0)
    prv = jnp.where(row == 0, prev_row, pltpu.roll(rw, 1, 0))
    nxt = jnp.where(row == tq - 1, next_row, pltpu.roll(rw, tq - 1, 0))
    xs = rw + ts_ref[0:1, :] * (prv - rw) + ts_ref[1:2, :] * (nxt - rw)

    e = e_ref[...]
    r = xs[:, 0:RW_DIM]
    k = xs[:, RW_DIM:2 * RW_DIM]
    v = xs[:, 2 * RW_DIM:3 * RW_DIM]
    la = xs[:, 3 * RW_DIM:3 * RW_DIM + LANES]
    lane = lax.broadcasted_iota(jnp.int32, (1, LANES), 1)
    low = lane < HEAD_DIM
    la = jnp.where(low, jnp.tanh(la), la)
    lora = _dot3(la, wlh_ref[0], wll_ref[0])
    logw = -DECAY_SCALE * jax.nn.sigmoid(pd_ref[0, 0:1, :] + lora[:, 0:RW_DIM])
    iclr = jax.nn.sigmoid(pd_ref[0, 1:2, :] + lora[:, RW_DIM:2 * RW_DIM])
    if vres:
        mix = jax.nn.sigmoid(ps_ref[3:4, :] + _dot3(hv_ref[...], vuh_ref[...], vul_ref[...]))
        v = v + (vf_ref[...].astype(F32) - v) * mix
    kk = k * ps_ref[0:1, :]
    kk = kk * lax.rsqrt(_segsum(kk * kk, e) + 1e-12)
    kh = k * (1.0 + (iclr - 1.0) * ps_ref[1:2, :])
    bb = kk * iclr
    aux_ref[:, 0:RW_DIM] = (_segsum(r * kh * ps_ref[2:3, :], e) * v).astype(aux_ref.dtype)
    if drn == 0:
        glo = xs[:, 3 * RW_DIM + LANES:RW_IN]
        aux_ref[:, RW_DIM:2 * RW_DIM] = _bdot(jax.nn.sigmoid(glo), wg_ref[...]).astype(aux_ref.dtype)
    else:
        aux_ref[:, RW_DIM:2 * RW_DIM] = v.astype(aux_ref.dtype)

    ti = lax.broadcasted_iota(jnp.int32, (tq, tq), 0)
    si = lax.broadcasted_iota(jnp.int32, (tq, tq), 1)
    same = (ti // CHUNK) == (si // CHUNK)
    tri = (same & (((si - ti) * sgn) <= 0)).astype(BF16)
    ones = same.astype(BF16)
    cin = jnp.zeros_like(logw)
    ctot = jnp.zeros_like(logw)
    for part in _split3(logw):
        cin = cin + jnp.dot(tri, part, preferred_element_type=F32)
        ctot = ctot + jnp.dot(ones, part, preferred_element_type=F32)
    e_neg = jnp.exp(-cin)
    e_rem = jnp.exp(ctot - cin)
    ops_sc[0] = kk * jnp.exp(cin - logw)
    ops_sc[1] = r * jnp.exp(cin)
    ops_sc[2] = bb * e_neg
    ops_sc[3] = kh * e_neg
    ops_sc[4] = kh * e_rem
    ops_sc[5] = bb * e_rem
    ops_sc[6] = v
    ops_sc[7] = jnp.exp(ctot)

    tr = lax.broadcasted_iota(jnp.int32, (CHUNK, LANES), 0)
    sc = lax.broadcasted_iota(jnp.int32, (CHUNK, LANES), 1) & (CHUNK - 1)
    rel = (sc - tr) * sgn
    strict = rel < 0
    incl = rel <= 0
    eye = (sc == tr).astype(F32)
    same_head = ((lax.broadcasted_iota(jnp.int32, (LANES, LANES), 0) // HEAD_DIM)
                 == (lax.broadcasted_iota(jnp.int32, (LANES, LANES), 1) // HEAD_DIM))

    def stack(x):
        return jnp.concatenate([jnp.where(low, x, 0.0), jnp.where(low, 0.0, x)], axis=0)

    def hdot(x, y):
        return _bdot(x, stack(y))

    probs = []
    for ii in range(nch):
        off = (ii if drn == 0 else nch - 1 - ii) * CHUNK
        for pr in range(N_PAIRS):
            ls = slice(pr * LANES, (pr + 1) * LANES)
            q = {"off": off, "ls": ls, "pr": pr}
            for n, name in enumerate(("kkt", "rt", "bh", "kh", "kp", "bp", "v")):
                q[name] = ops_sc[n, off:off + CHUNK, ls]
            q["wl"] = ops_sc[7, off:off + 1, ls]
            probs.append(q)
    for q in probs:
        g = _bdot_nt(jnp.concatenate([q["kkt"], q["rt"]], axis=0),
                     jnp.concatenate([stack(q["bh"]), stack(q["kh"])], axis=0))
        q["nn"] = jnp.where(strict, -g[0:CHUNK, 0:LANES], 0.0)
        q["avk"] = jnp.where(strict, g[0:CHUNK, LANES:], 0.0)
        q["arb"] = jnp.where(incl, g[CHUNK:, 0:LANES], 0.0)
        q["ark"] = jnp.where(incl, g[CHUNK:, LANES:], 0.0)
        q["t"] = eye + q["nn"]
    for q in probs:
        q["q"] = hdot(q["nn"], q["nn"])
    for _ in range(4):
        for q in probs:
            tq = _bdot(jnp.concatenate([q["t"], q["q"]], axis=0), stack(q["q"]))
            q["t"] = q["t"] + tq[:CHUNK]
            q["q"] = tq[CHUNK:]
    for q in probs:
        q["t"] = q["t"] + hdot(q["t"], q["q"])
    for q in probs:
        av = _bdot(jnp.concatenate([q["avk"], q["ark"]], axis=0), stack(q["v"]))
        q["av"], q["arkv"] = av[:CHUNK], av[CHUNK:]
    for q in probs:
        tt2 = _bdot(q["t"], jnp.concatenate([stack(q["kkt"]), stack(q["av"])], axis=1))
        q["tk"], q["tav"] = tt2[:, :LANES], tt2[:, LANES:]
    for q in probs:
        tn = _bdot_tn(jnp.concatenate([q["tk"], q["tav"]], axis=1), q["bp"])
        q["mb"] = jnp.where(same_head, tn[:LANES], 0.0)
        q["nm"] = jnp.where(same_head, _bdot_tn(q["v"], q["kp"]) - tn[LANES:], 0.0)
        z = _bdot(q["arb"], jnp.concatenate([stack(q["tk"]), stack(q["tav"])], axis=1))
        q["r2"] = q["rt"] - z[:, :LANES]
        q["y0"] = q["arkv"] - z[:, LANES:]

    st = [s_sc[pr] for pr in range(N_PAIRS)]
    outs = []
    for q in probs:
        sm = st[q["pr"]]
        outs.append(_bdot_nt(q["r2"], sm) + q["y0"])
        st[q["pr"]] = sm * q["wl"] - _bdot(sm, q["mb"]) + q["nm"]
    for q, yo in zip(probs, outs):
        y_ref[q["off"]:q["off"] + CHUNK, q["ls"]] = yo.astype(y_ref.dtype)
    for pr in range(N_PAIRS):
        s_sc[pr] = st[pr]

    @pl.when(j == nt - 1)
    def _():
        sout_ref[0] = s_sc[...]


def _rwkv(p, vf_src, hv, state_in, prm, nb, rows, drn):
    m = p.shape[0]
    tq = _row_tile(rows, 256)
    nt = rows // tq
    rb = tq // PACK
    vres = vf_src is not None

    def tile(b, j):
        return b * nt + (j if drn == 0 else nt - 1 - j)

    const2 = lambda b, j: (0, 0)
    dir3 = lambda b, j: (drn, 0, 0)
    in_specs = [pl.BlockSpec((tq, RW_IN), lambda b, j: (tile(b, j), 0)),
                pl.BlockSpec((PACK, RW_IN), lambda b, j: (jnp.maximum(tile(b, j) * rb - 1, 0), 0)),
                pl.BlockSpec((PACK, RW_IN),
                             lambda b, j: (jnp.minimum((tile(b, j) + 1) * rb, m // PACK - 1), 0))]
    args = [p, p, p]
    if vres:
        in_specs += [pl.BlockSpec((tq, RW_DIM), lambda b, j: (tile(b, j), 1)),
                     pl.BlockSpec((tq, LANES), lambda b, j: (tile(b, j), 0))]
        args += [vf_src, hv]
    in_specs += [pl.BlockSpec((2, RW_IN), const2),
                 pl.BlockSpec((1, SUBLANES, RW_DIM), dir3),
                 pl.BlockSpec((SUBLANES, RW_DIM), const2),
                 pl.BlockSpec((1, LANES, 2 * RW_DIM), dir3),
                 pl.BlockSpec((1, LANES, 2 * RW_DIM), dir3),
                 pl.BlockSpec((LANES, RW_DIM), const2)]
    args += [prm["tshift"], prm["pd"], prm["ps"], prm["wla_hi"], prm["wla_lo"], prm["wg"]]
    if vres:
        in_specs += [pl.BlockSpec((LANES, RW_DIM), const2), pl.BlockSpec((LANES, RW_DIM), const2)]
        args += [prm["vu_hi"], prm["vu_lo"]]
    sspec = pl.BlockSpec((1, N_PAIRS, LANES, LANES), lambda b, j: (b, 0, 0, 0))
    in_specs += [pl.BlockSpec((LANES, LANES), const2), sspec]
    args += [prm["e128"], state_in]
    return pl.pallas_call(
        functools.partial(_rwkv_kernel, nt=nt, tq=tq, vres=vres, drn=drn),
        grid=(nb, nt),
        in_specs=in_specs,
        out_specs=[pl.BlockSpec((tq, RW_DIM), lambda b, j: (tile(b, j), 0)),
                   pl.BlockSpec((tq, 2 * RW_DIM), lambda b, j: (tile(b, j), 0)),
                   sspec],
        out_shape=[jax.ShapeDtypeStruct((m, RW_DIM), ACT),
                   jax.ShapeDtypeStruct((m, 2 * RW_DIM), ACT),
                   jax.ShapeDtypeStruct((nb, N_PAIRS, LANES, LANES), F32)],
        scratch_shapes=[pltpu.VMEM((8, tq, RW_DIM), F32), pltpu.VMEM((N_PAIRS, LANES, LANES), F32)],
        compiler_params=_cparams("parallel", "arbitrary"),
        name="rwkv7_fwd" if drn == 0 else "rwkv7_bwd",
    )(*args)


def _rwkv_both(p, vf_src, hv, states, prm, nb, rows):
    outs = [_rwkv(p, vf_src, hv, states[d], prm, nb, rows, d) for d in range(2)]
    return [o[0] for o in outs], [o[1] for o in outs], [o[2] for o in outs]


def _merge_kernel(x_ref, y0_ref, y1_ref, a0_ref, a1_ref, ya_ref, g0_ref, g1_ref, g2_ref, g3_ref, ln_ref, e_ref,
                  wbr_ref, wba_ref, wo_ref, mod_ref, o_ref):
    e = e_ref[...]
    y = y0_ref[...].astype(F32) + y1_ref[...].astype(F32)
    mu = _segsum(y, e) * (1.0 / HEAD_DIM)
    yc = y - mu
    var = _segsum(yc * yc, e) * (1.0 / HEAD_DIM)
    yn = yc * lax.rsqrt(var + GN_EPS) * ln_ref[0:1, :] + ln_ref[1:2, :]
    cv = a0_ref[:, 0:RW_DIM].astype(F32) + a1_ref[:, 0:RW_DIM].astype(F32)
    yrw = (yn + cv) * a0_ref[:, RW_DIM:2 * RW_DIM].astype(F32)
    a = _bdot(yrw, wbr_ref[...])
    b = jnp.dot(ya_ref[...], wba_ref[...], preferred_element_type=F32)
    h = a.shape[1] // 2
    mrg = jnp.concatenate(
        [jax.nn.sigmoid(g0_ref[...].astype(F32)) * a[:, :h] + jax.nn.sigmoid(g2_ref[...].astype(F32)) * b[:, :h],
         jax.nn.sigmoid(g1_ref[...].astype(F32)) * a[:, h:] + jax.nn.sigmoid(g3_ref[...].astype(F32)) * b[:, h:]],
        axis=1)
    o_ref[...] = x_ref[...] + mod_ref[0, 2:3, :] * _bdot(mrg, wo_ref[...])


def _merge(x2, ys, auxs, ya, p, ln, e128, wbr, wba, wo, mod, rows):
    m, d = x2.shape
    tm = _row_tile(rows, 512)
    per = rows // tm
    many = mod.shape[0] > 1
    gb = GATE_OFF // 512
    const2 = lambda i: (0, 0)
    in_specs = [pl.BlockSpec((tm, d), lambda i: (i, 0)),
                pl.BlockSpec((tm, RW_DIM), lambda i: (i, 0)),
                pl.BlockSpec((tm, RW_DIM), lambda i: (i, 0)),
                pl.BlockSpec((tm, 2 * RW_DIM), lambda i: (i, 0)),
                pl.BlockSpec((tm, 2 * RW_DIM), lambda i: (i, 0)),
                pl.BlockSpec((tm, ATT_DIM), lambda i: (i, 0))]
    in_specs += [pl.BlockSpec((tm, 512), (lambda i, c=c: (i, gb + c))) for c in range(4)]
    in_specs += [pl.BlockSpec((2, RW_DIM), const2),
                 pl.BlockSpec((LANES, LANES), const2),
                 pl.BlockSpec(wbr.shape, const2),
                 pl.BlockSpec(wba.shape, const2),
                 pl.BlockSpec(wo.shape, const2),
                 pl.BlockSpec((1, 6, d), lambda i: ((i // per) if many else 0, 0, 0))]
    return pl.pallas_call(
        _merge_kernel,
        grid=(m // tm,),
        in_specs=in_specs,
        out_specs=pl.BlockSpec((tm, d), lambda i: (i, 0)),
        out_shape=jax.ShapeDtypeStruct((m, d), F32),
        compiler_params=_cparams("parallel"),
        name="branch_merge",
    )(x2, ys[0], ys[1], auxs[0], auxs[1], ya, p, p, p, p, ln, e128, wbr, wba, wo, mod)


def _ffn_kernel(x_ref, g_ref, mod_ref, wg_ref, wu_ref, wd_ref, o_ref, h_sc, acc_sc, *, nf):
    f = pl.program_id(1)

    @pl.when(f == 0)
    def _():
        h = _norm_mod(x_ref[...], g_ref[...], mod_ref[0, 3:4, :], mod_ref[0, 4:5, :])
        h_sc[...] = h.astype(BF16)
        acc_sc[...] = jnp.zeros_like(acc_sc)

    h = h_sc[...]
    a = jnp.dot(h, wg_ref[...], preferred_element_type=F32)
    u = jnp.dot(h, wu_ref[...], preferred_element_type=F32)
    act = (a * jax.nn.sigmoid(a)) * u
    acc_sc[...] += jnp.dot(act.astype(BF16), wd_ref[...], preferred_element_type=F32)

    @pl.when(f == nf - 1)
    def _():
        o_ref[...] = x_ref[...] + mod_ref[0, 5:6, :] * acc_sc[...]


def _ffn(x2, gain, mod, wg, wu, wd, rows):
    m, d = x2.shape
    dff = wg.shape[1]
    tm = _row_tile(rows, 512)
    per = rows // tm
    many = mod.shape[0] > 1
    tf = dff // 2 if (dff // 2) % LANES == 0 else dff
    nf = dff // tf
    return pl.pallas_call(
        functools.partial(_ffn_kernel, nf=nf),
        grid=(m // tm, nf),
        in_specs=[pl.BlockSpec((tm, d), lambda i, f: (i, 0)),
                  pl.BlockSpec((1, d), lambda i, f: (0, 0)),
                  pl.BlockSpec((1, 6, d), lambda i, f: ((i // per) if many else 0, 0, 0)),
                  pl.BlockSpec((d, tf), lambda i, f: (0, f)),
                  pl.BlockSpec((d, tf), lambda i, f: (0, f)),
                  pl.BlockSpec((tf, d), lambda i, f: (f, 0))],
        out_specs=pl.BlockSpec((tm, d), lambda i, f: (i, 0)),
        out_shape=jax.ShapeDtypeStruct((m, d), F32),
        scratch_shapes=[pltpu.VMEM((tm, d), BF16), pltpu.VMEM((tm, d), F32)],
        compiler_params=_cparams("parallel", "arbitrary"),
        name="ffn_swiglu",
    )(x2, gain.reshape(1, d), mod, wg, wu, wd)


def _dma_pieces(count, sizes, make):
    for k, size in enumerate(sizes):
        @pl.when((count & size) != 0)
        def _(k=k, size=size):
            make(k, pl.multiple_of(count & ~(2 * size - 1), PACK), size)


def _moe_gather_kernel(pos_ref, cnt_ref, h_ref, rankt_ref, zero_ref, out_ref, hs_sc, sem, *, ne, nsteps, sizes):
    del zero_ref
    i = pl.program_id(0)
    ex = pl.program_id(1)
    step = i * ne + ex
    slot = step % 2
    tt = h_ref.shape[0]

    def copies(st, sl, act):
        base = pos_ref[st]

        def make(k, off, size):
            act(pltpu.make_async_copy(hs_sc.at[sl, pl.ds(off, size)],
                                      out_ref.at[pl.ds(pl.multiple_of(base + off, PACK), size)], sem.at[sl, k]))

        _dma_pieces(cnt_ref[st], sizes, make)

    @pl.when(step >= 2)
    def _():
        copies(step - 2, slot, lambda cp: cp.wait())

    rrow = rankt_ref[0, pl.ds(ex, 1), :]
    rowid = lax.broadcasted_iota(jnp.int32, (MOE_BLK, tt), 0).astype(F32)

    def gather(b, carry):
        sel = ((rrow - (b * MOE_BLK).astype(F32)) == rowid).astype(BF16)
        r0 = pl.multiple_of(b * MOE_BLK, MOE_BLK)
        hs_sc[slot, pl.ds(r0, MOE_BLK), :] = jnp.dot(sel, h_ref[...], preferred_element_type=F32).astype(BF16)
        return carry

    lax.fori_loop(0, (cnt_ref[step] + MOE_BLK - 1) // MOE_BLK, gather, 0)
    copies(step, slot, lambda cp: cp.start())

    @pl.when(step == nsteps - 1)
    def _():
        if nsteps >= 2:
            copies(step - 1, 1 - slot, lambda cp: cp.wait())
        copies(step, slot, lambda cp: cp.wait())


def _moe_ffn_kernel(te_ref, tm_ref, tv_ref, hs_ref, wg_ref, wu_ref, wd_ref, os_ref, acc_sc, *, nf):
    del te_ref, tm_ref
    t = pl.program_id(0)
    f = pl.program_id(1)
    valid = tv_ref[t] != 0

    @pl.when(valid)
    def _():
        hb = hs_ref[...]
        a = jnp.dot(hb, wg_ref[0], preferred_element_type=F32)
        u = jnp.dot(hb, wu_ref[0], preferred_element_type=F32)
        act = ((a * jax.nn.sigmoid(a)) * u).astype(BF16)
        part = jnp.dot(act, wd_ref[0], preferred_element_type=F32)

        @pl.when(f == 0)
        def _():
            acc_sc[...] = part

        @pl.when(f > 0)
        def _():
            acc_sc[...] += part

    @pl.when(f == nf - 1)
    def _():
        os_ref[...] = jnp.where(valid, acc_sc[...], 0.0).astype(os_ref.dtype)


def _moe_combine_kernel(pos_ref, cnt_ref, x_ref, we_ref, rank_ref, mod_ref, os_ref, o_ref, acc_sc, ob_sc, sem,
                        *, ne, nsteps, sizes):
    i = pl.program_id(0)
    ex = pl.program_id(1)
    step = i * ne + ex
    slot = step % 2
    tt = x_ref.shape[0]
    sblk = min(2 * MOE_BLK, tt)
    lane = lax.broadcasted_iota(jnp.int32, (tt, LANES), 1)

    def copies(st, sl, act):
        base = pos_ref[st]

        def make(k, off, size):
            act(pltpu.make_async_copy(os_ref.at[pl.ds(pl.multiple_of(base + off, PACK), size)],
                                      ob_sc.at[sl, pl.ds(off, size)], sem.at[sl, k]))

        _dma_pieces(cnt_ref[st], sizes, make)

    @pl.when(step == 0)
    def _():
        ob_sc[...] = jnp.zeros_like(ob_sc)
        copies(0, 0, lambda cp: cp.start())

    @pl.when(step + 1 < nsteps)
    def _():
        copies(step + 1, 1 - slot, lambda cp: cp.start())

    copies(step, slot, lambda cp: cp.wait())

    @pl.when(ex == 0)
    def _():
        acc_sc[...] = jnp.zeros_like(acc_sc)

    rcol = jnp.sum(jnp.where(lane == ex, rank_ref[...], 0.0), axis=1, keepdims=True)
    wcol = jnp.sum(jnp.where(lane == ex, we_ref[...], 0.0), axis=1, keepdims=True)
    colid = lax.broadcasted_iota(jnp.int32, (tt, sblk), 1).astype(F32)

    def scatter(b, carry):
        sel = ((rcol - (b * sblk).astype(F32)) == colid).astype(BF16)
        r0 = pl.multiple_of(b * sblk, sblk)
        acc_sc[...] += wcol * jnp.dot(sel, ob_sc[slot, pl.ds(r0, sblk), :], preferred_element_type=F32)
        return carry

    lax.fori_loop(0, (cnt_ref[step] + sblk - 1) // sblk, scatter, 0)

    @pl.when(ex == ne - 1)
    def _():
        o_ref[...] = x_ref[...] + mod_ref[0, 5:6, :] * acc_sc[...]


def _route_kernel(x_ref, g_ref, mod_ref, rt_ref, h_ref, we_ref, rank_ref, rankt_ref, cnt_ref, *, ne):
    tt = x_ref.shape[0]
    h = _norm_mod(x_ref[...], g_ref[...], mod_ref[0, 3:4, :], mod_ref[0, 4:5, :])
    h_ref[...] = h.astype(BF16)
    lane = lax.broadcasted_iota(jnp.int32, (tt, LANES), 1)
    logits = jnp.dot(h, rt_ref[...], precision=HIGHEST, preferred_element_type=F32)
    lg = jnp.where(lane < ne, logits, NEG_INF)
    t1 = jnp.max(lg, axis=1, keepdims=True)
    i1 = jnp.min(jnp.where(lg == t1, lane, LANES), axis=1, keepdims=True)
    lg2 = jnp.where(lane == i1, NEG_INF, lg)
    t2 = jnp.max(lg2, axis=1, keepdims=True)
    i2 = jnp.min(jnp.where(lg2 == t2, lane, LANES), axis=1, keepdims=True)
    e2 = jnp.exp(t2 - t1)
    den = 1.0 + e2
    we_ref[...] = jnp.where(lane == i1, 1.0 / den, 0.0) + jnp.where(lane == i2, e2 / den, 0.0)
    sel = (lane == i1) | (lane == i2)
    self32 = sel.astype(F32)
    before = (lax.broadcasted_iota(jnp.int32, (tt, tt), 1) < lax.broadcasted_iota(jnp.int32, (tt, tt), 0))
    rank = jnp.dot(before.astype(BF16), sel.astype(BF16), preferred_element_type=F32)
    rank = jnp.where(sel, rank, -1.0)
    rank_ref[...] = rank
    rankt_ref[0] = rank.T[0:SUBLANES, :]
    cnt_ref[0] = jnp.broadcast_to(jnp.sum(self32, axis=0, keepdims=True), (SUBLANES, LANES))


def _moe(x2, gain, mod, router_pad, wg, wu, wd, rows):
    m, d = x2.shape
    ne, _, dff = wg.shape
    assert ne <= SUBLANES
    tt = _row_tile(rows, MOE_TILE)
    nti = m // tt
    per = rows // tt
    many = mod.shape[0] > 1
    h, we, rank, rankt, cnt = pl.pallas_call(
        functools.partial(_route_kernel, ne=ne),
        grid=(nti,),
        in_specs=[pl.BlockSpec((tt, d), lambda i: (i, 0)),
                  pl.BlockSpec((1, d), lambda i: (0, 0)),
                  pl.BlockSpec((1, 6, d), lambda i: ((i // per) if many else 0, 0, 0)),
                  pl.BlockSpec((d, LANES), lambda i: (0, 0))],
        out_specs=[pl.BlockSpec((tt, d), lambda i: (i, 0)),
                   pl.BlockSpec((tt, LANES), lambda i: (i, 0)),
                   pl.BlockSpec((tt, LANES), lambda i: (i, 0)),
                   pl.BlockSpec((1, SUBLANES, tt), lambda i: (i, 0, 0)),
                   pl.BlockSpec((1, SUBLANES, LANES), lambda i: (i, 0, 0))],
        out_shape=[jax.ShapeDtypeStruct((m, d), BF16),
                   jax.ShapeDtypeStruct((m, LANES), F32),
                   jax.ShapeDtypeStruct((m, LANES), F32),
                   jax.ShapeDtypeStruct((nti, SUBLANES, tt), F32),
                   jax.ShapeDtypeStruct((nti, SUBLANES, LANES), F32)],
        compiler_params=_cparams("parallel"),
        name="moe_route",
    )(x2, gain.reshape(1, d), mod, router_pad)
    c16 = (cnt[:, 0, :ne].astype(jnp.int32) + (PACK - 1)) // PACK * PACK
    region = (jnp.sum(c16, axis=0) + (MOE_FT - 1)) // MOE_FT * MOE_FT
    e_off = jnp.cumsum(region) - region
    pos = (e_off[None, :] + jnp.cumsum(c16, axis=0) - c16).reshape(-1).astype(jnp.int32)
    c16 = c16.reshape(-1)
    rmax = -(-(2 * m + nti * ne * PACK + ne * MOE_FT) // MOE_FT) * MOE_FT
    nft = rmax // MOE_FT
    tiles_e = jnp.cumsum(region // MOE_FT)
    tid = jnp.arange(nft, dtype=jnp.int32)
    tile_valid = (tid < tiles_e[-1]).astype(jnp.int32)
    tile_map = jnp.minimum(tid, tiles_e[-1] - 1).astype(jnp.int32)
    tile_exp = jnp.minimum(jnp.searchsorted(tiles_e, tile_map, side="right"), ne - 1).astype(jnp.int32)

    sizes = tuple(tt >> k for k in range(tt.bit_length()) if (tt >> k) >= PACK)
    nsteps = nti * ne
    hs = pl.pallas_call(
        functools.partial(_moe_gather_kernel, ne=ne, nsteps=nsteps, sizes=sizes),
        grid_spec=pltpu.PrefetchScalarGridSpec(
            num_scalar_prefetch=2,
            grid=(nti, ne),
            in_specs=[pl.BlockSpec((tt, d), lambda i, e, p_, c_: (i, 0)),
                      pl.BlockSpec((1, SUBLANES, tt), lambda i, e, p_, c_: (i, 0, 0)),
                      pl.BlockSpec(memory_space=pl.ANY)],
            out_specs=pl.BlockSpec(memory_space=pl.ANY),
            scratch_shapes=[pltpu.VMEM((2, tt, d), BF16), pltpu.SemaphoreType.DMA((2, len(sizes)))]),
        out_shape=jax.ShapeDtypeStruct((rmax, d), BF16),
        input_output_aliases={4: 0},
        compiler_params=_cparams("arbitrary", "arbitrary"),
        name="moe_gather",
    )(pos, c16, h, rankt, jnp.zeros((rmax, d), BF16))

    tf = dff // 2 if (dff // 2) % LANES == 0 else dff
    nf = dff // tf
    os_sorted = pl.pallas_call(
        functools.partial(_moe_ffn_kernel, nf=nf),
        grid_spec=pltpu.PrefetchScalarGridSpec(
            num_scalar_prefetch=3,
            grid=(nft, nf),
            in_specs=[pl.BlockSpec((MOE_FT, d), lambda t, f, te, tm, tv: (tm[t], 0)),
                      pl.BlockSpec((1, d, tf), lambda t, f, te, tm, tv: (te[t], 0, f)),
                      pl.BlockSpec((1, d, tf), lambda t, f, te, tm, tv: (te[t], 0, f)),
                      pl.BlockSpec((1, tf, d), lambda t, f, te, tm, tv: (te[t], f, 0))],
            out_specs=pl.BlockSpec((MOE_FT, d), lambda t, f, te, tm, tv: (t, 0)),
            scratch_shapes=[pltpu.VMEM((MOE_FT, d), F32)]),
        out_shape=jax.ShapeDtypeStruct((rmax, d), BF16),
        compiler_params=_cparams("parallel", "arbitrary"),
        name="moe_experts",
    )(tile_exp, tile_map, tile_valid, hs, wg, wu, wd)

    return pl.pallas_call(
        functools.partial(_moe_combine_kernel, ne=ne, nsteps=nsteps, sizes=sizes),
        grid_spec=pltpu.PrefetchScalarGridSpec(
            num_scalar_prefetch=2,
            grid=(nti, ne),
            in_specs=[pl.BlockSpec((tt, d), lambda i, e, p_, c_: (i, 0)),
                      pl.BlockSpec((tt, LANES), lambda i, e, p_, c_: (i, 0)),
                      pl.BlockSpec((tt, LANES), lambda i, e, p_, c_: (i, 0)),
                      pl.BlockSpec((1, 6, d), lambda i, e, p_, c_: ((i // per) if many else 0, 0, 0)),
                      pl.BlockSpec(memory_space=pl.ANY)],
            out_specs=pl.BlockSpec((tt, d), lambda i, e, p_, c_: (i, 0)),
            scratch_shapes=[pltpu.VMEM((tt, d), F32), pltpu.VMEM((2, tt, d), BF16),
                            pltpu.SemaphoreType.DMA((2, len(sizes)))]),
        out_shape=jax.ShapeDtypeStruct((m, d), F32),
        compiler_params=_cparams("arbitrary", "arbitrary"),
        name="moe_combine",
    )(pos, c16, x2, we, rank, mod, os_sorted)


def _rope_table(t):
    rows = t // GRID_W
    row = jnp.repeat(jnp.arange(rows, dtype=F32), GRID_W)
    col = jnp.tile(jnp.arange(GRID_W, dtype=F32), rows)
    half = HEAD_DIM // 2
    inv_freq = ROPE_THETA ** (-jnp.arange(0, half, 2, dtype=F32) / half)
    ar = row[:, None] * inv_freq
    ac = col[:, None] * inv_freq
    ang = jnp.concatenate([ar, ar, ac, ac] * 2, axis=-1)
    return jnp.stack([jnp.cos(ang), jnp.sin(ang)])


def _pad_rows(a, rows):
    return jnp.zeros((rows,) + a.shape[1:], a.dtype).at[:a.shape[0]].set(a)


def _hi_lo(w):
    hi = w.astype(BF16)
    return hi, (w - hi.astype(F32)).astype(BF16)


def kernel(x, c, ctx, c_ctx, norm1, norm2, w_mod, b_mod, w_in, tshift, w0, w_decay_up, a0, w_iclr_up, v0, w_vres_down, w_vres_up, w_gate_up, k_k, k_a, r_k, lnx_g, lnx_b, q_norm, k_norm, sink, w_br_rwkv, w_br_attn, w_out, ffn_gate, ffn_up, ffn_down, router, moe_gate, moe_up, moe_down):
    nb, t, d = x.shape
    cn = ctx.shape[1]
    depth = w_in.shape[0]
    mods = _modulation(c, c_ctx, w_mod, b_mod)
    cs = _rope_table(t)
    e128 = jnp.kron(jnp.eye(2, dtype=F32), jnp.ones((HEAD_DIM, HEAD_DIM), F32)).astype(BF16)
    xl = x.reshape(nb * t, d)
    xc = ctx.reshape(nb * cn, d)
    vf_l = vf_c = None
    for layer in range(depth):
        last = layer == depth - 1
        mod_l = mods[layer, :nb].reshape(nb, 6, d)
        mod_c = mods[layer, nb:nb + 1].reshape(1, 6, d)
        w_in_b = w_in[layer].astype(BF16)
        vd_b = None
        prm = {"tshift": tshift[layer], "e128": e128, "wg": w_gate_up[layer].astype(BF16)}
        prm["pd"] = jnp.stack(
            [_pad_rows(jnp.stack([w0[layer, dd], a0[layer, dd]]), SUBLANES) for dd in range(2)])
        shared = [k_k[layer], k_a[layer], r_k[layer].reshape(RW_DIM)]
        if layer > 0:
            shared.append(v0[layer - 1])
            vd_b = _pad_rows(w_vres_down[layer - 1].T, LANES).T.astype(BF16)
            prm["vu_hi"], prm["vu_lo"] = _hi_lo(_pad_rows(w_vres_up[layer - 1], LANES))
        prm["ps"] = _pad_rows(jnp.stack(shared), SUBLANES)
        wla = jnp.zeros((2, LANES, 2 * RW_DIM), F32)
        wla = wla.at[:, :HEAD_DIM, :RW_DIM].set(w_decay_up[layer]).at[:, HEAD_DIM:, RW_DIM:].set(w_iclr_up[layer])
        prm["wla_hi"], prm["wla_lo"] = _hi_lo(wla)

        p_l, hv_l = _inproj(xl, norm1[layer], mod_l, w_in_b, vd_b, t)
        p_c, hv_c = _inproj(xc, norm1[layer], mod_c, w_in_b, vd_b, cn)

        s0 = jnp.zeros((nb, N_PAIRS, LANES, LANES), F32)
        y_c, aux_c, s_ctx = _rwkv_both(p_c, vf_c, hv_c, [s0, s0], prm, nb, cn)
        y_l, aux_l, _ = _rwkv_both(p_l, vf_l, hv_l, s_ctx, prm, nb, t)
        if layer == 0:
            vf_l, vf_c = aux_l[1], aux_c[1]

        qg = jnp.tile(q_norm[layer], 2).reshape(1, LANES)
        kg = jnp.tile(k_norm[layer], 2).reshape(1, LANES)
        qn_l, kn_l, vb_l = _qk_prep(p_l, cs, qg, kg, e128, t, True)
        qn_c, kn_c, vb_c = _qk_prep(p_c, None, qg, kg, e128, cn, False)
        sk = sink[layer].reshape(1, -1)
        ya_l = _attention(qn_l, kn_l, vb_l, kn_c, vb_c, sk, nb, t, cn, True)

        ln = jnp.stack([lnx_g[layer], lnx_b[layer]])
        wbr = w_br_rwkv[layer].astype(BF16)
        wba = w_br_attn[layer].astype(BF16)
        wo = w_out[layer].astype(BF16)
        xl = _merge(xl, y_l, aux_l, ya_l, p_l, ln, e128, wbr, wba, wo, mod_l, t)
        if not last:
            ya_c = _attention(qn_c, None, None, kn_c, vb_c, sk, nb, cn, cn, False)
            xc = _merge(xc, y_c, aux_c, ya_c, p_c, ln, e128, wbr, wba, wo, mod_c, cn)

        i = layer // 2
        if layer % 2 == 0:
            fw = (ffn_gate[i].astype(BF16), ffn_up[i].astype(BF16), ffn_down[i].astype(BF16))
            xl = _ffn(xl, norm2[layer], mod_l, *fw, t)
            if not last:
                xc = _ffn(xc, norm2[layer], mod_c, *fw, cn)
        else:
            rt = _pad_rows(router[i].T, LANES).T
            mw = (moe_gate[i].astype(BF16), moe_up[i].astype(BF16), moe_down[i].astype(BF16))
            xl = _moe(xl, norm2[layer], mod_l, rt, *mw, t)
            if not last:
                xc = _moe(xc, norm2[layer], mod_c, rt, *mw, cn)
    return xl.reshape(nb, t, d)
```

```python
import functools

import jax
import jax.numpy as jnp
from jax import lax
from jax.experimental import pallas as pl
from jax.experimental.pallas import tpu as pltpu

F32 = jnp.float32
BF16 = jnp.bfloat16
HIGHEST = lax.Precision.HIGHEST

LANES = 128
SUBLANES = 8
PACK = 16
ACT = jnp.bfloat16
HEAD_DIM = 64
RW_DIM = 512
ATT_DIM = 512
KV_DIM = 128
RW_IN = 1792
ATT_OFF = RW_IN
GATE_OFF = RW_IN + ATT_DIM + 2 * KV_DIM
N_EXPERTS = 8
CHUNK = 64
N_PAIRS = RW_DIM // LANES
MOE_TILE = 1024
MOE_BLK = 128
MOE_FT = 512
ATT_SCALE = HEAD_DIM ** -0.5
ROPE_THETA = 10000.0
GRID_W = 64
RMS_EPS = 1e-6
GN_EPS = 64e-5
NEG_INF = -1e30
DECAY_SCALE = 0.6065306597126334
VMEM_LIMIT = 48 * 1024 * 1024


def _cparams(*sem):
    return pltpu.CompilerParams(dimension_semantics=sem, vmem_limit_bytes=VMEM_LIMIT)


def _row_tile(rows, cap):
    t = cap
    while rows % t:
        t //= 2
    return t


def _bdot(a, b):
    return jnp.dot(a.astype(BF16), b.astype(BF16), preferred_element_type=F32)


def _bdot_nt(a, b):
    return lax.dot_general(a.astype(BF16), b.astype(BF16), (((1,), (1,)), ((), ())),
                           preferred_element_type=F32)


def _bdot_tn(a, b):
    return lax.dot_general(a.astype(BF16), b.astype(BF16), (((0,), (0,)), ((), ())),
                           preferred_element_type=F32)


def _split2(x):
    hi = x.astype(BF16)
    lo = (x - hi.astype(F32)).astype(BF16)
    return hi, lo


def _dot3(a, w_hi, w_lo):
    a_hi, a_lo = _split2(a)
    return (jnp.dot(a_hi, w_hi, preferred_element_type=F32)
            + (jnp.dot(a_lo, w_hi, preferred_element_type=F32)
               + jnp.dot(a_hi, w_lo, preferred_element_type=F32)))


def _segsum(x, e, exact=True):
    outs = []
    for s in range(x.shape[1] // LANES):
        hi, lo = _split2(x[:, s * LANES:(s + 1) * LANES])
        acc = jnp.dot(hi, e, preferred_element_type=F32)
        outs.append(acc + jnp.dot(lo, e, preferred_element_type=F32) if exact else acc)
    return outs[0] if len(outs) == 1 else jnp.concatenate(outs, axis=1)


def _norm_mod(x, gain, shift, scale):
    y = x * lax.rsqrt(jnp.mean(x * x, axis=-1, keepdims=True) + RMS_EPS) * gain
    return y * (1.0 + scale) + shift


def _mod_kernel(c_ref, w_ref, b_ref, o_ref):
    c = c_ref[...]
    s = c * jax.nn.sigmoid(c)
    o_ref[0] = jnp.dot(s, w_ref[0], precision=HIGHEST, preferred_element_type=F32) + b_ref[0]


def _modulation(c, c_ctx, w_mod, b_mod):
    depth, d, n6 = w_mod.shape
    nb = c.shape[0]
    rm = -(-(nb + 1) // SUBLANES) * SUBLANES
    cc = jnp.zeros((rm, d), F32).at[:nb].set(c).at[nb].set(c_ctx)
    tn = _row_tile(n6, 1536)
    return pl.pallas_call(
        _mod_kernel,
        grid=(depth, n6 // tn),
        in_specs=[pl.BlockSpec((rm, d), lambda l, n: (0, 0)),
                  pl.BlockSpec((1, d, tn), lambda l, n: (l, 0, n)),
                  pl.BlockSpec((1, 1, tn), lambda l, n: (l, 0, n))],
        out_specs=pl.BlockSpec((1, rm, tn), lambda l, n: (l, 0, n)),
        out_shape=jax.ShapeDtypeStruct((depth, rm, n6), F32),
        compiler_params=_cparams("parallel", "parallel"),
        name="adaln_mod",
    )(cc, w_mod, b_mod.reshape(depth, 1, n6))


def _inproj_kernel(*refs, vres):
    if vres:
        x_ref, g_ref, mod_ref, w_ref, vd_ref, o_ref, hv_ref, h_sc = refs
    else:
        x_ref, g_ref, mod_ref, w_ref, o_ref, h_sc = refs

    @pl.when(pl.program_id(1) == 0)
    def _():
        h = _norm_mod(x_ref[...], g_ref[...], mod_ref[0, 0:1, :], mod_ref[0, 1:2, :])
        hb = h.astype(BF16)
        h_sc[...] = hb
        if vres:
            hv_ref[...] = jnp.dot(hb, vd_ref[...], preferred_element_type=F32)

    o_ref[...] = jnp.dot(h_sc[...], w_ref[...], preferred_element_type=F32).astype(o_ref.dtype)


def _inproj(x2, gain, mod, w_b, vd_b, rows):
    m, d = x2.shape
    n = w_b.shape[1]
    tm = _row_tile(rows, 1024)
    tn = _row_tile(n, 1536)
    per = rows // tm
    many = mod.shape[0] > 1
    vres = vd_b is not None
    in_specs = [pl.BlockSpec((tm, d), lambda i, j: (i, 0)),
                pl.BlockSpec((1, d), lambda i, j: (0, 0)),
                pl.BlockSpec((1, 6, d), lambda i, j: ((i // per) if many else 0, 0, 0)),
                pl.BlockSpec((d, tn), lambda i, j: (0, j))]
    args = [x2, gain.reshape(1, d), mod, w_b]
    out_specs = [pl.BlockSpec((tm, tn), lambda i, j: (i, j))]
    out_shape = [jax.ShapeDtypeStruct((m, n), ACT)]
    if vres:
        in_specs.append(pl.BlockSpec((d, LANES), lambda i, j: (0, 0)))
        args.append(vd_b)
        out_specs.append(pl.BlockSpec((tm, LANES), lambda i, j: (i, 0)))
        out_shape.append(jax.ShapeDtypeStruct((m, LANES), F32))
    outs = pl.pallas_call(
        functools.partial(_inproj_kernel, vres=vres),
        grid=(m // tm, n // tn),
        in_specs=in_specs, out_specs=out_specs, out_shape=out_shape,
        scratch_shapes=[pltpu.VMEM((tm, d), BF16)],
        compiler_params=_cparams("parallel", "arbitrary"),
        name="in_proj",
    )(*args)
    return (outs[0], outs[1]) if vres else (outs[0], None)


def _qkprep_kernel(*refs, rope):
    if rope:
        q0_ref, q1_ref, k_ref, v_ref, cs_ref, qg_ref, kg_ref, e_ref, qn_ref, kn_ref, vb_ref = refs
    else:
        q0_ref, q1_ref, k_ref, v_ref, qg_ref, kg_ref, e_ref, qn_ref, kn_ref, vb_ref = refs
    e = e_ref[...]
    lane = lax.broadcasted_iota(jnp.int32, (1, LANES), 1)
    first = (lane & 31) < 16

    def norm_rope(u, gain):
        y = u * lax.rsqrt(_segsum(u * u, e) * (1.0 / HEAD_DIM) + RMS_EPS) * gain
        if rope:
            rot = jnp.where(first, -pltpu.roll(y, LANES - 16, 1), pltpu.roll(y, 16, 1))
            y = y * cs_ref[0] + rot * cs_ref[1]
        return y

    qg = qg_ref[...]
    for s, ref in enumerate((q0_ref, q1_ref)):
        for t in range(2):
            u = ref[:, t * LANES:(t + 1) * LANES].astype(F32)
            c0 = (2 * s + t) * LANES
            qn_ref[:, c0:c0 + LANES] = (norm_rope(u, qg) * ATT_SCALE).astype(BF16)
    low = lane < HEAD_DIM
    for src_val, dst in ((norm_rope(k_ref[...].astype(F32), kg_ref[...]), kn_ref), (v_ref[...].astype(F32), vb_ref)):
        swapped = pltpu.roll(src_val, HEAD_DIM, 1)
        dst[:, 0:LANES] = jnp.where(low, src_val, swapped).astype(BF16)
        dst[:, LANES:2 * LANES] = jnp.where(low, swapped, src_val).astype(BF16)


def _qk_prep(p, cs, qg, kg, e128, rows, rope):
    m = p.shape[0]
    tm = _row_tile(rows, 512)
    per = rows // tm
    qb = ATT_OFF // 256
    kb = (ATT_OFF + ATT_DIM) // LANES
    in_specs = [pl.BlockSpec((tm, 256), lambda i: (i, qb)),
                pl.BlockSpec((tm, 256), lambda i: (i, qb + 1)),
                pl.BlockSpec((tm, LANES), lambda i: (i, kb)),
                pl.BlockSpec((tm, LANES), lambda i: (i, kb + 1))]
    args = [p, p, p, p]
    if rope:
        in_specs.append(pl.BlockSpec((2, tm, LANES), lambda i: (0, i % per, 0)))
        args.append(cs)
    in_specs += [pl.BlockSpec((1, LANES), lambda i: (0, 0)),
                 pl.BlockSpec((1, LANES), lambda i: (0, 0)),
                 pl.BlockSpec((LANES, LANES), lambda i: (0, 0))]
    args += [qg, kg, e128]
    return pl.pallas_call(
        functools.partial(_qkprep_kernel, rope=rope),
        grid=(m // tm,),
        in_specs=in_specs,
        out_specs=[pl.BlockSpec((tm, ATT_DIM), lambda i: (i, 0)),
                   pl.BlockSpec((tm, 2 * KV_DIM), lambda i: (i, 0)),
                   pl.BlockSpec((tm, 2 * KV_DIM), lambda i: (i, 0))],
        out_shape=[jax.ShapeDtypeStruct((m, ATT_DIM), BF16),
                   jax.ShapeDtypeStruct((m, 2 * KV_DIM), BF16),
                   jax.ShapeDtypeStruct((m, 2 * KV_DIM), BF16)],
        compiler_params=_cparams("parallel"),
        name="qk_prep",
    )(*args)


def _attn_kernel(*refs, nq, local):
    if local:
        sink_ref, q_ref, kp_ref, kc_ref, kn_ref, vp_ref, vc_ref, vn_ref, kx_ref, vx_ref, wm_ref, o_ref = refs
        k = jnp.concatenate([kp_ref[...], kc_ref[...], kn_ref[...], kx_ref[...]], axis=0)
        v = jnp.concatenate([vp_ref[...], vc_ref[...], vn_ref[...], vx_ref[...]], axis=0)
    else:
        sink_ref, q_ref, kx_ref, vx_ref, o_ref = refs
        k = kx_ref[...]
        v = vx_ref[...]
    i = pl.program_id(1)
    bq = q_ref.shape[0]
    low = lax.broadcasted_iota(jnp.int32, (1, LANES), 1) < HEAD_DIM
    row = lax.broadcasted_iota(jnp.int32, (4 * bq, 1), 0)
    if local:
        kblk = lax.broadcasted_iota(jnp.int32, (1, 3 * bq), 1) // bq
        off_end = ((kblk == 0) & (i == 0)) | ((kblk == 2) & (i == nq - 1))
        bias = wm_ref[...] + jnp.where(off_end, NEG_INF, 0.0)
    for g in range(2):
        kb = k[:, g * LANES:(g + 1) * LANES]
        vb = v[:, g * LANES:(g + 1) * LANES]
        qa = q_ref[:, (2 * g) * LANES:(2 * g + 1) * LANES].astype(F32)
        qb = q_ref[:, (2 * g + 1) * LANES:(2 * g + 2) * LANES].astype(F32)
        qs = jnp.concatenate([jnp.where(low, qa, 0.0), jnp.where(low, 0.0, qa),
                              jnp.where(low, qb, 0.0), jnp.where(low, 0.0, qb)], axis=0).astype(BF16)
        s = lax.dot_general(qs, kb, (((1,), (1,)), ((), ())), preferred_element_type=F32)
        if local:
            s = jnp.concatenate([s[:, :3 * bq] + bias, s[:, 3 * bq:]], axis=1)
        sk = jnp.where(row < bq, sink_ref[0, 4 * g],
                       jnp.where(row < 2 * bq, sink_ref[0, 4 * g + 1],
                                 jnp.where(row < 3 * bq, sink_ref[0, 4 * g + 2], sink_ref[0, 4 * g + 3])))
        mx = jnp.maximum(jnp.max(s, axis=1, keepdims=True), sk)
        ex = jnp.exp(s - mx)
        den = jnp.sum(ex, axis=1, keepdims=True) + jnp.exp(sk - mx)
        o = jnp.dot(ex.astype(BF16), vb, preferred_element_type=F32) / den
        o_ref[:, (2 * g) * LANES:(2 * g + 1) * LANES] = jnp.where(low, o[0:bq], o[bq:2 * bq]).astype(o_ref.dtype)
        o_ref[:, (2 * g + 1) * LANES:(2 * g + 2) * LANES] = jnp.where(
            low, o[2 * bq:3 * bq], o[3 * bq:4 * bq]).astype(o_ref.dtype)


def _attention(qn, kn, vb, kctx, vctx, sink, nb, rows, cn, local):
    m = qn.shape[0]
    bq = 128
    nq = rows // bq
    smem = pl.BlockSpec(memory_space=pltpu.SMEM)
    qspec = pl.BlockSpec((bq, ATT_DIM), lambda b, i: (b * nq + i, 0))
    cspec = pl.BlockSpec((cn, 2 * KV_DIM), lambda b, i: (b, 0))
    if local:
        prv = pl.BlockSpec((bq, 2 * KV_DIM), lambda b, i: (b * nq + jnp.maximum(i - 1, 0), 0))
        cur = pl.BlockSpec((bq, 2 * KV_DIM), lambda b, i: (b * nq + i, 0))
        nxt = pl.BlockSpec((bq, 2 * KV_DIM), lambda b, i: (b * nq + jnp.minimum(i + 1, nq - 1), 0))
        qo = jnp.arange(4 * bq)[:, None] % bq
        kcol = jnp.arange(3 * bq)[None, :]
        inwin = ((kcol // bq == 1) | ((kcol // bq == 0) & (kcol % bq >= qo)) | ((kcol // bq == 2) & (kcol % bq <= qo)))
        wmask = jnp.where(inwin, 0.0, NEG_INF).astype(F32)
        in_specs = [smem, qspec, prv, cur, nxt, prv, cur, nxt, cspec, cspec,
                    pl.BlockSpec((4 * bq, 3 * bq), lambda b, i: (0, 0))]
        args = [sink, qn, kn, kn, kn, vb, vb, vb, kctx, vctx, wmask]
    else:
        in_specs = [smem, qspec, cspec, cspec]
        args = [sink, qn, kctx, vctx]
    return pl.pallas_call(
        functools.partial(_attn_kernel, nq=nq, local=local),
        grid=(nb, nq),
        in_specs=in_specs,
        out_specs=pl.BlockSpec((bq, ATT_DIM), lambda b, i: (b * nq + i, 0)),
        out_shape=jax.ShapeDtypeStruct((m, ATT_DIM), BF16),
        compiler_params=_cparams("parallel", "parallel"),
        name="window_attn" if local else "ctx_attn",
    )(*args)


def _rwkv_kernel(*refs, nt, tq, vres, drn):
    it = iter(refs)
    p_ref, pv_ref, nx_ref = next(it), next(it), next(it)
    if vres:
        vf_ref, hv_ref = next(it), next(it)
    ts_ref, pd_ref, ps_ref, wla_ref, wg_ref = (next(it) for _ in range(5))
    if vres:
        vuh_ref, vul_ref = next(it), next(it)
    e_ref, sin_ref = next(it), next(it)
    y_ref, aux_ref, sout_ref = next(it), next(it), next(it)
    ops_sc, s_sc = next(it), next(it)

    j = pl.program_id(1)
    sgn = 1 - 2 * drn
    tile = j if drn == 0 else nt - 1 - j
    nch = tq // CHUNK

    @pl.when(j == 0)
    def _():
        s_sc[...] = sin_ref[0]

    rw = p_ref[...].astype(F32)
    row = lax.broadcasted_iota(jnp.int32, (tq, 1), 0)
    prev_row = jnp.where(tile > 0, pv_ref[PACK - 1:PACK, :].astype(F32), 0.0)
    next_row = jnp.where(tile < nt - 1, nx_ref[0:1, :].astype(F32), 0.0)
    prv = jnp.where(row == 0, prev_row, pltpu.roll(rw, 1, 0))
    nxt = jnp.where(row == tq - 1, next_row, pltpu.roll(rw, tq - 1, 0))
    xs = rw + ts_ref[0:1, :] * (prv - rw) + ts_ref[1:2, :] * (nxt - rw)

    e = e_ref[...]
    r = xs[:, 0:RW_DIM]
    k = xs[:, RW_DIM:2 * RW_DIM]
    v = xs[:, 2 * RW_DIM:3 * RW_DIM]
    la = xs[:, 3 * RW_DIM:3 * RW_DIM + LANES]
    lane = lax.broadcasted_iota(jnp.int32, (1, LANES), 1)
    low = lane < HEAD_DIM
    la = jnp.where(low, jnp.tanh(la), la)
    la_hi, la_lo = _split2(la)
    lora = (jnp.dot(la_hi, wla_ref[0], preferred_element_type=F32)
            + jnp.dot(la_lo, wla_ref[0], preferred_element_type=F32))
    logw = -DECAY_SCALE * jax.nn.sigmoid(pd_ref[0, 0:1, :] + lora[:, 0:RW_DIM])
    iclr = jax.nn.sigmoid(pd_ref[0, 1:2, :] + lora[:, RW_DIM:2 * RW_DIM])
    if vres:
        mix = jax.nn.sigmoid(ps_ref[3:4, :] + _dot3(hv_ref[...], vuh_ref[...], vul_ref[...]))
        v = v + (vf_ref[...].astype(F32) - v) * mix
    kk = k * ps_ref[0:1, :]
    kk = kk * lax.rsqrt(_segsum(kk * kk, e, exact=False) + 1e-12)
    kh = k * (1.0 + (iclr - 1.0) * ps_ref[1:2, :])
    bb = kk * iclr
    aux_ref[:, 0:RW_DIM] = (_segsum(r * kh * ps_ref[2:3, :], e, exact=False) * v).astype(aux_ref.dtype)
    if drn == 0:
        glo = xs[:, 3 * RW_DIM + LANES:RW_IN]
        aux_ref[:, RW_DIM:2 * RW_DIM] = _bdot(jax.nn.sigmoid(glo), wg_ref[...]).astype(aux_ref.dtype)
    else:
        aux_ref[:, RW_DIM:2 * RW_DIM] = v.astype(aux_ref.dtype)

    ti = lax.broadcasted_iota(jnp.int32, (tq, tq), 0)
    si = lax.broadcasted_iota(jnp.int32, (tq, tq), 1)
    tri = (((ti // CHUNK) == (si // CHUNK)) & (((si - ti) * sgn) <= 0)).astype(BF16)
    lw_hi, lw_lo = _split2(logw)
    cin = jnp.dot(tri, lw_hi, preferred_element_type=F32) + jnp.dot(tri, lw_lo, preferred_element_type=F32)
    cin3 = cin.reshape(nch, CHUNK, RW_DIM)
    last = CHUNK - 1 if drn == 0 else 0
    ctot = jnp.broadcast_to(cin3[:, last:last + 1, :], cin3.shape).reshape(tq, RW_DIM)
    e_neg = jnp.exp(-cin)
    e_rem = jnp.exp(ctot - cin)
    ops_sc[0] = kk * jnp.exp(cin - logw)
    ops_sc[1] = r * jnp.exp(cin)
    ops_sc[2] = bb * e_neg
    ops_sc[3] = kh * e_neg
    ops_sc[4] = kh * e_rem
    ops_sc[5] = bb * e_rem
    ops_sc[6] = v
    ops_sc[7] = jnp.exp(ctot)

    tr = lax.broadcasted_iota(jnp.int32, (CHUNK, LANES), 0)
    sc = lax.broadcasted_iota(jnp.int32, (CHUNK, LANES), 1) & (CHUNK - 1)
    rel = (sc - tr) * sgn
    strict = rel < 0
    incl = rel <= 0
    eye = (sc == tr).astype(F32)
    same_head = ((lax.broadcasted_iota(jnp.int32, (LANES, LANES), 0) // HEAD_DIM)
                 == (lax.broadcasted_iota(jnp.int32, (LANES, LANES), 1) // HEAD_DIM))

    def stack(x):
        return jnp.concatenate([jnp.where(low, x, 0.0), jnp.where(low, 0.0, x)], axis=0)

    def hdot(x, y):
        return _bdot(x, stack(y))

    probs = []
    for ii in range(nch):
        off = (ii if drn == 0 else nch - 1 - ii) * CHUNK
        for pr in range(N_PAIRS):
            ls = slice(pr * LANES, (pr + 1) * LANES)
            q = {"off": off, "ls": ls, "pr": pr}
            for n, name in enumerate(("kkt", "rt", "bh", "kh", "kp", "bp", "v")):
                q[name] = ops_sc[n, off:off + CHUNK, ls]
            q["wl"] = ops_sc[7, off:off + 1, ls]
            probs.append(q)
    for q in probs:
        g = _bdot_nt(jnp.concatenate([q["kkt"], q["rt"]], axis=0),
                     jnp.concatenate([stack(q["bh"]), stack(q["kh"])], axis=0))
        q["nn"] = jnp.where(strict, -g[0:CHUNK, 0:LANES], 0.0)
        q["avk"] = jnp.where(strict, g[0:CHUNK, LANES:], 0.0)
        q["arb"] = jnp.where(incl, g[CHUNK:, 0:LANES], 0.0)
        q["ark"] = jnp.where(incl, g[CHUNK:, LANES:], 0.0)
        q["t"] = eye + q["nn"]
    for q in probs:
        q["q"] = hdot(q["nn"], q["nn"])
    for _ in range(4):
        for q in probs:
            tq = _bdot(jnp.concatenate([q["t"], q["q"]], axis=0), stack(q["q"]))
            q["t"] = q["t"] + tq[:CHUNK]
            q["q"] = tq[CHUNK:]
    for q in probs:
        q["t"] = q["t"] + hdot(q["t"], q["q"])
    for q in probs:
        av = _bdot(jnp.concatenate([q["avk"], q["ark"]], axis=0), stack(q["v"]))
        q["av"], q["arkv"] = av[:CHUNK], av[CHUNK:]
    for q in probs:
        tt2 = _bdot(q["t"], jnp.concatenate([stack(q["kkt"]), stack(q["av"])], axis=1))
        q["tk"], q["tav"] = tt2[:, :LANES], tt2[:, LANES:]
    for q in probs:
        tn = _bdot_tn(jnp.concatenate([q["tk"], q["tav"]], axis=1), q["bp"])
        q["mb"] = jnp.where(same_head, tn[:LANES], 0.0)
        q["nm"] = jnp.where(same_head, _bdot_tn(q["v"], q["kp"]) - tn[LANES:], 0.0)
        z = _bdot(q["arb"], jnp.concatenate([stack(q["tk"]), stack(q["tav"])], axis=1))
        q["r2"] = q["rt"] - z[:, :LANES]
        q["y0"] = q["arkv"] - z[:, LANES:]

    st = [s_sc[pr] for pr in range(N_PAIRS)]
    outs = []
    for q in probs:
        sm = st[q["pr"]]
        outs.append(_bdot_nt(q["r2"], sm) + q["y0"])
        st[q["pr"]] = sm * q["wl"] - _bdot(sm, q["mb"]) + q["nm"]
    for q, yo in zip(probs, outs):
        y_ref[q["off"]:q["off"] + CHUNK, q["ls"]] = yo.astype(y_ref.dtype)
    for pr in range(N_PAIRS):
        s_sc[pr] = st[pr]

    @pl.when(j == nt - 1)
    def _():
        sout_ref[0] = s_sc[...]


def _rwkv(p, vf_src, hv, state_in, prm, nb, rows, drn):
    m = p.shape[0]
    tq = _row_tile(rows, 256)
    nt = rows // tq
    rb = tq // PACK
    vres = vf_src is not None

    def tile(b, j):
        return b * nt + (j if drn == 0 else nt - 1 - j)

    const2 = lambda b, j: (0, 0)
    dir3 = lambda b, j: (drn, 0, 0)
    in_specs = [pl.BlockSpec((tq, RW_IN), lambda b, j: (tile(b, j), 0)),
                pl.BlockSpec((PACK, RW_IN), lambda b, j: (jnp.maximum(tile(b, j) * rb - 1, 0), 0)),
                pl.BlockSpec((PACK, RW_IN),
                             lambda b, j: (jnp.minimum((tile(b, j) + 1) * rb, m // PACK - 1), 0))]
    args = [p, p, p]
    if vres:
        in_specs += [pl.BlockSpec((tq, RW_DIM), lambda b, j: (tile(b, j), 1)),
                     pl.BlockSpec((tq, LANES), lambda b, j: (tile(b, j), 0))]
        args += [vf_src, hv]
    in_specs += [pl.BlockSpec((2, RW_IN), const2),
                 pl.BlockSpec((1, SUBLANES, RW_DIM), dir3),
                 pl.BlockSpec((SUBLANES, RW_DIM), const2),
                 pl.BlockSpec((1, LANES, 2 * RW_DIM), dir3),
                 pl.BlockSpec((LANES, RW_DIM), const2)]
    args += [prm["tshift"], prm["pd"], prm["ps"], prm["wla"], prm["wg"]]
    if vres:
        in_specs += [pl.BlockSpec((LANES, RW_DIM), const2), pl.BlockSpec((LANES, RW_DIM), const2)]
        args += [prm["vu_hi"], prm["vu_lo"]]
    sspec = pl.BlockSpec((1, N_PAIRS, LANES, LANES), lambda b, j: (b, 0, 0, 0))
    in_specs += [pl.BlockSpec((LANES, LANES), const2), sspec]
    args += [prm["e128"], state_in]
    return pl.pallas_call(
        functools.partial(_rwkv_kernel, nt=nt, tq=tq, vres=vres, drn=drn),
        grid=(nb, nt),
        in_specs=in_specs,
        out_specs=[pl.BlockSpec((tq, RW_DIM), lambda b, j: (tile(b, j), 0)),
                   pl.BlockSpec((tq, 2 * RW_DIM), lambda b, j: (tile(b, j), 0)),
                   sspec],
        out_shape=[jax.ShapeDtypeStruct((m, RW_DIM), ACT),
                   jax.ShapeDtypeStruct((m, 2 * RW_DIM), ACT),
                   jax.ShapeDtypeStruct((nb, N_PAIRS, LANES, LANES), F32)],
        scratch_shapes=[pltpu.VMEM((8, tq, RW_DIM), F32), pltpu.VMEM((N_PAIRS, LANES, LANES), F32)],
        compiler_params=_cparams("parallel", "arbitrary"),
        name="rwkv7_fwd" if drn == 0 else "rwkv7_bwd",
    )(*args)


def _rwkv_both(p, vf_src, hv, states, prm, nb, rows):
    outs = [_rwkv(p, vf_src, hv, states[d], prm, nb, rows, d) for d in range(2)]
    return [o[0] for o in outs], [o[1] for o in outs], [o[2] for o in outs]


def _merge_kernel(x_ref, y0_ref, y1_ref, a0_ref, a1_ref, ya_ref, g0_ref, g1_ref, g2_ref, g3_ref, ln_ref, e_ref,
                  wbr_ref, wba_ref, wo_ref, mod_ref, o_ref):
    e = e_ref[...]
    y = y0_ref[...].astype(F32) + y1_ref[...].astype(F32)
    mu = _segsum(y, e) * (1.0 / HEAD_DIM)
    yc = y - mu
    var = _segsum(yc * yc, e) * (1.0 / HEAD_DIM)
    yn = yc * lax.rsqrt(var + GN_EPS) * ln_ref[0:1, :] + ln_ref[1:2, :]
    cv = a0_ref[:, 0:RW_DIM].astype(F32) + a1_ref[:, 0:RW_DIM].astype(F32)
    yrw = (yn + cv) * a0_ref[:, RW_DIM:2 * RW_DIM].astype(F32)
    a = _bdot(yrw, wbr_ref[...])
    b = jnp.dot(ya_ref[...], wba_ref[...], preferred_element_type=F32)
    h = a.shape[1] // 2
    mrg = jnp.concatenate(
        [jax.nn.sigmoid(g0_ref[...].astype(F32)) * a[:, :h] + jax.nn.sigmoid(g2_ref[...].astype(F32)) * b[:, :h],
         jax.nn.sigmoid(g1_ref[...].astype(F32)) * a[:, h:] + jax.nn.sigmoid(g3_ref[...].astype(F32)) * b[:, h:]],
        axis=1)
    o_ref[...] = x_ref[...] + mod_ref[0, 2:3, :] * _bdot(mrg, wo_ref[...])


def _merge(x2, ys, auxs, ya, p, ln, e128, wbr, wba, wo, mod, rows):
    m, d = x2.shape
    tm = _row_tile(rows, 512)
    per = rows // tm
    many = mod.shape[0] > 1
    gb = GATE_OFF // 512
    const2 = lambda i: (0, 0)
    in_specs = [pl.BlockSpec((tm, d), lambda i: (i, 0)),
                pl.BlockSpec((tm, RW_DIM), lambda i: (i, 0)),
                pl.BlockSpec((tm, RW_DIM), lambda i: (i, 0)),
                pl.BlockSpec((tm, 2 * RW_DIM), lambda i: (i, 0)),
                pl.BlockSpec((tm, 2 * RW_DIM), lambda i: (i, 0)),
                pl.BlockSpec((tm, ATT_DIM), lambda i: (i, 0))]
    in_specs += [pl.BlockSpec((tm, 512), (lambda i, c=c: (i, gb + c))) for c in range(4)]
    in_specs += [pl.BlockSpec((2, RW_DIM), const2),
                 pl.BlockSpec((LANES, LANES), const2),
                 pl.BlockSpec(wbr.shape, const2),
                 pl.BlockSpec(wba.shape, const2),
                 pl.BlockSpec(wo.shape, const2),
                 pl.BlockSpec((1, 6, d), lambda i: ((i // per) if many else 0, 0, 0))]
    return pl.pallas_call(
        _merge_kernel,
        grid=(m // tm,),
        in_specs=in_specs,
        out_specs=pl.BlockSpec((tm, d), lambda i: (i, 0)),
        out_shape=jax.ShapeDtypeStruct((m, d), F32),
        compiler_params=_cparams("parallel"),
        name="branch_merge",
    )(x2, ys[0], ys[1], auxs[0], auxs[1], ya, p, p, p, p, ln, e128, wbr, wba, wo, mod)


def _ffn_kernel(x_ref, g_ref, mod_ref, wg_ref, wu_ref, wd_ref, o_ref, h_sc, acc_sc, *, nf):
    f = pl.program_id(1)

    @pl.when(f == 0)
    def _():
        h = _norm_mod(x_ref[...], g_ref[...], mod_ref[0, 3:4, :], mod_ref[0, 4:5, :])
        h_sc[...] = h.astype(BF16)
        acc_sc[...] = jnp.zeros_like(acc_sc)

    h = h_sc[...]
    a = jnp.dot(h, wg_ref[...], preferred_element_type=F32)
    u = jnp.dot(h, wu_ref[...], preferred_element_type=F32)
    act = (a * jax.nn.sigmoid(a)) * u
    acc_sc[...] += jnp.dot(act.astype(BF16), wd_ref[...], preferred_element_type=F32)

    @pl.when(f == nf - 1)
    def _():
        o_ref[...] = x_ref[...] + mod_ref[0, 5:6, :] * acc_sc[...]


def _ffn(x2, gain, mod, wg, wu, wd, rows):
    m, d = x2.shape
    dff = wg.shape[1]
    tm = _row_tile(rows, 512)
    per = rows // tm
    many = mod.shape[0] > 1
    tf = dff // 2 if (dff // 2) % LANES == 0 else dff
    nf = dff // tf
    return pl.pallas_call(
        functools.partial(_ffn_kernel, nf=nf),
        grid=(m // tm, nf),
        in_specs=[pl.BlockSpec((tm, d), lambda i, f: (i, 0)),
                  pl.BlockSpec((1, d), lambda i, f: (0, 0)),
                  pl.BlockSpec((1, 6, d), lambda i, f: ((i // per) if many else 0, 0, 0)),
                  pl.BlockSpec((d, tf), lambda i, f: (0, f)),
                  pl.BlockSpec((d, tf), lambda i, f: (0, f)),
                  pl.BlockSpec((tf, d), lambda i, f: (f, 0))],
        out_specs=pl.BlockSpec((tm, d), lambda i, f: (i, 0)),
        out_shape=jax.ShapeDtypeStruct((m, d), F32),
        scratch_shapes=[pltpu.VMEM((tm, d), BF16), pltpu.VMEM((tm, d), F32)],
        compiler_params=_cparams("parallel", "arbitrary"),
        name="ffn_swiglu",
    )(x2, gain.reshape(1, d), mod, wg, wu, wd)


def _dma_pieces(count, sizes, make):
    for k, size in enumerate(sizes):
        @pl.when((count & size) != 0)
        def _(k=k, size=size):
            make(k, pl.multiple_of(count & ~(2 * size - 1), PACK), size)


def _moe_gather_kernel(pos_ref, cnt_ref, h_ref, rankt_ref, zero_ref, out_ref, hs_sc, sem, *, ne, nsteps, sizes):
    del zero_ref
    i = pl.program_id(0)
    ex = pl.program_id(1)
    step = i * ne + ex
    slot = step % 2
    tt = h_ref.shape[0]

    def copies(st, sl, act):
        base = pos_ref[st]

        def make(k, off, size):
            act(pltpu.make_async_copy(hs_sc.at[sl, pl.ds(off, size)],
                                      out_ref.at[pl.ds(pl.multiple_of(base + off, PACK), size)], sem.at[sl, k]))

        _dma_pieces(cnt_ref[st], sizes, make)

    @pl.when(step >= 2)
    def _():
        copies(step - 2, slot, lambda cp: cp.wait())

    rrow = rankt_ref[0, pl.ds(ex, 1), :]
    rowid = lax.broadcasted_iota(jnp.int32, (MOE_BLK, tt), 0).astype(F32)

    def gather(b, carry):
        sel = ((rrow - (b * MOE_BLK).astype(F32)) == rowid).astype(BF16)
        r0 = pl.multiple_of(b * MOE_BLK, MOE_BLK)
        hs_sc[slot, pl.ds(r0, MOE_BLK), :] = jnp.dot(sel, h_ref[...], preferred_element_type=F32).astype(BF16)
        return carry

    lax.fori_loop(0, (cnt_ref[step] + MOE_BLK - 1) // MOE_BLK, gather, 0)
    copies(step, slot, lambda cp: cp.start())

    @pl.when(step == nsteps - 1)
    def _():
        if nsteps >= 2:
            copies(step - 1, 1 - slot, lambda cp: cp.wait())
        copies(step, slot, lambda cp: cp.wait())


def _moe_ffn_kernel(te_ref, tm_ref, tv_ref, hs_ref, wg_ref, wu_ref, wd_ref, os_ref, acc_sc, *, nf):
    del te_ref, tm_ref
    t = pl.program_id(0)
    f = pl.program_id(1)
    valid = tv_ref[t] != 0

    @pl.when(valid)
    def _():
        hb = hs_ref[...]
        a = jnp.dot(hb, wg_ref[0], preferred_element_type=F32)
        u = jnp.dot(hb, wu_ref[0], preferred_element_type=F32)
        act = ((a * jax.nn.sigmoid(a)) * u).astype(BF16)
        part = jnp.dot(act, wd_ref[0], preferred_element_type=F32)

        @pl.when(f == 0)
        def _():
            acc_sc[...] = part

        @pl.when(f > 0)
        def _():
            acc_sc[...] += part

    @pl.when(f == nf - 1)
    def _():
        os_ref[...] = jnp.where(valid, acc_sc[...], 0.0).astype(os_ref.dtype)


def _moe_combine_kernel(pos_ref, cnt_ref, x_ref, we_ref, rank_ref, mod_ref, os_ref, o_ref, acc_sc, ob_sc, sem,
                        *, ne, nsteps, sizes):
    i = pl.program_id(0)
    ex = pl.program_id(1)
    step = i * ne + ex
    slot = step % 2
    tt = x_ref.shape[0]
    sblk = min(2 * MOE_BLK, tt)
    lane = lax.broadcasted_iota(jnp.int32, (tt, LANES), 1)

    def copies(st, sl, act):
        base = pos_ref[st]

        def make(k, off, size):
            act(pltpu.make_async_copy(os_ref.at[pl.ds(pl.multiple_of(base + off, PACK), size)],
                                      ob_sc.at[sl, pl.ds(off, size)], sem.at[sl, k]))

        _dma_pieces(cnt_ref[st], sizes, make)

    @pl.when(step == 0)
    def _():
        ob_sc[...] = jnp.zeros_like(ob_sc)
        copies(0, 0, lambda cp: cp.start())

    @pl.when(step + 1 < nsteps)
    def _():
        copies(step + 1, 1 - slot, lambda cp: cp.start())

    copies(step, slot, lambda cp: cp.wait())

    @pl.when(ex == 0)
    def _():
        acc_sc[...] = jnp.zeros_like(acc_sc)

    rcol = jnp.sum(jnp.where(lane == ex, rank_ref[...], 0.0), axis=1, keepdims=True)
    wcol = jnp.sum(jnp.where(lane == ex, we_ref[...], 0.0), axis=1, keepdims=True)
    colid = lax.broadcasted_iota(jnp.int32, (tt, sblk), 1).astype(F32)

    def scatter(b, carry):
        sel = ((rcol - (b * sblk).astype(F32)) == colid).astype(BF16)
        r0 = pl.multiple_of(b * sblk, sblk)
        acc_sc[...] += wcol * jnp.dot(sel, ob_sc[slot, pl.ds(r0, sblk), :], preferred_element_type=F32)
        return carry

    lax.fori_loop(0, (cnt_ref[step] + sblk - 1) // sblk, scatter, 0)

    @pl.when(ex == ne - 1)
    def _():
        o_ref[...] = x_ref[...] + mod_ref[0, 5:6, :] * acc_sc[...]


def _route_kernel(x_ref, g_ref, mod_ref, rt_ref, h_ref, we_ref, rank_ref, rankt_ref, cnt_ref, *, ne):
    tt = x_ref.shape[0]
    h = _norm_mod(x_ref[...], g_ref[...], mod_ref[0, 3:4, :], mod_ref[0, 4:5, :])
    h_ref[...] = h.astype(BF16)
    lane = lax.broadcasted_iota(jnp.int32, (tt, LANES), 1)
    logits = jnp.dot(h, rt_ref[...], precision=HIGHEST, preferred_element_type=F32)
    lg = jnp.where(lane < ne, logits, NEG_INF)
    t1 = jnp.max(lg, axis=1, keepdims=True)
    i1 = jnp.min(jnp.where(lg == t1, lane, LANES), axis=1, keepdims=True)
    lg2 = jnp.where(lane == i1, NEG_INF, lg)
    t2 = jnp.max(lg2, axis=1, keepdims=True)
    i2 = jnp.min(jnp.where(lg2 == t2, lane, LANES), axis=1, keepdims=True)
    e2 = jnp.exp(t2 - t1)
    den = 1.0 + e2
    we_ref[...] = jnp.where(lane == i1, 1.0 / den, 0.0) + jnp.where(lane == i2, e2 / den, 0.0)
    sel = (lane == i1) | (lane == i2)
    self32 = sel.astype(F32)
    before = (lax.broadcasted_iota(jnp.int32, (tt, tt), 1) < lax.broadcasted_iota(jnp.int32, (tt, tt), 0))
    rank = jnp.dot(before.astype(BF16), sel.astype(BF16), preferred_element_type=F32)
    rank = jnp.where(sel, rank, -1.0)
    rank_ref[...] = rank
    rankt_ref[0] = rank.T[0:SUBLANES, :]
    cnt_ref[0] = jnp.broadcast_to(jnp.sum(self32, axis=0, keepdims=True), (SUBLANES, LANES))


def _moe(x2, gain, mod, router_pad, wg, wu, wd, rows):
    m, d = x2.shape
    ne, _, dff = wg.shape
    assert ne <= SUBLANES
    tt = _row_tile(rows, MOE_TILE)
    nti = m // tt
    per = rows // tt
    many = mod.shape[0] > 1
    h, we, rank, rankt, cnt = pl.pallas_call(
        functools.partial(_route_kernel, ne=ne),
        grid=(nti,),
        in_specs=[pl.BlockSpec((tt, d), lambda i: (i, 0)),
                  pl.BlockSpec((1, d), lambda i: (0, 0)),
                  pl.BlockSpec((1, 6, d), lambda i: ((i // per) if many else 0, 0, 0)),
                  pl.BlockSpec((d, LANES), lambda i: (0, 0))],
        out_specs=[pl.BlockSpec((tt, d), lambda i: (i, 0)),
                   pl.BlockSpec((tt, LANES), lambda i: (i, 0)),
                   pl.BlockSpec((tt, LANES), lambda i: (i, 0)),
                   pl.BlockSpec((1, SUBLANES, tt), lambda i: (i, 0, 0)),
                   pl.BlockSpec((1, SUBLANES, LANES), lambda i: (i, 0, 0))],
        out_shape=[jax.ShapeDtypeStruct((m, d), BF16),
                   jax.ShapeDtypeStruct((m, LANES), F32),
                   jax.ShapeDtypeStruct((m, LANES), F32),
                   jax.ShapeDtypeStruct((nti, SUBLANES, tt), F32),
                   jax.ShapeDtypeStruct((nti, SUBLANES, LANES), F32)],
        compiler_params=_cparams("parallel"),
        name="moe_route",
    )(x2, gain.reshape(1, d), mod, router_pad)
    c16 = (cnt[:, 0, :ne].astype(jnp.int32) + (PACK - 1)) // PACK * PACK
    region = (jnp.sum(c16, axis=0) + (MOE_FT - 1)) // MOE_FT * MOE_FT
    e_off = jnp.cumsum(region) - region
    pos = (e_off[None, :] + jnp.cumsum(c16, axis=0) - c16).reshape(-1).astype(jnp.int32)
    c16 = c16.reshape(-1)
    rmax = -(-(2 * m + nti * ne * PACK + ne * MOE_FT) // MOE_FT) * MOE_FT
    nft = rmax // MOE_FT
    tiles_e = jnp.cumsum(region // MOE_FT)
    tid = jnp.arange(nft, dtype=jnp.int32)
    tile_valid = (tid < tiles_e[-1]).astype(jnp.int32)
    tile_map = jnp.minimum(tid, tiles_e[-1] - 1).astype(jnp.int32)
    tile_exp = jnp.minimum(jnp.searchsorted(tiles_e, tile_map, side="right"), ne - 1).astype(jnp.int32)

    sizes = tuple(tt >> k for k in range(tt.bit_length()) if (tt >> k) >= PACK)
    nsteps = nti * ne
    hs = pl.pallas_call(
        functools.partial(_moe_gather_kernel, ne=ne, nsteps=nsteps, sizes=sizes),
        grid_spec=pltpu.PrefetchScalarGridSpec(
            num_scalar_prefetch=2,
            grid=(nti, ne),
            in_specs=[pl.BlockSpec((tt, d), lambda i, e, p_, c_: (i, 0)),
                      pl.BlockSpec((1, SUBLANES, tt), lambda i, e, p_, c_: (i, 0, 0)),
                      pl.BlockSpec(memory_space=pl.ANY)],
            out_specs=pl.BlockSpec(memory_space=pl.ANY),
            scratch_shapes=[pltpu.VMEM((2, tt, d), BF16), pltpu.SemaphoreType.DMA((2, len(sizes)))]),
        out_shape=jax.ShapeDtypeStruct((rmax, d), BF16),
        input_output_aliases={4: 0},
        compiler_params=_cparams("arbitrary", "arbitrary"),
        name="moe_gather",
    )(pos, c16, h, rankt, jnp.zeros((rmax, d), BF16))

    tf = dff // 2 if (dff // 2) % LANES == 0 else dff
    nf = dff // tf
    os_sorted = pl.pallas_call(
        functools.partial(_moe_ffn_kernel, nf=nf),
        grid_spec=pltpu.PrefetchScalarGridSpec(
            num_scalar_prefetch=3,
            grid=(nft, nf),
            in_specs=[pl.BlockSpec((MOE_FT, d), lambda t, f, te, tm, tv: (tm[t], 0)),
                      pl.BlockSpec((1, d, tf), lambda t, f, te, tm, tv: (te[t], 0, f)),
                      pl.BlockSpec((1, d, tf), lambda t, f, te, tm, tv: (te[t], 0, f)),
                      pl.BlockSpec((1, tf, d), lambda t, f, te, tm, tv: (te[t], f, 0))],
            out_specs=pl.BlockSpec((MOE_FT, d), lambda t, f, te, tm, tv: (t, 0)),
            scratch_shapes=[pltpu.VMEM((MOE_FT, d), F32)]),
        out_shape=jax.ShapeDtypeStruct((rmax, d), BF16),
        compiler_params=_cparams("parallel", "arbitrary"),
        name="moe_experts",
    )(tile_exp, tile_map, tile_valid, hs, wg, wu, wd)

    return pl.pallas_call(
        functools.partial(_moe_combine_kernel, ne=ne, nsteps=nsteps, sizes=sizes),
        grid_spec=pltpu.PrefetchScalarGridSpec(
            num_scalar_prefetch=2,
            grid=(nti, ne),
            in_specs=[pl.BlockSpec((tt, d), lambda i, e, p_, c_: (i, 0)),
                      pl.BlockSpec((tt, LANES), lambda i, e, p_, c_: (i, 0)),
                      pl.BlockSpec((tt, LANES), lambda i, e, p_, c_: (i, 0)),
                      pl.BlockSpec((1, 6, d), lambda i, e, p_, c_: ((i // per) if many else 0, 0, 0)),
                      pl.BlockSpec(memory_space=pl.ANY)],
            out_specs=pl.BlockSpec((tt, d), lambda i, e, p_, c_: (i, 0)),
            scratch_shapes=[pltpu.VMEM((tt, d), F32), pltpu.VMEM((2, tt, d), BF16),
                            pltpu.SemaphoreType.DMA((2, len(sizes)))]),
        out_shape=jax.ShapeDtypeStruct((m, d), F32),
        compiler_params=_cparams("arbitrary", "arbitrary"),
        name="moe_combine",
    )(pos, c16, x2, we, rank, mod, os_sorted)


def _rope_table(t):
    rows = t // GRID_W
    row = jnp.repeat(jnp.arange(rows, dtype=F32), GRID_W)
    col = jnp.tile(jnp.arange(GRID_W, dtype=F32), rows)
    half = HEAD_DIM // 2
    inv_freq = ROPE_THETA ** (-jnp.arange(0, half, 2, dtype=F32) / half)
    ar = row[:, None] * inv_freq
    ac = col[:, None] * inv_freq
    ang = jnp.concatenate([ar, ar, ac, ac] * 2, axis=-1)
    return jnp.stack([jnp.cos(ang), jnp.sin(ang)])


def _pad_rows(a, rows):
    return jnp.zeros((rows,) + a.shape[1:], a.dtype).at[:a.shape[0]].set(a)


def _hi_lo(w):
    hi = w.astype(BF16)
    return hi, (w - hi.astype(F32)).astype(BF16)


def kernel(x, c, ctx, c_ctx, norm1, norm2, w_mod, b_mod, w_in, tshift, w0, w_decay_up, a0, w_iclr_up, v0, w_vres_down, w_vres_up, w_gate_up, k_k, k_a, r_k, lnx_g, lnx_b, q_norm, k_norm, sink, w_br_rwkv, w_br_attn, w_out, ffn_gate, ffn_up, ffn_down, router, moe_gate, moe_up, moe_down):
    nb, t, d = x.shape
    cn = ctx.shape[1]
    depth = w_in.shape[0]
    mods = _modulation(c, c_ctx, w_mod, b_mod)
    cs = _rope_table(t)
    e128 = jnp.kron(jnp.eye(2, dtype=F32), jnp.ones((HEAD_DIM, HEAD_DIM), F32)).astype(BF16)
    xl = x.reshape(nb * t, d)
    xc = ctx.reshape(nb * cn, d)
    vf_l = vf_c = None
    for layer in range(depth):
        last = layer == depth - 1
        mod_l = mods[layer, :nb].reshape(nb, 6, d)
        mod_c = mods[layer, nb:nb + 1].reshape(1, 6, d)
        w_in_b = w_in[layer].astype(BF16)
        vd_b = None
        prm = {"tshift": tshift[layer], "e128": e128, "wg": w_gate_up[layer].astype(BF16)}
        prm["pd"] = jnp.stack(
            [_pad_rows(jnp.stack([w0[layer, dd], a0[layer, dd]]), SUBLANES) for dd in range(2)])
        shared = [k_k[layer], k_a[layer], r_k[layer].reshape(RW_DIM)]
        if layer > 0:
            shared.append(v0[layer - 1])
            vd_b = _pad_rows(w_vres_down[layer - 1].T, LANES).T.astype(BF16)
            prm["vu_hi"], prm["vu_lo"] = _hi_lo(_pad_rows(w_vres_up[layer - 1], LANES))
        prm["ps"] = _pad_rows(jnp.stack(shared), SUBLANES)
        wla = jnp.zeros((2, LANES, 2 * RW_DIM), F32)
        wla = wla.at[:, :HEAD_DIM, :RW_DIM].set(w_decay_up[layer]).at[:, HEAD_DIM:, RW_DIM:].set(w_iclr_up[layer])
        prm["wla"] = wla.astype(BF16)

        p_l, hv_l = _inproj(xl, norm1[layer], mod_l, w_in_b, vd_b, t)
        p_c, hv_c = _inproj(xc, norm1[layer], mod_c, w_in_b, vd_b, cn)

        s0 = jnp.zeros((nb, N_PAIRS, LANES, LANES), F32)
        y_c, aux_c, s_ctx = _rwkv_both(p_c, vf_c, hv_c, [s0, s0], prm, nb, cn)
        y_l, aux_l, _ = _rwkv_both(p_l, vf_l, hv_l, s_ctx, prm, nb, t)
        if layer == 0:
            vf_l, vf_c = aux_l[1], aux_c[1]

        qg = jnp.tile(q_norm[layer], 2).reshape(1, LANES)
        kg = jnp.tile(k_norm[layer], 2).reshape(1, LANES)
        qn_l, kn_l, vb_l = _qk_prep(p_l, cs, qg, kg, e128, t, True)
        qn_c, kn_c, vb_c = _qk_prep(p_c, None, qg, kg, e128, cn, False)
        sk = sink[layer].reshape(1, -1)
        ya_l = _attention(qn_l, kn_l, vb_l, kn_c, vb_c, sk, nb, t, cn, True)

        ln = jnp.stack([lnx_g[layer], lnx_b[layer]])
        wbr = w_br_rwkv[layer].astype(BF16)
        wba = w_br_attn[layer].astype(BF16)
        wo = w_out[layer].astype(BF16)
        xl = _merge(xl, y_l, aux_l, ya_l, p_l, ln, e128, wbr, wba, wo, mod_l, t)
        if not last:
            ya_c = _attention(qn_c, None, None, kn_c, vb_c, sk, nb, cn, cn, False)
            xc = _merge(xc, y_c, aux_c, ya_c, p_c, ln, e128, wbr, wba, wo, mod_c, cn)

        i = layer // 2
        if layer % 2 == 0:
            fw = (ffn_gate[i].astype(BF16), ffn_up[i].astype(BF16), ffn_down[i].astype(BF16))
            xl = _ffn(xl, norm2[layer], mod_l, *fw, t)
            if not last:
                xc = _ffn(xc, norm2[layer], mod_c, *fw, cn)
        else:
            rt = _pad_rows(router[i].T, LANES).T
            mw = (moe_gate[i].astype(BF16), moe_up[i].astype(BF16), moe_down[i].astype(BF16))
            xl = _moe(xl, norm2[layer], mod_l, rt, *mw, t)
            if not last:
                xc = _moe(xc, norm2[layer], mod_c, rt, *mw, cn)
    return xl.reshape(nb, t, d)
```

```python
import functools

import jax
import jax.numpy as jnp
from jax import lax
from jax.experimental import pallas as pl
from jax.experimental.pallas import tpu as pltpu

F32 = jnp.float32
BF16 = jnp.bfloat16
HIGHEST = lax.Precision.HIGHEST

LANES = 128
SUBLANES = 8
PACK = 16
ACT = jnp.bfloat16
HEAD_DIM = 64
RW_DIM = 512
ATT_DIM = 512
KV_DIM = 128
RW_IN = 1792
ATT_OFF = RW_IN
GATE_OFF = RW_IN + ATT_DIM + 2 * KV_DIM
N_EXPERTS = 8
CHUNK = 64
N_PAIRS = RW_DIM // LANES
MOE_TILE = 1024
MOE_BLK = 128
MOE_FT = 512
ATT_SCALE = HEAD_DIM ** -0.5
ROPE_THETA = 10000.0
GRID_W = 64
RMS_EPS = 1e-6
GN_EPS = 64e-5
NEG_INF = -1e30
DECAY_SCALE = 0.6065306597126334
VMEM_LIMIT = 48 * 1024 * 1024


def _cparams(*sem):
    return pltpu.CompilerParams(dimension_semantics=sem, vmem_limit_bytes=VMEM_LIMIT)


def _row_tile(rows, cap):
    t = cap
    while rows % t:
        t //= 2
    return t


def _bdot(a, b):
    return jnp.dot(a.astype(BF16), b.astype(BF16), preferred_element_type=F32)


def _bdot_nt(a, b):
    return lax.dot_general(a.astype(BF16), b.astype(BF16), (((1,), (1,)), ((), ())),
                           preferred_element_type=F32)


def _bdot_tn(a, b):
    return lax.dot_general(a.astype(BF16), b.astype(BF16), (((0,), (0,)), ((), ())),
                           preferred_element_type=F32)


def _split2(x):
    hi = x.astype(BF16)
    lo = (x - hi.astype(F32)).astype(BF16)
    return hi, lo


def _dot3(a, w_hi, w_lo):
    a_hi, a_lo = _split2(a)
    return (jnp.dot(a_hi, w_hi, preferred_element_type=F32)
            + (jnp.dot(a_lo, w_hi, preferred_element_type=F32)
               + jnp.dot(a_hi, w_lo, preferred_element_type=F32)))


def _segsum(x, e, exact=True):
    outs = []
    for s in range(x.shape[1] // LANES):
        hi, lo = _split2(x[:, s * LANES:(s + 1) * LANES])
        acc = jnp.dot(hi, e, preferred_element_type=F32)
        outs.append(acc + jnp.dot(lo, e, preferred_element_type=F32) if exact else acc)
    return outs[0] if len(outs) == 1 else jnp.concatenate(outs, axis=1)


def _norm_mod(x, gain, shift, scale):
    y = x * lax.rsqrt(jnp.mean(x * x, axis=-1, keepdims=True) + RMS_EPS) * gain
    return y * (1.0 + scale) + shift


def _mod_kernel(c_ref, w_ref, b_ref, o_ref):
    c = c_ref[...]
    s = c * jax.nn.sigmoid(c)
    o_ref[0] = jnp.dot(s, w_ref[0], precision=HIGHEST, preferred_element_type=F32) + b_ref[0]


def _modulation(c, c_ctx, w_mod, b_mod):
    depth, d, n6 = w_mod.shape
    nb = c.shape[0]
    rm = -(-(nb + 1) // SUBLANES) * SUBLANES
    cc = jnp.zeros((rm, d), F32).at[:nb].set(c).at[nb].set(c_ctx)
    tn = _row_tile(n6, 1536)
    return pl.pallas_call(
        _mod_kernel,
        grid=(depth, n6 // tn),
        in_specs=[pl.BlockSpec((rm, d), lambda l, n: (0, 0)),
                  pl.BlockSpec((1, d, tn), lambda l, n: (l, 0, n)),
                  pl.BlockSpec((1, 1, tn), lambda l, n: (l, 0, n))],
        out_specs=pl.BlockSpec((1, rm, tn), lambda l, n: (l, 0, n)),
        out_shape=jax.ShapeDtypeStruct((depth, rm, n6), F32),
        compiler_params=_cparams("parallel", "parallel"),
        name="adaln_mod",
    )(cc, w_mod, b_mod.reshape(depth, 1, n6))


def _inproj_kernel(*refs, vres):
    if vres:
        x_ref, g_ref, mod_ref, w_ref, vd_ref, o_ref, hv_ref, h_sc = refs
    else:
        x_ref, g_ref, mod_ref, w_ref, o_ref, h_sc = refs

    @pl.when(pl.program_id(1) == 0)
    def _():
        h = _norm_mod(x_ref[...], g_ref[...], mod_ref[0, 0:1, :], mod_ref[0, 1:2, :])
        hb = h.astype(BF16)
        h_sc[...] = hb
        if vres:
            hv_ref[...] = jnp.dot(hb, vd_ref[...], preferred_element_type=F32)

    o_ref[...] = jnp.dot(h_sc[...], w_ref[...], preferred_element_type=F32).astype(o_ref.dtype)


def _inproj(x2, gain, mod, w_b, vd_b, rows):
    m, d = x2.shape
    n = w_b.shape[1]
    tm = _row_tile(rows, 1024)
    tn = _row_tile(n, 1536)
    per = rows // tm
    many = mod.shape[0] > 1
    vres = vd_b is not None
    in_specs = [pl.BlockSpec((tm, d), lambda i, j: (i, 0)),
                pl.BlockSpec((1, d), lambda i, j: (0, 0)),
                pl.BlockSpec((1, 6, d), lambda i, j: ((i // per) if many else 0, 0, 0)),
                pl.BlockSpec((d, tn), lambda i, j: (0, j))]
    args = [x2, gain.reshape(1, d), mod, w_b]
    out_specs = [pl.BlockSpec((tm, tn), lambda i, j: (i, j))]
    out_shape = [jax.ShapeDtypeStruct((m, n), ACT)]
    if vres:
        in_specs.append(pl.BlockSpec((d, LANES), lambda i, j: (0, 0)))
        args.append(vd_b)
        out_specs.append(pl.BlockSpec((tm, LANES), lambda i, j: (i, 0)))
        out_shape.append(jax.ShapeDtypeStruct((m, LANES), F32))
    outs = pl.pallas_call(
        functools.partial(_inproj_kernel, vres=vres),
        grid=(m // tm, n // tn),
        in_specs=in_specs, out_specs=out_specs, out_shape=out_shape,
        scratch_shapes=[pltpu.VMEM((tm, d), BF16)],
        compiler_params=_cparams("parallel", "arbitrary"),
        name="in_proj",
    )(*args)
    return (outs[0], outs[1]) if vres else (outs[0], None)


def _qkprep_kernel(*refs, rope):
    if rope:
        q0_ref, q1_ref, k_ref, v_ref, cs_ref, qg_ref, kg_ref, e_ref, qn_ref, kn_ref, vb_ref = refs
    else:
        q0_ref, q1_ref, k_ref, v_ref, qg_ref, kg_ref, e_ref, qn_ref, kn_ref, vb_ref = refs
    e = e_ref[...]
    lane = lax.broadcasted_iota(jnp.int32, (1, LANES), 1)
    first = (lane & 31) < 16

    def norm_rope(u, gain):
        y = u * lax.rsqrt(_segsum(u * u, e) * (1.0 / HEAD_DIM) + RMS_EPS) * gain
        if rope:
            rot = jnp.where(first, -pltpu.roll(y, LANES - 16, 1), pltpu.roll(y, 16, 1))
            y = y * cs_ref[0] + rot * cs_ref[1]
        return y

    qg = qg_ref[...]
    for s, ref in enumerate((q0_ref, q1_ref)):
        for t in range(2):
            u = ref[:, t * LANES:(t + 1) * LANES].astype(F32)
            c0 = (2 * s + t) * LANES
            qn_ref[:, c0:c0 + LANES] = (norm_rope(u, qg) * ATT_SCALE).astype(BF16)
    low = lane < HEAD_DIM
    for src_val, dst in ((norm_rope(k_ref[...].astype(F32), kg_ref[...]), kn_ref), (v_ref[...].astype(F32), vb_ref)):
        swapped = pltpu.roll(src_val, HEAD_DIM, 1)
        dst[:, 0:LANES] = jnp.where(low, src_val, swapped).astype(BF16)
        dst[:, LANES:2 * LANES] = jnp.where(low, swapped, src_val).astype(BF16)


def _qk_prep(p, cs, qg, kg, e128, rows, rope):
    m = p.shape[0]
    tm = _row_tile(rows, 512)
    per = rows // tm
    qb = ATT_OFF // 256
    kb = (ATT_OFF + ATT_DIM) // LANES
    in_specs = [pl.BlockSpec((tm, 256), lambda i: (i, qb)),
                pl.BlockSpec((tm, 256), lambda i: (i, qb + 1)),
                pl.BlockSpec((tm, LANES), lambda i: (i, kb)),
                pl.BlockSpec((tm, LANES), lambda i: (i, kb + 1))]
    args = [p, p, p, p]
    if rope:
        in_specs.append(pl.BlockSpec((2, tm, LANES), lambda i: (0, i % per, 0)))
        args.append(cs)
    in_specs += [pl.BlockSpec((1, LANES), lambda i: (0, 0)),
                 pl.BlockSpec((1, LANES), lambda i: (0, 0)),
                 pl.BlockSpec((LANES, LANES), lambda i: (0, 0))]
    args += [qg, kg, e128]
    return pl.pallas_call(
        functools.partial(_qkprep_kernel, rope=rope),
        grid=(m // tm,),
        in_specs=in_specs,
        out_specs=[pl.BlockSpec((tm, ATT_DIM), lambda i: (i, 0)),
                   pl.BlockSpec((tm, 2 * KV_DIM), lambda i: (i, 0)),
                   pl.BlockSpec((tm, 2 * KV_DIM), lambda i: (i, 0))],
        out_shape=[jax.ShapeDtypeStruct((m, ATT_DIM), BF16),
                   jax.ShapeDtypeStruct((m, 2 * KV_DIM), BF16),
                   jax.ShapeDtypeStruct((m, 2 * KV_DIM), BF16)],
        compiler_params=_cparams("parallel"),
        name="qk_prep",
    )(*args)


def _attn_kernel(*refs, nq, local):
    if local:
        sink_ref, q_ref, kp_ref, kc_ref, kn_ref, vp_ref, vc_ref, vn_ref, kx_ref, vx_ref, wm_ref, o_ref = refs
        k = jnp.concatenate([kp_ref[...], kc_ref[...], kn_ref[...], kx_ref[...]], axis=0)
        v = jnp.concatenate([vp_ref[...], vc_ref[...], vn_ref[...], vx_ref[...]], axis=0)
    else:
        sink_ref, q_ref, kx_ref, vx_ref, o_ref = refs
        k = kx_ref[...]
        v = vx_ref[...]
    i = pl.program_id(1)
    bq = q_ref.shape[0]
    low = lax.broadcasted_iota(jnp.int32, (1, LANES), 1) < HEAD_DIM
    row = lax.broadcasted_iota(jnp.int32, (4 * bq, 1), 0)
    if local:
        kblk = lax.broadcasted_iota(jnp.int32, (1, 3 * bq), 1) // bq
        off_end = ((kblk == 0) & (i == 0)) | ((kblk == 2) & (i == nq - 1))
        bias = wm_ref[...] + jnp.where(off_end, NEG_INF, 0.0)
    for g in range(2):
        kb = k[:, g * LANES:(g + 1) * LANES]
        vb = v[:, g * LANES:(g + 1) * LANES]
        qa = q_ref[:, (2 * g) * LANES:(2 * g + 1) * LANES].astype(F32)
        qb = q_ref[:, (2 * g + 1) * LANES:(2 * g + 2) * LANES].astype(F32)
        qs = jnp.concatenate([jnp.where(low, qa, 0.0), jnp.where(low, 0.0, qa),
                              jnp.where(low, qb, 0.0), jnp.where(low, 0.0, qb)], axis=0).astype(BF16)
        s = lax.dot_general(qs, kb, (((1,), (1,)), ((), ())), preferred_element_type=F32)
        if local:
            s = jnp.concatenate([s[:, :3 * bq] + bias, s[:, 3 * bq:]], axis=1)
        sk = jnp.where(row < bq, sink_ref[0, 4 * g],
                       jnp.where(row < 2 * bq, sink_ref[0, 4 * g + 1],
                                 jnp.where(row < 3 * bq, sink_ref[0, 4 * g + 2], sink_ref[0, 4 * g + 3])))
        mx = jnp.maximum(jnp.max(s, axis=1, keepdims=True), sk)
        ex = jnp.exp(s - mx)
        den = jnp.sum(ex, axis=1, keepdims=True) + jnp.exp(sk - mx)
        o = jnp.dot(ex.astype(BF16), vb, preferred_element_type=F32) / den
        o_ref[:, (2 * g) * LANES:(2 * g + 1) * LANES] = jnp.where(low, o[0:bq], o[bq:2 * bq]).astype(o_ref.dtype)
        o_ref[:, (2 * g + 1) * LANES:(2 * g + 2) * LANES] = jnp.where(
            low, o[2 * bq:3 * bq], o[3 * bq:4 * bq]).astype(o_ref.dtype)


def _attention(qn, kn, vb, kctx, vctx, sink, nb, rows, cn, local):
    m = qn.shape[0]
    bq = 128
    nq = rows // bq
    smem = pl.BlockSpec(memory_space=pltpu.SMEM)
    qspec = pl.BlockSpec((bq, ATT_DIM), lambda b, i: (b * nq + i, 0))
    cspec = pl.BlockSpec((cn, 2 * KV_DIM), lambda b, i: (b, 0))
    if local:
        prv = pl.BlockSpec((bq, 2 * KV_DIM), lambda b, i: (b * nq + jnp.maximum(i - 1, 0), 0))
        cur = pl.BlockSpec((bq, 2 * KV_DIM), lambda b, i: (b * nq + i, 0))
        nxt = pl.BlockSpec((bq, 2 * KV_DIM), lambda b, i: (b * nq + jnp.minimum(i + 1, nq - 1), 0))
        qo = jnp.arange(4 * bq)[:, None] % bq
        kcol = jnp.arange(3 * bq)[None, :]
        inwin = ((kcol // bq == 1) | ((kcol // bq == 0) & (kcol % bq >= qo)) | ((kcol // bq == 2) & (kcol % bq <= qo)))
        wmask = jnp.where(inwin, 0.0, NEG_INF).astype(F32)
        in_specs = [smem, qspec, prv, cur, nxt, prv, cur, nxt, cspec, cspec,
                    pl.BlockSpec((4 * bq, 3 * bq), lambda b, i: (0, 0))]
        args = [sink, qn, kn, kn, kn, vb, vb, vb, kctx, vctx, wmask]
    else:
        in_specs = [smem, qspec, cspec, cspec]
        args = [sink, qn, kctx, vctx]
    return pl.pallas_call(
        functools.partial(_attn_kernel, nq=nq, local=local),
        grid=(nb, nq),
        in_specs=in_specs,
        out_specs=pl.BlockSpec((bq, ATT_DIM), lambda b, i: (b * nq + i, 0)),
        out_shape=jax.ShapeDtypeStruct((m, ATT_DIM), BF16),
        compiler_params=_cparams("parallel", "parallel"),
        name="window_attn" if local else "ctx_attn",
    )(*args)


def _rwkv_kernel(*refs, nt, tq, vres, drn, tps):
    it = iter(refs)
    n_in = 5 if vres else 3
    first = [next(it) for _ in range(n_in)] if tps == 2 else None
    tiles = [[next(it) for _ in range(n_in)] for _ in range(tps)]
    ts_ref, pd_ref, ps_ref, wla_ref, wg_ref = (next(it) for _ in range(5))
    if vres:
        vuh_ref, vul_ref = next(it), next(it)
    e_ref, sin_ref = next(it), next(it)
    y_ref, aux_ref, sout_ref = next(it), next(it), next(it)
    ops = [next(it) for _ in range(tps)]
    auxs = [next(it) for _ in range(tps)]
    s_sc = next(it)

    j = pl.program_id(1)
    sgn = 1 - 2 * drn
    nch = tq // CHUNK
    nsteps = nt // tps
    e = e_ref[...]
    lane = lax.broadcasted_iota(jnp.int32, (1, LANES), 1)
    low = lane < HEAD_DIM

    def natural(k):
        return k if drn == 0 else nt - 1 - k

    def prepare(tile, in_refs, ops_sc, aux_sc):
        p_ref, pv_ref, nx_ref = in_refs[:3]
        rw = p_ref[...].astype(F32)
        row = lax.broadcasted_iota(jnp.int32, (tq, 1), 0)
        prev_row = jnp.where(tile > 0, pv_ref[PACK - 1:PACK, :].astype(F32), 0.0)
        next_row = jnp.where(tile < nt - 1, nx_ref[0:1, :].astype(F32), 0.0)
        prv = jnp.where(row == 0, prev_row, pltpu.roll(rw, 1, 0))
        yield
        nxt = jnp.where(row == tq - 1, next_row, pltpu.roll(rw, tq - 1, 0))
        xs = rw + ts_ref[0:1, :] * (prv - rw) + ts_ref[1:2, :] * (nxt - rw)
        yield

        r = xs[:, 0:RW_DIM]
        k = xs[:, RW_DIM:2 * RW_DIM]
        v = xs[:, 2 * RW_DIM:3 * RW_DIM]
        la = xs[:, 3 * RW_DIM:3 * RW_DIM + LANES]
        la = jnp.where(low, jnp.tanh(la), la)
        la_hi, la_lo = _split2(la)
        lora = (jnp.dot(la_hi, wla_ref[0], preferred_element_type=F32)
                + jnp.dot(la_lo, wla_ref[0], preferred_element_type=F32))
        yield
        logw = -DECAY_SCALE * jax.nn.sigmoid(pd_ref[0, 0:1, :] + lora[:, 0:RW_DIM])
        iclr = jax.nn.sigmoid(pd_ref[0, 1:2, :] + lora[:, RW_DIM:2 * RW_DIM])
        if vres:
            vf_ref, hv_ref = in_refs[3:]
            mix = jax.nn.sigmoid(ps_ref[3:4, :] + _dot3(hv_ref[...], vuh_ref[...], vul_ref[...]))
            v = v + (vf_ref[...].astype(F32) - v) * mix
        yield
        kk = k * ps_ref[0:1, :]
        kk = kk * lax.rsqrt(_segsum(kk * kk, e, exact=False) + 1e-12)
        yield
        kh = k * (1.0 + (iclr - 1.0) * ps_ref[1:2, :])
        bb = kk * iclr
        aux_sc[:, 0:RW_DIM] = (_segsum(r * kh * ps_ref[2:3, :], e, exact=False) * v).astype(aux_sc.dtype)
        if drn == 0:
            glo = xs[:, 3 * RW_DIM + LANES:RW_IN]
            aux_sc[:, RW_DIM:2 * RW_DIM] = _bdot(jax.nn.sigmoid(glo), wg_ref[...]).astype(aux_sc.dtype)
        else:
            aux_sc[:, RW_DIM:2 * RW_DIM] = v.astype(aux_sc.dtype)

        yield
        ti = lax.broadcasted_iota(jnp.int32, (tq, tq), 0)
        si = lax.broadcasted_iota(jnp.int32, (tq, tq), 1)
        tri = (((ti // CHUNK) == (si // CHUNK)) & (((si - ti) * sgn) <= 0)).astype(BF16)
        lw_hi, lw_lo = _split2(logw)
        cin = jnp.dot(tri, lw_hi, preferred_element_type=F32) + jnp.dot(tri, lw_lo, preferred_element_type=F32)
        cin3 = cin.reshape(nch, CHUNK, RW_DIM)
        last = CHUNK - 1 if drn == 0 else 0
        ctot = jnp.broadcast_to(cin3[:, last:last + 1, :], cin3.shape).reshape(tq, RW_DIM)
        yield
        e_neg = jnp.exp(-cin)
        e_rem = jnp.exp(ctot - cin)
        ops_sc[0] = kk * jnp.exp(cin - logw)
        ops_sc[1] = r * jnp.exp(cin)
        ops_sc[2] = bb * e_neg
        ops_sc[3] = kh * e_neg
        yield
        ops_sc[4] = kh * e_rem
        ops_sc[5] = bb * e_rem
        ops_sc[6] = v
        ops_sc[7] = jnp.exp(ctot)

    tr = lax.broadcasted_iota(jnp.int32, (CHUNK, LANES), 0)
    sc = lax.broadcasted_iota(jnp.int32, (CHUNK, LANES), 1) & (CHUNK - 1)
    rel = (sc - tr) * sgn
    strict = rel < 0
    incl = rel <= 0
    eye = (sc == tr).astype(F32)
    same_head = ((lax.broadcasted_iota(jnp.int32, (LANES, LANES), 0) // HEAD_DIM)
                 == (lax.broadcasted_iota(jnp.int32, (LANES, LANES), 1) // HEAD_DIM))

    def stack(x):
        return jnp.concatenate([jnp.where(low, x, 0.0), jnp.where(low, 0.0, x)], axis=0)

    def hdot(x, y):
        return _bdot(x, stack(y))

    def scan_tile(ops_sc, row0):
        probs = []
        for ii in range(nch):
            off = (ii if drn == 0 else nch - 1 - ii) * CHUNK
            for pr in range(N_PAIRS):
                ls = slice(pr * LANES, (pr + 1) * LANES)
                q = {"off": off, "ls": ls, "pr": pr}
                for n, name in enumerate(("kkt", "rt", "bh", "kh", "kp", "bp", "v")):
                    q[name] = ops_sc[n, off:off + CHUNK, ls]
                q["wl"] = ops_sc[7, off:off + 1, ls]
                probs.append(q)
        for q in probs:
            g = _bdot_nt(jnp.concatenate([q["kkt"], q["rt"]], axis=0),
                         jnp.concatenate([stack(q["bh"]), stack(q["kh"])], axis=0))
            q["nn"] = jnp.where(strict, -g[0:CHUNK, 0:LANES], 0.0)
            q["avk"] = jnp.where(strict, g[0:CHUNK, LANES:], 0.0)
            q["arb"] = jnp.where(incl, g[CHUNK:, 0:LANES], 0.0)
            q["ark"] = jnp.where(incl, g[CHUNK:, LANES:], 0.0)
            q["t"] = eye + q["nn"]
        yield
        for q in probs:
            q["q"] = hdot(q["nn"], q["nn"])
        for _ in range(4):
            for q in probs:
                tq2 = _bdot(jnp.concatenate([q["t"], q["q"]], axis=0), stack(q["q"]))
                q["t"] = q["t"] + tq2[:CHUNK]
                q["q"] = tq2[CHUNK:]
            yield
        for q in probs:
            q["t"] = q["t"] + hdot(q["t"], q["q"])
        for q in probs:
            av = _bdot(jnp.concatenate([q["avk"], q["ark"]], axis=0), stack(q["v"]))
            q["av"], q["arkv"] = av[:CHUNK], av[CHUNK:]
        yield
        for q in probs:
            tt2 = _bdot(q["t"], jnp.concatenate([stack(q["kkt"]), stack(q["av"])], axis=1))
            q["tk"], q["tav"] = tt2[:, :LANES], tt2[:, LANES:]
        yield
        for q in probs:
            tn = _bdot_tn(jnp.concatenate([q["tk"], q["tav"]], axis=1), q["bp"])
            q["mb"] = jnp.where(same_head, tn[:LANES], 0.0)
            q["nm"] = jnp.where(same_head, _bdot_tn(q["v"], q["kp"]) - tn[LANES:], 0.0)
            z = _bdot(q["arb"], jnp.concatenate([stack(q["tk"]), stack(q["tav"])], axis=1))
            q["r2"] = q["rt"] - z[:, :LANES]
            q["y0"] = q["arkv"] - z[:, LANES:]
        yield

        st = [s_sc[pr] for pr in range(N_PAIRS)]
        outs = []
        for q in probs:
            sm = st[q["pr"]]
            outs.append(_bdot_nt(q["r2"], sm) + q["y0"])
            st[q["pr"]] = sm * q["wl"] - _bdot(sm, q["mb"]) + q["nm"]
        for q, yo in zip(probs, outs):
            y_ref[row0 + q["off"]:row0 + q["off"] + CHUNK, q["ls"]] = yo.astype(y_ref.dtype)
        for pr in range(N_PAIRS):
            s_sc[pr] = st[pr]

    def run(*stages):
        live = list(stages)
        while live:
            for g in list(live):
                if next(g, live) is live:
                    live.remove(g)

    @pl.when(j == 0)
    def _():
        s_sc[...] = sin_ref[0]

    if tps == 1:
        run(prepare(natural(j), tiles[0], ops[0], auxs[0]))
        aux_ref[...] = auxs[0][...]
        run(scan_tile(ops[0], 0))
    else:
        @pl.when(j == 0)
        def _():
            run(prepare(natural(0), first, ops[0], auxs[0]))

        rows = [0, tq] if drn == 0 else [tq, 0]
        aux_ref[rows[0]:rows[0] + tq, :] = auxs[0][...]
        run(scan_tile(ops[0], rows[0]), prepare(natural(2 * j + 1), tiles[0], ops[1], auxs[1]))
        aux_ref[rows[1]:rows[1] + tq, :] = auxs[1][...]
        run(scan_tile(ops[1], rows[1]),
            prepare(natural(jnp.minimum(2 * j + 2, nt - 1)), tiles[1], ops[0], auxs[0]))

    @pl.when(j == nsteps - 1)
    def _():
        sout_ref[0] = s_sc[...]


def _rwkv(p, vf_src, hv, state_in, prm, nb, rows, drn):
    m = p.shape[0]
    tq = _row_tile(rows, 256)
    nt = rows // tq
    tps = 2 if nt % 2 == 0 else 1
    nsteps = nt // tps
    rb = tq // PACK
    vres = vf_src is not None

    def tile_specs(scan_pos):
        def tile(b, j):
            k = jnp.minimum(scan_pos(j), nt - 1)
            return b * nt + (k if drn == 0 else nt - 1 - k)

        specs = [pl.BlockSpec((tq, RW_IN), lambda b, j: (tile(b, j), 0)),
                 pl.BlockSpec((PACK, RW_IN), lambda b, j: (jnp.maximum(tile(b, j) * rb - 1, 0), 0)),
                 pl.BlockSpec((PACK, RW_IN), lambda b, j: (jnp.minimum((tile(b, j) + 1) * rb, m // PACK - 1), 0))]
        args = [p, p, p]
        if vres:
            specs += [pl.BlockSpec((tq, RW_DIM), lambda b, j: (tile(b, j), 1)),
                      pl.BlockSpec((tq, LANES), lambda b, j: (tile(b, j), 0))]
            args += [vf_src, hv]
        return specs, args

    if tps == 2:
        positions = [lambda j: 0 * j, lambda j: 2 * j + 1, lambda j: 2 * j + 2]
    else:
        positions = [lambda j: j]
    in_specs, args = [], []
    for pos in positions:
        s_, a_ = tile_specs(pos)
        in_specs += s_
        args += a_

    const2 = lambda b, j: (0, 0)
    dir3 = lambda b, j: (drn, 0, 0)
    in_specs += [pl.BlockSpec((2, RW_IN), const2),
                 pl.BlockSpec((1, SUBLANES, RW_DIM), dir3),
                 pl.BlockSpec((SUBLANES, RW_DIM), const2),
                 pl.BlockSpec((1, LANES, 2 * RW_DIM), dir3),
                 pl.BlockSpec((LANES, RW_DIM), const2)]
    args += [prm["tshift"], prm["pd"], prm["ps"], prm["wla"], prm["wg"]]
    if vres:
        in_specs += [pl.BlockSpec((LANES, RW_DIM), const2), pl.BlockSpec((LANES, RW_DIM), const2)]
        args += [prm["vu_hi"], prm["vu_lo"]]
    sspec = pl.BlockSpec((1, N_PAIRS, LANES, LANES), lambda b, j: (b, 0, 0, 0))
    in_specs += [pl.BlockSpec((LANES, LANES), const2), sspec]
    args += [prm["e128"], state_in]

    def out_block(b, j):
        return (b * nsteps + (j if drn == 0 else nsteps - 1 - j), 0)

    return pl.pallas_call(
        functools.partial(_rwkv_kernel, nt=nt, tq=tq, vres=vres, drn=drn, tps=tps),
        grid=(nb, nsteps),
        in_specs=in_specs,
        out_specs=[pl.BlockSpec((tps * tq, RW_DIM), out_block),
                   pl.BlockSpec((tps * tq, 2 * RW_DIM), out_block),
                   sspec],
        out_shape=[jax.ShapeDtypeStruct((m, RW_DIM), ACT),
                   jax.ShapeDtypeStruct((m, 2 * RW_DIM), ACT),
                   jax.ShapeDtypeStruct((nb, N_PAIRS, LANES, LANES), F32)],
        scratch_shapes=([pltpu.VMEM((8, tq, RW_DIM), F32)] * tps + [pltpu.VMEM((tq, 2 * RW_DIM), ACT)] * tps
                        + [pltpu.VMEM((N_PAIRS, LANES, LANES), F32)]),
        compiler_params=_cparams("parallel", "arbitrary"),
        name="rwkv7_fwd" if drn == 0 else "rwkv7_bwd",
    )(*args)


def _rwkv_both(p, vf_src, hv, states, prm, nb, rows):
    outs = [_rwkv(p, vf_src, hv, states[d], prm, nb, rows, d) for d in range(2)]
    return [o[0] for o in outs], [o[1] for o in outs], [o[2] for o in outs]


def _merge_kernel(x_ref, y0_ref, y1_ref, a0_ref, a1_ref, ya_ref, g0_ref, g1_ref, g2_ref, g3_ref, ln_ref, e_ref,
                  wbr_ref, wba_ref, wo_ref, mod_ref, o_ref):
    e = e_ref[...]
    y = y0_ref[...].astype(F32) + y1_ref[...].astype(F32)
    mu = _segsum(y, e) * (1.0 / HEAD_DIM)
    yc = y - mu
    var = _segsum(yc * yc, e) * (1.0 / HEAD_DIM)
    yn = yc * lax.rsqrt(var + GN_EPS) * ln_ref[0:1, :] + ln_ref[1:2, :]
    cv = a0_ref[:, 0:RW_DIM].astype(F32) + a1_ref[:, 0:RW_DIM].astype(F32)
    yrw = (yn + cv) * a0_ref[:, RW_DIM:2 * RW_DIM].astype(F32)
    a = _bdot(yrw, wbr_ref[...])
    b = jnp.dot(ya_ref[...], wba_ref[...], preferred_element_type=F32)
    h = a.shape[1] // 2
    mrg = jnp.concatenate(
        [jax.nn.sigmoid(g0_ref[...].astype(F32)) * a[:, :h] + jax.nn.sigmoid(g2_ref[...].astype(F32)) * b[:, :h],
         jax.nn.sigmoid(g1_ref[...].astype(F32)) * a[:, h:] + jax.nn.sigmoid(g3_ref[...].astype(F32)) * b[:, h:]],
        axis=1)
    o_ref[...] = x_ref[...] + mod_ref[0, 2:3, :] * _bdot(mrg, wo_ref[...])


def _merge(x2, ys, auxs, ya, p, ln, e128, wbr, wba, wo, mod, rows):
    m, d = x2.shape
    tm = _row_tile(rows, 512)
    per = rows // tm
    many = mod.shape[0] > 1
    gb = GATE_OFF // 512
    const2 = lambda i: (0, 0)
    in_specs = [pl.BlockSpec((tm, d), lambda i: (i, 0)),
                pl.BlockSpec((tm, RW_DIM), lambda i: (i, 0)),
                pl.BlockSpec((tm, RW_DIM), lambda i: (i, 0)),
                pl.BlockSpec((tm, 2 * RW_DIM), lambda i: (i, 0)),
                pl.BlockSpec((tm, 2 * RW_DIM), lambda i: (i, 0)),
                pl.BlockSpec((tm, ATT_DIM), lambda i: (i, 0))]
    in_specs += [pl.BlockSpec((tm, 512), (lambda i, c=c: (i, gb + c))) for c in range(4)]
    in_specs += [pl.BlockSpec((2, RW_DIM), const2),
                 pl.BlockSpec((LANES, LANES), const2),
                 pl.BlockSpec(wbr.shape, const2),
                 pl.BlockSpec(wba.shape, const2),
                 pl.BlockSpec(wo.shape, const2),
                 pl.BlockSpec((1, 6, d), lambda i: ((i // per) if many else 0, 0, 0))]
    return pl.pallas_call(
        _merge_kernel,
        grid=(m // tm,),
        in_specs=in_specs,
        out_specs=pl.BlockSpec((tm, d), lambda i: (i, 0)),
        out_shape=jax.ShapeDtypeStruct((m, d), F32),
        compiler_params=_cparams("parallel"),
        name="branch_merge",
    )(x2, ys[0], ys[1], auxs[0], auxs[1], ya, p, p, p, p, ln, e128, wbr, wba, wo, mod)


def _ffn_kernel(x_ref, g_ref, mod_ref, wg_ref, wu_ref, wd_ref, o_ref, h_sc, acc_sc, *, nf):
    f = pl.program_id(1)

    @pl.when(f == 0)
    def _():
        h = _norm_mod(x_ref[...], g_ref[...], mod_ref[0, 3:4, :], mod_ref[0, 4:5, :])
        h_sc[...] = h.astype(BF16)
        acc_sc[...] = jnp.zeros_like(acc_sc)

    h = h_sc[...]
    a = jnp.dot(h, wg_ref[...], preferred_element_type=F32)
    u = jnp.dot(h, wu_ref[...], preferred_element_type=F32)
    act = (a * jax.nn.sigmoid(a)) * u
    acc_sc[...] += jnp.dot(act.astype(BF16), wd_ref[...], preferred_element_type=F32)

    @pl.when(f == nf - 1)
    def _():
        o_ref[...] = x_ref[...] + mod_ref[0, 5:6, :] * acc_sc[...]


def _ffn(x2, gain, mod, wg, wu, wd, rows):
    m, d = x2.shape
    dff = wg.shape[1]
    tm = _row_tile(rows, 512)
    per = rows // tm
    many = mod.shape[0] > 1
    tf = dff // 2 if (dff // 2) % LANES == 0 else dff
    nf = dff // tf
    return pl.pallas_call(
        functools.partial(_ffn_kernel, nf=nf),
        grid=(m // tm, nf),
        in_specs=[pl.BlockSpec((tm, d), lambda i, f: (i, 0)),
                  pl.BlockSpec((1, d), lambda i, f: (0, 0)),
                  pl.BlockSpec((1, 6, d), lambda i, f: ((i // per) if many else 0, 0, 0)),
                  pl.BlockSpec((d, tf), lambda i, f: (0, f)),
                  pl.BlockSpec((d, tf), lambda i, f: (0, f)),
                  pl.BlockSpec((tf, d), lambda i, f: (f, 0))],
        out_specs=pl.BlockSpec((tm, d), lambda i, f: (i, 0)),
        out_shape=jax.ShapeDtypeStruct((m, d), F32),
        scratch_shapes=[pltpu.VMEM((tm, d), BF16), pltpu.VMEM((tm, d), F32)],
        compiler_params=_cparams("parallel", "arbitrary"),
        name="ffn_swiglu",
    )(x2, gain.reshape(1, d), mod, wg, wu, wd)


def _dma_pieces(count, sizes, make):
    for k, size in enumerate(sizes):
        @pl.when((count & size) != 0)
        def _(k=k, size=size):
            make(k, pl.multiple_of(count & ~(2 * size - 1), PACK), size)


def _moe_gather_kernel(pos_ref, cnt_ref, h_ref, rankt_ref, zero_ref, out_ref, hs_sc, sem, *, ne, nsteps, sizes):
    del zero_ref
    i = pl.program_id(0)
    ex = pl.program_id(1)
    step = i * ne + ex
    slot = step % 2
    tt = h_ref.shape[0]

    def copies(st, sl, act):
        base = pos_ref[st]

        def make(k, off, size):
            act(pltpu.make_async_copy(hs_sc.at[sl, pl.ds(off, size)],
                                      out_ref.at[pl.ds(pl.multiple_of(base + off, PACK), size)], sem.at[sl, k]))

        _dma_pieces(cnt_ref[st], sizes, make)

    @pl.when(step >= 2)
    def _():
        copies(step - 2, slot, lambda cp: cp.wait())

    rrow = rankt_ref[0, pl.ds(ex, 1), :]
    rowid = lax.broadcasted_iota(jnp.int32, (MOE_BLK, tt), 0).astype(F32)

    def gather(b, carry):
        sel = ((rrow - (b * MOE_BLK).astype(F32)) == rowid).astype(BF16)
        r0 = pl.multiple_of(b * MOE_BLK, MOE_BLK)
        hs_sc[slot, pl.ds(r0, MOE_BLK), :] = jnp.dot(sel, h_ref[...], preferred_element_type=F32).astype(BF16)
        return carry

    lax.fori_loop(0, (cnt_ref[step] + MOE_BLK - 1) // MOE_BLK, gather, 0)
    copies(step, slot, lambda cp: cp.start())

    @pl.when(step == nsteps - 1)
    def _():
        if nsteps >= 2:
            copies(step - 1, 1 - slot, lambda cp: cp.wait())
        copies(step, slot, lambda cp: cp.wait())


def _moe_ffn_kernel(te_ref, tm_ref, tv_ref, hs_ref, wg_ref, wu_ref, wd_ref, os_ref, acc_sc, *, nf):
    del te_ref, tm_ref
    t = pl.program_id(0)
    f = pl.program_id(1)
    valid = tv_ref[t] != 0

    @pl.when(valid)
    def _():
        hb = hs_ref[...]
        a = jnp.dot(hb, wg_ref[0], preferred_element_type=F32)
        u = jnp.dot(hb, wu_ref[0], preferred_element_type=F32)
        act = ((a * jax.nn.sigmoid(a)) * u).astype(BF16)
        part = jnp.dot(act, wd_ref[0], preferred_element_type=F32)

        @pl.when(f == 0)
        def _():
            acc_sc[...] = part

        @pl.when(f > 0)
        def _():
            acc_sc[...] += part

    @pl.when(f == nf - 1)
    def _():
        os_ref[...] = jnp.where(valid, acc_sc[...], 0.0).astype(os_ref.dtype)


def _moe_combine_kernel(pos_ref, cnt_ref, x_ref, we_ref, rank_ref, mod_ref, os_ref, o_ref, acc_sc, ob_sc, sem,
                        *, ne, nsteps, sizes):
    i = pl.program_id(0)
    ex = pl.program_id(1)
    step = i * ne + ex
    slot = step % 2
    tt = x_ref.shape[0]
    sblk = min(2 * MOE_BLK, tt)
    lane = lax.broadcasted_iota(jnp.int32, (tt, LANES), 1)

    def copies(st, sl, act):
        base = pos_ref[st]

        def make(k, off, size):
            act(pltpu.make_async_copy(os_ref.at[pl.ds(pl.multiple_of(base + off, PACK), size)],
                                      ob_sc.at[sl, pl.ds(off, size)], sem.at[sl, k]))

        _dma_pieces(cnt_ref[st], sizes, make)

    @pl.when(step == 0)
    def _():
        ob_sc[...] = jnp.zeros_like(ob_sc)
        copies(0, 0, lambda cp: cp.start())

    @pl.when(step + 1 < nsteps)
    def _():
        copies(step + 1, 1 - slot, lambda cp: cp.start())

    copies(step, slot, lambda cp: cp.wait())

    @pl.when(ex == 0)
    def _():
        acc_sc[...] = jnp.zeros_like(acc_sc)

    rcol = jnp.sum(jnp.where(lane == ex, rank_ref[...], 0.0), axis=1, keepdims=True)
    wcol = jnp.sum(jnp.where(lane == ex, we_ref[...], 0.0), axis=1, keepdims=True)
    colid = lax.broadcasted_iota(jnp.int32, (tt, sblk), 1).astype(F32)

    def scatter(b, carry):
        sel = ((rcol - (b * sblk).astype(F32)) == colid).astype(BF16)
        r0 = pl.multiple_of(b * sblk, sblk)
        acc_sc[...] += wcol * jnp.dot(sel, ob_sc[slot, pl.ds(r0, sblk), :], preferred_element_type=F32)
        return carry

    lax.fori_loop(0, (cnt_ref[step] + sblk - 1) // sblk, scatter, 0)

    @pl.when(ex == ne - 1)
    def _():
        o_ref[...] = x_ref[...] + mod_ref[0, 5:6, :] * acc_sc[...]


def _route_kernel(x_ref, g_ref, mod_ref, rt_ref, h_ref, we_ref, rank_ref, rankt_ref, cnt_ref, *, ne):
    tt = x_ref.shape[0]
    h = _norm_mod(x_ref[...], g_ref[...], mod_ref[0, 3:4, :], mod_ref[0, 4:5, :])
    h_ref[...] = h.astype(BF16)
    lane = lax.broadcasted_iota(jnp.int32, (tt, LANES), 1)
    logits = jnp.dot(h, rt_ref[...], precision=HIGHEST, preferred_element_type=F32)
    lg = jnp.where(lane < ne, logits, NEG_INF)
    t1 = jnp.max(lg, axis=1, keepdims=True)
    i1 = jnp.min(jnp.where(lg == t1, lane, LANES), axis=1, keepdims=True)
    lg2 = jnp.where(lane == i1, NEG_INF, lg)
    t2 = jnp.max(lg2, axis=1, keepdims=True)
    i2 = jnp.min(jnp.where(lg2 == t2, lane, LANES), axis=1, keepdims=True)
    e2 = jnp.exp(t2 - t1)
    den = 1.0 + e2
    we_ref[...] = jnp.where(lane == i1, 1.0 / den, 0.0) + jnp.where(lane == i2, e2 / den, 0.0)
    sel = (lane == i1) | (lane == i2)
    self32 = sel.astype(F32)
    before = (lax.broadcasted_iota(jnp.int32, (tt, tt), 1) < lax.broadcasted_iota(jnp.int32, (tt, tt), 0))
    rank = jnp.dot(before.astype(BF16), sel.astype(BF16), preferred_element_type=F32)
    rank = jnp.where(sel, rank, -1.0)
    rank_ref[...] = rank
    rankt_ref[0] = rank.T[0:SUBLANES, :]
    cnt_ref[0] = jnp.broadcast_to(jnp.sum(self32, axis=0, keepdims=True), (SUBLANES, LANES))


def _moe(x2, gain, mod, router_pad, wg, wu, wd, rows):
    m, d = x2.shape
    ne, _, dff = wg.shape
    assert ne <= SUBLANES
    tt = _row_tile(rows, MOE_TILE)
    nti = m // tt
    per = rows // tt
    many = mod.shape[0] > 1
    h, we, rank, rankt, cnt = pl.pallas_call(
        functools.partial(_route_kernel, ne=ne),
        grid=(nti,),
        in_specs=[pl.BlockSpec((tt, d), lambda i: (i, 0)),
                  pl.BlockSpec((1, d), lambda i: (0, 0)),
                  pl.BlockSpec((1, 6, d), lambda i: ((i // per) if many else 0, 0, 0)),
                  pl.BlockSpec((d, LANES), lambda i: (0, 0))],
        out_specs=[pl.BlockSpec((tt, d), lambda i: (i, 0)),
                   pl.BlockSpec((tt, LANES), lambda i: (i, 0)),
                   pl.BlockSpec((tt, LANES), lambda i: (i, 0)),
                   pl.BlockSpec((1, SUBLANES, tt), lambda i: (i, 0, 0)),
                   pl.BlockSpec((1, SUBLANES, LANES), lambda i: (i, 0, 0))],
        out_shape=[jax.ShapeDtypeStruct((m, d), BF16),
                   jax.ShapeDtypeStruct((m, LANES), F32),
                   jax.ShapeDtypeStruct((m, LANES), F32),
                   jax.ShapeDtypeStruct((nti, SUBLANES, tt), F32),
                   jax.ShapeDtypeStruct((nti, SUBLANES, LANES), F32)],
        compiler_params=_cparams("parallel"),
        name="moe_route",
    )(x2, gain.reshape(1, d), mod, router_pad)
    c16 = (cnt[:, 0, :ne].astype(jnp.int32) + (PACK - 1)) // PACK * PACK
    region = (jnp.sum(c16, axis=0) + (MOE_FT - 1)) // MOE_FT * MOE_FT
    e_off = jnp.cumsum(region) - region
    pos = (e_off[None, :] + jnp.cumsum(c16, axis=0) - c16).reshape(-1).astype(jnp.int32)
    c16 = c16.reshape(-1)
    rmax = -(-(2 * m + nti * ne * PACK + ne * MOE_FT) // MOE_FT) * MOE_FT
    nft = rmax // MOE_FT
    tiles_e = jnp.cumsum(region // MOE_FT)
    tid = jnp.arange(nft, dtype=jnp.int32)
    tile_valid = (tid < tiles_e[-1]).astype(jnp.int32)
    tile_map = jnp.minimum(tid, tiles_e[-1] - 1).astype(jnp.int32)
    tile_exp = jnp.minimum(jnp.sum(tile_map[:, None] >= tiles_e[None, :], axis=1), ne - 1).astype(jnp.int32)

    sizes = tuple(tt >> k for k in range(tt.bit_length()) if (tt >> k) >= PACK)
    nsteps = nti * ne
    hs = pl.pallas_call(
        functools.partial(_moe_gather_kernel, ne=ne, nsteps=nsteps, sizes=sizes),
        grid_spec=pltpu.PrefetchScalarGridSpec(
            num_scalar_prefetch=2,
            grid=(nti, ne),
            in_specs=[pl.BlockSpec((tt, d), lambda i, e, p_, c_: (i, 0)),
                      pl.BlockSpec((1, SUBLANES, tt), lambda i, e, p_, c_: (i, 0, 0)),
                      pl.BlockSpec(memory_space=pl.ANY)],
            out_specs=pl.BlockSpec(memory_space=pl.ANY),
            scratch_shapes=[pltpu.VMEM((2, tt, d), BF16), pltpu.SemaphoreType.DMA((2, len(sizes)))]),
        out_shape=jax.ShapeDtypeStruct((rmax, d), BF16),
        input_output_aliases={4: 0},
        compiler_params=_cparams("arbitrary", "arbitrary"),
        name="moe_gather",
    )(pos, c16, h, rankt, jnp.zeros((rmax, d), BF16))

    tf = dff // 2 if (dff // 2) % LANES == 0 else dff
    nf = dff // tf
    os_sorted = pl.pallas_call(
        functools.partial(_moe_ffn_kernel, nf=nf),
        grid_spec=pltpu.PrefetchScalarGridSpec(
            num_scalar_prefetch=3,
            grid=(nft, nf),
            in_specs=[pl.BlockSpec((MOE_FT, d), lambda t, f, te, tm, tv: (tm[t], 0)),
                      pl.BlockSpec((1, d, tf), lambda t, f, te, tm, tv: (te[t], 0, f)),
                      pl.BlockSpec((1, d, tf), lambda t, f, te, tm, tv: (te[t], 0, f)),
                      pl.BlockSpec((1, tf, d), lambda t, f, te, tm, tv: (te[t], f, 0))],
            out_specs=pl.BlockSpec((MOE_FT, d), lambda t, f, te, tm, tv: (t, 0)),
            scratch_shapes=[pltpu.VMEM((MOE_FT, d), F32)]),
        out_shape=jax.ShapeDtypeStruct((rmax, d), BF16),
        compiler_params=_cparams("parallel", "arbitrary"),
        name="moe_experts",
    )(tile_exp, tile_map, tile_valid, hs, wg, wu, wd)

    return pl.pallas_call(
        functools.partial(_moe_combine_kernel, ne=ne, nsteps=nsteps, sizes=sizes),
        grid_spec=pltpu.PrefetchScalarGridSpec(
            num_scalar_prefetch=2,
            grid=(nti, ne),
            in_specs=[pl.BlockSpec((tt, d), lambda i, e, p_, c_: (i, 0)),
                      pl.BlockSpec((tt, LANES), lambda i, e, p_, c_: (i, 0)),
                      pl.BlockSpec((tt, LANES), lambda i, e, p_, c_: (i, 0)),
                      pl.BlockSpec((1, 6, d), lambda i, e, p_, c_: ((i // per) if many else 0, 0, 0)),
                      pl.BlockSpec(memory_space=pl.ANY)],
            out_specs=pl.BlockSpec((tt, d), lambda i, e, p_, c_: (i, 0)),
            scratch_shapes=[pltpu.VMEM((tt, d), F32), pltpu.VMEM((2, tt, d), BF16),
                            pltpu.SemaphoreType.DMA((2, len(sizes)))]),
        out_shape=jax.ShapeDtypeStruct((m, d), F32),
        compiler_params=_cparams("arbitrary", "arbitrary"),
        name="moe_combine",
    )(pos, c16, x2, we, rank, mod, os_sorted)


def _rope_table(t):
    rows = t // GRID_W
    row = jnp.repeat(jnp.arange(rows, dtype=F32), GRID_W)
    col = jnp.tile(jnp.arange(GRID_W, dtype=F32), rows)
    half = HEAD_DIM // 2
    inv_freq = ROPE_THETA ** (-jnp.arange(0, half, 2, dtype=F32) / half)
    ar = row[:, None] * inv_freq
    ac = col[:, None] * inv_freq
    ang = jnp.concatenate([ar, ar, ac, ac] * 2, axis=-1)
    return jnp.stack([jnp.cos(ang), jnp.sin(ang)])


def _pad_rows(a, rows):
    return jnp.zeros((rows,) + a.shape[1:], a.dtype).at[:a.shape[0]].set(a)


def _hi_lo(w):
    hi = w.astype(BF16)
    return hi, (w - hi.astype(F32)).astype(BF16)


def kernel(x, c, ctx, c_ctx, norm1, norm2, w_mod, b_mod, w_in, tshift, w0, w_decay_up, a0, w_iclr_up, v0, w_vres_down, w_vres_up, w_gate_up, k_k, k_a, r_k, lnx_g, lnx_b, q_norm, k_norm, sink, w_br_rwkv, w_br_attn, w_out, ffn_gate, ffn_up, ffn_down, router, moe_gate, moe_up, moe_down):
    nb, t, d = x.shape
    cn = ctx.shape[1]
    depth = w_in.shape[0]
    mods = _modulation(c, c_ctx, w_mod, b_mod)
    cs = _rope_table(t)
    e128 = jnp.kron(jnp.eye(2, dtype=F32), jnp.ones((HEAD_DIM, HEAD_DIM), F32)).astype(BF16)
    xl = x.reshape(nb * t, d)
    xc = ctx.reshape(nb * cn, d)
    vf_l = vf_c = None
    for layer in range(depth):
        last = layer == depth - 1
        mod_l = mods[layer, :nb].reshape(nb, 6, d)
        mod_c = mods[layer, nb:nb + 1].reshape(1, 6, d)
        w_in_b = w_in[layer].astype(BF16)
        vd_b = None
        prm = {"tshift": tshift[layer], "e128": e128, "wg": w_gate_up[layer].astype(BF16)}
        prm["pd"] = jnp.stack(
            [_pad_rows(jnp.stack([w0[layer, dd], a0[layer, dd]]), SUBLANES) for dd in range(2)])
        shared = [k_k[layer], k_a[layer], r_k[layer].reshape(RW_DIM)]
        if layer > 0:
            shared.append(v0[layer - 1])
            vd_b = _pad_rows(w_vres_down[layer - 1].T, LANES).T.astype(BF16)
            prm["vu_hi"], prm["vu_lo"] = _hi_lo(_pad_rows(w_vres_up[layer - 1], LANES))
        prm["ps"] = _pad_rows(jnp.stack(shared), SUBLANES)
        wla = jnp.zeros((2, LANES, 2 * RW_DIM), F32)
        wla = wla.at[:, :HEAD_DIM, :RW_DIM].set(w_decay_up[layer]).at[:, HEAD_DIM:, RW_DIM:].set(w_iclr_up[layer])
        prm["wla"] = wla.astype(BF16)

        p_l, hv_l = _inproj(xl, norm1[layer], mod_l, w_in_b, vd_b, t)
        p_c, hv_c = _inproj(xc, norm1[layer], mod_c, w_in_b, vd_b, cn)

        s0 = jnp.zeros((nb, N_PAIRS, LANES, LANES), F32)
        y_c, aux_c, s_ctx = _rwkv_both(p_c, vf_c, hv_c, [s0, s0], prm, nb, cn)
        y_l, aux_l, _ = _rwkv_both(p_l, vf_l, hv_l, s_ctx, prm, nb, t)
        if layer == 0:
            vf_l, vf_c = aux_l[1], aux_c[1]

        qg = jnp.tile(q_norm[layer], 2).reshape(1, LANES)
        kg = jnp.tile(k_norm[layer], 2).reshape(1, LANES)
        qn_l, kn_l, vb_l = _qk_prep(p_l, cs, qg, kg, e128, t, True)
        qn_c, kn_c, vb_c = _qk_prep(p_c, None, qg, kg, e128, cn, False)
        sk = sink[layer].reshape(1, -1)
        ya_l = _attention(qn_l, kn_l, vb_l, kn_c, vb_c, sk, nb, t, cn, True)

        ln = jnp.stack([lnx_g[layer], lnx_b[layer]])
        wbr = w_br_rwkv[layer].astype(BF16)
        wba = w_br_attn[layer].astype(BF16)
        wo = w_out[layer].astype(BF16)
        xl = _merge(xl, y_l, aux_l, ya_l, p_l, ln, e128, wbr, wba, wo, mod_l, t)
        if not last:
            ya_c = _attention(qn_c, None, None, kn_c, vb_c, sk, nb, cn, cn, False)
            xc = _merge(xc, y_c, aux_c, ya_c, p_c, ln, e128, wbr, wba, wo, mod_c, cn)

        i = layer // 2
        if layer % 2 == 0:
            fw = (ffn_gate[i].astype(BF16), ffn_up[i].astype(BF16), ffn_down[i].astype(BF16))
            xl = _ffn(xl, norm2[layer], mod_l, *fw, t)
            if not last:
                xc = _ffn(xc, norm2[layer], mod_c, *fw, cn)
        else:
            rt = _pad_rows(router[i].T, LANES).T
            mw = (moe_gate[i].astype(BF16), moe_up[i].astype(BF16), moe_down[i].astype(BF16))
            xl = _moe(xl, norm2[layer], mod_l, rt, *mw, t)
            if not last:
                xc = _moe(xc, norm2[layer], mod_c, rt, *mw, cn)
    return xl.reshape(nb, t, d)
```

```python
import functools

import jax
import jax.numpy as jnp
from jax import lax
from jax.experimental import pallas as pl
from jax.experimental.pallas import tpu as pltpu

F32 = jnp.float32
BF16 = jnp.bfloat16
HIGHEST = lax.Precision.HIGHEST

LANES = 128
SUBLANES = 8
PACK = 16
ACT = jnp.bfloat16
HEAD_DIM = 64
RW_DIM = 512
ATT_DIM = 512
KV_DIM = 128
RW_IN = 1792
ATT_OFF = RW_IN
GATE_OFF = RW_IN + ATT_DIM + 2 * KV_DIM
N_EXPERTS = 8
CHUNK = 64
N_PAIRS = RW_DIM // LANES
MOE_TILE = 1024
MOE_BLK = 128
MOE_FT = 512
SCAN_GROUP = 4
ATT_SCALE = HEAD_DIM ** -0.5
ROPE_THETA = 10000.0
GRID_W = 64
RMS_EPS = 1e-6
GN_EPS = 64e-5
NEG_INF = -1e30
DECAY_SCALE = 0.6065306597126334
VMEM_LIMIT = 48 * 1024 * 1024


def _cparams(*sem):
    return pltpu.CompilerParams(dimension_semantics=sem, vmem_limit_bytes=VMEM_LIMIT)


def _row_tile(rows, cap):
    t = cap
    while rows % t:
        t //= 2
    return t


def _bdot(a, b):
    return jnp.dot(a.astype(BF16), b.astype(BF16), preferred_element_type=F32)


def _bdot_nt(a, b):
    return lax.dot_general(a.astype(BF16), b.astype(BF16), (((1,), (1,)), ((), ())),
                           preferred_element_type=F32)


def _bdot_tn(a, b):
    return lax.dot_general(a.astype(BF16), b.astype(BF16), (((0,), (0,)), ((), ())),
                           preferred_element_type=F32)


def _split2(x):
    hi = x.astype(BF16)
    lo = (x - hi.astype(F32)).astype(BF16)
    return hi, lo


def _dot3(a, w_hi, w_lo):
    a_hi, a_lo = _split2(a)
    return (jnp.dot(a_hi, w_hi, preferred_element_type=F32)
            + (jnp.dot(a_lo, w_hi, preferred_element_type=F32)
               + jnp.dot(a_hi, w_lo, preferred_element_type=F32)))


def _segsum(x, e, exact=True):
    outs = []
    for s in range(x.shape[1] // LANES):
        hi, lo = _split2(x[:, s * LANES:(s + 1) * LANES])
        acc = jnp.dot(hi, e, preferred_element_type=F32)
        outs.append(acc + jnp.dot(lo, e, preferred_element_type=F32) if exact else acc)
    return outs[0] if len(outs) == 1 else jnp.concatenate(outs, axis=1)


def _norm_mod(x, gain, shift, scale):
    y = x * lax.rsqrt(jnp.mean(x * x, axis=-1, keepdims=True) + RMS_EPS) * gain
    return y * (1.0 + scale) + shift


def _mod_kernel(c_ref, w_ref, b_ref, o_ref):
    c = c_ref[...]
    s = c * jax.nn.sigmoid(c)
    o_ref[0] = jnp.dot(s, w_ref[0], precision=HIGHEST, preferred_element_type=F32) + b_ref[0]


def _modulation(c, c_ctx, w_mod, b_mod):
    depth, d, n6 = w_mod.shape
    nb = c.shape[0]
    rm = -(-(nb + 1) // SUBLANES) * SUBLANES
    cc = jnp.zeros((rm, d), F32).at[:nb].set(c).at[nb].set(c_ctx)
    tn = _row_tile(n6, 1536)
    return pl.pallas_call(
        _mod_kernel,
        grid=(depth, n6 // tn),
        in_specs=[pl.BlockSpec((rm, d), lambda l, n: (0, 0)),
                  pl.BlockSpec((1, d, tn), lambda l, n: (l, 0, n)),
                  pl.BlockSpec((1, 1, tn), lambda l, n: (l, 0, n))],
        out_specs=pl.BlockSpec((1, rm, tn), lambda l, n: (l, 0, n)),
        out_shape=jax.ShapeDtypeStruct((depth, rm, n6), F32),
        compiler_params=_cparams("parallel", "parallel"),
        name="adaln_mod",
    )(cc, w_mod, b_mod.reshape(depth, 1, n6))


def _inproj_kernel(*refs, vres):
    if vres:
        x_ref, g_ref, mod_ref, w_ref, vd_ref, o_ref, hv_ref, h_sc = refs
    else:
        x_ref, g_ref, mod_ref, w_ref, o_ref, h_sc = refs

    @pl.when(pl.program_id(1) == 0)
    def _():
        h = _norm_mod(x_ref[...], g_ref[...], mod_ref[0, 0:1, :], mod_ref[0, 1:2, :])
        hb = h.astype(BF16)
        h_sc[...] = hb
        if vres:
            hv_ref[...] = jnp.dot(hb, vd_ref[...], preferred_element_type=F32)

    o_ref[...] = jnp.dot(h_sc[...], w_ref[...], preferred_element_type=F32).astype(o_ref.dtype)


def _inproj(x2, gain, mod, w_b, vd_b, rows):
    m, d = x2.shape
    n = w_b.shape[1]
    tm = _row_tile(rows, 1024)
    tn = _row_tile(n, 1536)
    per = rows // tm
    many = mod.shape[0] > 1
    vres = vd_b is not None
    in_specs = [pl.BlockSpec((tm, d), lambda i, j: (i, 0)),
                pl.BlockSpec((1, d), lambda i, j: (0, 0)),
                pl.BlockSpec((1, 6, d), lambda i, j: ((i // per) if many else 0, 0, 0)),
                pl.BlockSpec((d, tn), lambda i, j: (0, j))]
    args = [x2, gain.reshape(1, d), mod, w_b]
    out_specs = [pl.BlockSpec((tm, tn), lambda i, j: (i, j))]
    out_shape = [jax.ShapeDtypeStruct((m, n), ACT)]
    if vres:
        in_specs.append(pl.BlockSpec((d, LANES), lambda i, j: (0, 0)))
        args.append(vd_b)
        out_specs.append(pl.BlockSpec((tm, LANES), lambda i, j: (i, 0)))
        out_shape.append(jax.ShapeDtypeStruct((m, LANES), F32))
    outs = pl.pallas_call(
        functools.partial(_inproj_kernel, vres=vres),
        grid=(m // tm, n // tn),
        in_specs=in_specs, out_specs=out_specs, out_shape=out_shape,
        scratch_shapes=[pltpu.VMEM((tm, d), BF16)],
        compiler_params=_cparams("parallel", "arbitrary"),
        name="in_proj",
    )(*args)
    return (outs[0], outs[1]) if vres else (outs[0], None)


def _qkprep_kernel(*refs, rope):
    if rope:
        q0_ref, q1_ref, k_ref, v_ref, cs_ref, qg_ref, kg_ref, e_ref, qn_ref, kn_ref, vb_ref = refs
    else:
        q0_ref, q1_ref, k_ref, v_ref, qg_ref, kg_ref, e_ref, qn_ref, kn_ref, vb_ref = refs
    e = e_ref[...]
    lane = lax.broadcasted_iota(jnp.int32, (1, LANES), 1)
    first = (lane & 31) < 16

    def norm_rope(u, gain):
        y = u * lax.rsqrt(_segsum(u * u, e) * (1.0 / HEAD_DIM) + RMS_EPS) * gain
        if rope:
            rot = jnp.where(first, -pltpu.roll(y, LANES - 16, 1), pltpu.roll(y, 16, 1))
            y = y * cs_ref[0] + rot * cs_ref[1]
        return y

    qg = qg_ref[...]
    for s, ref in enumerate((q0_ref, q1_ref)):
        for t in range(2):
            u = ref[:, t * LANES:(t + 1) * LANES].astype(F32)
            c0 = (2 * s + t) * LANES
            qn_ref[:, c0:c0 + LANES] = (norm_rope(u, qg) * ATT_SCALE).astype(BF16)
    low = lane < HEAD_DIM
    for src_val, dst in ((norm_rope(k_ref[...].astype(F32), kg_ref[...]), kn_ref), (v_ref[...].astype(F32), vb_ref)):
        swapped = pltpu.roll(src_val, HEAD_DIM, 1)
        dst[:, 0:LANES] = jnp.where(low, src_val, swapped).astype(BF16)
        dst[:, LANES:2 * LANES] = jnp.where(low, swapped, src_val).astype(BF16)


def _qk_prep(p, cs, qg, kg, e128, rows, rope):
    m = p.shape[0]
    tm = _row_tile(rows, 512)
    per = rows // tm
    qb = ATT_OFF // 256
    kb = (ATT_OFF + ATT_DIM) // LANES
    in_specs = [pl.BlockSpec((tm, 256), lambda i: (i, qb)),
                pl.BlockSpec((tm, 256), lambda i: (i, qb + 1)),
                pl.BlockSpec((tm, LANES), lambda i: (i, kb)),
                pl.BlockSpec((tm, LANES), lambda i: (i, kb + 1))]
    args = [p, p, p, p]
    if rope:
        in_specs.append(pl.BlockSpec((2, tm, LANES), lambda i: (0, i % per, 0)))
        args.append(cs)
    in_specs += [pl.BlockSpec((1, LANES), lambda i: (0, 0)),
                 pl.BlockSpec((1, LANES), lambda i: (0, 0)),
                 pl.BlockSpec((LANES, LANES), lambda i: (0, 0))]
    args += [qg, kg, e128]
    return pl.pallas_call(
        functools.partial(_qkprep_kernel, rope=rope),
        grid=(m // tm,),
        in_specs=in_specs,
        out_specs=[pl.BlockSpec((tm, ATT_DIM), lambda i: (i, 0)),
                   pl.BlockSpec((tm, 2 * KV_DIM), lambda i: (i, 0)),
                   pl.BlockSpec((tm, 2 * KV_DIM), lambda i: (i, 0))],
        out_shape=[jax.ShapeDtypeStruct((m, ATT_DIM), BF16),
                   jax.ShapeDtypeStruct((m, 2 * KV_DIM), BF16),
                   jax.ShapeDtypeStruct((m, 2 * KV_DIM), BF16)],
        compiler_params=_cparams("parallel"),
        name="qk_prep",
    )(*args)


def _attn_kernel(*refs, nq, local):
    if local:
        sink_ref, q_ref, kp_ref, kc_ref, kn_ref, vp_ref, vc_ref, vn_ref, kx_ref, vx_ref, wm_ref, o_ref = refs
        k = jnp.concatenate([kp_ref[...], kc_ref[...], kn_ref[...], kx_ref[...]], axis=0)
        v = jnp.concatenate([vp_ref[...], vc_ref[...], vn_ref[...], vx_ref[...]], axis=0)
    else:
        sink_ref, q_ref, kx_ref, vx_ref, o_ref = refs
        k = kx_ref[...]
        v = vx_ref[...]
    i = pl.program_id(1)
    bq = q_ref.shape[0]
    low = lax.broadcasted_iota(jnp.int32, (1, LANES), 1) < HEAD_DIM
    row = lax.broadcasted_iota(jnp.int32, (4 * bq, 1), 0)
    if local:
        kblk = lax.broadcasted_iota(jnp.int32, (1, 3 * bq), 1) // bq
        off_end = ((kblk == 0) & (i == 0)) | ((kblk == 2) & (i == nq - 1))
        bias = wm_ref[...] + jnp.where(off_end, NEG_INF, 0.0)
    for g in range(2):
        kb = k[:, g * LANES:(g + 1) * LANES]
        vb = v[:, g * LANES:(g + 1) * LANES]
        qa = q_ref[:, (2 * g) * LANES:(2 * g + 1) * LANES].astype(F32)
        qb = q_ref[:, (2 * g + 1) * LANES:(2 * g + 2) * LANES].astype(F32)
        qs = jnp.concatenate([jnp.where(low, qa, 0.0), jnp.where(low, 0.0, qa),
                              jnp.where(low, qb, 0.0), jnp.where(low, 0.0, qb)], axis=0).astype(BF16)
        s = lax.dot_general(qs, kb, (((1,), (1,)), ((), ())), preferred_element_type=F32)
        if local:
            s = jnp.concatenate([s[:, :3 * bq] + bias, s[:, 3 * bq:]], axis=1)
        sk = jnp.where(row < bq, sink_ref[0, 4 * g],
                       jnp.where(row < 2 * bq, sink_ref[0, 4 * g + 1],
                                 jnp.where(row < 3 * bq, sink_ref[0, 4 * g + 2], sink_ref[0, 4 * g + 3])))
        mx = jnp.maximum(jnp.max(s, axis=1, keepdims=True), sk)
        ex = jnp.exp(s - mx)
        den = jnp.sum(ex, axis=1, keepdims=True) + jnp.exp(sk - mx)
        o = jnp.dot(ex.astype(BF16), vb, preferred_element_type=F32) / den
        o_ref[:, (2 * g) * LANES:(2 * g + 1) * LANES] = jnp.where(low, o[0:bq], o[bq:2 * bq]).astype(o_ref.dtype)
        o_ref[:, (2 * g + 1) * LANES:(2 * g + 2) * LANES] = jnp.where(
            low, o[2 * bq:3 * bq], o[3 * bq:4 * bq]).astype(o_ref.dtype)


def _attention(qn, kn, vb, kctx, vctx, sink, nb, rows, cn, local):
    m = qn.shape[0]
    bq = 128
    nq = rows // bq
    smem = pl.BlockSpec(memory_space=pltpu.SMEM)
    qspec = pl.BlockSpec((bq, ATT_DIM), lambda b, i: (b * nq + i, 0))
    cspec = pl.BlockSpec((cn, 2 * KV_DIM), lambda b, i: (b, 0))
    if local:
        prv = pl.BlockSpec((bq, 2 * KV_DIM), lambda b, i: (b * nq + jnp.maximum(i - 1, 0), 0))
        cur = pl.BlockSpec((bq, 2 * KV_DIM), lambda b, i: (b * nq + i, 0))
        nxt = pl.BlockSpec((bq, 2 * KV_DIM), lambda b, i: (b * nq + jnp.minimum(i + 1, nq - 1), 0))
        qo = jnp.arange(4 * bq)[:, None] % bq
        kcol = jnp.arange(3 * bq)[None, :]
        inwin = ((kcol // bq == 1) | ((kcol // bq == 0) & (kcol % bq >= qo)) | ((kcol // bq == 2) & (kcol % bq <= qo)))
        wmask = jnp.where(inwin, 0.0, NEG_INF).astype(F32)
        in_specs = [smem, qspec, prv, cur, nxt, prv, cur, nxt, cspec, cspec,
                    pl.BlockSpec((4 * bq, 3 * bq), lambda b, i: (0, 0))]
        args = [sink, qn, kn, kn, kn, vb, vb, vb, kctx, vctx, wmask]
    else:
        in_specs = [smem, qspec, cspec, cspec]
        args = [sink, qn, kctx, vctx]
    return pl.pallas_call(
        functools.partial(_attn_kernel, nq=nq, local=local),
        grid=(nb, nq),
        in_specs=in_specs,
        out_specs=pl.BlockSpec((bq, ATT_DIM), lambda b, i: (b * nq + i, 0)),
        out_shape=jax.ShapeDtypeStruct((m, ATT_DIM), BF16),
        compiler_params=_cparams("parallel", "parallel"),
        name="window_attn" if local else "ctx_attn",
    )(*args)


def _rwkv_kernel(*refs, nt, tq, vres, drn, tps):
    it = iter(refs)
    n_in = 5 if vres else 3
    first = [next(it) for _ in range(n_in)] if tps == 2 else None
    tiles = [[next(it) for _ in range(n_in)] for _ in range(tps)]
    ts_ref, pd_ref, ps_ref, wla_ref, wg_ref = (next(it) for _ in range(5))
    if vres:
        vuh_ref, vul_ref = next(it), next(it)
    e_ref, sin_ref = next(it), next(it)
    y_ref, aux_ref, sout_ref = next(it), next(it), next(it)
    ops = [next(it) for _ in range(tps)]
    wls = [next(it) for _ in range(tps)]
    auxs = [next(it) for _ in range(tps)]
    s_sc = next(it)

    j = pl.program_id(1)
    sgn = 1 - 2 * drn
    nch = tq // CHUNK
    nsteps = nt // tps
    e = e_ref[...]
    lane = lax.broadcasted_iota(jnp.int32, (1, LANES), 1)
    low = lane < HEAD_DIM

    def natural(k):
        return k if drn == 0 else nt - 1 - k

    def prepare(tile, in_refs, ops_sc, wl_sc, aux_sc):
        p_ref, pv_ref, nx_ref = in_refs[:3]
        rw = p_ref[...].astype(F32)
        row = lax.broadcasted_iota(jnp.int32, (tq, 1), 0)
        prev_row = jnp.where(tile > 0, pv_ref[PACK - 1:PACK, :].astype(F32), 0.0)
        next_row = jnp.where(tile < nt - 1, nx_ref[0:1, :].astype(F32), 0.0)
        prv = jnp.where(row == 0, prev_row, pltpu.roll(rw, 1, 0))
        yield
        nxt = jnp.where(row == tq - 1, next_row, pltpu.roll(rw, tq - 1, 0))
        xs = rw + ts_ref[0:1, :] * (prv - rw) + ts_ref[1:2, :] * (nxt - rw)
        yield

        r = xs[:, 0:RW_DIM]
        k = xs[:, RW_DIM:2 * RW_DIM]
        v = xs[:, 2 * RW_DIM:3 * RW_DIM]
        la = xs[:, 3 * RW_DIM:3 * RW_DIM + LANES]
        la = jnp.where(low, jnp.tanh(la), la)
        la_hi, la_lo = _split2(la)
        lora = (jnp.dot(la_hi, wla_ref[0], preferred_element_type=F32)
                + jnp.dot(la_lo, wla_ref[0], preferred_element_type=F32))
        yield
        logw = -DECAY_SCALE * jax.nn.sigmoid(pd_ref[0, 0:1, :] + lora[:, 0:RW_DIM])
        iclr = jax.nn.sigmoid(pd_ref[0, 1:2, :] + lora[:, RW_DIM:2 * RW_DIM])
        if vres:
            vf_ref, hv_ref = in_refs[3:]
            mix = jax.nn.sigmoid(ps_ref[3:4, :] + _dot3(hv_ref[...], vuh_ref[...], vul_ref[...]))
            v = v + (vf_ref[...].astype(F32) - v) * mix
        yield
        kk = k * ps_ref[0:1, :]
        kk = kk * lax.rsqrt(_segsum(kk * kk, e, exact=False) + 1e-12)
        yield
        kh = k * (1.0 + (iclr - 1.0) * ps_ref[1:2, :])
        bb = kk * iclr
        aux_sc[:, 0:RW_DIM] = (_segsum(r * kh * ps_ref[2:3, :], e, exact=False) * v).astype(aux_sc.dtype)
        if drn == 0:
            glo = xs[:, 3 * RW_DIM + LANES:RW_IN]
            aux_sc[:, RW_DIM:2 * RW_DIM] = _bdot(jax.nn.sigmoid(glo), wg_ref[...]).astype(aux_sc.dtype)
        else:
            aux_sc[:, RW_DIM:2 * RW_DIM] = v.astype(aux_sc.dtype)

        yield
        ti = lax.broadcasted_iota(jnp.int32, (tq, tq), 0)
        si = lax.broadcasted_iota(jnp.int32, (tq, tq), 1)
        tri = (((ti // CHUNK) == (si // CHUNK)) & (((si - ti) * sgn) <= 0)).astype(BF16)
        lw_hi, lw_lo = _split2(logw)
        cin = jnp.dot(tri, lw_hi, preferred_element_type=F32) + jnp.dot(tri, lw_lo, preferred_element_type=F32)
        cin3 = cin.reshape(nch, CHUNK, RW_DIM)
        last = CHUNK - 1 if drn == 0 else 0
        ctot = jnp.broadcast_to(cin3[:, last:last + 1, :], cin3.shape).reshape(tq, RW_DIM)
        yield
        e_neg = jnp.exp(-cin)
        e_rem = jnp.exp(ctot - cin)
        ops_sc[0] = (kk * jnp.exp(cin - logw)).astype(BF16)
        ops_sc[1] = (r * jnp.exp(cin)).astype(BF16)
        ops_sc[2] = (bb * e_neg).astype(BF16)
        ops_sc[3] = (kh * e_neg).astype(BF16)
        yield
        ops_sc[4] = (kh * e_rem).astype(BF16)
        ops_sc[5] = (bb * e_rem).astype(BF16)
        ops_sc[6] = v.astype(BF16)
        wl_sc[...] = jnp.exp(ctot)

    tr = lax.broadcasted_iota(jnp.int32, (CHUNK, LANES), 0)
    sc = lax.broadcasted_iota(jnp.int32, (CHUNK, LANES), 1) & (CHUNK - 1)
    rel = (sc - tr) * sgn
    strict = rel < 0
    incl = rel <= 0
    eye = (sc == tr).astype(F32)
    same_head = ((lax.broadcasted_iota(jnp.int32, (LANES, LANES), 0) // HEAD_DIM)
                 == (lax.broadcasted_iota(jnp.int32, (LANES, LANES), 1) // HEAD_DIM))

    def stack(x):
        zero = jnp.zeros_like(x)
        return jnp.concatenate([jnp.where(low, x, zero), jnp.where(low, zero, x)], axis=0)

    def hdot(x, y):
        return _bdot(x, stack(y))

    def scan_tile(ops_sc, wl_sc, row0):
        st = [s_sc[pr] for pr in range(N_PAIRS)]
        for g0 in range(0, nch, SCAN_GROUP):
            yield from scan_chunks(ops_sc, wl_sc, row0, st, range(g0, g0 + SCAN_GROUP))
        for pr in range(N_PAIRS):
            s_sc[pr] = st[pr]

    def scan_chunks(ops_sc, wl_sc, row0, st, chunk_ids):
        probs = []
        for ii in chunk_ids:
            off = (ii if drn == 0 else nch - 1 - ii) * CHUNK
            for pr in range(N_PAIRS):
                ls = slice(pr * LANES, (pr + 1) * LANES)
                q = {"off": off, "ls": ls, "pr": pr}
                for n, name in enumerate(("kkt", "rt", "bh", "kh", "kp", "bp", "v")):
                    q[name] = ops_sc[n, off:off + CHUNK, ls]
                q["wl"] = wl_sc[off:off + 1, ls]
                probs.append(q)
        for q in probs:
            g = _bdot_nt(jnp.concatenate([q["kkt"], q["rt"]], axis=0),
                         jnp.concatenate([stack(q["bh"]), stack(q["kh"])], axis=0))
            nn = jnp.where(strict, -g[0:CHUNK, 0:LANES], 0.0)
            q["nn"] = nn.astype(BF16)
            q["avk"] = jnp.where(strict, g[0:CHUNK, LANES:], 0.0).astype(BF16)
            q["arb"] = jnp.where(incl, g[CHUNK:, 0:LANES], 0.0).astype(BF16)
            q["ark"] = jnp.where(incl, g[CHUNK:, LANES:], 0.0).astype(BF16)
            q["t"] = eye + nn
        yield
        for q in probs:
            q["q"] = hdot(q["nn"], q["nn"]).astype(BF16)
        for _ in range(4):
            for q in probs:
                tq2 = _bdot(jnp.concatenate([q["t"].astype(BF16), q["q"]], axis=0), stack(q["q"]))
                q["t"] = q["t"] + tq2[:CHUNK]
                q["q"] = tq2[CHUNK:].astype(BF16)
            yield
        for q in probs:
            q["t"] = (q["t"] + hdot(q["t"], q["q"])).astype(BF16)
        for q in probs:
            av = _bdot(jnp.concatenate([q["avk"], q["ark"]], axis=0), stack(q["v"]))
            q["av"], q["arkv"] = av[:CHUNK].astype(BF16), av[CHUNK:]
        yield
        for q in probs:
            tt2 = _bdot(q["t"], jnp.concatenate([stack(q["kkt"]), stack(q["av"])], axis=1))
            q["tk"], q["tav"] = tt2[:, :LANES].astype(BF16), tt2[:, LANES:].astype(BF16)
        yield
        for q in probs:
            tn = _bdot_tn(jnp.concatenate([q["tk"], q["tav"]], axis=1), q["bp"])
            q["mb"] = jnp.where(same_head, tn[:LANES], 0.0)
            q["nm"] = jnp.where(same_head, _bdot_tn(q["v"], q["kp"]) - tn[LANES:], 0.0)
            z = _bdot(q["arb"], jnp.concatenate([stack(q["tk"]), stack(q["tav"])], axis=1))
            q["r2"] = (q["rt"].astype(F32) - z[:, :LANES]).astype(BF16)
            q["y0"] = q["arkv"] - z[:, LANES:]
        yield

        outs = []
        for q in probs:
            sm = st[q["pr"]]
            outs.append(_bdot_nt(q["r2"], sm) + q["y0"])
            st[q["pr"]] = sm * q["wl"] - _bdot(sm, q["mb"]) + q["nm"]
        for q, yo in zip(probs, outs):
            y_ref[row0 + q["off"]:row0 + q["off"] + CHUNK, q["ls"]] = yo.astype(y_ref.dtype)

    def run(*stages):
        live = list(stages)
        while live:
            for g in list(live):
                if next(g, live) is live:
                    live.remove(g)

    @pl.when(j == 0)
    def _():
        s_sc[...] = sin_ref[0]

    if tps == 1:
        run(prepare(natural(j), tiles[0], ops[0], wls[0], auxs[0]))
        aux_ref[...] = auxs[0][...]
        run(scan_tile(ops[0], wls[0], 0))
    else:
        @pl.when(j == 0)
        def _():
            run(prepare(natural(0), first, ops[0], wls[0], auxs[0]))

        rows = [0, tq] if drn == 0 else [tq, 0]
        aux_ref[rows[0]:rows[0] + tq, :] = auxs[0][...]
        run(scan_tile(ops[0], wls[0], rows[0]), prepare(natural(2 * j + 1), tiles[0], ops[1], wls[1], auxs[1]))
        aux_ref[rows[1]:rows[1] + tq, :] = auxs[1][...]
        run(scan_tile(ops[1], wls[1], rows[1]),
            prepare(natural(jnp.minimum(2 * j + 2, nt - 1)), tiles[1], ops[0], wls[0], auxs[0]))

    @pl.when(j == nsteps - 1)
    def _():
        sout_ref[0] = s_sc[...]


def _rwkv(p, vf_src, hv, state_in, prm, nb, rows, drn):
    m = p.shape[0]
    tq = _row_tile(rows, 256)
    nt = rows // tq
    tps = 2 if nt % 2 == 0 else 1
    nsteps = nt // tps
    rb = tq // PACK
    vres = vf_src is not None

    def tile_specs(scan_pos):
        def tile(b, j):
            k = jnp.minimum(scan_pos(j), nt - 1)
            return b * nt + (k if drn == 0 else nt - 1 - k)

        specs = [pl.BlockSpec((tq, RW_IN), lambda b, j: (tile(b, j), 0)),
                 pl.BlockSpec((PACK, RW_IN), lambda b, j: (jnp.maximum(tile(b, j) * rb - 1, 0), 0)),
                 pl.BlockSpec((PACK, RW_IN), lambda b, j: (jnp.minimum((tile(b, j) + 1) * rb, m // PACK - 1), 0))]
        args = [p, p, p]
        if vres:
            specs += [pl.BlockSpec((tq, RW_DIM), lambda b, j: (tile(b, j), 1)),
                      pl.BlockSpec((tq, LANES), lambda b, j: (tile(b, j), 0))]
            args += [vf_src, hv]
        return specs, args

    if tps == 2:
        positions = [lambda j: 0 * j, lambda j: 2 * j + 1, lambda j: 2 * j + 2]
    else:
        positions = [lambda j: j]
    in_specs, args = [], []
    for pos in positions:
        s_, a_ = tile_specs(pos)
        in_specs += s_
        args += a_

    const2 = lambda b, j: (0, 0)
    dir3 = lambda b, j: (drn, 0, 0)
    in_specs += [pl.BlockSpec((2, RW_IN), const2),
                 pl.BlockSpec((1, SUBLANES, RW_DIM), dir3),
                 pl.BlockSpec((SUBLANES, RW_DIM), const2),
                 pl.BlockSpec((1, LANES, 2 * RW_DIM), dir3),
                 pl.BlockSpec((LANES, RW_DIM), const2)]
    args += [prm["tshift"], prm["pd"], prm["ps"], prm["wla"], prm["wg"]]
    if vres:
        in_specs += [pl.BlockSpec((LANES, RW_DIM), const2), pl.BlockSpec((LANES, RW_DIM), const2)]
        args += [prm["vu_hi"], prm["vu_lo"]]
    sspec = pl.BlockSpec((1, N_PAIRS, LANES, LANES), lambda b, j: (b, 0, 0, 0))
    in_specs += [pl.BlockSpec((LANES, LANES), const2), sspec]
    args += [prm["e128"], state_in]

    def out_block(b, j):
        return (b * nsteps + (j if drn == 0 else nsteps - 1 - j), 0)

    return pl.pallas_call(
        functools.partial(_rwkv_kernel, nt=nt, tq=tq, vres=vres, drn=drn, tps=tps),
        grid=(nb, nsteps),
        in_specs=in_specs,
        out_specs=[pl.BlockSpec((tps * tq, RW_DIM), out_block),
                   pl.BlockSpec((tps * tq, 2 * RW_DIM), out_block),
                   sspec],
        out_shape=[jax.ShapeDtypeStruct((m, RW_DIM), ACT),
                   jax.ShapeDtypeStruct((m, 2 * RW_DIM), ACT),
                   jax.ShapeDtypeStruct((nb, N_PAIRS, LANES, LANES), F32)],
        scratch_shapes=([pltpu.VMEM((7, tq, RW_DIM), BF16)] * tps + [pltpu.VMEM((tq, RW_DIM), F32)] * tps
                        + [pltpu.VMEM((tq, 2 * RW_DIM), ACT)] * tps
                        + [pltpu.VMEM((N_PAIRS, LANES, LANES), F32)]),
        compiler_params=_cparams("parallel", "arbitrary"),
        name="rwkv7_fwd" if drn == 0 else "rwkv7_bwd",
    )(*args)


def _rwkv_both(p, vf_src, hv, states, prm, nb, rows):
    outs = [_rwkv(p, vf_src, hv, states[d], prm, nb, rows, d) for d in range(2)]
    return [o[0] for o in outs], [o[1] for o in outs], [o[2] for o in outs]


def _merge_kernel(x_ref, y0_ref, y1_ref, a0_ref, a1_ref, ya_ref, g0_ref, g1_ref, g2_ref, g3_ref, ln_ref, e_ref,
                  wbr_ref, wba_ref, wo_ref, mod_ref, o_ref):
    e = e_ref[...]
    y = y0_ref[...].astype(F32) + y1_ref[...].astype(F32)
    mu = _segsum(y, e) * (1.0 / HEAD_DIM)
    yc = y - mu
    var = _segsum(yc * yc, e) * (1.0 / HEAD_DIM)
    yn = yc * lax.rsqrt(var + GN_EPS) * ln_ref[0:1, :] + ln_ref[1:2, :]
    cv = a0_ref[:, 0:RW_DIM].astype(F32) + a1_ref[:, 0:RW_DIM].astype(F32)
    yrw = (yn + cv) * a0_ref[:, RW_DIM:2 * RW_DIM].astype(F32)
    a = _bdot(yrw, wbr_ref[...])
    b = jnp.dot(ya_ref[...], wba_ref[...], preferred_element_type=F32)
    h = a.shape[1] // 2
    mrg = jnp.concatenate(
        [jax.nn.sigmoid(g0_ref[...].astype(F32)) * a[:, :h] + jax.nn.sigmoid(g2_ref[...].astype(F32)) * b[:, :h],
         jax.nn.sigmoid(g1_ref[...].astype(F32)) * a[:, h:] + jax.nn.sigmoid(g3_ref[...].astype(F32)) * b[:, h:]],
        axis=1)
    o_ref[...] = x_ref[...] + mod_ref[0, 2:3, :] * _bdot(mrg, wo_ref[...])


def _merge(x2, ys, auxs, ya, p, ln, e128, wbr, wba, wo, mod, rows):
    m, d = x2.shape
    tm = _row_tile(rows, 512)
    per = rows // tm
    many = mod.shape[0] > 1
    gb = GATE_OFF // 512
    const2 = lambda i: (0, 0)
    in_specs = [pl.BlockSpec((tm, d), lambda i: (i, 0)),
                pl.BlockSpec((tm, RW_DIM), lambda i: (i, 0)),
                pl.BlockSpec((tm, RW_DIM), lambda i: (i, 0)),
                pl.BlockSpec((tm, 2 * RW_DIM), lambda i: (i, 0)),
                pl.BlockSpec((tm, 2 * RW_DIM), lambda i: (i, 0)),
                pl.BlockSpec((tm, ATT_DIM), lambda i: (i, 0))]
    in_specs += [pl.BlockSpec((tm, 512), (lambda i, c=c: (i, gb + c))) for c in range(4)]
    in_specs += [pl.BlockSpec((2, RW_DIM), const2),
                 pl.BlockSpec((LANES, LANES), const2),
                 pl.BlockSpec(wbr.shape, const2),
                 pl.BlockSpec(wba.shape, const2),
                 pl.BlockSpec(wo.shape, const2),
                 pl.BlockSpec((1, 6, d), lambda i: ((i // per) if many else 0, 0, 0))]
    return pl.pallas_call(
        _merge_kernel,
        grid=(m // tm,),
        in_specs=in_specs,
        out_specs=pl.BlockSpec((tm, d), lambda i: (i, 0)),
        out_shape=jax.ShapeDtypeStruct((m, d), F32),
        compiler_params=_cparams("parallel"),
        name="branch_merge",
    )(x2, ys[0], ys[1], auxs[0], auxs[1], ya, p, p, p, p, ln, e128, wbr, wba, wo, mod)


def _ffn_kernel(x_ref, g_ref, mod_ref, wg_ref, wu_ref, wd_ref, o_ref, h_sc, acc_sc, *, nf):
    f = pl.program_id(1)

    @pl.when(f == 0)
    def _():
        h = _norm_mod(x_ref[...], g_ref[...], mod_ref[0, 3:4, :], mod_ref[0, 4:5, :])
        h_sc[...] = h.astype(BF16)
        acc_sc[...] = jnp.zeros_like(acc_sc)

    h = h_sc[...]
    a = jnp.dot(h, wg_ref[...], preferred_element_type=F32)
    u = jnp.dot(h, wu_ref[...], preferred_element_type=F32)
    act = (a * jax.nn.sigmoid(a)) * u
    acc_sc[...] += jnp.dot(act.astype(BF16), wd_ref[...], preferred_element_type=F32)

    @pl.when(f == nf - 1)
    def _():
        o_ref[...] = x_ref[...] + mod_ref[0, 5:6, :] * acc_sc[...]


def _ffn(x2, gain, mod, wg, wu, wd, rows):
    m, d = x2.shape
    dff = wg.shape[1]
    tm = _row_tile(rows, 512)
    per = rows // tm
    many = mod.shape[0] > 1
    tf = dff
    nf = dff // tf
    once = pl.Buffered(1)
    return pl.pallas_call(
        functools.partial(_ffn_kernel, nf=nf),
        grid=(m // tm, nf),
        in_specs=[pl.BlockSpec((tm, d), lambda i, f: (i, 0)),
                  pl.BlockSpec((1, d), lambda i, f: (0, 0)),
                  pl.BlockSpec((1, 6, d), lambda i, f: ((i // per) if many else 0, 0, 0)),
                  pl.BlockSpec((d, tf), lambda i, f: (0, f), pipeline_mode=once),
                  pl.BlockSpec((d, tf), lambda i, f: (0, f), pipeline_mode=once),
                  pl.BlockSpec((tf, d), lambda i, f: (f, 0), pipeline_mode=once)],
        out_specs=pl.BlockSpec((tm, d), lambda i, f: (i, 0)),
        out_shape=jax.ShapeDtypeStruct((m, d), F32),
        scratch_shapes=[pltpu.VMEM((tm, d), BF16), pltpu.VMEM((tm, d), F32)],
        compiler_params=_cparams("parallel", "arbitrary"),
        name="ffn_swiglu",
    )(x2, gain.reshape(1, d), mod, wg, wu, wd)


def _dma_pieces(count, sizes, make):
    for k, size in enumerate(sizes):
        @pl.when((count & size) != 0)
        def _(k=k, size=size):
            make(k, pl.multiple_of(count & ~(2 * size - 1), PACK), size)


def _moe_gather_kernel(pos_ref, cnt_ref, h_ref, rankt_ref, zero_ref, out_ref, hs_sc, sem, *, ne, nsteps, sizes):
    del zero_ref
    i = pl.program_id(0)
    ex = pl.program_id(1)
    step = i * ne + ex
    slot = step % 2
    tt = h_ref.shape[0]

    def copies(st, sl, act):
        base = pos_ref[st]

        def make(k, off, size):
            act(pltpu.make_async_copy(hs_sc.at[sl, pl.ds(off, size)],
                                      out_ref.at[pl.ds(pl.multiple_of(base + off, PACK), size)], sem.at[sl, k]))

        _dma_pieces(cnt_ref[st], sizes, make)

    @pl.when(step >= 2)
    def _():
        copies(step - 2, slot, lambda cp: cp.wait())

    rrow = rankt_ref[0, pl.ds(ex, 1), :]
    rowid = lax.broadcasted_iota(jnp.int32, (MOE_BLK, tt), 0).astype(F32)

    def gather(b, carry):
        sel = ((rrow - (b * MOE_BLK).astype(F32)) == rowid).astype(BF16)
        r0 = pl.multiple_of(b * MOE_BLK, MOE_BLK)
        hs_sc[slot, pl.ds(r0, MOE_BLK), :] = jnp.dot(sel, h_ref[...], preferred_element_type=F32).astype(BF16)
        return carry

    lax.fori_loop(0, (cnt_ref[step] + MOE_BLK - 1) // MOE_BLK, gather, 0)
    copies(step, slot, lambda cp: cp.start())

    @pl.when(step == nsteps - 1)
    def _():
        if nsteps >= 2:
            copies(step - 1, 1 - slot, lambda cp: cp.wait())
        copies(step, slot, lambda cp: cp.wait())


def _moe_ffn_kernel(te_ref, tm_ref, tv_ref, hs_ref, wg_ref, wu_ref, wd_ref, os_ref, acc_sc, *, nf):
    del te_ref, tm_ref
    t = pl.program_id(0)
    f = pl.program_id(1)
    valid = tv_ref[t] != 0

    @pl.when(valid)
    def _():
        hb = hs_ref[...]
        a = jnp.dot(hb, wg_ref[0], preferred_element_type=F32)
        u = jnp.dot(hb, wu_ref[0], preferred_element_type=F32)
        act = ((a * jax.nn.sigmoid(a)) * u).astype(BF16)
        part = jnp.dot(act, wd_ref[0], preferred_element_type=F32)

        @pl.when(f == 0)
        def _():
            acc_sc[...] = part

        @pl.when(f > 0)
        def _():
            acc_sc[...] += part

    @pl.when(f == nf - 1)
    def _():
        os_ref[...] = jnp.where(valid, acc_sc[...], 0.0).astype(os_ref.dtype)


def _moe_combine_kernel(pos_ref, cnt_ref, x_ref, we_ref, rank_ref, mod_ref, os_ref, o_ref, acc_sc, ob_sc, sem,
                        *, ne, nsteps, sizes):
    i = pl.program_id(0)
    ex = pl.program_id(1)
    step = i * ne + ex
    slot = step % 2
    tt = x_ref.shape[0]
    sblk = min(2 * MOE_BLK, tt)
    lane = lax.broadcasted_iota(jnp.int32, (tt, LANES), 1)

    def copies(st, sl, act):
        base = pos_ref[st]

        def make(k, off, size):
            act(pltpu.make_async_copy(os_ref.at[pl.ds(pl.multiple_of(base + off, PACK), size)],
                                      ob_sc.at[sl, pl.ds(off, size)], sem.at[sl, k]))

        _dma_pieces(cnt_ref[st], sizes, make)

    @pl.when(step == 0)
    def _():
        ob_sc[...] = jnp.zeros_like(ob_sc)
        copies(0, 0, lambda cp: cp.start())

    @pl.when(step + 1 < nsteps)
    def _():
        copies(step + 1, 1 - slot, lambda cp: cp.start())

    copies(step, slot, lambda cp: cp.wait())

    @pl.when(ex == 0)
    def _():
        acc_sc[...] = jnp.zeros_like(acc_sc)

    rcol = jnp.sum(jnp.where(lane == ex, rank_ref[...], 0.0), axis=1, keepdims=True)
    wcol = jnp.sum(jnp.where(lane == ex, we_ref[...], 0.0), axis=1, keepdims=True)
    colid = lax.broadcasted_iota(jnp.int32, (tt, sblk), 1).astype(F32)

    def scatter(b, carry):
        sel = ((rcol - (b * sblk).astype(F32)) == colid).astype(BF16)
        r0 = pl.multiple_of(b * sblk, sblk)
        acc_sc[...] += wcol * jnp.dot(sel, ob_sc[slot, pl.ds(r0, sblk), :], preferred_element_type=F32)
        return carry

    lax.fori_loop(0, (cnt_ref[step] + sblk - 1) // sblk, scatter, 0)

    @pl.when(ex == ne - 1)
    def _():
        o_ref[...] = x_ref[...] + mod_ref[0, 5:6, :] * acc_sc[...]


def _route_kernel(x_ref, g_ref, mod_ref, rt_ref, h_ref, we_ref, rank_ref, rankt_ref, cnt_ref, *, ne):
    tt = x_ref.shape[0]
    h = _norm_mod(x_ref[...], g_ref[...], mod_ref[0, 3:4, :], mod_ref[0, 4:5, :])
    h_ref[...] = h.astype(BF16)
    lane = lax.broadcasted_iota(jnp.int32, (tt, LANES), 1)
    logits = jnp.dot(h, rt_ref[...], precision=HIGHEST, preferred_element_type=F32)
    lg = jnp.where(lane < ne, logits, NEG_INF)
    t1 = jnp.max(lg, axis=1, keepdims=True)
    i1 = jnp.min(jnp.where(lg == t1, lane, LANES), axis=1, keepdims=True)
    lg2 = jnp.where(lane == i1, NEG_INF, lg)
    t2 = jnp.max(lg2, axis=1, keepdims=True)
    i2 = jnp.min(jnp.where(lg2 == t2, lane, LANES), axis=1, keepdims=True)
    e2 = jnp.exp(t2 - t1)
    den = 1.0 + e2
    we_ref[...] = jnp.where(lane == i1, 1.0 / den, 0.0) + jnp.where(lane == i2, e2 / den, 0.0)
    sel = (lane == i1) | (lane == i2)
    self32 = sel.astype(F32)
    before = (lax.broadcasted_iota(jnp.int32, (tt, tt), 1) < lax.broadcasted_iota(jnp.int32, (tt, tt), 0))
    rank = jnp.dot(before.astype(BF16), sel.astype(BF16), preferred_element_type=F32)
    rank = jnp.where(sel, rank, -1.0)
    rank_ref[...] = rank
    rankt_ref[0] = rank.T[0:SUBLANES, :]
    cnt_ref[0] = jnp.broadcast_to(jnp.sum(self32, axis=0, keepdims=True), (SUBLANES, LANES))


def _moe(x2, gain, mod, router_pad, wg, wu, wd, rows):
    m, d = x2.shape
    ne, _, dff = wg.shape
    assert ne <= SUBLANES
    tt = _row_tile(rows, MOE_TILE)
    nti = m // tt
    per = rows // tt
    many = mod.shape[0] > 1
    h, we, rank, rankt, cnt = pl.pallas_call(
        functools.partial(_route_kernel, ne=ne),
        grid=(nti,),
        in_specs=[pl.BlockSpec((tt, d), lambda i: (i, 0)),
                  pl.BlockSpec((1, d), lambda i: (0, 0)),
                  pl.BlockSpec((1, 6, d), lambda i: ((i // per) if many else 0, 0, 0)),
                  pl.BlockSpec((d, LANES), lambda i: (0, 0))],
        out_specs=[pl.BlockSpec((tt, d), lambda i: (i, 0)),
                   pl.BlockSpec((tt, LANES), lambda i: (i, 0)),
                   pl.BlockSpec((tt, LANES), lambda i: (i, 0)),
                   pl.BlockSpec((1, SUBLANES, tt), lambda i: (i, 0, 0)),
                   pl.BlockSpec((1, SUBLANES, LANES), lambda i: (i, 0, 0))],
        out_shape=[jax.ShapeDtypeStruct((m, d), BF16),
                   jax.ShapeDtypeStruct((m, LANES), F32),
                   jax.ShapeDtypeStruct((m, LANES), F32),
                   jax.ShapeDtypeStruct((nti, SUBLANES, tt), F32),
                   jax.ShapeDtypeStruct((nti, SUBLANES, LANES), F32)],
        compiler_params=_cparams("parallel"),
        name="moe_route",
    )(x2, gain.reshape(1, d), mod, router_pad)
    c16 = (cnt[:, 0, :ne].astype(jnp.int32) + (PACK - 1)) // PACK * PACK
    region = (jnp.sum(c16, axis=0) + (MOE_FT - 1)) // MOE_FT * MOE_FT
    e_off = jnp.cumsum(region) - region
    pos = (e_off[None, :] + jnp.cumsum(c16, axis=0) - c16).reshape(-1).astype(jnp.int32)
    c16 = c16.reshape(-1)
    rmax = -(-(2 * m + nti * ne * PACK + ne * MOE_FT) // MOE_FT) * MOE_FT
    nft = rmax // MOE_FT
    tiles_e = jnp.cumsum(region // MOE_FT)
    tid = jnp.arange(nft, dtype=jnp.int32)
    tile_valid = (tid < tiles_e[-1]).astype(jnp.int32)
    tile_map = jnp.minimum(tid, tiles_e[-1] - 1).astype(jnp.int32)
    tile_exp = jnp.minimum(jnp.sum(tile_map[:, None] >= tiles_e[None, :], axis=1), ne - 1).astype(jnp.int32)

    sizes = tuple(tt >> k for k in range(tt.bit_length()) if (tt >> k) >= PACK)
    nsteps = nti * ne
    hs = pl.pallas_call(
        functools.partial(_moe_gather_kernel, ne=ne, nsteps=nsteps, sizes=sizes),
        grid_spec=pltpu.PrefetchScalarGridSpec(
            num_scalar_prefetch=2,
            grid=(nti, ne),
            in_specs=[pl.BlockSpec((tt, d), lambda i, e, p_, c_: (i, 0)),
                      pl.BlockSpec((1, SUBLANES, tt), lambda i, e, p_, c_: (i, 0, 0)),
                      pl.BlockSpec(memory_space=pl.ANY)],
            out_specs=pl.BlockSpec(memory_space=pl.ANY),
            scratch_shapes=[pltpu.VMEM((2, tt, d), BF16), pltpu.SemaphoreType.DMA((2, len(sizes)))]),
        out_shape=jax.ShapeDtypeStruct((rmax, d), BF16),
        input_output_aliases={4: 0},
        compiler_params=_cparams("arbitrary", "arbitrary"),
        name="moe_gather",
    )(pos, c16, h, rankt, jnp.zeros((rmax, d), BF16))

    tf = dff // 2 if (dff // 2) % LANES == 0 else dff
    nf = dff // tf
    os_sorted = pl.pallas_call(
        functools.partial(_moe_ffn_kernel, nf=nf),
        grid_spec=pltpu.PrefetchScalarGridSpec(
            num_scalar_prefetch=3,
            grid=(nft, nf),
            in_specs=[pl.BlockSpec((MOE_FT, d), lambda t, f, te, tm, tv: (tm[t], 0)),
                      pl.BlockSpec((1, d, tf), lambda t, f, te, tm, tv: (te[t], 0, f)),
                      pl.BlockSpec((1, d, tf), lambda t, f, te, tm, tv: (te[t], 0, f)),
                      pl.BlockSpec((1, tf, d), lambda t, f, te, tm, tv: (te[t], f, 0))],
            out_specs=pl.BlockSpec((MOE_FT, d), lambda t, f, te, tm, tv: (t, 0)),
            scratch_shapes=[pltpu.VMEM((MOE_FT, d), F32)]),
        out_shape=jax.ShapeDtypeStruct((rmax, d), BF16),
        compiler_params=_cparams("parallel", "arbitrary"),
        name="moe_experts",
    )(tile_exp, tile_map, tile_valid, hs, wg, wu, wd)

    return pl.pallas_call(
        functools.partial(_moe_combine_kernel, ne=ne, nsteps=nsteps, sizes=sizes),
        grid_spec=pltpu.PrefetchScalarGridSpec(
            num_scalar_prefetch=2,
            grid=(nti, ne),
            in_specs=[pl.BlockSpec((tt, d), lambda i, e, p_, c_: (i, 0)),
                      pl.BlockSpec((tt, LANES), lambda i, e, p_, c_: (i, 0)),
                      pl.BlockSpec((tt, LANES), lambda i, e, p_, c_: (i, 0)),
                      pl.BlockSpec((1, 6, d), lambda i, e, p_, c_: ((i // per) if many else 0, 0, 0)),
                      pl.BlockSpec(memory_space=pl.ANY)],
            out_specs=pl.BlockSpec((tt, d), lambda i, e, p_, c_: (i, 0)),
            scratch_shapes=[pltpu.VMEM((tt, d), F32), pltpu.VMEM((2, tt, d), BF16),
                            pltpu.SemaphoreType.DMA((2, len(sizes)))]),
        out_shape=jax.ShapeDtypeStruct((m, d), F32),
        compiler_params=_cparams("arbitrary", "arbitrary"),
        name="moe_combine",
    )(pos, c16, x2, we, rank, mod, os_sorted)


def _rope_table(t):
    rows = t // GRID_W
    row = jnp.repeat(jnp.arange(rows, dtype=F32), GRID_W)
    col = jnp.tile(jnp.arange(GRID_W, dtype=F32), rows)
    half = HEAD_DIM // 2
    inv_freq = ROPE_THETA ** (-jnp.arange(0, half, 2, dtype=F32) / half)
    ar = row[:, None] * inv_freq
    ac = col[:, None] * inv_freq
    ang = jnp.concatenate([ar, ar, ac, ac] * 2, axis=-1)
    return jnp.stack([jnp.cos(ang), jnp.sin(ang)])


def _pad_rows(a, rows):
    return jnp.zeros((rows,) + a.shape[1:], a.dtype).at[:a.shape[0]].set(a)


def _hi_lo(w):
    hi = w.astype(BF16)
    return hi, (w - hi.astype(F32)).astype(BF16)


def kernel(x, c, ctx, c_ctx, norm1, norm2, w_mod, b_mod, w_in, tshift, w0, w_decay_up, a0, w_iclr_up, v0, w_vres_down, w_vres_up, w_gate_up, k_k, k_a, r_k, lnx_g, lnx_b, q_norm, k_norm, sink, w_br_rwkv, w_br_attn, w_out, ffn_gate, ffn_up, ffn_down, router, moe_gate, moe_up, moe_down):
    nb, t, d = x.shape
    cn = ctx.shape[1]
    depth = w_in.shape[0]
    mods = _modulation(c, c_ctx, w_mod, b_mod)
    cs = _rope_table(t)
    e128 = jnp.kron(jnp.eye(2, dtype=F32), jnp.ones((HEAD_DIM, HEAD_DIM), F32)).astype(BF16)
    xl = x.reshape(nb * t, d)
    xc = ctx.reshape(nb * cn, d)
    vf_l = vf_c = None
    for layer in range(depth):
        last = layer == depth - 1
        mod_l = mods[layer, :nb].reshape(nb, 6, d)
        mod_c = mods[layer, nb:nb + 1].reshape(1, 6, d)
        w_in_b = w_in[layer].astype(BF16)
        vd_b = None
        prm = {"tshift": tshift[layer], "e128": e128, "wg": w_gate_up[layer].astype(BF16)}
        prm["pd"] = jnp.stack(
            [_pad_rows(jnp.stack([w0[layer, dd], a0[layer, dd]]), SUBLANES) for dd in range(2)])
        shared = [k_k[layer], k_a[layer], r_k[layer].reshape(RW_DIM)]
        if layer > 0:
            shared.append(v0[layer - 1])
            vd_b = _pad_rows(w_vres_down[layer - 1].T, LANES).T.astype(BF16)
            prm["vu_hi"], prm["vu_lo"] = _hi_lo(_pad_rows(w_vres_up[layer - 1], LANES))
        prm["ps"] = _pad_rows(jnp.stack(shared), SUBLANES)
        wla = jnp.zeros((2, LANES, 2 * RW_DIM), F32)
        wla = wla.at[:, :HEAD_DIM, :RW_DIM].set(w_decay_up[layer]).at[:, HEAD_DIM:, RW_DIM:].set(w_iclr_up[layer])
        prm["wla"] = wla.astype(BF16)

        p_l, hv_l = _inproj(xl, norm1[layer], mod_l, w_in_b, vd_b, t)
        p_c, hv_c = _inproj(xc, norm1[layer], mod_c, w_in_b, vd_b, cn)

        s0 = jnp.zeros((nb, N_PAIRS, LANES, LANES), F32)
        y_c, aux_c, s_ctx = _rwkv_both(p_c, vf_c, hv_c, [s0, s0], prm, nb, cn)
        y_l, aux_l, _ = _rwkv_both(p_l, vf_l, hv_l, s_ctx, prm, nb, t)
        if layer == 0:
            vf_l, vf_c = aux_l[1], aux_c[1]

        qg = jnp.tile(q_norm[layer], 2).reshape(1, LANES)
        kg = jnp.tile(k_norm[layer], 2).reshape(1, LANES)
        qn_l, kn_l, vb_l = _qk_prep(p_l, cs, qg, kg, e128, t, True)
        qn_c, kn_c, vb_c = _qk_prep(p_c, None, qg, kg, e128, cn, False)
        sk = sink[layer].reshape(1, -1)
        ya_l = _attention(qn_l, kn_l, vb_l, kn_c, vb_c, sk, nb, t, cn, True)

        ln = jnp.stack([lnx_g[layer], lnx_b[layer]])
        wbr = w_br_rwkv[layer].astype(BF16)
        wba = w_br_attn[layer].astype(BF16)
        wo = w_out[layer].astype(BF16)
        xl = _merge(xl, y_l, aux_l, ya_l, p_l, ln, e128, wbr, wba, wo, mod_l, t)
        if not last:
            ya_c = _attention(qn_c, None, None, kn_c, vb_c, sk, nb, cn, cn, False)
            xc = _merge(xc, y_c, aux_c, ya_c, p_c, ln, e128, wbr, wba, wo, mod_c, cn)

        i = layer // 2
        if layer % 2 == 0:
            fw = (ffn_gate[i].astype(BF16), ffn_up[i].astype(BF16), ffn_down[i].astype(BF16))
            xl = _ffn(xl, norm2[layer], mod_l, *fw, t)
            if not last:
                xc = _ffn(xc, norm2[layer], mod_c, *fw, cn)
        else:
            rt = _pad_rows(router[i].T, LANES).T
            mw = (moe_gate[i].astype(BF16), moe_up[i].astype(BF16), moe_down[i].astype(BF16))
            xl = _moe(xl, norm2[layer], mod_l, rt, *mw, t)
            if not last:
                xc = _moe(xc, norm2[layer], mod_c, rt, *mw, cn)
    return xl.reshape(nb, t, d)
```

```python
import functools

import jax
import jax.numpy as jnp
from jax import lax
from jax.experimental import pallas as pl
from jax.experimental.pallas import tpu as pltpu

F32 = jnp.float32
BF16 = jnp.bfloat16
HIGHEST = lax.Precision.HIGHEST

LANES = 128
SUBLANES = 8
PACK = 16
ACT = jnp.bfloat16
HEAD_DIM = 64
RW_DIM = 512
ATT_DIM = 512
KV_DIM = 128
RW_IN = 1792
ATT_OFF = RW_IN
GATE_OFF = RW_IN + ATT_DIM + 2 * KV_DIM
N_EXPERTS = 8
CHUNK = 64
N_PAIRS = RW_DIM // LANES
MOE_TILE = 1024
MOE_BLK = 128
MOE_FT = 512
SCAN_GROUP = 4
ATT_SCALE = HEAD_DIM ** -0.5
ROPE_THETA = 10000.0
GRID_W = 64
RMS_EPS = 1e-6
GN_EPS = 64e-5
NEG_INF = -1e30
DECAY_SCALE = 0.6065306597126334
VMEM_LIMIT = 48 * 1024 * 1024


def _cparams(*sem):
    return pltpu.CompilerParams(dimension_semantics=sem, vmem_limit_bytes=VMEM_LIMIT)


def _row_tile(rows, cap):
    t = cap
    while rows % t:
        t //= 2
    return t


def _bdot(a, b):
    return jnp.dot(a.astype(BF16), b.astype(BF16), preferred_element_type=F32)


def _bdot_nt(a, b):
    return lax.dot_general(a.astype(BF16), b.astype(BF16), (((1,), (1,)), ((), ())),
                           preferred_element_type=F32)


def _bdot_tn(a, b):
    return lax.dot_general(a.astype(BF16), b.astype(BF16), (((0,), (0,)), ((), ())),
                           preferred_element_type=F32)


def _split2(x):
    hi = x.astype(BF16)
    lo = (x - hi.astype(F32)).astype(BF16)
    return hi, lo


def _dot3(a, w_hi, w_lo):
    a_hi, a_lo = _split2(a)
    return (jnp.dot(a_hi, w_hi, preferred_element_type=F32)
            + (jnp.dot(a_lo, w_hi, preferred_element_type=F32)
               + jnp.dot(a_hi, w_lo, preferred_element_type=F32)))


def _segsum(x, e, exact=True):
    outs = []
    for s in range(x.shape[1] // LANES):
        hi, lo = _split2(x[:, s * LANES:(s + 1) * LANES])
        acc = jnp.dot(hi, e, preferred_element_type=F32)
        outs.append(acc + jnp.dot(lo, e, preferred_element_type=F32) if exact else acc)
    return outs[0] if len(outs) == 1 else jnp.concatenate(outs, axis=1)


def _norm_mod(x, gain, shift, scale):
    y = x * lax.rsqrt(jnp.mean(x * x, axis=-1, keepdims=True) + RMS_EPS) * gain
    return y * (1.0 + scale) + shift


def _mod_kernel(c_ref, w_ref, b_ref, o_ref):
    c = c_ref[...]
    s = c * jax.nn.sigmoid(c)
    o_ref[0] = jnp.dot(s, w_ref[0], precision=HIGHEST, preferred_element_type=F32) + b_ref[0]


def _modulation(c, c_ctx, w_mod, b_mod):
    depth, d, n6 = w_mod.shape
    nb = c.shape[0]
    rm = -(-(nb + 1) // SUBLANES) * SUBLANES
    cc = jnp.zeros((rm, d), F32).at[:nb].set(c).at[nb].set(c_ctx)
    tn = _row_tile(n6, 1536)
    return pl.pallas_call(
        _mod_kernel,
        grid=(depth, n6 // tn),
        in_specs=[pl.BlockSpec((rm, d), lambda l, n: (0, 0)),
                  pl.BlockSpec((1, d, tn), lambda l, n: (l, 0, n)),
                  pl.BlockSpec((1, 1, tn), lambda l, n: (l, 0, n))],
        out_specs=pl.BlockSpec((1, rm, tn), lambda l, n: (l, 0, n)),
        out_shape=jax.ShapeDtypeStruct((depth, rm, n6), F32),
        compiler_params=_cparams("parallel", "parallel"),
        name="adaln_mod",
    )(cc, w_mod, b_mod.reshape(depth, 1, n6))


def _inproj_kernel(*refs, vres, nsub):
    if vres:
        x_ref, g_ref, mod_ref, w_ref, vd_ref, o_ref, hv_ref = refs
    else:
        x_ref, g_ref, mod_ref, w_ref, o_ref = refs
    sub = x_ref.shape[0] // nsub
    for s in range(nsub):
        rs = slice(s * sub, (s + 1) * sub)
        h = _norm_mod(x_ref[rs, :], g_ref[...], mod_ref[0, 0:1, :], mod_ref[0, 1:2, :]).astype(BF16)
        if vres:
            hv_ref[rs, :] = jnp.dot(h, vd_ref[...], preferred_element_type=F32)
        o_ref[rs, :] = jnp.dot(h, w_ref[...], preferred_element_type=F32).astype(o_ref.dtype)


def _inproj(x2, gain, mod, w_b, vd_b, rows):
    m, d = x2.shape
    n = w_b.shape[1]
    tm = _row_tile(rows, 1024)
    per = rows // tm
    many = mod.shape[0] > 1
    vres = vd_b is not None
    once = pl.Buffered(1)
    in_specs = [pl.BlockSpec((tm, d), lambda i: (i, 0)),
                pl.BlockSpec((1, d), lambda i: (0, 0)),
                pl.BlockSpec((1, 6, d), lambda i: ((i // per) if many else 0, 0, 0)),
                pl.BlockSpec((d, n), lambda i: (0, 0), pipeline_mode=once)]
    args = [x2, gain.reshape(1, d), mod, w_b]
    out_specs = [pl.BlockSpec((tm, n), lambda i: (i, 0))]
    out_shape = [jax.ShapeDtypeStruct((m, n), ACT)]
    if vres:
        in_specs.append(pl.BlockSpec((d, LANES), lambda i: (0, 0)))
        args.append(vd_b)
        out_specs.append(pl.BlockSpec((tm, LANES), lambda i: (i, 0)))
        out_shape.append(jax.ShapeDtypeStruct((m, LANES), F32))
    outs = pl.pallas_call(
        functools.partial(_inproj_kernel, vres=vres, nsub=max(1, tm // 256)),
        grid=(m // tm,),
        in_specs=in_specs, out_specs=out_specs, out_shape=out_shape,
        compiler_params=_cparams("parallel"),
        name="in_proj",
    )(*args)
    return (outs[0], outs[1]) if vres else (outs[0], None)


def _qkprep_kernel(*refs, rope):
    if rope:
        q0_ref, q1_ref, k_ref, v_ref, cs_ref, qg_ref, kg_ref, e_ref, qn_ref, kn_ref, vb_ref = refs
    else:
        q0_ref, q1_ref, k_ref, v_ref, qg_ref, kg_ref, e_ref, qn_ref, kn_ref, vb_ref = refs
    e = e_ref[...]
    lane = lax.broadcasted_iota(jnp.int32, (1, LANES), 1)
    first = (lane & 31) < 16

    def norm_rope(u, gain):
        y = u * lax.rsqrt(_segsum(u * u, e) * (1.0 / HEAD_DIM) + RMS_EPS) * gain
        if rope:
            rot = jnp.where(first, -pltpu.roll(y, LANES - 16, 1), pltpu.roll(y, 16, 1))
            y = y * cs_ref[0] + rot * cs_ref[1]
        return y

    qg = qg_ref[...]
    for s, ref in enumerate((q0_ref, q1_ref)):
        for t in range(2):
            u = ref[:, t * LANES:(t + 1) * LANES].astype(F32)
            c0 = (2 * s + t) * LANES
            qn_ref[:, c0:c0 + LANES] = (norm_rope(u, qg) * ATT_SCALE).astype(BF16)
    low = lane < HEAD_DIM
    for src_val, dst in ((norm_rope(k_ref[...].astype(F32), kg_ref[...]), kn_ref), (v_ref[...].astype(F32), vb_ref)):
        swapped = pltpu.roll(src_val, HEAD_DIM, 1)
        dst[:, 0:LANES] = jnp.where(low, src_val, swapped).astype(BF16)
        dst[:, LANES:2 * LANES] = jnp.where(low, swapped, src_val).astype(BF16)


def _qk_prep(p, cs, qg, kg, e128, rows, rope):
    m = p.shape[0]
    tm = _row_tile(rows, 512)
    per = rows // tm
    qb = ATT_OFF // 256
    kb = (ATT_OFF + ATT_DIM) // LANES
    in_specs = [pl.BlockSpec((tm, 256), lambda i: (i, qb)),
                pl.BlockSpec((tm, 256), lambda i: (i, qb + 1)),
                pl.BlockSpec((tm, LANES), lambda i: (i, kb)),
                pl.BlockSpec((tm, LANES), lambda i: (i, kb + 1))]
    args = [p, p, p, p]
    if rope:
        in_specs.append(pl.BlockSpec((2, tm, LANES), lambda i: (0, i % per, 0)))
        args.append(cs)
    in_specs += [pl.BlockSpec((1, LANES), lambda i: (0, 0)),
                 pl.BlockSpec((1, LANES), lambda i: (0, 0)),
                 pl.BlockSpec((LANES, LANES), lambda i: (0, 0))]
    args += [qg, kg, e128]
    return pl.pallas_call(
        functools.partial(_qkprep_kernel, rope=rope),
        grid=(m // tm,),
        in_specs=in_specs,
        out_specs=[pl.BlockSpec((tm, ATT_DIM), lambda i: (i, 0)),
                   pl.BlockSpec((tm, 2 * KV_DIM), lambda i: (i, 0)),
                   pl.BlockSpec((tm, 2 * KV_DIM), lambda i: (i, 0))],
        out_shape=[jax.ShapeDtypeStruct((m, ATT_DIM), BF16),
                   jax.ShapeDtypeStruct((m, 2 * KV_DIM), BF16),
                   jax.ShapeDtypeStruct((m, 2 * KV_DIM), BF16)],
        compiler_params=_cparams("parallel"),
        name="qk_prep",
    )(*args)


def _attn_kernel(*refs, nq, local):
    if local:
        sink_ref, q_ref, kp_ref, kc_ref, kn_ref, vp_ref, vc_ref, vn_ref, kx_ref, vx_ref, wm_ref, o_ref = refs
        k = jnp.concatenate([kp_ref[...], kc_ref[...], kn_ref[...], kx_ref[...]], axis=0)
        v = jnp.concatenate([vp_ref[...], vc_ref[...], vn_ref[...], vx_ref[...]], axis=0)
    else:
        sink_ref, q_ref, kx_ref, vx_ref, o_ref = refs
        k = kx_ref[...]
        v = vx_ref[...]
    i = pl.program_id(1)
    bq = q_ref.shape[0]
    low = lax.broadcasted_iota(jnp.int32, (1, LANES), 1) < HEAD_DIM
    row = lax.broadcasted_iota(jnp.int32, (4 * bq, 1), 0)
    if local:
        kblk = lax.broadcasted_iota(jnp.int32, (1, 3 * bq), 1) // bq
        off_end = ((kblk == 0) & (i == 0)) | ((kblk == 2) & (i == nq - 1))
        bias = wm_ref[...] + jnp.where(off_end, NEG_INF, 0.0)
    for g in range(2):
        kb = k[:, g * LANES:(g + 1) * LANES]
        vb = v[:, g * LANES:(g + 1) * LANES]
        qa = q_ref[:, (2 * g) * LANES:(2 * g + 1) * LANES].astype(F32)
        qb = q_ref[:, (2 * g + 1) * LANES:(2 * g + 2) * LANES].astype(F32)
        qs = jnp.concatenate([jnp.where(low, qa, 0.0), jnp.where(low, 0.0, qa),
                              jnp.where(low, qb, 0.0), jnp.where(low, 0.0, qb)], axis=0).astype(BF16)
        s = lax.dot_general(qs, kb, (((1,), (1,)), ((), ())), preferred_element_type=F32)
        if local:
            s = jnp.concatenate([s[:, :3 * bq] + bias, s[:, 3 * bq:]], axis=1)
        sk = jnp.where(row < bq, sink_ref[0, 4 * g],
                       jnp.where(row < 2 * bq, sink_ref[0, 4 * g + 1],
                                 jnp.where(row < 3 * bq, sink_ref[0, 4 * g + 2], sink_ref[0, 4 * g + 3])))
        mx = jnp.maximum(jnp.max(s, axis=1, keepdims=True), sk)
        ex = jnp.exp(s - mx)
        den = jnp.sum(ex, axis=1, keepdims=True) + jnp.exp(sk - mx)
        o = jnp.dot(ex.astype(BF16), vb, preferred_element_type=F32) / den
        o_ref[:, (2 * g) * LANES:(2 * g + 1) * LANES] = jnp.where(low, o[0:bq], o[bq:2 * bq]).astype(o_ref.dtype)
        o_ref[:, (2 * g + 1) * LANES:(2 * g + 2) * LANES] = jnp.where(
            low, o[2 * bq:3 * bq], o[3 * bq:4 * bq]).astype(o_ref.dtype)


def _attention(qn, kn, vb, kctx, vctx, sink, nb, rows, cn, local):
    m = qn.shape[0]
    bq = 128
    nq = rows // bq
    smem = pl.BlockSpec(memory_space=pltpu.SMEM)
    qspec = pl.BlockSpec((bq, ATT_DIM), lambda b, i: (b * nq + i, 0))
    cspec = pl.BlockSpec((cn, 2 * KV_DIM), lambda b, i: (b, 0))
    if local:
        prv = pl.BlockSpec((bq, 2 * KV_DIM), lambda b, i: (b * nq + jnp.maximum(i - 1, 0), 0))
        cur = pl.BlockSpec((bq, 2 * KV_DIM), lambda b, i: (b * nq + i, 0))
        nxt = pl.BlockSpec((bq, 2 * KV_DIM), lambda b, i: (b * nq + jnp.minimum(i + 1, nq - 1), 0))
        qo = jnp.arange(4 * bq)[:, None] % bq
        kcol = jnp.arange(3 * bq)[None, :]
        inwin = ((kcol // bq == 1) | ((kcol // bq == 0) & (kcol % bq >= qo)) | ((kcol // bq == 2) & (kcol % bq <= qo)))
        wmask = jnp.where(inwin, 0.0, NEG_INF).astype(F32)
        in_specs = [smem, qspec, prv, cur, nxt, prv, cur, nxt, cspec, cspec,
                    pl.BlockSpec((4 * bq, 3 * bq), lambda b, i: (0, 0))]
        args = [sink, qn, kn, kn, kn, vb, vb, vb, kctx, vctx, wmask]
    else:
        in_specs = [smem, qspec, cspec, cspec]
        args = [sink, qn, kctx, vctx]
    return pl.pallas_call(
        functools.partial(_attn_kernel, nq=nq, local=local),
        grid=(nb, nq),
        in_specs=in_specs,
        out_specs=pl.BlockSpec((bq, ATT_DIM), lambda b, i: (b * nq + i, 0)),
        out_shape=jax.ShapeDtypeStruct((m, ATT_DIM), BF16),
        compiler_params=_cparams("parallel", "parallel"),
        name="window_attn" if local else "ctx_attn",
    )(*args)


def _rwkv_kernel(*refs, nt, tq, vres, drn, tps):
    it = iter(refs)
    n_in = 5 if vres else 3
    first = [next(it) for _ in range(n_in)] if tps == 2 else None
    tiles = [[next(it) for _ in range(n_in)] for _ in range(tps)]
    ts_ref, pd_ref, ps_ref, wla_ref, wg_ref = (next(it) for _ in range(5))
    if vres:
        vuh_ref, vul_ref = next(it), next(it)
    e_ref, sin_ref = next(it), next(it)
    y_ref, aux_ref, sout_ref = next(it), next(it), next(it)
    ops = [next(it) for _ in range(tps)]
    auxs = [next(it) for _ in range(tps)]
    s_sc = next(it)

    j = pl.program_id(1)
    sgn = 1 - 2 * drn
    nch = tq // CHUNK
    nsteps = nt // tps
    e = e_ref[...]
    lane = lax.broadcasted_iota(jnp.int32, (1, LANES), 1)
    low = lane < HEAD_DIM

    def natural(k):
        return k if drn == 0 else nt - 1 - k

    def prepare(tile, in_refs, ops_sc, aux_sc):
        p_ref, pv_ref, nx_ref = in_refs[:3]
        rw = p_ref[...].astype(F32)
        row = lax.broadcasted_iota(jnp.int32, (tq, 1), 0)
        prev_row = jnp.where(tile > 0, pv_ref[PACK - 1:PACK, :].astype(F32), 0.0)
        next_row = jnp.where(tile < nt - 1, nx_ref[0:1, :].astype(F32), 0.0)
        prv = jnp.where(row == 0, prev_row, pltpu.roll(rw, 1, 0))
        yield
        nxt = jnp.where(row == tq - 1, next_row, pltpu.roll(rw, tq - 1, 0))
        xs = rw + ts_ref[0:1, :] * (prv - rw) + ts_ref[1:2, :] * (nxt - rw)
        yield

        r = xs[:, 0:RW_DIM]
        k = xs[:, RW_DIM:2 * RW_DIM]
        v = xs[:, 2 * RW_DIM:3 * RW_DIM]
        la = xs[:, 3 * RW_DIM:3 * RW_DIM + LANES]
        la = jnp.where(low, jnp.tanh(la), la)
        la_hi, la_lo = _split2(la)
        lora = (jnp.dot(la_hi, wla_ref[0], preferred_element_type=F32)
                + jnp.dot(la_lo, wla_ref[0], preferred_element_type=F32))
        yield
        logw = -DECAY_SCALE * jax.nn.sigmoid(pd_ref[0, 0:1, :] + lora[:, 0:RW_DIM])
        iclr = jax.nn.sigmoid(pd_ref[0, 1:2, :] + lora[:, RW_DIM:2 * RW_DIM])
        if vres:
            vf_ref, hv_ref = in_refs[3:]
            mix = jax.nn.sigmoid(ps_ref[3:4, :] + _dot3(hv_ref[...], vuh_ref[...], vul_ref[...]))
            v = v + (vf_ref[...].astype(F32) - v) * mix
        yield
        kk = k * ps_ref[0:1, :]
        kk = kk * lax.rsqrt(_segsum(kk * kk, e, exact=False) + 1e-12)
        yield
        kh = k * (1.0 + (iclr - 1.0) * ps_ref[1:2, :])
        bb = kk * iclr
        aux_sc[:, 0:RW_DIM] = (_segsum(r * kh * ps_ref[2:3, :], e, exact=False) * v).astype(aux_sc.dtype)
        if drn == 0:
            glo = xs[:, 3 * RW_DIM + LANES:RW_IN]
            aux_sc[:, RW_DIM:2 * RW_DIM] = _bdot(jax.nn.sigmoid(glo), wg_ref[...]).astype(aux_sc.dtype)
        else:
            aux_sc[:, RW_DIM:2 * RW_DIM] = v.astype(aux_sc.dtype)

        yield
        ti = lax.broadcasted_iota(jnp.int32, (tq, tq), 0)
        si = lax.broadcasted_iota(jnp.int32, (tq, tq), 1)
        tri = (((ti // CHUNK) == (si // CHUNK)) & (((si - ti) * sgn) <= 0)).astype(BF16)
        lw_hi, lw_lo = _split2(logw)
        cin = jnp.dot(tri, lw_hi, preferred_element_type=F32) + jnp.dot(tri, lw_lo, preferred_element_type=F32)
        cin3 = cin.reshape(nch, CHUNK, RW_DIM)
        last = CHUNK - 1 if drn == 0 else 0
        ctot = jnp.broadcast_to(cin3[:, last:last + 1, :], cin3.shape).reshape(tq, RW_DIM)
        yield
        e_neg = jnp.exp(-cin)
        e_rem = jnp.exp(ctot - cin)
        ops_sc[0] = kk * jnp.exp(cin - logw)
        ops_sc[1] = r * jnp.exp(cin)
        ops_sc[2] = bb * e_neg
        ops_sc[3] = kh * e_neg
        yield
        ops_sc[4] = kh * e_rem
        ops_sc[5] = bb * e_rem
        ops_sc[6] = v
        ops_sc[7] = jnp.exp(ctot)

    tr = lax.broadcasted_iota(jnp.int32, (CHUNK, LANES), 0)
    sc = lax.broadcasted_iota(jnp.int32, (CHUNK, LANES), 1) & (CHUNK - 1)
    rel = (sc - tr) * sgn
    strict = rel < 0
    incl = rel <= 0
    eye = (sc == tr).astype(F32)
    same_head = ((lax.broadcasted_iota(jnp.int32, (LANES, LANES), 0) // HEAD_DIM)
                 == (lax.broadcasted_iota(jnp.int32, (LANES, LANES), 1) // HEAD_DIM))

    def stack(x):
        return jnp.concatenate([jnp.where(low, x, 0.0), jnp.where(low, 0.0, x)], axis=0)

    def hdot(x, y):
        return _bdot(x, stack(y))

    def scan_tile(ops_sc, row0):
        st = [s_sc[pr] for pr in range(N_PAIRS)]
        for g0 in range(0, nch, SCAN_GROUP):
            yield from scan_chunks(ops_sc, row0, st, range(g0, g0 + SCAN_GROUP))
        for pr in range(N_PAIRS):
            s_sc[pr] = st[pr]

    def scan_chunks(ops_sc, row0, st, chunk_ids):
        probs = []
        for ii in chunk_ids:
            off = (ii if drn == 0 else nch - 1 - ii) * CHUNK
            for pr in range(N_PAIRS):
                ls = slice(pr * LANES, (pr + 1) * LANES)
                q = {"off": off, "ls": ls, "pr": pr}
                for n, name in enumerate(("kkt", "rt", "bh", "kh", "kp", "bp", "v")):
                    q[name] = ops_sc[n, off:off + CHUNK, ls]
                q["wl"] = ops_sc[7, off:off + 1, ls]
                probs.append(q)
        for q in probs:
            g = _bdot_nt(jnp.concatenate([q["kkt"], q["rt"]], axis=0),
                         jnp.concatenate([stack(q["bh"]), stack(q["kh"])], axis=0))
            q["nn"] = jnp.where(strict, -g[0:CHUNK, 0:LANES], 0.0)
            q["avk"] = jnp.where(strict, g[0:CHUNK, LANES:], 0.0)
            q["arb"] = jnp.where(incl, g[CHUNK:, 0:LANES], 0.0)
            q["ark"] = jnp.where(incl, g[CHUNK:, LANES:], 0.0)
            q["t"] = eye + q["nn"]
        yield
        for q in probs:
            q["q"] = hdot(q["nn"], q["nn"])
        for _ in range(4):
            for q in probs:
                tq2 = _bdot(jnp.concatenate([q["t"], q["q"]], axis=0), stack(q["q"]))
                q["t"] = q["t"] + tq2[:CHUNK]
                q["q"] = tq2[CHUNK:]
            yield
        for q in probs:
            q["t"] = q["t"] + hdot(q["t"], q["q"])
        for q in probs:
            av = _bdot(jnp.concatenate([q["avk"], q["ark"]], axis=0), stack(q["v"]))
            q["av"], q["arkv"] = av[:CHUNK], av[CHUNK:]
        yield
        for q in probs:
            tt2 = _bdot(q["t"], jnp.concatenate([stack(q["kkt"]), stack(q["av"])], axis=1))
            q["tk"], q["tav"] = tt2[:, :LANES], tt2[:, LANES:]
        yield
        for q in probs:
            tn = _bdot_tn(jnp.concatenate([q["tk"], q["tav"]], axis=1), q["bp"])
            q["mb"] = jnp.where(same_head, tn[:LANES], 0.0)
            q["nm"] = jnp.where(same_head, _bdot_tn(q["v"], q["kp"]) - tn[LANES:], 0.0)
            z = _bdot(q["arb"], jnp.concatenate([stack(q["tk"]), stack(q["tav"])], axis=1))
            q["r2"] = q["rt"] - z[:, :LANES]
            q["y0"] = q["arkv"] - z[:, LANES:]
        yield

        outs = []
        for q in probs:
            sm = st[q["pr"]]
            outs.append(_bdot_nt(q["r2"], sm) + q["y0"])
            st[q["pr"]] = sm * q["wl"] - _bdot(sm, q["mb"]) + q["nm"]
        for q, yo in zip(probs, outs):
            y_ref[row0 + q["off"]:row0 + q["off"] + CHUNK, q["ls"]] = yo.astype(y_ref.dtype)

    def run(*stages):
        live = list(stages)
        while live:
            for g in list(live):
                if next(g, live) is live:
                    live.remove(g)

    @pl.when(j == 0)
    def _():
        s_sc[...] = sin_ref[0]

    if tps == 1:
        run(prepare(natural(j), tiles[0], ops[0], auxs[0]))
        aux_ref[...] = auxs[0][...]
        run(scan_tile(ops[0], 0))
    else:
        @pl.when(j == 0)
        def _():
            run(prepare(natural(0), first, ops[0], auxs[0]))

        rows = [0, tq] if drn == 0 else [tq, 0]
        aux_ref[rows[0]:rows[0] + tq, :] = auxs[0][...]
        run(scan_tile(ops[0], rows[0]), prepare(natural(2 * j + 1), tiles[0], ops[1], auxs[1]))
        aux_ref[rows[1]:rows[1] + tq, :] = auxs[1][...]
        run(scan_tile(ops[1], rows[1]),
            prepare(natural(jnp.minimum(2 * j + 2, nt - 1)), tiles[1], ops[0], auxs[0]))

    @pl.when(j == nsteps - 1)
    def _():
        sout_ref[0] = s_sc[...]


def _rwkv(p, vf_src, hv, state_in, prm, nb, rows, drn):
    m = p.shape[0]
    tq = _row_tile(rows, 256)
    nt = rows // tq
    tps = 2 if nt % 2 == 0 else 1
    nsteps = nt // tps
    rb = tq // PACK
    vres = vf_src is not None

    def tile_specs(scan_pos):
        def tile(b, j):
            k = jnp.minimum(scan_pos(j), nt - 1)
            return b * nt + (k if drn == 0 else nt - 1 - k)

        specs = [pl.BlockSpec((tq, RW_IN), lambda b, j: (tile(b, j), 0)),
                 pl.BlockSpec((PACK, RW_IN), lambda b, j: (jnp.maximum(tile(b, j) * rb - 1, 0), 0)),
                 pl.BlockSpec((PACK, RW_IN), lambda b, j: (jnp.minimum((tile(b, j) + 1) * rb, m // PACK - 1), 0))]
        args = [p, p, p]
        if vres:
            specs += [pl.BlockSpec((tq, RW_DIM), lambda b, j: (tile(b, j), 1)),
                      pl.BlockSpec((tq, LANES), lambda b, j: (tile(b, j), 0))]
            args += [vf_src, hv]
        return specs, args

    if tps == 2:
        positions = [lambda j: 0 * j, lambda j: 2 * j + 1, lambda j: 2 * j + 2]
    else:
        positions = [lambda j: j]
    in_specs, args = [], []
    for pos in positions:
        s_, a_ = tile_specs(pos)
        in_specs += s_
        args += a_

    const2 = lambda b, j: (0, 0)
    dir3 = lambda b, j: (drn, 0, 0)
    in_specs += [pl.BlockSpec((2, RW_IN), const2),
                 pl.BlockSpec((1, SUBLANES, RW_DIM), dir3),
                 pl.BlockSpec((SUBLANES, RW_DIM), const2),
                 pl.BlockSpec((1, LANES, 2 * RW_DIM), dir3),
                 pl.BlockSpec((LANES, RW_DIM), const2)]
    args += [prm["tshift"], prm["pd"], prm["ps"], prm["wla"], prm["wg"]]
    if vres:
        in_specs += [pl.BlockSpec((LANES, RW_DIM), const2), pl.BlockSpec((LANES, RW_DIM), const2)]
        args += [prm["vu_hi"], prm["vu_lo"]]
    sspec = pl.BlockSpec((1, N_PAIRS, LANES, LANES), lambda b, j: (b, 0, 0, 0))
    in_specs += [pl.BlockSpec((LANES, LANES), const2), sspec]
    args += [prm["e128"], state_in]

    def out_block(b, j):
        return (b * nsteps + (j if drn == 0 else nsteps - 1 - j), 0)

    return pl.pallas_call(
        functools.partial(_rwkv_kernel, nt=nt, tq=tq, vres=vres, drn=drn, tps=tps),
        grid=(nb, nsteps),
        in_specs=in_specs,
        out_specs=[pl.BlockSpec((tps * tq, RW_DIM), out_block),
                   pl.BlockSpec((tps * tq, 2 * RW_DIM), out_block),
                   sspec],
        out_shape=[jax.ShapeDtypeStruct((m, RW_DIM), ACT),
                   jax.ShapeDtypeStruct((m, 2 * RW_DIM), ACT),
                   jax.ShapeDtypeStruct((nb, N_PAIRS, LANES, LANES), F32)],
        scratch_shapes=([pltpu.VMEM((8, tq, RW_DIM), F32)] * tps + [pltpu.VMEM((tq, 2 * RW_DIM), ACT)] * tps
                        + [pltpu.VMEM((N_PAIRS, LANES, LANES), F32)]),
        compiler_params=_cparams("parallel", "arbitrary"),
        name="rwkv7_fwd" if drn == 0 else "rwkv7_bwd",
    )(*args)


def _rwkv_both(p, vf_src, hv, states, prm, nb, rows):
    outs = [_rwkv(p, vf_src, hv, states[d], prm, nb, rows, d) for d in range(2)]
    return [o[0] for o in outs], [o[1] for o in outs], [o[2] for o in outs]


def _merge_kernel(x_ref, y0_ref, y1_ref, a0_ref, a1_ref, ya_ref, g0_ref, g1_ref, g2_ref, g3_ref, ln_ref, e_ref,
                  wbr_ref, wba_ref, wo_ref, mod_ref, o_ref):
    e = e_ref[...]
    y = y0_ref[...].astype(F32) + y1_ref[...].astype(F32)
    mu = _segsum(y, e) * (1.0 / HEAD_DIM)
    yc = y - mu
    var = _segsum(yc * yc, e) * (1.0 / HEAD_DIM)
    yn = yc * lax.rsqrt(var + GN_EPS) * ln_ref[0:1, :] + ln_ref[1:2, :]
    cv = a0_ref[:, 0:RW_DIM].astype(F32) + a1_ref[:, 0:RW_DIM].astype(F32)
    yrw = (yn + cv) * a0_ref[:, RW_DIM:2 * RW_DIM].astype(F32)
    a = _bdot(yrw, wbr_ref[...])
    b = jnp.dot(ya_ref[...], wba_ref[...], preferred_element_type=F32)
    h = a.shape[1] // 2
    mrg = jnp.concatenate(
        [jax.nn.sigmoid(g0_ref[...].astype(F32)) * a[:, :h] + jax.nn.sigmoid(g2_ref[...].astype(F32)) * b[:, :h],
         jax.nn.sigmoid(g1_ref[...].astype(F32)) * a[:, h:] + jax.nn.sigmoid(g3_ref[...].astype(F32)) * b[:, h:]],
        axis=1)
    o_ref[...] = x_ref[...] + mod_ref[0, 2:3, :] * _bdot(mrg, wo_ref[...])


def _merge(x2, ys, auxs, ya, p, ln, e128, wbr, wba, wo, mod, rows):
    m, d = x2.shape
    tm = _row_tile(rows, 512)
    per = rows // tm
    many = mod.shape[0] > 1
    gb = GATE_OFF // 512
    const2 = lambda i: (0, 0)
    in_specs = [pl.BlockSpec((tm, d), lambda i: (i, 0)),
                pl.BlockSpec((tm, RW_DIM), lambda i: (i, 0)),
                pl.BlockSpec((tm, RW_DIM), lambda i: (i, 0)),
                pl.BlockSpec((tm, 2 * RW_DIM), lambda i: (i, 0)),
                pl.BlockSpec((tm, 2 * RW_DIM), lambda i: (i, 0)),
                pl.BlockSpec((tm, ATT_DIM), lambda i: (i, 0))]
    in_specs += [pl.BlockSpec((tm, 512), (lambda i, c=c: (i, gb + c))) for c in range(4)]
    in_specs += [pl.BlockSpec((2, RW_DIM), const2),
                 pl.BlockSpec((LANES, LANES), const2),
                 pl.BlockSpec(wbr.shape, const2),
                 pl.BlockSpec(wba.shape, const2),
                 pl.BlockSpec(wo.shape, const2),
                 pl.BlockSpec((1, 6, d), lambda i: ((i // per) if many else 0, 0, 0))]
    return pl.pallas_call(
        _merge_kernel,
        grid=(m // tm,),
        in_specs=in_specs,
        out_specs=pl.BlockSpec((tm, d), lambda i: (i, 0)),
        out_shape=jax.ShapeDtypeStruct((m, d), F32),
        compiler_params=_cparams("parallel"),
        name="branch_merge",
    )(x2, ys[0], ys[1], auxs[0], auxs[1], ya, p, p, p, p, ln, e128, wbr, wba, wo, mod)


def _ffn_kernel(x_ref, g_ref, mod_ref, wg_ref, wu_ref, wd_ref, o_ref, h_sc, acc_sc, *, nf):
    f = pl.program_id(1)

    @pl.when(f == 0)
    def _():
        h = _norm_mod(x_ref[...], g_ref[...], mod_ref[0, 3:4, :], mod_ref[0, 4:5, :])
        h_sc[...] = h.astype(BF16)
        acc_sc[...] = jnp.zeros_like(acc_sc)

    h = h_sc[...]
    a = jnp.dot(h, wg_ref[...], preferred_element_type=F32)
    u = jnp.dot(h, wu_ref[...], preferred_element_type=F32)
    act = (a * jax.nn.sigmoid(a)) * u
    acc_sc[...] += jnp.dot(act.astype(BF16), wd_ref[...], preferred_element_type=F32)

    @pl.when(f == nf - 1)
    def _():
        o_ref[...] = x_ref[...] + mod_ref[0, 5:6, :] * acc_sc[...]


def _ffn(x2, gain, mod, wg, wu, wd, rows):
    m, d = x2.shape
    dff = wg.shape[1]
    tm = _row_tile(rows, 512)
    per = rows // tm
    many = mod.shape[0] > 1
    tf = dff
    nf = dff // tf
    once = pl.Buffered(1)
    return pl.pallas_call(
        functools.partial(_ffn_kernel, nf=nf),
        grid=(m // tm, nf),
        in_specs=[pl.BlockSpec((tm, d), lambda i, f: (i, 0)),
                  pl.BlockSpec((1, d), lambda i, f: (0, 0)),
                  pl.BlockSpec((1, 6, d), lambda i, f: ((i // per) if many else 0, 0, 0)),
                  pl.BlockSpec((d, tf), lambda i, f: (0, f), pipeline_mode=once),
                  pl.BlockSpec((d, tf), lambda i, f: (0, f), pipeline_mode=once),
                  pl.BlockSpec((tf, d), lambda i, f: (f, 0), pipeline_mode=once)],
        out_specs=pl.BlockSpec((tm, d), lambda i, f: (i, 0)),
        out_shape=jax.ShapeDtypeStruct((m, d), F32),
        scratch_shapes=[pltpu.VMEM((tm, d), BF16), pltpu.VMEM((tm, d), F32)],
        compiler_params=_cparams("parallel", "arbitrary"),
        name="ffn_swiglu",
    )(x2, gain.reshape(1, d), mod, wg, wu, wd)


def _dma_pieces(count, sizes, make):
    for k, size in enumerate(sizes):
        @pl.when((count & size) != 0)
        def _(k=k, size=size):
            make(k, pl.multiple_of(count & ~(2 * size - 1), PACK), size)


def _moe_gather_kernel(pos_ref, cnt_ref, h_ref, rankt_ref, zero_ref, out_ref, hs_sc, sem, *, ne, nsteps, sizes):
    del zero_ref
    i = pl.program_id(0)
    ex = pl.program_id(1)
    step = i * ne + ex
    slot = step % 2
    tt = h_ref.shape[0]

    def copies(st, sl, act):
        base = pos_ref[st]

        def make(k, off, size):
            act(pltpu.make_async_copy(hs_sc.at[sl, pl.ds(off, size)],
                                      out_ref.at[pl.ds(pl.multiple_of(base + off, PACK), size)], sem.at[sl, k]))

        _dma_pieces(cnt_ref[st], sizes, make)

    @pl.when(step >= 2)
    def _():
        copies(step - 2, slot, lambda cp: cp.wait())

    rrow = rankt_ref[0, pl.ds(ex, 1), :]
    rowid = lax.broadcasted_iota(jnp.int32, (MOE_BLK, tt), 0).astype(F32)

    def gather(b, carry):
        sel = ((rrow - (b * MOE_BLK).astype(F32)) == rowid).astype(BF16)
        r0 = pl.multiple_of(b * MOE_BLK, MOE_BLK)
        hs_sc[slot, pl.ds(r0, MOE_BLK), :] = jnp.dot(sel, h_ref[...], preferred_element_type=F32).astype(BF16)
        return carry

    lax.fori_loop(0, (cnt_ref[step] + MOE_BLK - 1) // MOE_BLK, gather, 0)
    copies(step, slot, lambda cp: cp.start())

    @pl.when(step == nsteps - 1)
    def _():
        if nsteps >= 2:
            copies(step - 1, 1 - slot, lambda cp: cp.wait())
        copies(step, slot, lambda cp: cp.wait())


def _moe_ffn_kernel(te_ref, tm_ref, tv_ref, hs_ref, wg_ref, wu_ref, wd_ref, os_ref, acc_sc, *, nf):
    del te_ref, tm_ref
    t = pl.program_id(0)
    f = pl.program_id(1)
    valid = tv_ref[t] != 0

    @pl.when(valid)
    def _():
        hb = hs_ref[...]
        a = jnp.dot(hb, wg_ref[0], preferred_element_type=F32)
        u = jnp.dot(hb, wu_ref[0], preferred_element_type=F32)
        act = ((a * jax.nn.sigmoid(a)) * u).astype(BF16)
        part = jnp.dot(act, wd_ref[0], preferred_element_type=F32)

        @pl.when(f == 0)
        def _():
            acc_sc[...] = part

        @pl.when(f > 0)
        def _():
            acc_sc[...] += part

    @pl.when(f == nf - 1)
    def _():
        os_ref[...] = jnp.where(valid, acc_sc[...], 0.0).astype(os_ref.dtype)


def _moe_combine_kernel(pos_ref, cnt_ref, x_ref, we_ref, rank_ref, mod_ref, os_ref, o_ref, acc_sc, ob_sc, sem,
                        *, ne, nsteps, sizes):
    i = pl.program_id(0)
    ex = pl.program_id(1)
    step = i * ne + ex
    slot = step % 2
    tt = x_ref.shape[0]
    sblk = min(2 * MOE_BLK, tt)
    lane = lax.broadcasted_iota(jnp.int32, (tt, LANES), 1)

    def copies(st, sl, act):
        base = pos_ref[st]

        def make(k, off, size):
            act(pltpu.make_async_copy(os_ref.at[pl.ds(pl.multiple_of(base + off, PACK), size)],
                                      ob_sc.at[sl, pl.ds(off, size)], sem.at[sl, k]))

        _dma_pieces(cnt_ref[st], sizes, make)

    @pl.when(step == 0)
    def _():
        ob_sc[...] = jnp.zeros_like(ob_sc)
        copies(0, 0, lambda cp: cp.start())

    @pl.when(step + 1 < nsteps)
    def _():
        copies(step + 1, 1 - slot, lambda cp: cp.start())

    copies(step, slot, lambda cp: cp.wait())

    @pl.when(ex == 0)
    def _():
        acc_sc[...] = jnp.zeros_like(acc_sc)

    rcol = jnp.sum(jnp.where(lane == ex, rank_ref[...], 0.0), axis=1, keepdims=True)
    wcol = jnp.sum(jnp.where(lane == ex, we_ref[...], 0.0), axis=1, keepdims=True)
    colid = lax.broadcasted_iota(jnp.int32, (tt, sblk), 1).astype(F32)

    def scatter(b, carry):
        sel = ((rcol - (b * sblk).astype(F32)) == colid).astype(BF16)
        r0 = pl.multiple_of(b * sblk, sblk)
        acc_sc[...] += wcol * jnp.dot(sel, ob_sc[slot, pl.ds(r0, sblk), :], preferred_element_type=F32)
        return carry

    lax.fori_loop(0, (cnt_ref[step] + sblk - 1) // sblk, scatter, 0)

    @pl.when(ex == ne - 1)
    def _():
        o_ref[...] = x_ref[...] + mod_ref[0, 5:6, :] * acc_sc[...]


def _route_kernel(x_ref, g_ref, mod_ref, rt_ref, h_ref, we_ref, rank_ref, rankt_ref, cnt_ref, *, ne):
    tt = x_ref.shape[0]
    h = _norm_mod(x_ref[...], g_ref[...], mod_ref[0, 3:4, :], mod_ref[0, 4:5, :])
    h_ref[...] = h.astype(BF16)
    lane = lax.broadcasted_iota(jnp.int32, (tt, LANES), 1)
    logits = jnp.dot(h, rt_ref[...], precision=HIGHEST, preferred_element_type=F32)
    lg = jnp.where(lane < ne, logits, NEG_INF)
    t1 = jnp.max(lg, axis=1, keepdims=True)
    i1 = jnp.min(jnp.where(lg == t1, lane, LANES), axis=1, keepdims=True)
    lg2 = jnp.where(lane == i1, NEG_INF, lg)
    t2 = jnp.max(lg2, axis=1, keepdims=True)
    i2 = jnp.min(jnp.where(lg2 == t2, lane, LANES), axis=1, keepdims=True)
    e2 = jnp.exp(t2 - t1)
    den = 1.0 + e2
    we_ref[...] = jnp.where(lane == i1, 1.0 / den, 0.0) + jnp.where(lane == i2, e2 / den, 0.0)
    sel = (lane == i1) | (lane == i2)
    self32 = sel.astype(F32)
    before = (lax.broadcasted_iota(jnp.int32, (tt, tt), 1) < lax.broadcasted_iota(jnp.int32, (tt, tt), 0))
    rank = jnp.dot(before.astype(BF16), sel.astype(BF16), preferred_element_type=F32)
    rank = jnp.where(sel, rank, -1.0)
    rank_ref[...] = rank
    rankt_ref[0] = rank.T[0:SUBLANES, :]
    cnt_ref[0] = jnp.broadcast_to(jnp.sum(self32, axis=0, keepdims=True), (SUBLANES, LANES))


def _moe(x2, gain, mod, router_pad, wg, wu, wd, rows):
    m, d = x2.shape
    ne, _, dff = wg.shape
    assert ne <= SUBLANES
    tt = _row_tile(rows, MOE_TILE)
    nti = m // tt
    per = rows // tt
    many = mod.shape[0] > 1
    h, we, rank, rankt, cnt = pl.pallas_call(
        functools.partial(_route_kernel, ne=ne),
        grid=(nti,),
        in_specs=[pl.BlockSpec((tt, d), lambda i: (i, 0)),
                  pl.BlockSpec((1, d), lambda i: (0, 0)),
                  pl.BlockSpec((1, 6, d), lambda i: ((i // per) if many else 0, 0, 0)),
                  pl.BlockSpec((d, LANES), lambda i: (0, 0))],
        out_specs=[pl.BlockSpec((tt, d), lambda i: (i, 0)),
                   pl.BlockSpec((tt, LANES), lambda i: (i, 0)),
                   pl.BlockSpec((tt, LANES), lambda i: (i, 0)),
                   pl.BlockSpec((1, SUBLANES, tt), lambda i: (i, 0, 0)),
                   pl.BlockSpec((1, SUBLANES, LANES), lambda i: (i, 0, 0))],
        out_shape=[jax.ShapeDtypeStruct((m, d), BF16),
                   jax.ShapeDtypeStruct((m, LANES), F32),
                   jax.ShapeDtypeStruct((m, LANES), F32),
                   jax.ShapeDtypeStruct((nti, SUBLANES, tt), F32),
                   jax.ShapeDtypeStruct((nti, SUBLANES, LANES), F32)],
        compiler_params=_cparams("parallel"),
        name="moe_route",
    )(x2, gain.reshape(1, d), mod, router_pad)
    c16 = (cnt[:, 0, :ne].astype(jnp.int32) + (PACK - 1)) // PACK * PACK
    region = (jnp.sum(c16, axis=0) + (MOE_FT - 1)) // MOE_FT * MOE_FT
    e_off = jnp.cumsum(region) - region
    pos = (e_off[None, :] + jnp.cumsum(c16, axis=0) - c16).reshape(-1).astype(jnp.int32)
    c16 = c16.reshape(-1)
    rmax = -(-(2 * m + nti * ne * PACK + ne * MOE_FT) // MOE_FT) * MOE_FT
    nft = rmax // MOE_FT
    tiles_e = jnp.cumsum(region // MOE_FT)
    tid = jnp.arange(nft, dtype=jnp.int32)
    tile_valid = (tid < tiles_e[-1]).astype(jnp.int32)
    tile_map = jnp.minimum(tid, tiles_e[-1] - 1).astype(jnp.int32)
    tile_exp = jnp.minimum(jnp.sum(tile_map[:, None] >= tiles_e[None, :], axis=1), ne - 1).astype(jnp.int32)

    sizes = tuple(tt >> k for k in range(tt.bit_length()) if (tt >> k) >= PACK)
    nsteps = nti * ne
    hs = pl.pallas_call(
        functools.partial(_moe_gather_kernel, ne=ne, nsteps=nsteps, sizes=sizes),
        grid_spec=pltpu.PrefetchScalarGridSpec(
            num_scalar_prefetch=2,
            grid=(nti, ne),
            in_specs=[pl.BlockSpec((tt, d), lambda i, e, p_, c_: (i, 0)),
                      pl.BlockSpec((1, SUBLANES, tt), lambda i, e, p_, c_: (i, 0, 0)),
                      pl.BlockSpec(memory_space=pl.ANY)],
            out_specs=pl.BlockSpec(memory_space=pl.ANY),
            scratch_shapes=[pltpu.VMEM((2, tt, d), BF16), pltpu.SemaphoreType.DMA((2, len(sizes)))]),
        out_shape=jax.ShapeDtypeStruct((rmax, d), BF16),
        input_output_aliases={4: 0},
        compiler_params=_cparams("arbitrary", "arbitrary"),
        name="moe_gather",
    )(pos, c16, h, rankt, jnp.zeros((rmax, d), BF16))

    tf = dff
    once = pl.Buffered(1)
    nf = dff // tf
    os_sorted = pl.pallas_call(
        functools.partial(_moe_ffn_kernel, nf=nf),
        grid_spec=pltpu.PrefetchScalarGridSpec(
            num_scalar_prefetch=3,
            grid=(nft, nf),
            in_specs=[pl.BlockSpec((MOE_FT, d), lambda t, f, te, tm, tv: (tm[t], 0)),
                      pl.BlockSpec((1, d, tf), lambda t, f, te, tm, tv: (te[t], 0, f), pipeline_mode=once),
                      pl.BlockSpec((1, d, tf), lambda t, f, te, tm, tv: (te[t], 0, f), pipeline_mode=once),
                      pl.BlockSpec((1, tf, d), lambda t, f, te, tm, tv: (te[t], f, 0), pipeline_mode=once)],
            out_specs=pl.BlockSpec((MOE_FT, d), lambda t, f, te, tm, tv: (t, 0)),
            scratch_shapes=[pltpu.VMEM((MOE_FT, d), F32)]),
        out_shape=jax.ShapeDtypeStruct((rmax, d), BF16),
        compiler_params=pltpu.CompilerParams(dimension_semantics=("parallel", "arbitrary"),
                                             vmem_limit_bytes=58 * 1024 * 1024),
        name="moe_experts",
    )(tile_exp, tile_map, tile_valid, hs, wg, wu, wd)

    return pl.pallas_call(
        functools.partial(_moe_combine_kernel, ne=ne, nsteps=nsteps, sizes=sizes),
        grid_spec=pltpu.PrefetchScalarGridSpec(
            num_scalar_prefetch=2,
            grid=(nti, ne),
            in_specs=[pl.BlockSpec((tt, d), lambda i, e, p_, c_: (i, 0)),
                      pl.BlockSpec((tt, LANES), lambda i, e, p_, c_: (i, 0)),
                      pl.BlockSpec((tt, LANES), lambda i, e, p_, c_: (i, 0)),
                      pl.BlockSpec((1, 6, d), lambda i, e, p_, c_: ((i // per) if many else 0, 0, 0)),
                      pl.BlockSpec(memory_space=pl.ANY)],
            out_specs=pl.BlockSpec((tt, d), lambda i, e, p_, c_: (i, 0)),
            scratch_shapes=[pltpu.VMEM((tt, d), F32), pltpu.VMEM((2, tt, d), BF16),
                            pltpu.SemaphoreType.DMA((2, len(sizes)))]),
        out_shape=jax.ShapeDtypeStruct((m, d), F32),
        compiler_params=_cparams("arbitrary", "arbitrary"),
        name="moe_combine",
    )(pos, c16, x2, we, rank, mod, os_sorted)


def _rope_table(t):
    rows = t // GRID_W
    row = jnp.repeat(jnp.arange(rows, dtype=F32), GRID_W)
    col = jnp.tile(jnp.arange(GRID_W, dtype=F32), rows)
    half = HEAD_DIM // 2
    inv_freq = ROPE_THETA ** (-jnp.arange(0, half, 2, dtype=F32) / half)
    ar = row[:, None] * inv_freq
    ac = col[:, None] * inv_freq
    ang = jnp.concatenate([ar, ar, ac, ac] * 2, axis=-1)
    return jnp.stack([jnp.cos(ang), jnp.sin(ang)])


def _pad_rows(a, rows):
    return jnp.zeros((rows,) + a.shape[1:], a.dtype).at[:a.shape[0]].set(a)


def _hi_lo(w):
    hi = w.astype(BF16)
    return hi, (w - hi.astype(F32)).astype(BF16)


def kernel(x, c, ctx, c_ctx, norm1, norm2, w_mod, b_mod, w_in, tshift, w0, w_decay_up, a0, w_iclr_up, v0, w_vres_down, w_vres_up, w_gate_up, k_k, k_a, r_k, lnx_g, lnx_b, q_norm, k_norm, sink, w_br_rwkv, w_br_attn, w_out, ffn_gate, ffn_up, ffn_down, router, moe_gate, moe_up, moe_down):
    nb, t, d = x.shape
    cn = ctx.shape[1]
    depth = w_in.shape[0]
    mods = _modulation(c, c_ctx, w_mod, b_mod)
    cs = _rope_table(t)
    e128 = jnp.kron(jnp.eye(2, dtype=F32), jnp.ones((HEAD_DIM, HEAD_DIM), F32)).astype(BF16)
    xl = x.reshape(nb * t, d)
    xc = ctx.reshape(nb * cn, d)
    vf_l = vf_c = None
    for layer in range(depth):
        last = layer == depth - 1
        mod_l = mods[layer, :nb].reshape(nb, 6, d)
        mod_c = mods[layer, nb:nb + 1].reshape(1, 6, d)
        w_in_b = w_in[layer].astype(BF16)
        vd_b = None
        prm = {"tshift": tshift[layer], "e128": e128, "wg": w_gate_up[layer].astype(BF16)}
        prm["pd"] = jnp.stack(
            [_pad_rows(jnp.stack([w0[layer, dd], a0[layer, dd]]), SUBLANES) for dd in range(2)])
        shared = [k_k[layer], k_a[layer], r_k[layer].reshape(RW_DIM)]
        if layer > 0:
            shared.append(v0[layer - 1])
            vd_b = _pad_rows(w_vres_down[layer - 1].T, LANES).T.astype(BF16)
            prm["vu_hi"], prm["vu_lo"] = _hi_lo(_pad_rows(w_vres_up[layer - 1], LANES))
        prm["ps"] = _pad_rows(jnp.stack(shared), SUBLANES)
        wla = jnp.zeros((2, LANES, 2 * RW_DIM), F32)
        wla = wla.at[:, :HEAD_DIM, :RW_DIM].set(w_decay_up[layer]).at[:, HEAD_DIM:, RW_DIM:].set(w_iclr_up[layer])
        prm["wla"] = wla.astype(BF16)

        p_l, hv_l = _inproj(xl, norm1[layer], mod_l, w_in_b, vd_b, t)
        p_c, hv_c = _inproj(xc, norm1[layer], mod_c, w_in_b, vd_b, cn)

        s0 = jnp.zeros((nb, N_PAIRS, LANES, LANES), F32)
        y_c, aux_c, s_ctx = _rwkv_both(p_c, vf_c, hv_c, [s0, s0], prm, nb, cn)
        y_l, aux_l, _ = _rwkv_both(p_l, vf_l, hv_l, s_ctx, prm, nb, t)
        if layer == 0:
            vf_l, vf_c = aux_l[1], aux_c[1]

        qg = jnp.tile(q_norm[layer], 2).reshape(1, LANES)
        kg = jnp.tile(k_norm[layer], 2).reshape(1, LANES)
        qn_l, kn_l, vb_l = _qk_prep(p_l, cs, qg, kg, e128, t, True)
        qn_c, kn_c, vb_c = _qk_prep(p_c, None, qg, kg, e128, cn, False)
        sk = sink[layer].reshape(1, -1)
        ya_l = _attention(qn_l, kn_l, vb_l, kn_c, vb_c, sk, nb, t, cn, True)

        ln = jnp.stack([lnx_g[layer], lnx_b[layer]])
        wbr = w_br_rwkv[layer].astype(BF16)
        wba = w_br_attn[layer].astype(BF16)
        wo = w_out[layer].astype(BF16)
        xl = _merge(xl, y_l, aux_l, ya_l, p_l, ln, e128, wbr, wba, wo, mod_l, t)
        if not last:
            ya_c = _attention(qn_c, None, None, kn_c, vb_c, sk, nb, cn, cn, False)
            xc = _merge(xc, y_c, aux_c, ya_c, p_c, ln, e128, wbr, wba, wo, mod_c, cn)

        i = layer // 2
        if layer % 2 == 0:
            fw = (ffn_gate[i].astype(BF16), ffn_up[i].astype(BF16), ffn_down[i].astype(BF16))
            xl = _ffn(xl, norm2[layer], mod_l, *fw, t)
            if not last:
                xc = _ffn(xc, norm2[layer], mod_c, *fw, cn)
        else:
            rt = _pad_rows(router[i].T, LANES).T
            mw = (moe_gate[i].astype(BF16), moe_up[i].astype(BF16), moe_down[i].astype(BF16))
            xl = _moe(xl, norm2[layer], mod_l, rt, *mw, t)
            if not last:
                xc = _moe(xc, norm2[layer], mod_c, rt, *mw, cn)
    return xl.reshape(nb, t, d)
```

```python
import functools

import jax
import jax.numpy as jnp
from jax import lax
from jax.experimental import pallas as pl
from jax.experimental.pallas import tpu as pltpu

F32 = jnp.float32
BF16 = jnp.bfloat16
HIGHEST = lax.Precision.HIGHEST

LANES = 128
SUBLANES = 8
PACK = 16
ACT = jnp.bfloat16
HEAD_DIM = 64
RW_DIM = 512
ATT_DIM = 512
KV_DIM = 128
RW_IN = 1792
ATT_OFF = RW_IN
GATE_OFF = RW_IN + ATT_DIM + 2 * KV_DIM
N_EXPERTS = 8
CHUNK = 64
N_PAIRS = RW_DIM // LANES
MOE_TILE = 1024
MOE_BLK = 128
MOE_FT = 512
SCAN_GROUP = 4
ATT_SCALE = HEAD_DIM ** -0.5
ROPE_THETA = 10000.0
GRID_W = 64
RMS_EPS = 1e-6
GN_EPS = 64e-5
NEG_INF = -1e30
DECAY_SCALE = 0.6065306597126334
VMEM_LIMIT = 48 * 1024 * 1024


def _cparams(*sem):
    return pltpu.CompilerParams(dimension_semantics=sem, vmem_limit_bytes=VMEM_LIMIT)


def _row_tile(rows, cap):
    t = cap
    while rows % t:
        t //= 2
    return t


def _bdot(a, b):
    return jnp.dot(a.astype(BF16), b.astype(BF16), preferred_element_type=F32)


def _bdot_nt(a, b):
    return lax.dot_general(a.astype(BF16), b.astype(BF16), (((1,), (1,)), ((), ())),
                           preferred_element_type=F32)


def _bdot_tn(a, b):
    return lax.dot_general(a.astype(BF16), b.astype(BF16), (((0,), (0,)), ((), ())),
                           preferred_element_type=F32)


def _split2(x):
    hi = x.astype(BF16)
    lo = (x - hi.astype(F32)).astype(BF16)
    return hi, lo


def _dot3(a, w_hi, w_lo):
    a_hi, a_lo = _split2(a)
    return (jnp.dot(a_hi, w_hi, preferred_element_type=F32)
            + (jnp.dot(a_lo, w_hi, preferred_element_type=F32)
               + jnp.dot(a_hi, w_lo, preferred_element_type=F32)))


def _segsum(x, e, exact=True):
    outs = []
    for s in range(x.shape[1] // LANES):
        hi, lo = _split2(x[:, s * LANES:(s + 1) * LANES])
        acc = jnp.dot(hi, e, preferred_element_type=F32)
        outs.append(acc + jnp.dot(lo, e, preferred_element_type=F32) if exact else acc)
    return outs[0] if len(outs) == 1 else jnp.concatenate(outs, axis=1)


def _interleave(*stages):
    live = list(stages)
    while live:
        for g in list(live):
            if next(g, live) is live:
                live.remove(g)


def _norm_mod(x, gain, shift, scale):
    y = x * lax.rsqrt(jnp.mean(x * x, axis=-1, keepdims=True) + RMS_EPS) * gain
    return y * (1.0 + scale) + shift


def _mod_kernel(c_ref, w_ref, b_ref, o_ref):
    c = c_ref[...]
    s = c * jax.nn.sigmoid(c)
    o_ref[0] = jnp.dot(s, w_ref[0], precision=HIGHEST, preferred_element_type=F32) + b_ref[0]


def _modulation(c, c_ctx, w_mod, b_mod):
    depth, d, n6 = w_mod.shape
    nb = c.shape[0]
    rm = -(-(nb + 1) // SUBLANES) * SUBLANES
    cc = jnp.zeros((rm, d), F32).at[:nb].set(c).at[nb].set(c_ctx)
    tn = _row_tile(n6, 1536)
    return pl.pallas_call(
        _mod_kernel,
        grid=(depth, n6 // tn),
        in_specs=[pl.BlockSpec((rm, d), lambda l, n: (0, 0)),
                  pl.BlockSpec((1, d, tn), lambda l, n: (l, 0, n)),
                  pl.BlockSpec((1, 1, tn), lambda l, n: (l, 0, n))],
        out_specs=pl.BlockSpec((1, rm, tn), lambda l, n: (l, 0, n)),
        out_shape=jax.ShapeDtypeStruct((depth, rm, n6), F32),
        compiler_params=_cparams("parallel", "parallel"),
        name="adaln_mod",
    )(cc, w_mod, b_mod.reshape(depth, 1, n6))


def _inproj_kernel(*refs, vres, nsub):
    if vres:
        x_ref, g_ref, mod_ref, w_ref, vd_ref, o_ref, hv_ref = refs
    else:
        x_ref, g_ref, mod_ref, w_ref, o_ref = refs
    sub = x_ref.shape[0] // nsub
    for s in range(nsub):
        rs = slice(s * sub, (s + 1) * sub)
        h = _norm_mod(x_ref[rs, :], g_ref[...], mod_ref[0, 0:1, :], mod_ref[0, 1:2, :]).astype(BF16)
        if vres:
            hv_ref[rs, :] = jnp.dot(h, vd_ref[...], preferred_element_type=F32)
        o_ref[rs, :] = jnp.dot(h, w_ref[...], preferred_element_type=F32).astype(o_ref.dtype)


def _inproj(x2, gain, mod, w_b, vd_b, rows):
    m, d = x2.shape
    n = w_b.shape[1]
    tm = _row_tile(rows, 1024)
    per = rows // tm
    many = mod.shape[0] > 1
    vres = vd_b is not None
    once = pl.Buffered(1)
    in_specs = [pl.BlockSpec((tm, d), lambda i: (i, 0)),
                pl.BlockSpec((1, d), lambda i: (0, 0)),
                pl.BlockSpec((1, 6, d), lambda i: ((i // per) if many else 0, 0, 0)),
                pl.BlockSpec((d, n), lambda i: (0, 0), pipeline_mode=once)]
    args = [x2, gain.reshape(1, d), mod, w_b]
    out_specs = [pl.BlockSpec((tm, n), lambda i: (i, 0))]
    out_shape = [jax.ShapeDtypeStruct((m, n), ACT)]
    if vres:
        in_specs.append(pl.BlockSpec((d, LANES), lambda i: (0, 0)))
        args.append(vd_b)
        out_specs.append(pl.BlockSpec((tm, LANES), lambda i: (i, 0)))
        out_shape.append(jax.ShapeDtypeStruct((m, LANES), F32))
    outs = pl.pallas_call(
        functools.partial(_inproj_kernel, vres=vres, nsub=max(1, tm // 256)),
        grid=(m // tm,),
        in_specs=in_specs, out_specs=out_specs, out_shape=out_shape,
        compiler_params=_cparams("parallel"),
        name="in_proj",
    )(*args)
    return (outs[0], outs[1]) if vres else (outs[0], None)


def _qkprep_kernel(*refs, rope):
    if rope:
        q0_ref, q1_ref, k_ref, v_ref, cs_ref, qg_ref, kg_ref, e_ref, qn_ref, kn_ref, vb_ref = refs
    else:
        q0_ref, q1_ref, k_ref, v_ref, qg_ref, kg_ref, e_ref, qn_ref, kn_ref, vb_ref = refs
    e = e_ref[...]
    lane = lax.broadcasted_iota(jnp.int32, (1, LANES), 1)
    first = (lane & 31) < 16

    def norm_rope(u, gain):
        y = u * lax.rsqrt(_segsum(u * u, e) * (1.0 / HEAD_DIM) + RMS_EPS) * gain
        if rope:
            rot = jnp.where(first, -pltpu.roll(y, LANES - 16, 1), pltpu.roll(y, 16, 1))
            y = y * cs_ref[0] + rot * cs_ref[1]
        return y

    qg = qg_ref[...]
    for s, ref in enumerate((q0_ref, q1_ref)):
        for t in range(2):
            u = ref[:, t * LANES:(t + 1) * LANES].astype(F32)
            c0 = (2 * s + t) * LANES
            qn_ref[:, c0:c0 + LANES] = (norm_rope(u, qg) * ATT_SCALE).astype(BF16)
    low = lane < HEAD_DIM
    for src_val, dst in ((norm_rope(k_ref[...].astype(F32), kg_ref[...]), kn_ref), (v_ref[...].astype(F32), vb_ref)):
        swapped = pltpu.roll(src_val, HEAD_DIM, 1)
        dst[:, 0:LANES] = jnp.where(low, src_val, swapped).astype(BF16)
        dst[:, LANES:2 * LANES] = jnp.where(low, swapped, src_val).astype(BF16)


def _qk_prep(p, cs, qg, kg, e128, rows, rope):
    m = p.shape[0]
    tm = _row_tile(rows, 512)
    per = rows // tm
    qb = ATT_OFF // 256
    kb = (ATT_OFF + ATT_DIM) // LANES
    in_specs = [pl.BlockSpec((tm, 256), lambda i: (i, qb)),
                pl.BlockSpec((tm, 256), lambda i: (i, qb + 1)),
                pl.BlockSpec((tm, LANES), lambda i: (i, kb)),
                pl.BlockSpec((tm, LANES), lambda i: (i, kb + 1))]
    args = [p, p, p, p]
    if rope:
        in_specs.append(pl.BlockSpec((2, tm, LANES), lambda i: (0, i % per, 0)))
        args.append(cs)
    in_specs += [pl.BlockSpec((1, LANES), lambda i: (0, 0)),
                 pl.BlockSpec((1, LANES), lambda i: (0, 0)),
                 pl.BlockSpec((LANES, LANES), lambda i: (0, 0))]
    args += [qg, kg, e128]
    return pl.pallas_call(
        functools.partial(_qkprep_kernel, rope=rope),
        grid=(m // tm,),
        in_specs=in_specs,
        out_specs=[pl.BlockSpec((tm, ATT_DIM), lambda i: (i, 0)),
                   pl.BlockSpec((tm, 2 * KV_DIM), lambda i: (i, 0)),
                   pl.BlockSpec((tm, 2 * KV_DIM), lambda i: (i, 0))],
        out_shape=[jax.ShapeDtypeStruct((m, ATT_DIM), BF16),
                   jax.ShapeDtypeStruct((m, 2 * KV_DIM), BF16),
                   jax.ShapeDtypeStruct((m, 2 * KV_DIM), BF16)],
        compiler_params=_cparams("parallel"),
        name="qk_prep",
    )(*args)


def _attn_kernel(*refs, nq, local):
    if local:
        sink_ref, q_ref, kp_ref, kc_ref, kn_ref, vp_ref, vc_ref, vn_ref, kx_ref, vx_ref, wm_ref, o_ref = refs
        k = jnp.concatenate([kp_ref[...], kc_ref[...], kn_ref[...], kx_ref[...]], axis=0)
        v = jnp.concatenate([vp_ref[...], vc_ref[...], vn_ref[...], vx_ref[...]], axis=0)
    else:
        sink_ref, q_ref, kx_ref, vx_ref, o_ref = refs
        k = kx_ref[...]
        v = vx_ref[...]
    i = pl.program_id(1)
    bq = q_ref.shape[0]
    low = lax.broadcasted_iota(jnp.int32, (1, LANES), 1) < HEAD_DIM
    row = lax.broadcasted_iota(jnp.int32, (4 * bq, 1), 0)
    if local:
        kblk = lax.broadcasted_iota(jnp.int32, (1, 3 * bq), 1) // bq
        off_end = ((kblk == 0) & (i == 0)) | ((kblk == 2) & (i == nq - 1))
        bias = wm_ref[...] + jnp.where(off_end, NEG_INF, 0.0)
    def head_group(g):
        kb = k[:, g * LANES:(g + 1) * LANES]
        vb = v[:, g * LANES:(g + 1) * LANES]
        qa = q_ref[:, (2 * g) * LANES:(2 * g + 1) * LANES].astype(F32)
        qb = q_ref[:, (2 * g + 1) * LANES:(2 * g + 2) * LANES].astype(F32)
        qs = jnp.concatenate([jnp.where(low, qa, 0.0), jnp.where(low, 0.0, qa),
                              jnp.where(low, qb, 0.0), jnp.where(low, 0.0, qb)], axis=0).astype(BF16)
        s = lax.dot_general(qs, kb, (((1,), (1,)), ((), ())), preferred_element_type=F32)
        yield
        if local:
            s = jnp.concatenate([s[:, :3 * bq] + bias, s[:, 3 * bq:]], axis=1)
        sk = jnp.where(row < bq, sink_ref[0, 4 * g],
                       jnp.where(row < 2 * bq, sink_ref[0, 4 * g + 1],
                                 jnp.where(row < 3 * bq, sink_ref[0, 4 * g + 2], sink_ref[0, 4 * g + 3])))
        mx = jnp.maximum(jnp.max(s, axis=1, keepdims=True), sk)
        yield
        ex = jnp.exp(s - mx)
        den = jnp.sum(ex, axis=1, keepdims=True) + jnp.exp(sk - mx)
        yield
        o = jnp.dot(ex.astype(BF16), vb, preferred_element_type=F32) / den
        o_ref[:, (2 * g) * LANES:(2 * g + 1) * LANES] = jnp.where(low, o[0:bq], o[bq:2 * bq]).astype(o_ref.dtype)
        o_ref[:, (2 * g + 1) * LANES:(2 * g + 2) * LANES] = jnp.where(
            low, o[2 * bq:3 * bq], o[3 * bq:4 * bq]).astype(o_ref.dtype)

    _interleave(head_group(0), head_group(1))


def _attention(qn, kn, vb, kctx, vctx, sink, nb, rows, cn, local):
    m = qn.shape[0]
    bq = 128
    nq = rows // bq
    smem = pl.BlockSpec(memory_space=pltpu.SMEM)
    qspec = pl.BlockSpec((bq, ATT_DIM), lambda b, i: (b * nq + i, 0))
    cspec = pl.BlockSpec((cn, 2 * KV_DIM), lambda b, i: (b, 0))
    if local:
        prv = pl.BlockSpec((bq, 2 * KV_DIM), lambda b, i: (b * nq + jnp.maximum(i - 1, 0), 0))
        cur = pl.BlockSpec((bq, 2 * KV_DIM), lambda b, i: (b * nq + i, 0))
        nxt = pl.BlockSpec((bq, 2 * KV_DIM), lambda b, i: (b * nq + jnp.minimum(i + 1, nq - 1), 0))
        qo = jnp.arange(4 * bq)[:, None] % bq
        kcol = jnp.arange(3 * bq)[None, :]
        inwin = ((kcol // bq == 1) | ((kcol // bq == 0) & (kcol % bq >= qo)) | ((kcol // bq == 2) & (kcol % bq <= qo)))
        wmask = jnp.where(inwin, 0.0, NEG_INF).astype(F32)
        in_specs = [smem, qspec, prv, cur, nxt, prv, cur, nxt, cspec, cspec,
                    pl.BlockSpec((4 * bq, 3 * bq), lambda b, i: (0, 0))]
        args = [sink, qn, kn, kn, kn, vb, vb, vb, kctx, vctx, wmask]
    else:
        in_specs = [smem, qspec, cspec, cspec]
        args = [sink, qn, kctx, vctx]
    return pl.pallas_call(
        functools.partial(_attn_kernel, nq=nq, local=local),
        grid=(nb, nq),
        in_specs=in_specs,
        out_specs=pl.BlockSpec((bq, ATT_DIM), lambda b, i: (b * nq + i, 0)),
        out_shape=jax.ShapeDtypeStruct((m, ATT_DIM), BF16),
        compiler_params=_cparams("parallel", "parallel"),
        name="window_attn" if local else "ctx_attn",
    )(*args)


def _rwkv_kernel(*refs, nt, tq, vres, drn, tps):
    it = iter(refs)
    n_in = 5 if vres else 3
    first = [next(it) for _ in range(n_in)] if tps == 2 else None
    tiles = [[next(it) for _ in range(n_in)] for _ in range(tps)]
    ts_ref, pd_ref, ps_ref, wla_ref, wg_ref = (next(it) for _ in range(5))
    if vres:
        vuh_ref, vul_ref = next(it), next(it)
    e_ref, sin_ref = next(it), next(it)
    y_ref, aux_ref, sout_ref = next(it), next(it), next(it)
    ops = [next(it) for _ in range(tps)]
    auxs = [next(it) for _ in range(tps)]
    s_sc = next(it)

    j = pl.program_id(1)
    sgn = 1 - 2 * drn
    nch = tq // CHUNK
    nsteps = nt // tps
    e = e_ref[...]
    lane = lax.broadcasted_iota(jnp.int32, (1, LANES), 1)
    low = lane < HEAD_DIM

    def natural(k):
        return k if drn == 0 else nt - 1 - k

    def prepare(tile, in_refs, ops_sc, aux_sc):
        p_ref, pv_ref, nx_ref = in_refs[:3]
        rw = p_ref[...].astype(F32)
        row = lax.broadcasted_iota(jnp.int32, (tq, 1), 0)
        prev_row = jnp.where(tile > 0, pv_ref[PACK - 1:PACK, :].astype(F32), 0.0)
        next_row = jnp.where(tile < nt - 1, nx_ref[0:1, :].astype(F32), 0.0)
        prv = jnp.where(row == 0, prev_row, pltpu.roll(rw, 1, 0))
        yield
        nxt = jnp.where(row == tq - 1, next_row, pltpu.roll(rw, tq - 1, 0))
        xs = rw + ts_ref[0:1, :] * (prv - rw) + ts_ref[1:2, :] * (nxt - rw)
        yield

        r = xs[:, 0:RW_DIM]
        k = xs[:, RW_DIM:2 * RW_DIM]
        v = xs[:, 2 * RW_DIM:3 * RW_DIM]
        la = xs[:, 3 * RW_DIM:3 * RW_DIM + LANES]
        la = jnp.where(low, jnp.tanh(la), la)
        la_hi, la_lo = _split2(la)
        lora = (jnp.dot(la_hi, wla_ref[0], preferred_element_type=F32)
                + jnp.dot(la_lo, wla_ref[0], preferred_element_type=F32))
        yield
        logw = -DECAY_SCALE * jax.nn.sigmoid(pd_ref[0, 0:1, :] + lora[:, 0:RW_DIM])
        iclr = jax.nn.sigmoid(pd_ref[0, 1:2, :] + lora[:, RW_DIM:2 * RW_DIM])
        if vres:
            vf_ref, hv_ref = in_refs[3:]
            mix = jax.nn.sigmoid(ps_ref[3:4, :] + _dot3(hv_ref[...], vuh_ref[...], vul_ref[...]))
            v = v + (vf_ref[...].astype(F32) - v) * mix
        yield
        kk = k * ps_ref[0:1, :]
        kk = kk * lax.rsqrt(_segsum(kk * kk, e, exact=False) + 1e-12)
        yield
        kh = k * (1.0 + (iclr - 1.0) * ps_ref[1:2, :])
        bb = kk * iclr
        aux_sc[:, 0:RW_DIM] = (_segsum(r * kh * ps_ref[2:3, :], e, exact=False) * v).astype(aux_sc.dtype)
        if drn == 0:
            glo = xs[:, 3 * RW_DIM + LANES:RW_IN]
            aux_sc[:, RW_DIM:2 * RW_DIM] = _bdot(jax.nn.sigmoid(glo), wg_ref[...]).astype(aux_sc.dtype)
        else:
            aux_sc[:, RW_DIM:2 * RW_DIM] = v.astype(aux_sc.dtype)

        yield
        ti = lax.broadcasted_iota(jnp.int32, (tq, tq), 0)
        si = lax.broadcasted_iota(jnp.int32, (tq, tq), 1)
        tri = (((ti // CHUNK) == (si // CHUNK)) & (((si - ti) * sgn) <= 0)).astype(BF16)
        lw_hi, lw_lo = _split2(logw)
        cin = jnp.dot(tri, lw_hi, preferred_element_type=F32) + jnp.dot(tri, lw_lo, preferred_element_type=F32)
        cin3 = cin.reshape(nch, CHUNK, RW_DIM)
        last = CHUNK - 1 if drn == 0 else 0
        ctot = jnp.broadcast_to(cin3[:, last:last + 1, :], cin3.shape).reshape(tq, RW_DIM)
        yield
        e_neg = jnp.exp(-cin)
        e_rem = jnp.exp(ctot - cin)
        ops_sc[0] = kk * jnp.exp(cin - logw)
        ops_sc[1] = r * jnp.exp(cin)
        ops_sc[2] = bb * e_neg
        ops_sc[3] = kh * e_neg
        yield
        ops_sc[4] = kh * e_rem
        ops_sc[5] = bb * e_rem
        ops_sc[6] = v
        ops_sc[7] = jnp.exp(ctot)

    tr = lax.broadcasted_iota(jnp.int32, (CHUNK, LANES), 0)
    sc = lax.broadcasted_iota(jnp.int32, (CHUNK, LANES), 1) & (CHUNK - 1)
    rel = (sc - tr) * sgn
    strict = rel < 0
    incl = rel <= 0
    eye = (sc == tr).astype(F32)
    same_head = ((lax.broadcasted_iota(jnp.int32, (LANES, LANES), 0) // HEAD_DIM)
                 == (lax.broadcasted_iota(jnp.int32, (LANES, LANES), 1) // HEAD_DIM))

    def stack(x):
        return jnp.concatenate([jnp.where(low, x, 0.0), jnp.where(low, 0.0, x)], axis=0)

    def hdot(x, y):
        return _bdot(x, stack(y))

    def scan_tile(ops_sc, row0):
        st = [s_sc[pr] for pr in range(N_PAIRS)]
        for g0 in range(0, nch, SCAN_GROUP):
            yield from scan_chunks(ops_sc, row0, st, range(g0, g0 + SCAN_GROUP))
        for pr in range(N_PAIRS):
            s_sc[pr] = st[pr]

    def scan_chunks(ops_sc, row0, st, chunk_ids):
        probs = []
        for ii in chunk_ids:
            off = (ii if drn == 0 else nch - 1 - ii) * CHUNK
            for pr in range(N_PAIRS):
                ls = slice(pr * LANES, (pr + 1) * LANES)
                q = {"off": off, "ls": ls, "pr": pr}
                for n, name in enumerate(("kkt", "rt", "bh", "kh", "kp", "bp", "v")):
                    q[name] = ops_sc[n, off:off + CHUNK, ls]
                q["wl"] = ops_sc[7, off:off + 1, ls]
                probs.append(q)
        for q in probs:
            g = _bdot_nt(jnp.concatenate([q["kkt"], q["rt"]], axis=0),
                         jnp.concatenate([stack(q["bh"]), stack(q["kh"])], axis=0))
            q["nn"] = jnp.where(strict, -g[0:CHUNK, 0:LANES], 0.0)
            q["avk"] = jnp.where(strict, g[0:CHUNK, LANES:], 0.0)
            q["arb"] = jnp.where(incl, g[CHUNK:, 0:LANES], 0.0)
            q["ark"] = jnp.where(incl, g[CHUNK:, LANES:], 0.0)
            q["t"] = eye + q["nn"]
        yield
        for q in probs:
            q["q"] = hdot(q["nn"], q["nn"])
        for _ in range(4):
            for q in probs:
                tq2 = _bdot(jnp.concatenate([q["t"], q["q"]], axis=0), stack(q["q"]))
                q["t"] = q["t"] + tq2[:CHUNK]
                q["q"] = tq2[CHUNK:]
            yield
        for q in probs:
            q["t"] = q["t"] + hdot(q["t"], q["q"])
        for q in probs:
            av = _bdot(jnp.concatenate([q["avk"], q["ark"]], axis=0), stack(q["v"]))
            q["av"], q["arkv"] = av[:CHUNK], av[CHUNK:]
        yield
        for q in probs:
            tt2 = _bdot(q["t"], jnp.concatenate([stack(q["kkt"]), stack(q["av"])], axis=1))
            q["tk"], q["tav"] = tt2[:, :LANES], tt2[:, LANES:]
        yield
        for q in probs:
            tn = _bdot_tn(jnp.concatenate([q["tk"], q["tav"]], axis=1), q["bp"])
            q["mb"] = jnp.where(same_head, tn[:LANES], 0.0)
            q["nm"] = jnp.where(same_head, _bdot_tn(q["v"], q["kp"]) - tn[LANES:], 0.0)
            z = _bdot(q["arb"], jnp.concatenate([stack(q["tk"]), stack(q["tav"])], axis=1))
            q["r2"] = q["rt"] - z[:, :LANES]
            q["y0"] = q["arkv"] - z[:, LANES:]
        yield

        outs = []
        for q in probs:
            sm = st[q["pr"]]
            outs.append(_bdot_nt(q["r2"], sm) + q["y0"])
            st[q["pr"]] = sm * q["wl"] - _bdot(sm, q["mb"]) + q["nm"]
        for q, yo in zip(probs, outs):
            y_ref[row0 + q["off"]:row0 + q["off"] + CHUNK, q["ls"]] = yo.astype(y_ref.dtype)

    run = _interleave

    @pl.when(j == 0)
    def _():
        s_sc[...] = sin_ref[0]

    if tps == 1:
        run(prepare(natural(j), tiles[0], ops[0], auxs[0]))
        aux_ref[...] = auxs[0][...]
        run(scan_tile(ops[0], 0))
    else:
        @pl.when(j == 0)
        def _():
            run(prepare(natural(0), first, ops[0], auxs[0]))

        rows = [0, tq] if drn == 0 else [tq, 0]
        aux_ref[rows[0]:rows[0] + tq, :] = auxs[0][...]
        run(scan_tile(ops[0], rows[0]), prepare(natural(2 * j + 1), tiles[0], ops[1], auxs[1]))
        aux_ref[rows[1]:rows[1] + tq, :] = auxs[1][...]
        run(scan_tile(ops[1], rows[1]),
            prepare(natural(jnp.minimum(2 * j + 2, nt - 1)), tiles[1], ops[0], auxs[0]))

    @pl.when(j == nsteps - 1)
    def _():
        sout_ref[0] = s_sc[...]


def _rwkv(p, vf_src, hv, state_in, prm, nb, rows, drn):
    m = p.shape[0]
    tq = _row_tile(rows, 256)
    nt = rows // tq
    tps = 2 if nt % 2 == 0 else 1
    nsteps = nt // tps
    rb = tq // PACK
    vres = vf_src is not None

    def tile_specs(scan_pos):
        def tile(b, j):
            k = jnp.minimum(scan_pos(j), nt - 1)
            return b * nt + (k if drn == 0 else nt - 1 - k)

        specs = [pl.BlockSpec((tq, RW_IN), lambda b, j: (tile(b, j), 0)),
                 pl.BlockSpec((PACK, RW_IN), lambda b, j: (jnp.maximum(tile(b, j) * rb - 1, 0), 0)),
                 pl.BlockSpec((PACK, RW_IN), lambda b, j: (jnp.minimum((tile(b, j) + 1) * rb, m // PACK - 1), 0))]
        args = [p, p, p]
        if vres:
            specs += [pl.BlockSpec((tq, RW_DIM), lambda b, j: (tile(b, j), 1)),
                      pl.BlockSpec((tq, LANES), lambda b, j: (tile(b, j), 0))]
            args += [vf_src, hv]
        return specs, args

    if tps == 2:
        positions = [lambda j: 0 * j, lambda j: 2 * j + 1, lambda j: 2 * j + 2]
    else:
        positions = [lambda j: j]
    in_specs, args = [], []
    for pos in positions:
        s_, a_ = tile_specs(pos)
        in_specs += s_
        args += a_

    const2 = lambda b, j: (0, 0)
    dir3 = lambda b, j: (drn, 0, 0)
    in_specs += [pl.BlockSpec((2, RW_IN), const2),
                 pl.BlockSpec((1, SUBLANES, RW_DIM), dir3),
                 pl.BlockSpec((SUBLANES, RW_DIM), const2),
                 pl.BlockSpec((1, LANES, 2 * RW_DIM), dir3),
                 pl.BlockSpec((LANES, RW_DIM), const2)]
    args += [prm["tshift"], prm["pd"], prm["ps"], prm["wla"], prm["wg"]]
    if vres:
        in_specs += [pl.BlockSpec((LANES, RW_DIM), const2), pl.BlockSpec((LANES, RW_DIM), const2)]
        args += [prm["vu_hi"], prm["vu_lo"]]
    sspec = pl.BlockSpec((1, N_PAIRS, LANES, LANES), lambda b, j: (b, 0, 0, 0))
    in_specs += [pl.BlockSpec((LANES, LANES), const2), sspec]
    args += [prm["e128"], state_in]

    def out_block(b, j):
        return (b * nsteps + (j if drn == 0 else nsteps - 1 - j), 0)

    return pl.pallas_call(
        functools.partial(_rwkv_kernel, nt=nt, tq=tq, vres=vres, drn=drn, tps=tps),
        grid=(nb, nsteps),
        in_specs=in_specs,
        out_specs=[pl.BlockSpec((tps * tq, RW_DIM), out_block),
                   pl.BlockSpec((tps * tq, 2 * RW_DIM), out_block),
                   sspec],
        out_shape=[jax.ShapeDtypeStruct((m, RW_DIM), ACT),
                   jax.ShapeDtypeStruct((m, 2 * RW_DIM), ACT),
                   jax.ShapeDtypeStruct((nb, N_PAIRS, LANES, LANES), F32)],
        scratch_shapes=([pltpu.VMEM((8, tq, RW_DIM), F32)] * tps + [pltpu.VMEM((tq, 2 * RW_DIM), ACT)] * tps
                        + [pltpu.VMEM((N_PAIRS, LANES, LANES), F32)]),
        compiler_params=_cparams("parallel", "arbitrary"),
        name="rwkv7_fwd" if drn == 0 else "rwkv7_bwd",
    )(*args)


def _rwkv_both(p, vf_src, hv, states, prm, nb, rows):
    outs = [_rwkv(p, vf_src, hv, states[d], prm, nb, rows, d) for d in range(2)]
    return [o[0] for o in outs], [o[1] for o in outs], [o[2] for o in outs]


def _merge_kernel(x_ref, y0_ref, y1_ref, a0_ref, a1_ref, ya_ref, g0_ref, g1_ref, g2_ref, g3_ref, ln_ref, e_ref,
                  wbr_ref, wba_ref, wo_ref, mod_ref, o_ref):
    e = e_ref[...]
    y = y0_ref[...].astype(F32) + y1_ref[...].astype(F32)
    mu = _segsum(y, e) * (1.0 / HEAD_DIM)
    yc = y - mu
    var = _segsum(yc * yc, e) * (1.0 / HEAD_DIM)
    yn = yc * lax.rsqrt(var + GN_EPS) * ln_ref[0:1, :] + ln_ref[1:2, :]
    cv = a0_ref[:, 0:RW_DIM].astype(F32) + a1_ref[:, 0:RW_DIM].astype(F32)
    yrw = (yn + cv) * a0_ref[:, RW_DIM:2 * RW_DIM].astype(F32)
    a = _bdot(yrw, wbr_ref[...])
    b = jnp.dot(ya_ref[...], wba_ref[...], preferred_element_type=F32)
    h = a.shape[1] // 2
    mrg = jnp.concatenate(
        [jax.nn.sigmoid(g0_ref[...].astype(F32)) * a[:, :h] + jax.nn.sigmoid(g2_ref[...].astype(F32)) * b[:, :h],
         jax.nn.sigmoid(g1_ref[...].astype(F32)) * a[:, h:] + jax.nn.sigmoid(g3_ref[...].astype(F32)) * b[:, h:]],
        axis=1)
    o_ref[...] = x_ref[...] + mod_ref[0, 2:3, :] * _bdot(mrg, wo_ref[...])


def _merge(x2, ys, auxs, ya, p, ln, e128, wbr, wba, wo, mod, rows):
    m, d = x2.shape
    tm = _row_tile(rows, 512)
    per = rows // tm
    many = mod.shape[0] > 1
    gb = GATE_OFF // 512
    const2 = lambda i: (0, 0)
    in_specs = [pl.BlockSpec((tm, d), lambda i: (i, 0)),
                pl.BlockSpec((tm, RW_DIM), lambda i: (i, 0)),
                pl.BlockSpec((tm, RW_DIM), lambda i: (i, 0)),
                pl.BlockSpec((tm, 2 * RW_DIM), lambda i: (i, 0)),
                pl.BlockSpec((tm, 2 * RW_DIM), lambda i: (i, 0)),
                pl.BlockSpec((tm, ATT_DIM), lambda i: (i, 0))]
    in_specs += [pl.BlockSpec((tm, 512), (lambda i, c=c: (i, gb + c))) for c in range(4)]
    in_specs += [pl.BlockSpec((2, RW_DIM), const2),
                 pl.BlockSpec((LANES, LANES), const2),
                 pl.BlockSpec(wbr.shape, const2),
                 pl.BlockSpec(wba.shape, const2),
                 pl.BlockSpec(wo.shape, const2),
                 pl.BlockSpec((1, 6, d), lambda i: ((i // per) if many else 0, 0, 0))]
    return pl.pallas_call(
        _merge_kernel,
        grid=(m // tm,),
        in_specs=in_specs,
        out_specs=pl.BlockSpec((tm, d), lambda i: (i, 0)),
        out_shape=jax.ShapeDtypeStruct((m, d), F32),
        compiler_params=_cparams("parallel"),
        name="branch_merge",
    )(x2, ys[0], ys[1], auxs[0], auxs[1], ya, p, p, p, p, ln, e128, wbr, wba, wo, mod)


def _ffn_kernel(x_ref, g_ref, mod_ref, wg_ref, wu_ref, wd_ref, o_ref, h_sc, acc_sc, *, nf):
    f = pl.program_id(1)

    @pl.when(f == 0)
    def _():
        h = _norm_mod(x_ref[...], g_ref[...], mod_ref[0, 3:4, :], mod_ref[0, 4:5, :])
        h_sc[...] = h.astype(BF16)
        acc_sc[...] = jnp.zeros_like(acc_sc)

    h = h_sc[...]
    a = jnp.dot(h, wg_ref[...], preferred_element_type=F32)
    u = jnp.dot(h, wu_ref[...], preferred_element_type=F32)
    act = (a * jax.nn.sigmoid(a)) * u
    acc_sc[...] += jnp.dot(act.astype(BF16), wd_ref[...], preferred_element_type=F32)

    @pl.when(f == nf - 1)
    def _():
        o_ref[...] = x_ref[...] + mod_ref[0, 5:6, :] * acc_sc[...]


def _ffn(x2, gain, mod, wg, wu, wd, rows):
    m, d = x2.shape
    dff = wg.shape[1]
    tm = _row_tile(rows, 512)
    per = rows // tm
    many = mod.shape[0] > 1
    tf = dff
    nf = dff // tf
    once = pl.Buffered(1)
    return pl.pallas_call(
        functools.partial(_ffn_kernel, nf=nf),
        grid=(m // tm, nf),
        in_specs=[pl.BlockSpec((tm, d), lambda i, f: (i, 0)),
                  pl.BlockSpec((1, d), lambda i, f: (0, 0)),
                  pl.BlockSpec((1, 6, d), lambda i, f: ((i // per) if many else 0, 0, 0)),
                  pl.BlockSpec((d, tf), lambda i, f: (0, f), pipeline_mode=once),
                  pl.BlockSpec((d, tf), lambda i, f: (0, f), pipeline_mode=once),
                  pl.BlockSpec((tf, d), lambda i, f: (f, 0), pipeline_mode=once)],
        out_specs=pl.BlockSpec((tm, d), lambda i, f: (i, 0)),
        out_shape=jax.ShapeDtypeStruct((m, d), F32),
        scratch_shapes=[pltpu.VMEM((tm, d), BF16), pltpu.VMEM((tm, d), F32)],
        compiler_params=_cparams("parallel", "arbitrary"),
        name="ffn_swiglu",
    )(x2, gain.reshape(1, d), mod, wg, wu, wd)


def _dma_pieces(count, sizes, make):
    for k, size in enumerate(sizes):
        @pl.when((count & size) != 0)
        def _(k=k, size=size):
            make(k, pl.multiple_of(count & ~(2 * size - 1), PACK), size)


def _moe_gather_kernel(pos_ref, cnt_ref, h_ref, rankt_ref, zero_ref, out_ref, hs_sc, sem, *, ne, nsteps, sizes):
    del zero_ref
    i = pl.program_id(0)
    ex = pl.program_id(1)
    step = i * ne + ex
    slot = step % 2
    tt = h_ref.shape[0]

    def copies(st, sl, act):
        base = pos_ref[st]

        def make(k, off, size):
            act(pltpu.make_async_copy(hs_sc.at[sl, pl.ds(off, size)],
                                      out_ref.at[pl.ds(pl.multiple_of(base + off, PACK), size)], sem.at[sl, k]))

        _dma_pieces(cnt_ref[st], sizes, make)

    @pl.when(step >= 2)
    def _():
        copies(step - 2, slot, lambda cp: cp.wait())

    rrow = rankt_ref[0, pl.ds(ex, 1), :]
    rowid = lax.broadcasted_iota(jnp.int32, (MOE_BLK, tt), 0).astype(F32)

    def gather(b, carry):
        sel = ((rrow - (b * MOE_BLK).astype(F32)) == rowid).astype(BF16)
        r0 = pl.multiple_of(b * MOE_BLK, MOE_BLK)
        hs_sc[slot, pl.ds(r0, MOE_BLK), :] = jnp.dot(sel, h_ref[...], preferred_element_type=F32).astype(BF16)
        return carry

    lax.fori_loop(0, (cnt_ref[step] + MOE_BLK - 1) // MOE_BLK, gather, 0)
    copies(step, slot, lambda cp: cp.start())

    @pl.when(step == nsteps - 1)
    def _():
        if nsteps >= 2:
            copies(step - 1, 1 - slot, lambda cp: cp.wait())
        copies(step, slot, lambda cp: cp.wait())


def _moe_ffn_kernel(te_ref, tm_ref, tv_ref, hs_ref, wg_ref, wu_ref, wd_ref, os_ref, acc_sc, *, nf):
    del te_ref, tm_ref
    t = pl.program_id(0)
    f = pl.program_id(1)
    valid = tv_ref[t] != 0

    @pl.when(valid)
    def _():
        hb = hs_ref[...]
        a = jnp.dot(hb, wg_ref[0], preferred_element_type=F32)
        u = jnp.dot(hb, wu_ref[0], preferred_element_type=F32)
        act = ((a * jax.nn.sigmoid(a)) * u).astype(BF16)
        part = jnp.dot(act, wd_ref[0], preferred_element_type=F32)

        @pl.when(f == 0)
        def _():
            acc_sc[...] = part

        @pl.when(f > 0)
        def _():
            acc_sc[...] += part

    @pl.when(f == nf - 1)
    def _():
        os_ref[...] = jnp.where(valid, acc_sc[...], 0.0).astype(os_ref.dtype)


def _moe_combine_kernel(pos_ref, cnt_ref, x_ref, we_ref, rank_ref, mod_ref, os_ref, o_ref, acc_sc, ob_sc, sem,
                        *, ne, nsteps, sizes):
    i = pl.program_id(0)
    ex = pl.program_id(1)
    step = i * ne + ex
    slot = step % 2
    tt = x_ref.shape[0]
    sblk = min(2 * MOE_BLK, tt)
    lane = lax.broadcasted_iota(jnp.int32, (tt, LANES), 1)

    def copies(st, sl, act):
        base = pos_ref[st]

        def make(k, off, size):
            act(pltpu.make_async_copy(os_ref.at[pl.ds(pl.multiple_of(base + off, PACK), size)],
                                      ob_sc.at[sl, pl.ds(off, size)], sem.at[sl, k]))

        _dma_pieces(cnt_ref[st], sizes, make)

    @pl.when(step == 0)
    def _():
        ob_sc[...] = jnp.zeros_like(ob_sc)
        copies(0, 0, lambda cp: cp.start())

    @pl.when(step + 1 < nsteps)
    def _():
        copies(step + 1, 1 - slot, lambda cp: cp.start())

    copies(step, slot, lambda cp: cp.wait())

    @pl.when(ex == 0)
    def _():
        acc_sc[...] = jnp.zeros_like(acc_sc)

    rcol = jnp.sum(jnp.where(lane == ex, rank_ref[...], 0.0), axis=1, keepdims=True)
    wcol = jnp.sum(jnp.where(lane == ex, we_ref[...], 0.0), axis=1, keepdims=True)
    colid = lax.broadcasted_iota(jnp.int32, (tt, sblk), 1).astype(F32)

    def scatter(b, carry):
        sel = ((rcol - (b * sblk).astype(F32)) == colid).astype(BF16)
        r0 = pl.multiple_of(b * sblk, sblk)
        acc_sc[...] += wcol * jnp.dot(sel, ob_sc[slot, pl.ds(r0, sblk), :], preferred_element_type=F32)
        return carry

    lax.fori_loop(0, (cnt_ref[step] + sblk - 1) // sblk, scatter, 0)

    @pl.when(ex == ne - 1)
    def _():
        o_ref[...] = x_ref[...] + mod_ref[0, 5:6, :] * acc_sc[...]


def _route_kernel(x_ref, g_ref, mod_ref, rt_ref, h_ref, we_ref, rank_ref, rankt_ref, cnt_ref, *, ne):
    tt = x_ref.shape[0]
    h = _norm_mod(x_ref[...], g_ref[...], mod_ref[0, 3:4, :], mod_ref[0, 4:5, :])
    h_ref[...] = h.astype(BF16)
    lane = lax.broadcasted_iota(jnp.int32, (tt, LANES), 1)
    logits = jnp.dot(h, rt_ref[...], precision=HIGHEST, preferred_element_type=F32)
    lg = jnp.where(lane < ne, logits, NEG_INF)
    t1 = jnp.max(lg, axis=1, keepdims=True)
    i1 = jnp.min(jnp.where(lg == t1, lane, LANES), axis=1, keepdims=True)
    lg2 = jnp.where(lane == i1, NEG_INF, lg)
    t2 = jnp.max(lg2, axis=1, keepdims=True)
    i2 = jnp.min(jnp.where(lg2 == t2, lane, LANES), axis=1, keepdims=True)
    e2 = jnp.exp(t2 - t1)
    den = 1.0 + e2
    we_ref[...] = jnp.where(lane == i1, 1.0 / den, 0.0) + jnp.where(lane == i2, e2 / den, 0.0)
    sel = (lane == i1) | (lane == i2)
    self32 = sel.astype(F32)
    before = (lax.broadcasted_iota(jnp.int32, (tt, tt), 1) < lax.broadcasted_iota(jnp.int32, (tt, tt), 0))
    rank = jnp.dot(before.astype(BF16), sel.astype(BF16), preferred_element_type=F32)
    rank = jnp.where(sel, rank, -1.0)
    rank_ref[...] = rank
    rankt_ref[0] = rank.T[0:SUBLANES, :]
    cnt_ref[0] = jnp.broadcast_to(jnp.sum(self32, axis=0, keepdims=True), (SUBLANES, LANES))


def _moe(x2, gain, mod, router_pad, wg, wu, wd, rows):
    m, d = x2.shape
    ne, _, dff = wg.shape
    assert ne <= SUBLANES
    tt = _row_tile(rows, MOE_TILE)
    nti = m // tt
    per = rows // tt
    many = mod.shape[0] > 1
    h, we, rank, rankt, cnt = pl.pallas_call(
        functools.partial(_route_kernel, ne=ne),
        grid=(nti,),
        in_specs=[pl.BlockSpec((tt, d), lambda i: (i, 0)),
                  pl.BlockSpec((1, d), lambda i: (0, 0)),
                  pl.BlockSpec((1, 6, d), lambda i: ((i // per) if many else 0, 0, 0)),
                  pl.BlockSpec((d, LANES), lambda i: (0, 0))],
        out_specs=[pl.BlockSpec((tt, d), lambda i: (i, 0)),
                   pl.BlockSpec((tt, LANES), lambda i: (i, 0)),
                   pl.BlockSpec((tt, LANES), lambda i: (i, 0)),
                   pl.BlockSpec((1, SUBLANES, tt), lambda i: (i, 0, 0)),
                   pl.BlockSpec((1, SUBLANES, LANES), lambda i: (i, 0, 0))],
        out_shape=[jax.ShapeDtypeStruct((m, d), BF16),
                   jax.ShapeDtypeStruct((m, LANES), F32),
                   jax.ShapeDtypeStruct((m, LANES), F32),
                   jax.ShapeDtypeStruct((nti, SUBLANES, tt), F32),
                   jax.ShapeDtypeStruct((nti, SUBLANES, LANES), F32)],
        compiler_params=_cparams("parallel"),
        name="moe_route",
    )(x2, gain.reshape(1, d), mod, router_pad)
    c16 = (cnt[:, 0, :ne].astype(jnp.int32) + (PACK - 1)) // PACK * PACK
    region = (jnp.sum(c16, axis=0) + (MOE_FT - 1)) // MOE_FT * MOE_FT
    e_off = jnp.cumsum(region) - region
    pos = (e_off[None, :] + jnp.cumsum(c16, axis=0) - c16).reshape(-1).astype(jnp.int32)
    c16 = c16.reshape(-1)
    rmax = -(-(2 * m + nti * ne * PACK + ne * MOE_FT) // MOE_FT) * MOE_FT
    nft = rmax // MOE_FT
    tiles_e = jnp.cumsum(region // MOE_FT)
    tid = jnp.arange(nft, dtype=jnp.int32)
    tile_valid = (tid < tiles_e[-1]).astype(jnp.int32)
    tile_map = jnp.minimum(tid, tiles_e[-1] - 1).astype(jnp.int32)
    tile_exp = jnp.minimum(jnp.sum(tile_map[:, None] >= tiles_e[None, :], axis=1), ne - 1).astype(jnp.int32)

    sizes = tuple(tt >> k for k in range(tt.bit_length()) if (tt >> k) >= PACK)
    nsteps = nti * ne
    hs = pl.pallas_call(
        functools.partial(_moe_gather_kernel, ne=ne, nsteps=nsteps, sizes=sizes),
        grid_spec=pltpu.PrefetchScalarGridSpec(
            num_scalar_prefetch=2,
            grid=(nti, ne),
            in_specs=[pl.BlockSpec((tt, d), lambda i, e, p_, c_: (i, 0)),
                      pl.BlockSpec((1, SUBLANES, tt), lambda i, e, p_, c_: (i, 0, 0)),
                      pl.BlockSpec(memory_space=pl.ANY)],
            out_specs=pl.BlockSpec(memory_space=pl.ANY),
            scratch_shapes=[pltpu.VMEM((2, tt, d), BF16), pltpu.SemaphoreType.DMA((2, len(sizes)))]),
        out_shape=jax.ShapeDtypeStruct((rmax, d), BF16),
        input_output_aliases={4: 0},
        compiler_params=_cparams("arbitrary", "arbitrary"),
        name="moe_gather",
    )(pos, c16, h, rankt, jnp.zeros((rmax, d), BF16))

    tf = dff
    once = pl.Buffered(1)
    nf = dff // tf
    os_sorted = pl.pallas_call(
        functools.partial(_moe_ffn_kernel, nf=nf),
        grid_spec=pltpu.PrefetchScalarGridSpec(
            num_scalar_prefetch=3,
            grid=(nft, nf),
            in_specs=[pl.BlockSpec((MOE_FT, d), lambda t, f, te, tm, tv: (tm[t], 0)),
                      pl.BlockSpec((1, d, tf), lambda t, f, te, tm, tv: (te[t], 0, f), pipeline_mode=once),
                      pl.BlockSpec((1, d, tf), lambda t, f, te, tm, tv: (te[t], 0, f), pipeline_mode=once),
                      pl.BlockSpec((1, tf, d), lambda t, f, te, tm, tv: (te[t], f, 0), pipeline_mode=once)],
            out_specs=pl.BlockSpec((MOE_FT, d), lambda t, f, te, tm, tv: (t, 0)),
            scratch_shapes=[pltpu.VMEM((MOE_FT, d), F32)]),
        out_shape=jax.ShapeDtypeStruct((rmax, d), BF16),
        compiler_params=pltpu.CompilerParams(dimension_semantics=("parallel", "arbitrary"),
                                             vmem_limit_bytes=58 * 1024 * 1024),
        name="moe_experts",
    )(tile_exp, tile_map, tile_valid, hs, wg, wu, wd)

    return pl.pallas_call(
        functools.partial(_moe_combine_kernel, ne=ne, nsteps=nsteps, sizes=sizes),
        grid_spec=pltpu.PrefetchScalarGridSpec(
            num_scalar_prefetch=2,
            grid=(nti, ne),
            in_specs=[pl.BlockSpec((tt, d), lambda i, e, p_, c_: (i, 0)),
                      pl.BlockSpec((tt, LANES), lambda i, e, p_, c_: (i, 0)),
                      pl.BlockSpec((tt, LANES), lambda i, e, p_, c_: (i, 0)),
                      pl.BlockSpec((1, 6, d), lambda i, e, p_, c_: ((i // per) if many else 0, 0, 0)),
                      pl.BlockSpec(memory_space=pl.ANY)],
            out_specs=pl.BlockSpec((tt, d), lambda i, e, p_, c_: (i, 0)),
            scratch_shapes=[pltpu.VMEM((tt, d), F32), pltpu.VMEM((2, tt, d), BF16),
                            pltpu.SemaphoreType.DMA((2, len(sizes)))]),
        out_shape=jax.ShapeDtypeStruct((m, d), F32),
        compiler_params=_cparams("arbitrary", "arbitrary"),
        name="moe_combine",
    )(pos, c16, x2, we, rank, mod, os_sorted)


def _rope_table(t):
    rows = t // GRID_W
    row = jnp.repeat(jnp.arange(rows, dtype=F32), GRID_W)
    col = jnp.tile(jnp.arange(GRID_W, dtype=F32), rows)
    half = HEAD_DIM // 2
    inv_freq = ROPE_THETA ** (-jnp.arange(0, half, 2, dtype=F32) / half)
    ar = row[:, None] * inv_freq
    ac = col[:, None] * inv_freq
    ang = jnp.concatenate([ar, ar, ac, ac] * 2, axis=-1)
    return jnp.stack([jnp.cos(ang), jnp.sin(ang)])


def _pad_rows(a, rows):
    return jnp.zeros((rows,) + a.shape[1:], a.dtype).at[:a.shape[0]].set(a)


def _hi_lo(w):
    hi = w.astype(BF16)
    return hi, (w - hi.astype(F32)).astype(BF16)


def kernel(x, c, ctx, c_ctx, norm1, norm2, w_mod, b_mod, w_in, tshift, w0, w_decay_up, a0, w_iclr_up, v0, w_vres_down, w_vres_up, w_gate_up, k_k, k_a, r_k, lnx_g, lnx_b, q_norm, k_norm, sink, w_br_rwkv, w_br_attn, w_out, ffn_gate, ffn_up, ffn_down, router, moe_gate, moe_up, moe_down):
    nb, t, d = x.shape
    cn = ctx.shape[1]
    depth = w_in.shape[0]
    mods = _modulation(c, c_ctx, w_mod, b_mod)
    cs = _rope_table(t)
    e128 = jnp.kron(jnp.eye(2, dtype=F32), jnp.ones((HEAD_DIM, HEAD_DIM), F32)).astype(BF16)
    xl = x.reshape(nb * t, d)
    xc = ctx.reshape(nb * cn, d)
    vf_l = vf_c = None
    for layer in range(depth):
        last = layer == depth - 1
        mod_l = mods[layer, :nb].reshape(nb, 6, d)
        mod_c = mods[layer, nb:nb + 1].reshape(1, 6, d)
        w_in_b = w_in[layer].astype(BF16)
        vd_b = None
        prm = {"tshift": tshift[layer], "e128": e128, "wg": w_gate_up[layer].astype(BF16)}
        prm["pd"] = jnp.stack(
            [_pad_rows(jnp.stack([w0[layer, dd], a0[layer, dd]]), SUBLANES) for dd in range(2)])
        shared = [k_k[layer], k_a[layer], r_k[layer].reshape(RW_DIM)]
        if layer > 0:
            shared.append(v0[layer - 1])
            vd_b = _pad_rows(w_vres_down[layer - 1].T, LANES).T.astype(BF16)
            prm["vu_hi"], prm["vu_lo"] = _hi_lo(_pad_rows(w_vres_up[layer - 1], LANES))
        prm["ps"] = _pad_rows(jnp.stack(shared), SUBLANES)
        wla = jnp.zeros((2, LANES, 2 * RW_DIM), F32)
        wla = wla.at[:, :HEAD_DIM, :RW_DIM].set(w_decay_up[layer]).at[:, HEAD_DIM:, RW_DIM:].set(w_iclr_up[layer])
        prm["wla"] = wla.astype(BF16)

        p_l, hv_l = _inproj(xl, norm1[layer], mod_l, w_in_b, vd_b, t)
        p_c, hv_c = _inproj(xc, norm1[layer], mod_c, w_in_b, vd_b, cn)

        s0 = jnp.zeros((nb, N_PAIRS, LANES, LANES), F32)
        y_c, aux_c, s_ctx = _rwkv_both(p_c, vf_c, hv_c, [s0, s0], prm, nb, cn)
        y_l, aux_l, _ = _rwkv_both(p_l, vf_l, hv_l, s_ctx, prm, nb, t)
        if layer == 0:
            vf_l, vf_c = aux_l[1], aux_c[1]

        qg = jnp.tile(q_norm[layer], 2).reshape(1, LANES)
        kg = jnp.tile(k_norm[layer], 2).reshape(1, LANES)
        qn_l, kn_l, vb_l = _qk_prep(p_l, cs, qg, kg, e128, t, True)
        qn_c, kn_c, vb_c = _qk_prep(p_c, None, qg, kg, e128, cn, False)
        sk = sink[layer].reshape(1, -1)
        ya_l = _attention(qn_l, kn_l, vb_l, kn_c, vb_c, sk, nb, t, cn, True)

        ln = jnp.stack([lnx_g[layer], lnx_b[layer]])
        wbr = w_br_rwkv[layer].astype(BF16)
        wba = w_br_attn[layer].astype(BF16)
        wo = w_out[layer].astype(BF16)
        xl = _merge(xl, y_l, aux_l, ya_l, p_l, ln, e128, wbr, wba, wo, mod_l, t)
        if not last:
            ya_c = _attention(qn_c, None, None, kn_c, vb_c, sk, nb, cn, cn, False)
            xc = _merge(xc, y_c, aux_c, ya_c, p_c, ln, e128, wbr, wba, wo, mod_c, cn)

        i = layer // 2
        if layer % 2 == 0:
            fw = (ffn_gate[i].astype(BF16), ffn_up[i].astype(BF16), ffn_down[i].astype(BF16))
            xl = _ffn(xl, norm2[layer], mod_l, *fw, t)
            if not last:
                xc = _ffn(xc, norm2[layer], mod_c, *fw, cn)
        else:
            rt = _pad_rows(router[i].T, LANES).T
            mw = (moe_gate[i].astype(BF16), moe_up[i].astype(BF16), moe_down[i].astype(BF16))
            xl = _moe(xl, norm2[layer], mod_l, rt, *mw, t)
            if not last:
                xc = _moe(xc, norm2[layer], mod_c, rt, *mw, cn)
    return xl.reshape(nb, t, d)
```

```python
import functools

import jax
import jax.numpy as jnp
from jax import lax
from jax.experimental import pallas as pl
from jax.experimental.pallas import tpu as pltpu

F32 = jnp.float32
BF16 = jnp.bfloat16
HIGHEST = lax.Precision.HIGHEST

LANES = 128
SUBLANES = 8
PACK = 16
ACT = jnp.bfloat16
HEAD_DIM = 64
RW_DIM = 512
ATT_DIM = 512
KV_DIM = 128
RW_IN = 1792
ATT_OFF = RW_IN
GATE_OFF = RW_IN + ATT_DIM + 2 * KV_DIM
N_EXPERTS = 8
CHUNK = 64
N_PAIRS = RW_DIM // LANES
MOE_TILE = 1024
MOE_BLK = 128
MOE_FT = 512
SCAN_GROUP = 4
ATT_SCALE = HEAD_DIM ** -0.5
ROPE_THETA = 10000.0
GRID_W = 64
RMS_EPS = 1e-6
GN_EPS = 64e-5
NEG_INF = -1e30
DECAY_SCALE = 0.6065306597126334
VMEM_LIMIT = 48 * 1024 * 1024


def _cparams(*sem):
    return pltpu.CompilerParams(dimension_semantics=sem, vmem_limit_bytes=VMEM_LIMIT)


def _row_tile(rows, cap):
    t = cap
    while rows % t:
        t //= 2
    return t


def _bdot(a, b):
    return jnp.dot(a.astype(BF16), b.astype(BF16), preferred_element_type=F32)


def _bdot_nt(a, b):
    return lax.dot_general(a.astype(BF16), b.astype(BF16), (((1,), (1,)), ((), ())),
                           preferred_element_type=F32)


def _bdot_tn(a, b):
    return lax.dot_general(a.astype(BF16), b.astype(BF16), (((0,), (0,)), ((), ())),
                           preferred_element_type=F32)


def _split2(x):
    hi = x.astype(BF16)
    lo = (x - hi.astype(F32)).astype(BF16)
    return hi, lo


def _dot3(a, w_hi, w_lo):
    a_hi, a_lo = _split2(a)
    return (jnp.dot(a_hi, w_hi, preferred_element_type=F32)
            + (jnp.dot(a_lo, w_hi, preferred_element_type=F32)
               + jnp.dot(a_hi, w_lo, preferred_element_type=F32)))


def _segsum(x, e, exact=True):
    outs = []
    for s in range(x.shape[1] // LANES):
        hi, lo = _split2(x[:, s * LANES:(s + 1) * LANES])
        acc = jnp.dot(hi, e, preferred_element_type=F32)
        outs.append(acc + jnp.dot(lo, e, preferred_element_type=F32) if exact else acc)
    return outs[0] if len(outs) == 1 else jnp.concatenate(outs, axis=1)


def _interleave(*stages):
    live = list(stages)
    while live:
        for g in list(live):
            if next(g, live) is live:
                live.remove(g)


def _norm_mod(x, gain, shift, scale):
    y = x * lax.rsqrt(jnp.mean(x * x, axis=-1, keepdims=True) + RMS_EPS) * gain
    return y * (1.0 + scale) + shift


def _mod_kernel(c_ref, w_ref, b_ref, o_ref):
    c = c_ref[...]
    s = c * jax.nn.sigmoid(c)
    o_ref[0] = jnp.dot(s, w_ref[0], precision=HIGHEST, preferred_element_type=F32) + b_ref[0]


def _modulation(c, c_ctx, w_mod, b_mod):
    depth, d, n6 = w_mod.shape
    nb = c.shape[0]
    rm = -(-(nb + 1) // SUBLANES) * SUBLANES
    cc = jnp.zeros((rm, d), F32).at[:nb].set(c).at[nb].set(c_ctx)
    tn = _row_tile(n6, 1536)
    return pl.pallas_call(
        _mod_kernel,
        grid=(depth, n6 // tn),
        in_specs=[pl.BlockSpec((rm, d), lambda l, n: (0, 0)),
                  pl.BlockSpec((1, d, tn), lambda l, n: (l, 0, n)),
                  pl.BlockSpec((1, 1, tn), lambda l, n: (l, 0, n))],
        out_specs=pl.BlockSpec((1, rm, tn), lambda l, n: (l, 0, n)),
        out_shape=jax.ShapeDtypeStruct((depth, rm, n6), F32),
        compiler_params=_cparams("parallel", "parallel"),
        name="adaln_mod",
    )(cc, w_mod, b_mod.reshape(depth, 1, n6))


def _inproj_kernel(*refs, vres, nsub):
    if vres:
        x_ref, g_ref, mod_ref, w_ref, vd_ref, o_ref, hv_ref = refs
    else:
        x_ref, g_ref, mod_ref, w_ref, o_ref = refs
    sub = x_ref.shape[0] // nsub
    for s in range(nsub):
        rs = slice(s * sub, (s + 1) * sub)
        h = _norm_mod(x_ref[rs, :], g_ref[...], mod_ref[0, 0:1, :], mod_ref[0, 1:2, :]).astype(BF16)
        if vres:
            hv_ref[rs, :] = jnp.dot(h, vd_ref[...], preferred_element_type=F32)
        o_ref[rs, :] = jnp.dot(h, w_ref[...], preferred_element_type=F32).astype(o_ref.dtype)


def _inproj(x2, gain, mod, w_b, vd_b, rows):
    m, d = x2.shape
    n = w_b.shape[1]
    tm = _row_tile(rows, 1024)
    per = rows // tm
    many = mod.shape[0] > 1
    vres = vd_b is not None
    once = pl.Buffered(1)
    in_specs = [pl.BlockSpec((tm, d), lambda i: (i, 0)),
                pl.BlockSpec((1, d), lambda i: (0, 0)),
                pl.BlockSpec((1, 6, d), lambda i: ((i // per) if many else 0, 0, 0)),
                pl.BlockSpec((d, n), lambda i: (0, 0), pipeline_mode=once)]
    args = [x2, gain.reshape(1, d), mod, w_b]
    out_specs = [pl.BlockSpec((tm, n), lambda i: (i, 0))]
    out_shape = [jax.ShapeDtypeStruct((m, n), ACT)]
    if vres:
        in_specs.append(pl.BlockSpec((d, LANES), lambda i: (0, 0)))
        args.append(vd_b)
        out_specs.append(pl.BlockSpec((tm, LANES), lambda i: (i, 0)))
        out_shape.append(jax.ShapeDtypeStruct((m, LANES), F32))
    outs = pl.pallas_call(
        functools.partial(_inproj_kernel, vres=vres, nsub=max(1, tm // 256)),
        grid=(m // tm,),
        in_specs=in_specs, out_specs=out_specs, out_shape=out_shape,
        compiler_params=_cparams("parallel"),
        name="in_proj",
    )(*args)
    return (outs[0], outs[1]) if vres else (outs[0], None)


def _qkprep_kernel(*refs, rope):
    if rope:
        q0_ref, q1_ref, k_ref, v_ref, cs_ref, qg_ref, kg_ref, e_ref, qn_ref, kn_ref, vb_ref = refs
    else:
        q0_ref, q1_ref, k_ref, v_ref, qg_ref, kg_ref, e_ref, qn_ref, kn_ref, vb_ref = refs
    e = e_ref[...]
    lane = lax.broadcasted_iota(jnp.int32, (1, LANES), 1)
    first = (lane & 31) < 16

    def norm_rope(u, gain):
        y = u * lax.rsqrt(_segsum(u * u, e) * (1.0 / HEAD_DIM) + RMS_EPS) * gain
        if rope:
            rot = jnp.where(first, -pltpu.roll(y, LANES - 16, 1), pltpu.roll(y, 16, 1))
            y = y * cs_ref[0] + rot * cs_ref[1]
        return y

    qg = qg_ref[...]
    for s, ref in enumerate((q0_ref, q1_ref)):
        for t in range(2):
            u = ref[:, t * LANES:(t + 1) * LANES].astype(F32)
            c0 = (2 * s + t) * LANES
            qn_ref[:, c0:c0 + LANES] = (norm_rope(u, qg) * ATT_SCALE).astype(BF16)
    low = lane < HEAD_DIM
    for src_val, dst in ((norm_rope(k_ref[...].astype(F32), kg_ref[...]), kn_ref), (v_ref[...].astype(F32), vb_ref)):
        swapped = pltpu.roll(src_val, HEAD_DIM, 1)
        dst[:, 0:LANES] = jnp.where(low, src_val, swapped).astype(BF16)
        dst[:, LANES:2 * LANES] = jnp.where(low, swapped, src_val).astype(BF16)


def _qk_prep(p, cs, qg, kg, e128, rows, rope):
    m = p.shape[0]
    tm = _row_tile(rows, 512)
    per = rows // tm
    qb = ATT_OFF // 256
    kb = (ATT_OFF + ATT_DIM) // LANES
    in_specs = [pl.BlockSpec((tm, 256), lambda i: (i, qb)),
                pl.BlockSpec((tm, 256), lambda i: (i, qb + 1)),
                pl.BlockSpec((tm, LANES), lambda i: (i, kb)),
                pl.BlockSpec((tm, LANES), lambda i: (i, kb + 1))]
    args = [p, p, p, p]
    if rope:
        in_specs.append(pl.BlockSpec((2, tm, LANES), lambda i: (0, i % per, 0)))
        args.append(cs)
    in_specs += [pl.BlockSpec((1, LANES), lambda i: (0, 0)),
                 pl.BlockSpec((1, LANES), lambda i: (0, 0)),
                 pl.BlockSpec((LANES, LANES), lambda i: (0, 0))]
    args += [qg, kg, e128]
    return pl.pallas_call(
        functools.partial(_qkprep_kernel, rope=rope),
        grid=(m // tm,),
        in_specs=in_specs,
        out_specs=[pl.BlockSpec((tm, ATT_DIM), lambda i: (i, 0)),
                   pl.BlockSpec((tm, 2 * KV_DIM), lambda i: (i, 0)),
                   pl.BlockSpec((tm, 2 * KV_DIM), lambda i: (i, 0))],
        out_shape=[jax.ShapeDtypeStruct((m, ATT_DIM), BF16),
                   jax.ShapeDtypeStruct((m, 2 * KV_DIM), BF16),
                   jax.ShapeDtypeStruct((m, 2 * KV_DIM), BF16)],
        compiler_params=_cparams("parallel"),
        name="qk_prep",
    )(*args)


def _attn_kernel(*refs, nq, local):
    if local:
        sink_ref, q_ref, kp_ref, kc_ref, kn_ref, vp_ref, vc_ref, vn_ref, kx_ref, vx_ref, wm_ref, o_ref = refs
        k = jnp.concatenate([kp_ref[...], kc_ref[...], kn_ref[...], kx_ref[...]], axis=0)
        v = jnp.concatenate([vp_ref[...], vc_ref[...], vn_ref[...], vx_ref[...]], axis=0)
    else:
        sink_ref, q_ref, kx_ref, vx_ref, o_ref = refs
        k = kx_ref[...]
        v = vx_ref[...]
    i = pl.program_id(1)
    bq = q_ref.shape[0]
    low = lax.broadcasted_iota(jnp.int32, (1, LANES), 1) < HEAD_DIM
    row = lax.broadcasted_iota(jnp.int32, (2 * bq, 1), 0)
    if local:
        kblk = lax.broadcasted_iota(jnp.int32, (1, 3 * bq), 1) // bq
        off_end = ((kblk == 0) & (i == 0)) | ((kblk == 2) & (i == nq - 1))
        bias = wm_ref[...] + jnp.where(off_end, NEG_INF, 0.0)
    def head_pair(a):
        g = a // 2
        kb = k[:, g * LANES:(g + 1) * LANES]
        vb = v[:, g * LANES:(g + 1) * LANES]
        qa = q_ref[:, a * LANES:(a + 1) * LANES].astype(F32)
        qs = jnp.concatenate([jnp.where(low, qa, 0.0), jnp.where(low, 0.0, qa)], axis=0).astype(BF16)
        s = lax.dot_general(qs, kb, (((1,), (1,)), ((), ())), preferred_element_type=F32)
        yield
        if local:
            s = jnp.concatenate([s[:, :3 * bq] + bias, s[:, 3 * bq:]], axis=1)
        sk = jnp.where(row < bq, sink_ref[0, 2 * a], sink_ref[0, 2 * a + 1])
        mx = jnp.maximum(jnp.max(s, axis=1, keepdims=True), sk)
        yield
        ex = jnp.exp(s - mx)
        den = jnp.sum(ex, axis=1, keepdims=True) + jnp.exp(sk - mx)
        yield
        o = jnp.dot(ex.astype(BF16), vb, preferred_element_type=F32) / den
        o_ref[:, a * LANES:(a + 1) * LANES] = jnp.where(low, o[0:bq], o[bq:2 * bq]).astype(o_ref.dtype)

    _interleave(*[head_pair(a) for a in range(4)])


def _attention(qn, kn, vb, kctx, vctx, sink, nb, rows, cn, local):
    m = qn.shape[0]
    bq = 128
    nq = rows // bq
    smem = pl.BlockSpec(memory_space=pltpu.SMEM)
    qspec = pl.BlockSpec((bq, ATT_DIM), lambda b, i: (b * nq + i, 0))
    cspec = pl.BlockSpec((cn, 2 * KV_DIM), lambda b, i: (b, 0))
    if local:
        prv = pl.BlockSpec((bq, 2 * KV_DIM), lambda b, i: (b * nq + jnp.maximum(i - 1, 0), 0))
        cur = pl.BlockSpec((bq, 2 * KV_DIM), lambda b, i: (b * nq + i, 0))
        nxt = pl.BlockSpec((bq, 2 * KV_DIM), lambda b, i: (b * nq + jnp.minimum(i + 1, nq - 1), 0))
        qo = jnp.arange(2 * bq)[:, None] % bq
        kcol = jnp.arange(3 * bq)[None, :]
        inwin = ((kcol // bq == 1) | ((kcol // bq == 0) & (kcol % bq >= qo)) | ((kcol // bq == 2) & (kcol % bq <= qo)))
        wmask = jnp.where(inwin, 0.0, NEG_INF).astype(F32)
        in_specs = [smem, qspec, prv, cur, nxt, prv, cur, nxt, cspec, cspec,
                    pl.BlockSpec((2 * bq, 3 * bq), lambda b, i: (0, 0))]
        args = [sink, qn, kn, kn, kn, vb, vb, vb, kctx, vctx, wmask]
    else:
        in_specs = [smem, qspec, cspec, cspec]
        args = [sink, qn, kctx, vctx]
    return pl.pallas_call(
        functools.partial(_attn_kernel, nq=nq, local=local),
        grid=(nb, nq),
        in_specs=in_specs,
        out_specs=pl.BlockSpec((bq, ATT_DIM), lambda b, i: (b * nq + i, 0)),
        out_shape=jax.ShapeDtypeStruct((m, ATT_DIM), BF16),
        compiler_params=_cparams("parallel", "parallel"),
        name="window_attn" if local else "ctx_attn",
    )(*args)


def _rwkv_kernel(*refs, nt, tq, vres, drn, tps):
    it = iter(refs)
    n_in = 5 if vres else 3
    first = [next(it) for _ in range(n_in)] if tps == 2 else None
    tiles = [[next(it) for _ in range(n_in)] for _ in range(tps)]
    ts_ref, pd_ref, ps_ref, wla_ref, wg_ref = (next(it) for _ in range(5))
    if vres:
        vuh_ref, vul_ref = next(it), next(it)
    e_ref, sin_ref = next(it), next(it)
    y_ref, aux_ref, sout_ref = next(it), next(it), next(it)
    ops = [next(it) for _ in range(tps)]
    auxs = [next(it) for _ in range(tps)]
    s_sc = next(it)

    j = pl.program_id(1)
    sgn = 1 - 2 * drn
    nch = tq // CHUNK
    nsteps = nt // tps
    e = e_ref[...]
    lane = lax.broadcasted_iota(jnp.int32, (1, LANES), 1)
    low = lane < HEAD_DIM

    def natural(k):
        return k if drn == 0 else nt - 1 - k

    def prepare(tile, in_refs, ops_sc, aux_sc):
        p_ref, pv_ref, nx_ref = in_refs[:3]
        rw = p_ref[...].astype(F32)
        row = lax.broadcasted_iota(jnp.int32, (tq, 1), 0)
        prev_row = jnp.where(tile > 0, pv_ref[PACK - 1:PACK, :].astype(F32), 0.0)
        next_row = jnp.where(tile < nt - 1, nx_ref[0:1, :].astype(F32), 0.0)
        prv = jnp.where(row == 0, prev_row, pltpu.roll(rw, 1, 0))
        yield
        nxt = jnp.where(row == tq - 1, next_row, pltpu.roll(rw, tq - 1, 0))
        xs = rw + ts_ref[0:1, :] * (prv - rw) + ts_ref[1:2, :] * (nxt - rw)
        yield

        r = xs[:, 0:RW_DIM]
        k = xs[:, RW_DIM:2 * RW_DIM]
        v = xs[:, 2 * RW_DIM:3 * RW_DIM]
        la = xs[:, 3 * RW_DIM:3 * RW_DIM + LANES]
        la = jnp.where(low, jnp.tanh(la), la)
        la_hi, la_lo = _split2(la)
        lora = (jnp.dot(la_hi, wla_ref[0], preferred_element_type=F32)
                + jnp.dot(la_lo, wla_ref[0], preferred_element_type=F32))
        yield
        logw = -DECAY_SCALE * jax.nn.sigmoid(pd_ref[0, 0:1, :] + lora[:, 0:RW_DIM])
        iclr = jax.nn.sigmoid(pd_ref[0, 1:2, :] + lora[:, RW_DIM:2 * RW_DIM])
        if vres:
            vf_ref, hv_ref = in_refs[3:]
            mix = jax.nn.sigmoid(ps_ref[3:4, :] + _dot3(hv_ref[...], vuh_ref[...], vul_ref[...]))
            v = v + (vf_ref[...].astype(F32) - v) * mix
        yield
        kk = k * ps_ref[0:1, :]
        kk = kk * lax.rsqrt(_segsum(kk * kk, e, exact=False) + 1e-12)
        yield
        kh = k * (1.0 + (iclr - 1.0) * ps_ref[1:2, :])
        bb = kk * iclr
        aux_sc[:, 0:RW_DIM] = (_segsum(r * kh * ps_ref[2:3, :], e, exact=False) * v).astype(aux_sc.dtype)
        if drn == 0:
            glo = xs[:, 3 * RW_DIM + LANES:RW_IN]
            aux_sc[:, RW_DIM:2 * RW_DIM] = _bdot(jax.nn.sigmoid(glo), wg_ref[...]).astype(aux_sc.dtype)
        else:
            aux_sc[:, RW_DIM:2 * RW_DIM] = v.astype(aux_sc.dtype)

        yield
        ti = lax.broadcasted_iota(jnp.int32, (tq, tq), 0)
        si = lax.broadcasted_iota(jnp.int32, (tq, tq), 1)
        tri = (((ti // CHUNK) == (si // CHUNK)) & (((si - ti) * sgn) <= 0)).astype(BF16)
        lw_hi, lw_lo = _split2(logw)
        cin = jnp.dot(tri, lw_hi, preferred_element_type=F32) + jnp.dot(tri, lw_lo, preferred_element_type=F32)
        cin3 = cin.reshape(nch, CHUNK, RW_DIM)
        last = CHUNK - 1 if drn == 0 else 0
        ctot = jnp.broadcast_to(cin3[:, last:last + 1, :], cin3.shape).reshape(tq, RW_DIM)
        yield
        e_neg = jnp.exp(-cin)
        e_rem = jnp.exp(ctot - cin)
        ops_sc[0] = kk * jnp.exp(cin - logw)
        ops_sc[1] = r * jnp.exp(cin)
        ops_sc[2] = bb * e_neg
        ops_sc[3] = kh * e_neg
        yield
        ops_sc[4] = kh * e_rem
        ops_sc[5] = bb * e_rem
        ops_sc[6] = v
        ops_sc[7] = jnp.exp(ctot)

    tr = lax.broadcasted_iota(jnp.int32, (CHUNK, LANES), 0)
    sc = lax.broadcasted_iota(jnp.int32, (CHUNK, LANES), 1) & (CHUNK - 1)
    rel = (sc - tr) * sgn
    strict = rel < 0
    incl = rel <= 0
    eye = (sc == tr).astype(F32)
    same_head = ((lax.broadcasted_iota(jnp.int32, (LANES, LANES), 0) // HEAD_DIM)
                 == (lax.broadcasted_iota(jnp.int32, (LANES, LANES), 1) // HEAD_DIM))

    def stack(x):
        return jnp.concatenate([jnp.where(low, x, 0.0), jnp.where(low, 0.0, x)], axis=0)

    def hdot(x, y):
        return _bdot(x, stack(y))

    def scan_tile(ops_sc, row0):
        st = [s_sc[pr] for pr in range(N_PAIRS)]
        for g0 in range(0, nch, SCAN_GROUP):
            yield from scan_chunks(ops_sc, row0, st, range(g0, g0 + SCAN_GROUP))
        for pr in range(N_PAIRS):
            s_sc[pr] = st[pr]

    def scan_chunks(ops_sc, row0, st, chunk_ids):
        probs = []
        for ii in chunk_ids:
            off = (ii if drn == 0 else nch - 1 - ii) * CHUNK
            for pr in range(N_PAIRS):
                ls = slice(pr * LANES, (pr + 1) * LANES)
                q = {"off": off, "ls": ls, "pr": pr}
                for n, name in enumerate(("kkt", "rt", "bh", "kh", "kp", "bp", "v")):
                    q[name] = ops_sc[n, off:off + CHUNK, ls]
                q["wl"] = ops_sc[7, off:off + 1, ls]
                probs.append(q)
        for q in probs:
            g = _bdot_nt(jnp.concatenate([q["kkt"], q["rt"]], axis=0),
                         jnp.concatenate([stack(q["bh"]), stack(q["kh"])], axis=0))
            q["nn"] = jnp.where(strict, -g[0:CHUNK, 0:LANES], 0.0)
            q["avk"] = jnp.where(strict, g[0:CHUNK, LANES:], 0.0)
            q["arb"] = jnp.where(incl, g[CHUNK:, 0:LANES], 0.0)
            q["ark"] = jnp.where(incl, g[CHUNK:, LANES:], 0.0)
            q["t"] = eye + q["nn"]
        yield
        for q in probs:
            q["q"] = hdot(q["nn"], q["nn"])
        for _ in range(4):
            for q in probs:
                tq2 = _bdot(jnp.concatenate([q["t"], q["q"]], axis=0), stack(q["q"]))
                q["t"] = q["t"] + tq2[:CHUNK]
                q["q"] = tq2[CHUNK:]
            yield
        for q in probs:
            q["t"] = q["t"] + hdot(q["t"], q["q"])
        for q in probs:
            av = _bdot(jnp.concatenate([q["avk"], q["ark"]], axis=0), stack(q["v"]))
            q["av"], q["arkv"] = av[:CHUNK], av[CHUNK:]
        yield
        for q in probs:
            tt2 = _bdot(q["t"], jnp.concatenate([stack(q["kkt"]), stack(q["av"])], axis=1))
            q["tk"], q["tav"] = tt2[:, :LANES], tt2[:, LANES:]
        yield
        for q in probs:
            tn = _bdot_tn(jnp.concatenate([q["tk"], q["tav"]], axis=1), q["bp"])
            q["mb"] = jnp.where(same_head, tn[:LANES], 0.0)
            q["nm"] = jnp.where(same_head, _bdot_tn(q["v"], q["kp"]) - tn[LANES:], 0.0)
            z = _bdot(q["arb"], jnp.concatenate([stack(q["tk"]), stack(q["tav"])], axis=1))
            q["r2"] = q["rt"] - z[:, :LANES]
            q["y0"] = q["arkv"] - z[:, LANES:]
        yield

        outs = []
        for q in probs:
            sm = st[q["pr"]]
            outs.append(_bdot_nt(q["r2"], sm) + q["y0"])
            st[q["pr"]] = sm * q["wl"] - _bdot(sm, q["mb"]) + q["nm"]
        for q, yo in zip(probs, outs):
            y_ref[row0 + q["off"]:row0 + q["off"] + CHUNK, q["ls"]] = yo.astype(y_ref.dtype)

    run = _interleave

    @pl.when(j == 0)
    def _():
        s_sc[...] = sin_ref[0]

    if tps == 1:
        run(prepare(natural(j), tiles[0], ops[0], auxs[0]))
        aux_ref[...] = auxs[0][...]
        run(scan_tile(ops[0], 0))
    else:
        @pl.when(j == 0)
        def _():
            run(prepare(natural(0), first, ops[0], auxs[0]))

        rows = [0, tq] if drn == 0 else [tq, 0]
        aux_ref[rows[0]:rows[0] + tq, :] = auxs[0][...]
        run(scan_tile(ops[0], rows[0]), prepare(natural(2 * j + 1), tiles[0], ops[1], auxs[1]))
        aux_ref[rows[1]:rows[1] + tq, :] = auxs[1][...]
        run(scan_tile(ops[1], rows[1]),
            prepare(natural(jnp.minimum(2 * j + 2, nt - 1)), tiles[1], ops[0], auxs[0]))

    @pl.when(j == nsteps - 1)
    def _():
        sout_ref[0] = s_sc[...]


def _rwkv(p, vf_src, hv, state_in, prm, nb, rows, drn):
    m = p.shape[0]
    tq = _row_tile(rows, 256)
    nt = rows // tq
    tps = 2 if nt % 2 == 0 else 1
    nsteps = nt // tps
    rb = tq // PACK
    vres = vf_src is not None

    def tile_specs(scan_pos):
        def tile(b, j):
            k = jnp.minimum(scan_pos(j), nt - 1)
            return b * nt + (k if drn == 0 else nt - 1 - k)

        specs = [pl.BlockSpec((tq, RW_IN), lambda b, j: (tile(b, j), 0)),
                 pl.BlockSpec((PACK, RW_IN), lambda b, j: (jnp.maximum(tile(b, j) * rb - 1, 0), 0)),
                 pl.BlockSpec((PACK, RW_IN), lambda b, j: (jnp.minimum((tile(b, j) + 1) * rb, m // PACK - 1), 0))]
        args = [p, p, p]
        if vres:
            specs += [pl.BlockSpec((tq, RW_DIM), lambda b, j: (tile(b, j), 1)),
                      pl.BlockSpec((tq, LANES), lambda b, j: (tile(b, j), 0))]
            args += [vf_src, hv]
        return specs, args

    if tps == 2:
        positions = [lambda j: 0 * j, lambda j: 2 * j + 1, lambda j: 2 * j + 2]
    else:
        positions = [lambda j: j]
    in_specs, args = [], []
    for pos in positions:
        s_, a_ = tile_specs(pos)
        in_specs += s_
        args += a_

    const2 = lambda b, j: (0, 0)
    dir3 = lambda b, j: (drn, 0, 0)
    in_specs += [pl.BlockSpec((2, RW_IN), const2),
                 pl.BlockSpec((1, SUBLANES, RW_DIM), dir3),
                 pl.BlockSpec((SUBLANES, RW_DIM), const2),
                 pl.BlockSpec((1, LANES, 2 * RW_DIM), dir3),
                 pl.BlockSpec((LANES, RW_DIM), const2)]
    args += [prm["tshift"], prm["pd"], prm["ps"], prm["wla"], prm["wg"]]
    if vres:
        in_specs += [pl.BlockSpec((LANES, RW_DIM), const2), pl.BlockSpec((LANES, RW_DIM), const2)]
        args += [prm["vu_hi"], prm["vu_lo"]]
    sspec = pl.BlockSpec((1, N_PAIRS, LANES, LANES), lambda b, j: (b, 0, 0, 0))
    in_specs += [pl.BlockSpec((LANES, LANES), const2), sspec]
    args += [prm["e128"], state_in]

    def out_block(b, j):
        return (b * nsteps + (j if drn == 0 else nsteps - 1 - j), 0)

    return pl.pallas_call(
        functools.partial(_rwkv_kernel, nt=nt, tq=tq, vres=vres, drn=drn, tps=tps),
        grid=(nb, nsteps),
        in_specs=in_specs,
        out_specs=[pl.BlockSpec((tps * tq, RW_DIM), out_block),
                   pl.BlockSpec((tps * tq, 2 * RW_DIM), out_block),
                   sspec],
        out_shape=[jax.ShapeDtypeStruct((m, RW_DIM), ACT),
                   jax.ShapeDtypeStruct((m, 2 * RW_DIM), ACT),
                   jax.ShapeDtypeStruct((nb, N_PAIRS, LANES, LANES), F32)],
        scratch_shapes=([pltpu.VMEM((8, tq, RW_DIM), F32)] * tps + [pltpu.VMEM((tq, 2 * RW_DIM), ACT)] * tps
                        + [pltpu.VMEM((N_PAIRS, LANES, LANES), F32)]),
        compiler_params=_cparams("parallel", "arbitrary"),
        name="rwkv7_fwd" if drn == 0 else "rwkv7_bwd",
    )(*args)


def _rwkv_both(p, vf_src, hv, states, prm, nb, rows):
    outs = [_rwkv(p, vf_src, hv, states[d], prm, nb, rows, d) for d in range(2)]
    return [o[0] for o in outs], [o[1] for o in outs], [o[2] for o in outs]


def _merge_kernel(x_ref, y0_ref, y1_ref, a0_ref, a1_ref, ya_ref, g0_ref, g1_ref, g2_ref, g3_ref, ln_ref, e_ref,
                  wbr_ref, wba_ref, wo_ref, mod_ref, o_ref):
    e = e_ref[...]
    y = y0_ref[...].astype(F32) + y1_ref[...].astype(F32)
    mu = _segsum(y, e) * (1.0 / HEAD_DIM)
    yc = y - mu
    var = _segsum(yc * yc, e) * (1.0 / HEAD_DIM)
    yn = yc * lax.rsqrt(var + GN_EPS) * ln_ref[0:1, :] + ln_ref[1:2, :]
    cv = a0_ref[:, 0:RW_DIM].astype(F32) + a1_ref[:, 0:RW_DIM].astype(F32)
    yrw = (yn + cv) * a0_ref[:, RW_DIM:2 * RW_DIM].astype(F32)
    a = _bdot(yrw, wbr_ref[...])
    b = jnp.dot(ya_ref[...], wba_ref[...], preferred_element_type=F32)
    h = a.shape[1] // 2
    mrg = jnp.concatenate(
        [jax.nn.sigmoid(g0_ref[...].astype(F32)) * a[:, :h] + jax.nn.sigmoid(g2_ref[...].astype(F32)) * b[:, :h],
         jax.nn.sigmoid(g1_ref[...].astype(F32)) * a[:, h:] + jax.nn.sigmoid(g3_ref[...].astype(F32)) * b[:, h:]],
        axis=1)
    o_ref[...] = x_ref[...] + mod_ref[0, 2:3, :] * _bdot(mrg, wo_ref[...])


def _merge(x2, ys, auxs, ya, p, ln, e128, wbr, wba, wo, mod, rows):
    m, d = x2.shape
    tm = _row_tile(rows, 512)
    per = rows // tm
    many = mod.shape[0] > 1
    gb = GATE_OFF // 512
    const2 = lambda i: (0, 0)
    in_specs = [pl.BlockSpec((tm, d), lambda i: (i, 0)),
                pl.BlockSpec((tm, RW_DIM), lambda i: (i, 0)),
                pl.BlockSpec((tm, RW_DIM), lambda i: (i, 0)),
                pl.BlockSpec((tm, 2 * RW_DIM), lambda i: (i, 0)),
                pl.BlockSpec((tm, 2 * RW_DIM), lambda i: (i, 0)),
                pl.BlockSpec((tm, ATT_DIM), lambda i: (i, 0))]
    in_specs += [pl.BlockSpec((tm, 512), (lambda i, c=c: (i, gb + c))) for c in range(4)]
    in_specs += [pl.BlockSpec((2, RW_DIM), const2),
                 pl.BlockSpec((LANES, LANES), const2),
                 pl.BlockSpec(wbr.shape, const2),
                 pl.BlockSpec(wba.shape, const2),
                 pl.BlockSpec(wo.shape, const2),
                 pl.BlockSpec((1, 6, d), lambda i: ((i // per) if many else 0, 0, 0))]
    return pl.pallas_call(
        _merge_kernel,
        grid=(m // tm,),
        in_specs=in_specs,
        out_specs=pl.BlockSpec((tm, d), lambda i: (i, 0)),
        out_shape=jax.ShapeDtypeStruct((m, d), F32),
        compiler_params=_cparams("parallel"),
        name="branch_merge",
    )(x2, ys[0], ys[1], auxs[0], auxs[1], ya, p, p, p, p, ln, e128, wbr, wba, wo, mod)


def _ffn_kernel(x_ref, g_ref, mod_ref, wg_ref, wu_ref, wd_ref, o_ref):
    def rows(rs):
        x = x_ref[rs, :]
        h = _norm_mod(x, g_ref[...], mod_ref[0, 3:4, :], mod_ref[0, 4:5, :]).astype(BF16)
        yield
        a = jnp.dot(h, wg_ref[...], preferred_element_type=F32)
        u = jnp.dot(h, wu_ref[...], preferred_element_type=F32)
        yield
        act = ((a * jax.nn.sigmoid(a)) * u).astype(BF16)
        yield
        o_ref[rs, :] = x + mod_ref[0, 5:6, :] * jnp.dot(act, wd_ref[...], preferred_element_type=F32)

    half = x_ref.shape[0] // 2
    _interleave(rows(slice(0, half)), rows(slice(half, 2 * half)))


def _ffn(x2, gain, mod, wg, wu, wd, rows):
    m, d = x2.shape
    dff = wg.shape[1]
    tm = _row_tile(rows, 512)
    per = rows // tm
    many = mod.shape[0] > 1
    once = pl.Buffered(1)
    return pl.pallas_call(
        _ffn_kernel,
        grid=(m // tm,),
        in_specs=[pl.BlockSpec((tm, d), lambda i: (i, 0)),
                  pl.BlockSpec((1, d), lambda i: (0, 0)),
                  pl.BlockSpec((1, 6, d), lambda i: ((i // per) if many else 0, 0, 0)),
                  pl.BlockSpec((d, dff), lambda i: (0, 0), pipeline_mode=once),
                  pl.BlockSpec((d, dff), lambda i: (0, 0), pipeline_mode=once),
                  pl.BlockSpec((dff, d), lambda i: (0, 0), pipeline_mode=once)],
        out_specs=pl.BlockSpec((tm, d), lambda i: (i, 0)),
        out_shape=jax.ShapeDtypeStruct((m, d), F32),
        compiler_params=_cparams("parallel"),
        name="ffn_swiglu",
    )(x2, gain.reshape(1, d), mod, wg, wu, wd)


def _dma_pieces(count, sizes, make):
    for k, size in enumerate(sizes):
        @pl.when((count & size) != 0)
        def _(k=k, size=size):
            make(k, pl.multiple_of(count & ~(2 * size - 1), PACK), size)


def _moe_gather_kernel(pos_ref, cnt_ref, h_ref, rankt_ref, zero_ref, out_ref, hs_sc, sem, *, ne, nsteps, sizes):
    del zero_ref
    i = pl.program_id(0)
    ex = pl.program_id(1)
    step = i * ne + ex
    slot = step % 2
    tt = h_ref.shape[0]

    def copies(st, sl, act):
        base = pos_ref[st]

        def make(k, off, size):
            act(pltpu.make_async_copy(hs_sc.at[sl, pl.ds(off, size)],
                                      out_ref.at[pl.ds(pl.multiple_of(base + off, PACK), size)], sem.at[sl, k]))

        _dma_pieces(cnt_ref[st], sizes, make)

    @pl.when(step >= 2)
    def _():
        copies(step - 2, slot, lambda cp: cp.wait())

    rrow = rankt_ref[0, pl.ds(ex, 1), :]
    rowid = lax.broadcasted_iota(jnp.int32, (MOE_BLK, tt), 0).astype(F32)

    def gather(b, carry):
        sel = ((rrow - (b * MOE_BLK).astype(F32)) == rowid).astype(BF16)
        r0 = pl.multiple_of(b * MOE_BLK, MOE_BLK)
        hs_sc[slot, pl.ds(r0, MOE_BLK), :] = jnp.dot(sel, h_ref[...], preferred_element_type=F32).astype(BF16)
        return carry

    lax.fori_loop(0, (cnt_ref[step] + MOE_BLK - 1) // MOE_BLK, gather, 0)
    copies(step, slot, lambda cp: cp.start())

    @pl.when(step == nsteps - 1)
    def _():
        if nsteps >= 2:
            copies(step - 1, 1 - slot, lambda cp: cp.wait())
        copies(step, slot, lambda cp: cp.wait())


def _moe_ffn_kernel(te_ref, tm_ref, tv_ref, hs_ref, wg_ref, wu_ref, wd_ref, os_ref):
    del te_ref, tm_ref
    valid = tv_ref[pl.program_id(0)] != 0

    @pl.when(valid)
    def _():
        def rows(rs):
            hb = hs_ref[rs, :]
            a = jnp.dot(hb, wg_ref[0], preferred_element_type=F32)
            u = jnp.dot(hb, wu_ref[0], preferred_element_type=F32)
            yield
            act = ((a * jax.nn.sigmoid(a)) * u).astype(BF16)
            yield
            os_ref[rs, :] = jnp.dot(act, wd_ref[0], preferred_element_type=F32).astype(os_ref.dtype)

        half = hs_ref.shape[0] // 2
        _interleave(rows(slice(0, half)), rows(slice(half, 2 * half)))

    @pl.when(jnp.logical_not(valid))
    def _():
        os_ref[...] = jnp.zeros_like(os_ref)


def _moe_combine_kernel(pos_ref, cnt_ref, x_ref, we_ref, rank_ref, mod_ref, os_ref, o_ref, acc_sc, ob_sc, sem,
                        *, ne, nsteps, sizes):
    i = pl.program_id(0)
    ex = pl.program_id(1)
    step = i * ne + ex
    slot = step % 2
    tt = x_ref.shape[0]
    sblk = min(2 * MOE_BLK, tt)
    lane = lax.broadcasted_iota(jnp.int32, (tt, LANES), 1)

    def copies(st, sl, act):
        base = pos_ref[st]

        def make(k, off, size):
            act(pltpu.make_async_copy(os_ref.at[pl.ds(pl.multiple_of(base + off, PACK), size)],
                                      ob_sc.at[sl, pl.ds(off, size)], sem.at[sl, k]))

        _dma_pieces(cnt_ref[st], sizes, make)

    @pl.when(step == 0)
    def _():
        ob_sc[...] = jnp.zeros_like(ob_sc)
        copies(0, 0, lambda cp: cp.start())

    @pl.when(step + 1 < nsteps)
    def _():
        copies(step + 1, 1 - slot, lambda cp: cp.start())

    copies(step, slot, lambda cp: cp.wait())

    @pl.when(ex == 0)
    def _():
        acc_sc[...] = jnp.zeros_like(acc_sc)

    rcol = jnp.sum(jnp.where(lane == ex, rank_ref[...], 0.0), axis=1, keepdims=True)
    wcol = jnp.sum(jnp.where(lane == ex, we_ref[...], 0.0), axis=1, keepdims=True)
    colid = lax.broadcasted_iota(jnp.int32, (tt, sblk), 1).astype(F32)

    def scatter(b, carry):
        sel = ((rcol - (b * sblk).astype(F32)) == colid).astype(BF16)
        r0 = pl.multiple_of(b * sblk, sblk)
        acc_sc[...] += wcol * jnp.dot(sel, ob_sc[slot, pl.ds(r0, sblk), :], preferred_element_type=F32)
        return carry

    lax.fori_loop(0, (cnt_ref[step] + sblk - 1) // sblk, scatter, 0)

    @pl.when(ex == ne - 1)
    def _():
        o_ref[...] = x_ref[...] + mod_ref[0, 5:6, :] * acc_sc[...]


def _route_kernel(x_ref, g_ref, mod_ref, rt_ref, h_ref, we_ref, rank_ref, rankt_ref, cnt_ref, *, ne):
    tt = x_ref.shape[0]
    h = _norm_mod(x_ref[...], g_ref[...], mod_ref[0, 3:4, :], mod_ref[0, 4:5, :])
    h_ref[...] = h.astype(BF16)
    lane = lax.broadcasted_iota(jnp.int32, (tt, LANES), 1)
    logits = jnp.dot(h, rt_ref[...], precision=HIGHEST, preferred_element_type=F32)
    lg = jnp.where(lane < ne, logits, NEG_INF)
    t1 = jnp.max(lg, axis=1, keepdims=True)
    i1 = jnp.min(jnp.where(lg == t1, lane, LANES), axis=1, keepdims=True)
    lg2 = jnp.where(lane == i1, NEG_INF, lg)
    t2 = jnp.max(lg2, axis=1, keepdims=True)
    i2 = jnp.min(jnp.where(lg2 == t2, lane, LANES), axis=1, keepdims=True)
    e2 = jnp.exp(t2 - t1)
    den = 1.0 + e2
    we_ref[...] = jnp.where(lane == i1, 1.0 / den, 0.0) + jnp.where(lane == i2, e2 / den, 0.0)
    sel = (lane == i1) | (lane == i2)
    self32 = sel.astype(F32)
    before = (lax.broadcasted_iota(jnp.int32, (tt, tt), 1) < lax.broadcasted_iota(jnp.int32, (tt, tt), 0))
    rank = jnp.dot(before.astype(BF16), sel.astype(BF16), preferred_element_type=F32)
    rank = jnp.where(sel, rank, -1.0)
    rank_ref[...] = rank
    rankt_ref[0] = rank.T[0:SUBLANES, :]
    cnt_ref[0] = jnp.broadcast_to(jnp.sum(self32, axis=0, keepdims=True), (SUBLANES, LANES))


def _moe(x2, gain, mod, router_pad, wg, wu, wd, rows):
    m, d = x2.shape
    ne, _, dff = wg.shape
    assert ne <= SUBLANES
    tt = _row_tile(rows, MOE_TILE)
    nti = m // tt
    per = rows // tt
    many = mod.shape[0] > 1
    h, we, rank, rankt, cnt = pl.pallas_call(
        functools.partial(_route_kernel, ne=ne),
        grid=(nti,),
        in_specs=[pl.BlockSpec((tt, d), lambda i: (i, 0)),
                  pl.BlockSpec((1, d), lambda i: (0, 0)),
                  pl.BlockSpec((1, 6, d), lambda i: ((i // per) if many else 0, 0, 0)),
                  pl.BlockSpec((d, LANES), lambda i: (0, 0))],
        out_specs=[pl.BlockSpec((tt, d), lambda i: (i, 0)),
                   pl.BlockSpec((tt, LANES), lambda i: (i, 0)),
                   pl.BlockSpec((tt, LANES), lambda i: (i, 0)),
                   pl.BlockSpec((1, SUBLANES, tt), lambda i: (i, 0, 0)),
                   pl.BlockSpec((1, SUBLANES, LANES), lambda i: (i, 0, 0))],
        out_shape=[jax.ShapeDtypeStruct((m, d), BF16),
                   jax.ShapeDtypeStruct((m, LANES), F32),
                   jax.ShapeDtypeStruct((m, LANES), F32),
                   jax.ShapeDtypeStruct((nti, SUBLANES, tt), F32),
                   jax.ShapeDtypeStruct((nti, SUBLANES, LANES), F32)],
        compiler_params=_cparams("parallel"),
        name="moe_route",
    )(x2, gain.reshape(1, d), mod, router_pad)
    c16 = (cnt[:, 0, :ne].astype(jnp.int32) + (PACK - 1)) // PACK * PACK
    region = (jnp.sum(c16, axis=0) + (MOE_FT - 1)) // MOE_FT * MOE_FT
    e_off = jnp.cumsum(region) - region
    pos = (e_off[None, :] + jnp.cumsum(c16, axis=0) - c16).reshape(-1).astype(jnp.int32)
    c16 = c16.reshape(-1)
    rmax = -(-(2 * m + nti * ne * PACK + ne * MOE_FT) // MOE_FT) * MOE_FT
    nft = rmax // MOE_FT
    tiles_e = jnp.cumsum(region // MOE_FT)
    tid = jnp.arange(nft, dtype=jnp.int32)
    tile_valid = (tid < tiles_e[-1]).astype(jnp.int32)
    tile_map = jnp.minimum(tid, tiles_e[-1] - 1).astype(jnp.int32)
    tile_exp = jnp.minimum(jnp.sum(tile_map[:, None] >= tiles_e[None, :], axis=1), ne - 1).astype(jnp.int32)

    sizes = tuple(tt >> k for k in range(tt.bit_length()) if (tt >> k) >= PACK)
    nsteps = nti * ne
    hs = pl.pallas_call(
        functools.partial(_moe_gather_kernel, ne=ne, nsteps=nsteps, sizes=sizes),
        grid_spec=pltpu.PrefetchScalarGridSpec(
            num_scalar_prefetch=2,
            grid=(nti, ne),
            in_specs=[pl.BlockSpec((tt, d), lambda i, e, p_, c_: (i, 0)),
                      pl.BlockSpec((1, SUBLANES, tt), lambda i, e, p_, c_: (i, 0, 0)),
                      pl.BlockSpec(memory_space=pl.ANY)],
            out_specs=pl.BlockSpec(memory_space=pl.ANY),
            scratch_shapes=[pltpu.VMEM((2, tt, d), BF16), pltpu.SemaphoreType.DMA((2, len(sizes)))]),
        out_shape=jax.ShapeDtypeStruct((rmax, d), BF16),
        input_output_aliases={4: 0},
        compiler_params=_cparams("arbitrary", "arbitrary"),
        name="moe_gather",
    )(pos, c16, h, rankt, jnp.zeros((rmax, d), BF16))

    once = pl.Buffered(1)
    os_sorted = pl.pallas_call(
        _moe_ffn_kernel,
        grid_spec=pltpu.PrefetchScalarGridSpec(
            num_scalar_prefetch=3,
            grid=(nft,),
            in_specs=[pl.BlockSpec((MOE_FT, d), lambda t, te, tm, tv: (tm[t], 0)),
                      pl.BlockSpec((1, d, dff), lambda t, te, tm, tv: (te[t], 0, 0), pipeline_mode=once),
                      pl.BlockSpec((1, d, dff), lambda t, te, tm, tv: (te[t], 0, 0), pipeline_mode=once),
                      pl.BlockSpec((1, dff, d), lambda t, te, tm, tv: (te[t], 0, 0), pipeline_mode=once)],
            out_specs=pl.BlockSpec((MOE_FT, d), lambda t, te, tm, tv: (t, 0))),
        out_shape=jax.ShapeDtypeStruct((rmax, d), BF16),
        compiler_params=pltpu.CompilerParams(dimension_semantics=("parallel",),
                                             vmem_limit_bytes=58 * 1024 * 1024),
        name="moe_experts",
    )(tile_exp, tile_map, tile_valid, hs, wg, wu, wd)

    return pl.pallas_call(
        functools.partial(_moe_combine_kernel, ne=ne, nsteps=nsteps, sizes=sizes),
        grid_spec=pltpu.PrefetchScalarGridSpec(
            num_scalar_prefetch=2,
            grid=(nti, ne),
            in_specs=[pl.BlockSpec((tt, d), lambda i, e, p_, c_: (i, 0)),
                      pl.BlockSpec((tt, LANES), lambda i, e, p_, c_: (i, 0)),
                      pl.BlockSpec((tt, LANES), lambda i, e, p_, c_: (i, 0)),
                      pl.BlockSpec((1, 6, d), lambda i, e, p_, c_: ((i // per) if many else 0, 0, 0)),
                      pl.BlockSpec(memory_space=pl.ANY)],
            out_specs=pl.BlockSpec((tt, d), lambda i, e, p_, c_: (i, 0)),
            scratch_shapes=[pltpu.VMEM((tt, d), F32), pltpu.VMEM((2, tt, d), BF16),
                            pltpu.SemaphoreType.DMA((2, len(sizes)))]),
        out_shape=jax.ShapeDtypeStruct((m, d), F32),
        compiler_params=_cparams("arbitrary", "arbitrary"),
        name="moe_combine",
    )(pos, c16, x2, we, rank, mod, os_sorted)


def _rope_table(t):
    rows = t // GRID_W
    row = jnp.repeat(jnp.arange(rows, dtype=F32), GRID_W)
    col = jnp.tile(jnp.arange(GRID_W, dtype=F32), rows)
    half = HEAD_DIM // 2
    inv_freq = ROPE_THETA ** (-jnp.arange(0, half, 2, dtype=F32) / half)
    ar = row[:, None] * inv_freq
    ac = col[:, None] * inv_freq
    ang = jnp.concatenate([ar, ar, ac, ac] * 2, axis=-1)
    return jnp.stack([jnp.cos(ang), jnp.sin(ang)])


def _pad_rows(a, rows):
    return jnp.zeros((rows,) + a.shape[1:], a.dtype).at[:a.shape[0]].set(a)


def _hi_lo(w):
    hi = w.astype(BF16)
    return hi, (w - hi.astype(F32)).astype(BF16)


def kernel(x, c, ctx, c_ctx, norm1, norm2, w_mod, b_mod, w_in, tshift, w0, w_decay_up, a0, w_iclr_up, v0, w_vres_down, w_vres_up, w_gate_up, k_k, k_a, r_k, lnx_g, lnx_b, q_norm, k_norm, sink, w_br_rwkv, w_br_attn, w_out, ffn_gate, ffn_up, ffn_down, router, moe_gate, moe_up, moe_down):
    nb, t, d = x.shape
    cn = ctx.shape[1]
    depth = w_in.shape[0]
    mods = _modulation(c, c_ctx, w_mod, b_mod)
    cs = _rope_table(t)
    e128 = jnp.kron(jnp.eye(2, dtype=F32), jnp.ones((HEAD_DIM, HEAD_DIM), F32)).astype(BF16)
    xl = x.reshape(nb * t, d)
    xc = ctx.reshape(nb * cn, d)
    vf_l = vf_c = None
    for layer in range(depth):
        last = layer == depth - 1
        mod_l = mods[layer, :nb].reshape(nb, 6, d)
        mod_c = mods[layer, nb:nb + 1].reshape(1, 6, d)
        w_in_b = w_in[layer].astype(BF16)
        vd_b = None
        prm = {"tshift": tshift[layer], "e128": e128, "wg": w_gate_up[layer].astype(BF16)}
        prm["pd"] = jnp.stack(
            [_pad_rows(jnp.stack([w0[layer, dd], a0[layer, dd]]), SUBLANES) for dd in range(2)])
        shared = [k_k[layer], k_a[layer], r_k[layer].reshape(RW_DIM)]
        if layer > 0:
            shared.append(v0[layer - 1])
            vd_b = _pad_rows(w_vres_down[layer - 1].T, LANES).T.astype(BF16)
            prm["vu_hi"], prm["vu_lo"] = _hi_lo(_pad_rows(w_vres_up[layer - 1], LANES))
        prm["ps"] = _pad_rows(jnp.stack(shared), SUBLANES)
        wla = jnp.zeros((2, LANES, 2 * RW_DIM), F32)
        wla = wla.at[:, :HEAD_DIM, :RW_DIM].set(w_decay_up[layer]).at[:, HEAD_DIM:, RW_DIM:].set(w_iclr_up[layer])
        prm["wla"] = wla.astype(BF16)

        p_l, hv_l = _inproj(xl, norm1[layer], mod_l, w_in_b, vd_b, t)
        p_c, hv_c = _inproj(xc, norm1[layer], mod_c, w_in_b, vd_b, cn)

        s0 = jnp.zeros((nb, N_PAIRS, LANES, LANES), F32)
        y_c, aux_c, s_ctx = _rwkv_both(p_c, vf_c, hv_c, [s0, s0], prm, nb, cn)
        y_l, aux_l, _ = _rwkv_both(p_l, vf_l, hv_l, s_ctx, prm, nb, t)
        if layer == 0:
            vf_l, vf_c = aux_l[1], aux_c[1]

        qg = jnp.tile(q_norm[layer], 2).reshape(1, LANES)
        kg = jnp.tile(k_norm[layer], 2).reshape(1, LANES)
        qn_l, kn_l, vb_l = _qk_prep(p_l, cs, qg, kg, e128, t, True)
        qn_c, kn_c, vb_c = _qk_prep(p_c, None, qg, kg, e128, cn, False)
        sk = sink[layer].reshape(1, -1)
        ya_l = _attention(qn_l, kn_l, vb_l, kn_c, vb_c, sk, nb, t, cn, True)

        ln = jnp.stack([lnx_g[layer], lnx_b[layer]])
        wbr = w_br_rwkv[layer].astype(BF16)
        wba = w_br_attn[layer].astype(BF16)
        wo = w_out[layer].astype(BF16)
        xl = _merge(xl, y_l, aux_l, ya_l, p_l, ln, e128, wbr, wba, wo, mod_l, t)
        if not last:
            ya_c = _attention(qn_c, None, None, kn_c, vb_c, sk, nb, cn, cn, False)
            xc = _merge(xc, y_c, aux_c, ya_c, p_c, ln, e128, wbr, wba, wo, mod_c, cn)

        i = layer // 2
        if layer % 2 == 0:
            fw = (ffn_gate[i].astype(BF16), ffn_up[i].astype(BF16), ffn_down[i].astype(BF16))
            xl = _ffn(xl, norm2[layer], mod_l, *fw, t)
            if not last:
                xc = _ffn(xc, norm2[layer], mod_c, *fw, cn)
        else:
            rt = _pad_rows(router[i].T, LANES).T
            mw = (moe_gate[i].astype(BF16), moe_up[i].astype(BF16), moe_down[i].astype(BF16))
            xl = _moe(xl, norm2[layer], mod_l, rt, *mw, t)
            if not last:
                xc = _moe(xc, norm2[layer], mod_c, rt, *mw, cn)
    return xl.reshape(nb, t, d)
```

```python
import functools

import jax
import jax.numpy as jnp
from jax import lax
from jax.experimental import pallas as pl
from jax.experimental.pallas import tpu as pltpu

F32 = jnp.float32
BF16 = jnp.bfloat16
HIGHEST = lax.Precision.HIGHEST

LANES = 128
SUBLANES = 8
PACK = 16
ACT = jnp.bfloat16
HEAD_DIM = 64
RW_DIM = 512
ATT_DIM = 512
KV_DIM = 128
RW_IN = 1792
ATT_OFF = RW_IN
GATE_OFF = RW_IN + ATT_DIM + 2 * KV_DIM
N_EXPERTS = 8
CHUNK = 64
N_PAIRS = RW_DIM // LANES
MOE_TILE = 1024
MOE_BLK = 128
MOE_FT = 512
SCAN_GROUP = 4
ATT_SCALE = HEAD_DIM ** -0.5
ROPE_THETA = 10000.0
GRID_W = 64
RMS_EPS = 1e-6
GN_EPS = 64e-5
NEG_INF = -1e30
DECAY_SCALE = 0.6065306597126334
VMEM_LIMIT = 48 * 1024 * 1024


def _cparams(*sem):
    return pltpu.CompilerParams(dimension_semantics=sem, vmem_limit_bytes=VMEM_LIMIT)


def _row_tile(rows, cap):
    t = cap
    while rows % t:
        t //= 2
    return t


def _bdot(a, b):
    return jnp.dot(a.astype(BF16), b.astype(BF16), preferred_element_type=F32)


def _bdot_nt(a, b):
    return lax.dot_general(a.astype(BF16), b.astype(BF16), (((1,), (1,)), ((), ())),
                           preferred_element_type=F32)


def _bdot_tn(a, b):
    return lax.dot_general(a.astype(BF16), b.astype(BF16), (((0,), (0,)), ((), ())),
                           preferred_element_type=F32)


def _split2(x):
    hi = x.astype(BF16)
    lo = (x - hi.astype(F32)).astype(BF16)
    return hi, lo


def _dot3(a, w_hi, w_lo):
    a_hi, a_lo = _split2(a)
    return (jnp.dot(a_hi, w_hi, preferred_element_type=F32)
            + (jnp.dot(a_lo, w_hi, preferred_element_type=F32)
               + jnp.dot(a_hi, w_lo, preferred_element_type=F32)))


def _segsum(x, e, exact=True):
    outs = []
    for s in range(x.shape[1] // LANES):
        hi, lo = _split2(x[:, s * LANES:(s + 1) * LANES])
        acc = jnp.dot(hi, e, preferred_element_type=F32)
        outs.append(acc + jnp.dot(lo, e, preferred_element_type=F32) if exact else acc)
    return outs[0] if len(outs) == 1 else jnp.concatenate(outs, axis=1)


def _interleave(*stages):
    live = list(stages)
    while live:
        for g in list(live):
            if next(g, live) is live:
                live.remove(g)


def _norm_mod(x, gain, shift, scale):
    y = x * lax.rsqrt(jnp.mean(x * x, axis=-1, keepdims=True) + RMS_EPS) * gain
    return y * (1.0 + scale) + shift


def _mod_kernel(c_ref, w_ref, b_ref, o_ref):
    c = c_ref[...]
    s = c * jax.nn.sigmoid(c)
    o_ref[0] = jnp.dot(s, w_ref[0], precision=HIGHEST, preferred_element_type=F32) + b_ref[0]


def _modulation(c, c_ctx, w_mod, b_mod):
    depth, d, n6 = w_mod.shape
    nb = c.shape[0]
    rm = -(-(nb + 1) // SUBLANES) * SUBLANES
    cc = jnp.zeros((rm, d), F32).at[:nb].set(c).at[nb].set(c_ctx)
    tn = _row_tile(n6, 1536)
    return pl.pallas_call(
        _mod_kernel,
        grid=(depth, n6 // tn),
        in_specs=[pl.BlockSpec((rm, d), lambda l, n: (0, 0)),
                  pl.BlockSpec((1, d, tn), lambda l, n: (l, 0, n)),
                  pl.BlockSpec((1, 1, tn), lambda l, n: (l, 0, n))],
        out_specs=pl.BlockSpec((1, rm, tn), lambda l, n: (l, 0, n)),
        out_shape=jax.ShapeDtypeStruct((depth, rm, n6), F32),
        compiler_params=_cparams("parallel", "parallel"),
        name="adaln_mod",
    )(cc, w_mod, b_mod.reshape(depth, 1, n6))


def _inproj_kernel(*refs, vres, nsub):
    if vres:
        x_ref, g_ref, mod_ref, w_ref, vd_ref, o_ref, hv_ref = refs
    else:
        x_ref, g_ref, mod_ref, w_ref, o_ref = refs
    sub = x_ref.shape[0] // nsub
    for s in range(nsub):
        rs = slice(s * sub, (s + 1) * sub)
        h = _norm_mod(x_ref[rs, :], g_ref[...], mod_ref[0, 0:1, :], mod_ref[0, 1:2, :]).astype(BF16)
        if vres:
            hv_ref[rs, :] = jnp.dot(h, vd_ref[...], preferred_element_type=F32)
        o_ref[rs, :] = jnp.dot(h, w_ref[...], preferred_element_type=F32).astype(o_ref.dtype)


def _inproj(x2, gain, mod, w_b, vd_b, rows):
    m, d = x2.shape
    n = w_b.shape[1]
    tm = _row_tile(rows, 1024)
    per = rows // tm
    many = mod.shape[0] > 1
    vres = vd_b is not None
    once = pl.Buffered(1)
    in_specs = [pl.BlockSpec((tm, d), lambda i: (i, 0)),
                pl.BlockSpec((1, d), lambda i: (0, 0)),
                pl.BlockSpec((1, 6, d), lambda i: ((i // per) if many else 0, 0, 0)),
                pl.BlockSpec((d, n), lambda i: (0, 0), pipeline_mode=once)]
    args = [x2, gain.reshape(1, d), mod, w_b]
    out_specs = [pl.BlockSpec((tm, n), lambda i: (i, 0))]
    out_shape = [jax.ShapeDtypeStruct((m, n), ACT)]
    if vres:
        in_specs.append(pl.BlockSpec((d, LANES), lambda i: (0, 0)))
        args.append(vd_b)
        out_specs.append(pl.BlockSpec((tm, LANES), lambda i: (i, 0)))
        out_shape.append(jax.ShapeDtypeStruct((m, LANES), F32))
    outs = pl.pallas_call(
        functools.partial(_inproj_kernel, vres=vres, nsub=max(1, tm // 256)),
        grid=(m // tm,),
        in_specs=in_specs, out_specs=out_specs, out_shape=out_shape,
        compiler_params=_cparams("parallel"),
        name="in_proj",
    )(*args)
    return (outs[0], outs[1]) if vres else (outs[0], None)


def _qkprep_kernel(*refs, rope):
    if rope:
        q0_ref, q1_ref, k_ref, v_ref, cs_ref, qg_ref, kg_ref, e_ref, qn_ref, kn_ref, vb_ref = refs
    else:
        q0_ref, q1_ref, k_ref, v_ref, qg_ref, kg_ref, e_ref, qn_ref, kn_ref, vb_ref = refs
    e = e_ref[...]
    lane = lax.broadcasted_iota(jnp.int32, (1, LANES), 1)
    first = (lane & 31) < 16

    def norm_rope(u, gain):
        y = u * lax.rsqrt(_segsum(u * u, e) * (1.0 / HEAD_DIM) + RMS_EPS) * gain
        if rope:
            rot = jnp.where(first, -pltpu.roll(y, LANES - 16, 1), pltpu.roll(y, 16, 1))
            y = y * cs_ref[0] + rot * cs_ref[1]
        return y

    qg = qg_ref[...]
    for s, ref in enumerate((q0_ref, q1_ref)):
        for t in range(2):
            u = ref[:, t * LANES:(t + 1) * LANES].astype(F32)
            c0 = (2 * s + t) * LANES
            qn_ref[:, c0:c0 + LANES] = (norm_rope(u, qg) * ATT_SCALE).astype(BF16)
    low = lane < HEAD_DIM
    for src_val, dst in ((norm_rope(k_ref[...].astype(F32), kg_ref[...]), kn_ref), (v_ref[...].astype(F32), vb_ref)):
        swapped = pltpu.roll(src_val, HEAD_DIM, 1)
        dst[:, 0:LANES] = jnp.where(low, src_val, swapped).astype(BF16)
        dst[:, LANES:2 * LANES] = jnp.where(low, swapped, src_val).astype(BF16)


def _qk_prep(p, cs, qg, kg, e128, rows, rope):
    m = p.shape[0]
    tm = _row_tile(rows, 512)
    per = rows // tm
    qb = ATT_OFF // 256
    kb = (ATT_OFF + ATT_DIM) // LANES
    in_specs = [pl.BlockSpec((tm, 256), lambda i: (i, qb)),
                pl.BlockSpec((tm, 256), lambda i: (i, qb + 1)),
                pl.BlockSpec((tm, LANES), lambda i: (i, kb)),
                pl.BlockSpec((tm, LANES), lambda i: (i, kb + 1))]
    args = [p, p, p, p]
    if rope:
        in_specs.append(pl.BlockSpec((2, tm, LANES), lambda i: (0, i % per, 0)))
        args.append(cs)
    in_specs += [pl.BlockSpec((1, LANES), lambda i: (0, 0)),
                 pl.BlockSpec((1, LANES), lambda i: (0, 0)),
                 pl.BlockSpec((LANES, LANES), lambda i: (0, 0))]
    args += [qg, kg, e128]
    return pl.pallas_call(
        functools.partial(_qkprep_kernel, rope=rope),
        grid=(m // tm,),
        in_specs=in_specs,
        out_specs=[pl.BlockSpec((tm, ATT_DIM), lambda i: (i, 0)),
                   pl.BlockSpec((tm, 2 * KV_DIM), lambda i: (i, 0)),
                   pl.BlockSpec((tm, 2 * KV_DIM), lambda i: (i, 0))],
        out_shape=[jax.ShapeDtypeStruct((m, ATT_DIM), BF16),
                   jax.ShapeDtypeStruct((m, 2 * KV_DIM), BF16),
                   jax.ShapeDtypeStruct((m, 2 * KV_DIM), BF16)],
        compiler_params=_cparams("parallel"),
        name="qk_prep",
    )(*args)


def _attn_kernel(*refs, nq, local):
    if local:
        sink_ref, q_ref, kp_ref, kc_ref, kn_ref, vp_ref, vc_ref, vn_ref, kx_ref, vx_ref, wm_ref, o_ref = refs
        k = jnp.concatenate([kp_ref[...], kc_ref[...], kn_ref[...], kx_ref[...]], axis=0)
        v = jnp.concatenate([vp_ref[...], vc_ref[...], vn_ref[...], vx_ref[...]], axis=0)
    else:
        sink_ref, q_ref, kx_ref, vx_ref, o_ref = refs
        k = kx_ref[...]
        v = vx_ref[...]
    i = pl.program_id(1)
    bq = q_ref.shape[0]
    low = lax.broadcasted_iota(jnp.int32, (1, LANES), 1) < HEAD_DIM
    row = lax.broadcasted_iota(jnp.int32, (2 * bq, 1), 0)
    if local:
        kblk = lax.broadcasted_iota(jnp.int32, (1, 3 * bq), 1) // bq
        off_end = ((kblk == 0) & (i == 0)) | ((kblk == 2) & (i == nq - 1))
        bias = wm_ref[...] + jnp.where(off_end, NEG_INF, 0.0)
    def head_pair(a):
        g = a // 2
        kb = k[:, g * LANES:(g + 1) * LANES]
        vb = v[:, g * LANES:(g + 1) * LANES]
        qa = q_ref[:, a * LANES:(a + 1) * LANES].astype(F32)
        qs = jnp.concatenate([jnp.where(low, qa, 0.0), jnp.where(low, 0.0, qa)], axis=0).astype(BF16)
        s = lax.dot_general(qs, kb, (((1,), (1,)), ((), ())), preferred_element_type=F32)
        yield
        if local:
            s = jnp.concatenate([s[:, :3 * bq] + bias, s[:, 3 * bq:]], axis=1)
        sk = jnp.where(row < bq, sink_ref[0, 2 * a], sink_ref[0, 2 * a + 1])
        mx = jnp.maximum(jnp.max(s, axis=1, keepdims=True), sk)
        yield
        ex = jnp.exp(s - mx)
        den = jnp.sum(ex, axis=1, keepdims=True) + jnp.exp(sk - mx)
        yield
        o = jnp.dot(ex.astype(BF16), vb, preferred_element_type=F32) / den
        o_ref[:, a * LANES:(a + 1) * LANES] = jnp.where(low, o[0:bq], o[bq:2 * bq]).astype(o_ref.dtype)

    _interleave(*[head_pair(a) for a in range(4)])


def _attention(qn, kn, vb, kctx, vctx, sink, nb, rows, cn, local):
    m = qn.shape[0]
    bq = 128
    nq = rows // bq
    smem = pl.BlockSpec(memory_space=pltpu.SMEM)
    qspec = pl.BlockSpec((bq, ATT_DIM), lambda b, i: (b * nq + i, 0))
    cspec = pl.BlockSpec((cn, 2 * KV_DIM), lambda b, i: (b, 0))
    if local:
        prv = pl.BlockSpec((bq, 2 * KV_DIM), lambda b, i: (b * nq + jnp.maximum(i - 1, 0), 0))
        cur = pl.BlockSpec((bq, 2 * KV_DIM), lambda b, i: (b * nq + i, 0))
        nxt = pl.BlockSpec((bq, 2 * KV_DIM), lambda b, i: (b * nq + jnp.minimum(i + 1, nq - 1), 0))
        qo = jnp.arange(2 * bq)[:, None] % bq
        kcol = jnp.arange(3 * bq)[None, :]
        inwin = ((kcol // bq == 1) | ((kcol // bq == 0) & (kcol % bq >= qo)) | ((kcol // bq == 2) & (kcol % bq <= qo)))
        wmask = jnp.where(inwin, 0.0, NEG_INF).astype(F32)
        in_specs = [smem, qspec, prv, cur, nxt, prv, cur, nxt, cspec, cspec,
                    pl.BlockSpec((2 * bq, 3 * bq), lambda b, i: (0, 0))]
        args = [sink, qn, kn, kn, kn, vb, vb, vb, kctx, vctx, wmask]
    else:
        in_specs = [smem, qspec, cspec, cspec]
        args = [sink, qn, kctx, vctx]
    return pl.pallas_call(
        functools.partial(_attn_kernel, nq=nq, local=local),
        grid=(nb, nq),
        in_specs=in_specs,
        out_specs=pl.BlockSpec((bq, ATT_DIM), lambda b, i: (b * nq + i, 0)),
        out_shape=jax.ShapeDtypeStruct((m, ATT_DIM), BF16),
        compiler_params=_cparams("parallel", "parallel"),
        name="window_attn" if local else "ctx_attn",
    )(*args)


def _rwkv_kernel(*refs, nt, tq, vres, drn, tps):
    it = iter(refs)
    n_in = 5 if vres else 3
    first = [next(it) for _ in range(n_in)] if tps == 2 else None
    tiles = [[next(it) for _ in range(n_in)] for _ in range(tps)]
    ts_ref, pd_ref, ps_ref, wla_ref, wg_ref = (next(it) for _ in range(5))
    if vres:
        vuh_ref, vul_ref = next(it), next(it)
    e_ref, sin_ref = next(it), next(it)
    y_ref, aux_ref, sout_ref = next(it), next(it), next(it)
    ops = [next(it) for _ in range(tps)]
    auxs = [next(it) for _ in range(tps)]
    s_sc = next(it)

    j = pl.program_id(1)
    sgn = 1 - 2 * drn
    nch = tq // CHUNK
    nsteps = nt // tps
    e = e_ref[...]
    lane = lax.broadcasted_iota(jnp.int32, (1, LANES), 1)
    low = lane < HEAD_DIM

    def natural(k):
        return k if drn == 0 else nt - 1 - k

    def prepare(tile, in_refs, ops_sc, aux_sc):
        p_ref, pv_ref, nx_ref = in_refs[:3]
        rw = p_ref[...].astype(F32)
        row = lax.broadcasted_iota(jnp.int32, (tq, 1), 0)
        prev_row = jnp.where(tile > 0, pv_ref[PACK - 1:PACK, :].astype(F32), 0.0)
        next_row = jnp.where(tile < nt - 1, nx_ref[0:1, :].astype(F32), 0.0)
        prv = jnp.where(row == 0, prev_row, pltpu.roll(rw, 1, 0))
        yield
        nxt = jnp.where(row == tq - 1, next_row, pltpu.roll(rw, tq - 1, 0))
        xs = rw + ts_ref[0:1, :] * (prv - rw) + ts_ref[1:2, :] * (nxt - rw)
        yield

        r = xs[:, 0:RW_DIM]
        k = xs[:, RW_DIM:2 * RW_DIM]
        v = xs[:, 2 * RW_DIM:3 * RW_DIM]
        la = xs[:, 3 * RW_DIM:3 * RW_DIM + LANES]
        la = jnp.where(low, jnp.tanh(la), la)
        la_hi, la_lo = _split2(la)
        lora = (jnp.dot(la_hi, wla_ref[0], preferred_element_type=F32)
                + jnp.dot(la_lo, wla_ref[0], preferred_element_type=F32))
        yield
        logw = -DECAY_SCALE * jax.nn.sigmoid(pd_ref[0, 0:1, :] + lora[:, 0:RW_DIM])
        iclr = jax.nn.sigmoid(pd_ref[0, 1:2, :] + lora[:, RW_DIM:2 * RW_DIM])
        if vres:
            vf_ref, hv_ref = in_refs[3:]
            mix = jax.nn.sigmoid(ps_ref[3:4, :] + _dot3(hv_ref[...], vuh_ref[...], vul_ref[...]))
            v = v + (vf_ref[...].astype(F32) - v) * mix
        yield
        kk = k * ps_ref[0:1, :]
        kk = kk * lax.rsqrt(_segsum(kk * kk, e, exact=False) + 1e-12)
        yield
        kh = k * (1.0 + (iclr - 1.0) * ps_ref[1:2, :])
        bb = kk * iclr
        aux_sc[:, 0:RW_DIM] = (_segsum(r * kh * ps_ref[2:3, :], e, exact=False) * v).astype(aux_sc.dtype)
        if drn == 0:
            glo = xs[:, 3 * RW_DIM + LANES:RW_IN]
            aux_sc[:, RW_DIM:2 * RW_DIM] = _bdot(jax.nn.sigmoid(glo), wg_ref[...]).astype(aux_sc.dtype)
        else:
            aux_sc[:, RW_DIM:2 * RW_DIM] = v.astype(aux_sc.dtype)

        yield
        ti = lax.broadcasted_iota(jnp.int32, (tq, tq), 0)
        si = lax.broadcasted_iota(jnp.int32, (tq, tq), 1)
        tri = (((ti // CHUNK) == (si // CHUNK)) & (((si - ti) * sgn) <= 0)).astype(BF16)
        lw_hi, lw_lo = _split2(logw)
        cin = jnp.dot(tri, lw_hi, preferred_element_type=F32) + jnp.dot(tri, lw_lo, preferred_element_type=F32)
        cin3 = cin.reshape(nch, CHUNK, RW_DIM)
        last = CHUNK - 1 if drn == 0 else 0
        ctot = jnp.broadcast_to(cin3[:, last:last + 1, :], cin3.shape).reshape(tq, RW_DIM)
        yield
        e_neg = jnp.exp(-cin)
        e_rem = jnp.exp(ctot - cin)
        ops_sc[0] = kk * jnp.exp(cin - logw)
        ops_sc[1] = r * jnp.exp(cin)
        ops_sc[2] = bb * e_neg
        ops_sc[3] = kh * e_neg
        yield
        ops_sc[4] = kh * e_rem
        ops_sc[5] = bb * e_rem
        ops_sc[6] = v
        ops_sc[7] = jnp.exp(ctot)

    tr = lax.broadcasted_iota(jnp.int32, (CHUNK, LANES), 0)
    sc = lax.broadcasted_iota(jnp.int32, (CHUNK, LANES), 1) & (CHUNK - 1)
    rel = (sc - tr) * sgn
    strict = rel < 0
    incl = rel <= 0
    eye = (sc == tr).astype(F32)
    same_head = ((lax.broadcasted_iota(jnp.int32, (LANES, LANES), 0) // HEAD_DIM)
                 == (lax.broadcasted_iota(jnp.int32, (LANES, LANES), 1) // HEAD_DIM))

    def stack(x):
        return jnp.concatenate([jnp.where(low, x, 0.0), jnp.where(low, 0.0, x)], axis=0)

    def hdot(x, y):
        return _bdot(x, stack(y))

    def scan_tile(ops_sc, row0):
        st = [s_sc[pr] for pr in range(N_PAIRS)]
        for g0 in range(0, nch, SCAN_GROUP):
            yield from scan_chunks(ops_sc, row0, st, range(g0, g0 + SCAN_GROUP))
        for pr in range(N_PAIRS):
            s_sc[pr] = st[pr]

    def scan_chunks(ops_sc, row0, st, chunk_ids):
        probs = []
        for ii in chunk_ids:
            off = (ii if drn == 0 else nch - 1 - ii) * CHUNK
            for pr in range(N_PAIRS):
                ls = slice(pr * LANES, (pr + 1) * LANES)
                q = {"off": off, "ls": ls, "pr": pr}
                for n, name in enumerate(("kkt", "rt", "bh", "kh", "kp", "bp", "v")):
                    q[name] = ops_sc[n, off:off + CHUNK, ls]
                q["wl"] = ops_sc[7, off:off + 1, ls]
                probs.append(q)
        for q in probs:
            g = _bdot_nt(jnp.concatenate([q["kkt"], q["rt"]], axis=0),
                         jnp.concatenate([stack(q["bh"]), stack(q["kh"])], axis=0))
            q["nn"] = jnp.where(strict, -g[0:CHUNK, 0:LANES], 0.0)
            q["avk"] = jnp.where(strict, g[0:CHUNK, LANES:], 0.0)
            q["arb"] = jnp.where(incl, g[CHUNK:, 0:LANES], 0.0)
            q["ark"] = jnp.where(incl, g[CHUNK:, LANES:], 0.0)
            q["t"] = eye + q["nn"]
        yield
        for q in probs:
            q["q"] = hdot(q["nn"], q["nn"])
        for _ in range(4):
            for q in probs:
                tq2 = _bdot(jnp.concatenate([q["t"], q["q"]], axis=0), stack(q["q"]))
                q["t"] = q["t"] + tq2[:CHUNK]
                q["q"] = tq2[CHUNK:]
            yield
        for q in probs:
            q["t"] = q["t"] + hdot(q["t"], q["q"])
        for q in probs:
            av = _bdot(jnp.concatenate([q["avk"], q["ark"]], axis=0), stack(q["v"]))
            q["av"], q["arkv"] = av[:CHUNK], av[CHUNK:]
        yield
        for q in probs:
            tt2 = _bdot(q["t"], jnp.concatenate([stack(q["kkt"]), stack(q["av"])], axis=1))
            q["tk"], q["tav"] = tt2[:, :LANES], tt2[:, LANES:]
        yield
        for q in probs:
            tn = _bdot_tn(jnp.concatenate([q["tk"], q["tav"]], axis=1), q["bp"])
            q["mb"] = jnp.where(same_head, tn[:LANES], 0.0)
            q["nm"] = jnp.where(same_head, _bdot_tn(q["v"], q["kp"]) - tn[LANES:], 0.0)
            z = _bdot(q["arb"], jnp.concatenate([stack(q["tk"]), stack(q["tav"])], axis=1))
            q["r2"] = q["rt"] - z[:, :LANES]
            q["y0"] = q["arkv"] - z[:, LANES:]
        yield

        outs = []
        for q in probs:
            sm = st[q["pr"]]
            outs.append(_bdot_nt(q["r2"], sm) + q["y0"])
            st[q["pr"]] = sm * q["wl"] - _bdot(sm, q["mb"]) + q["nm"]
        for q, yo in zip(probs, outs):
            y_ref[row0 + q["off"]:row0 + q["off"] + CHUNK, q["ls"]] = yo.astype(y_ref.dtype)

    run = _interleave

    @pl.when(j == 0)
    def _():
        s_sc[...] = sin_ref[0]

    if tps == 1:
        run(prepare(natural(j), tiles[0], ops[0], auxs[0]))
        aux_ref[...] = auxs[0][...]
        run(scan_tile(ops[0], 0))
    else:
        @pl.when(j == 0)
        def _():
            run(prepare(natural(0), first, ops[0], auxs[0]))

        rows = [0, tq] if drn == 0 else [tq, 0]
        aux_ref[rows[0]:rows[0] + tq, :] = auxs[0][...]
        run(scan_tile(ops[0], rows[0]), prepare(natural(2 * j + 1), tiles[0], ops[1], auxs[1]))
        aux_ref[rows[1]:rows[1] + tq, :] = auxs[1][...]
        run(scan_tile(ops[1], rows[1]),
            prepare(natural(jnp.minimum(2 * j + 2, nt - 1)), tiles[1], ops[0], auxs[0]))

    @pl.when(j == nsteps - 1)
    def _():
        sout_ref[0] = s_sc[...]


def _rwkv(p, vf_src, hv, state_in, prm, nb, rows, drn):
    m = p.shape[0]
    tq = _row_tile(rows, 256)
    nt = rows // tq
    tps = 2 if nt % 2 == 0 else 1
    nsteps = nt // tps
    rb = tq // PACK
    vres = vf_src is not None

    def tile_specs(scan_pos):
        def tile(b, j):
            k = jnp.minimum(scan_pos(j), nt - 1)
            return b * nt + (k if drn == 0 else nt - 1 - k)

        specs = [pl.BlockSpec((tq, RW_IN), lambda b, j: (tile(b, j), 0)),
                 pl.BlockSpec((PACK, RW_IN), lambda b, j: (jnp.maximum(tile(b, j) * rb - 1, 0), 0)),
                 pl.BlockSpec((PACK, RW_IN), lambda b, j: (jnp.minimum((tile(b, j) + 1) * rb, m // PACK - 1), 0))]
        args = [p, p, p]
        if vres:
            specs += [pl.BlockSpec((tq, RW_DIM), lambda b, j: (tile(b, j), 1)),
                      pl.BlockSpec((tq, LANES), lambda b, j: (tile(b, j), 0))]
            args += [vf_src, hv]
        return specs, args

    if tps == 2:
        positions = [lambda j: 0 * j, lambda j: 2 * j + 1, lambda j: 2 * j + 2]
    else:
        positions = [lambda j: j]
    in_specs, args = [], []
    for pos in positions:
        s_, a_ = tile_specs(pos)
        in_specs += s_
        args += a_

    const2 = lambda b, j: (0, 0)
    dir3 = lambda b, j: (drn, 0, 0)
    in_specs += [pl.BlockSpec((2, RW_IN), const2),
                 pl.BlockSpec((1, SUBLANES, RW_DIM), dir3),
                 pl.BlockSpec((SUBLANES, RW_DIM), const2),
                 pl.BlockSpec((1, LANES, 2 * RW_DIM), dir3),
                 pl.BlockSpec((LANES, RW_DIM), const2)]
    args += [prm["tshift"], prm["pd"], prm["ps"], prm["wla"], prm["wg"]]
    if vres:
        in_specs += [pl.BlockSpec((LANES, RW_DIM), const2), pl.BlockSpec((LANES, RW_DIM), const2)]
        args += [prm["vu_hi"], prm["vu_lo"]]
    sspec = pl.BlockSpec((1, N_PAIRS, LANES, LANES), lambda b, j: (b, 0, 0, 0))
    in_specs += [pl.BlockSpec((LANES, LANES), const2), sspec]
    args += [prm["e128"], state_in]

    def out_block(b, j):
        return (b * nsteps + (j if drn == 0 else nsteps - 1 - j), 0)

    return pl.pallas_call(
        functools.partial(_rwkv_kernel, nt=nt, tq=tq, vres=vres, drn=drn, tps=tps),
        grid=(nb, nsteps),
        in_specs=in_specs,
        out_specs=[pl.BlockSpec((tps * tq, RW_DIM), out_block),
                   pl.BlockSpec((tps * tq, 2 * RW_DIM), out_block),
                   sspec],
        out_shape=[jax.ShapeDtypeStruct((m, RW_DIM), ACT),
                   jax.ShapeDtypeStruct((m, 2 * RW_DIM), ACT),
                   jax.ShapeDtypeStruct((nb, N_PAIRS, LANES, LANES), F32)],
        scratch_shapes=([pltpu.VMEM((8, tq, RW_DIM), F32)] * tps + [pltpu.VMEM((tq, 2 * RW_DIM), ACT)] * tps
                        + [pltpu.VMEM((N_PAIRS, LANES, LANES), F32)]),
        compiler_params=_cparams("parallel", "arbitrary"),
        name="rwkv7_fwd" if drn == 0 else "rwkv7_bwd",
    )(*args)


def _rwkv_both(p, vf_src, hv, states, prm, nb, rows):
    outs = [_rwkv(p, vf_src, hv, states[d], prm, nb, rows, d) for d in range(2)]
    return [o[0] for o in outs], [o[1] for o in outs], [o[2] for o in outs]


def _merge_kernel(x_ref, y0_ref, y1_ref, a0_ref, a1_ref, ya_ref, g0_ref, g1_ref, g2_ref, g3_ref, ln_ref, e_ref,
                  wbr_ref, wba_ref, wo_ref, mod_ref, o_ref):
    e = e_ref[...]
    y = y0_ref[...].astype(F32) + y1_ref[...].astype(F32)
    mu = _segsum(y, e) * (1.0 / HEAD_DIM)
    yc = y - mu
    var = _segsum(yc * yc, e) * (1.0 / HEAD_DIM)
    yn = yc * lax.rsqrt(var + GN_EPS) * ln_ref[0:1, :] + ln_ref[1:2, :]
    cv = a0_ref[:, 0:RW_DIM].astype(F32) + a1_ref[:, 0:RW_DIM].astype(F32)
    yrw = (yn + cv) * a0_ref[:, RW_DIM:2 * RW_DIM].astype(F32)
    a = _bdot(yrw, wbr_ref[...])
    b = jnp.dot(ya_ref[...], wba_ref[...], preferred_element_type=F32)
    h = a.shape[1] // 2
    mrg = jnp.concatenate(
        [jax.nn.sigmoid(g0_ref[...].astype(F32)) * a[:, :h] + jax.nn.sigmoid(g2_ref[...].astype(F32)) * b[:, :h],
         jax.nn.sigmoid(g1_ref[...].astype(F32)) * a[:, h:] + jax.nn.sigmoid(g3_ref[...].astype(F32)) * b[:, h:]],
        axis=1)
    o_ref[...] = x_ref[...] + mod_ref[0, 2:3, :] * _bdot(mrg, wo_ref[...])


def _merge(x2, ys, auxs, ya, p, ln, e128, wbr, wba, wo, mod, rows):
    m, d = x2.shape
    tm = _row_tile(rows, 512)
    per = rows // tm
    many = mod.shape[0] > 1
    gb = GATE_OFF // 512
    const2 = lambda i: (0, 0)
    in_specs = [pl.BlockSpec((tm, d), lambda i: (i, 0)),
                pl.BlockSpec((tm, RW_DIM), lambda i: (i, 0)),
                pl.BlockSpec((tm, RW_DIM), lambda i: (i, 0)),
                pl.BlockSpec((tm, 2 * RW_DIM), lambda i: (i, 0)),
                pl.BlockSpec((tm, 2 * RW_DIM), lambda i: (i, 0)),
                pl.BlockSpec((tm, ATT_DIM), lambda i: (i, 0))]
    in_specs += [pl.BlockSpec((tm, 512), (lambda i, c=c: (i, gb + c))) for c in range(4)]
    in_specs += [pl.BlockSpec((2, RW_DIM), const2),
                 pl.BlockSpec((LANES, LANES), const2),
                 pl.BlockSpec(wbr.shape, const2),
                 pl.BlockSpec(wba.shape, const2),
                 pl.BlockSpec(wo.shape, const2),
                 pl.BlockSpec((1, 6, d), lambda i: ((i // per) if many else 0, 0, 0))]
    return pl.pallas_call(
        _merge_kernel,
        grid=(m // tm,),
        in_specs=in_specs,
        out_specs=pl.BlockSpec((tm, d), lambda i: (i, 0)),
        out_shape=jax.ShapeDtypeStruct((m, d), F32),
        compiler_params=_cparams("parallel"),
        name="branch_merge",
    )(x2, ys[0], ys[1], auxs[0], auxs[1], ya, p, p, p, p, ln, e128, wbr, wba, wo, mod)


def _ffn_kernel(x_ref, g_ref, mod_ref, wg_ref, wu_ref, wd_ref, o_ref):
    def rows(rs):
        x = x_ref[rs, :]
        h = _norm_mod(x, g_ref[...], mod_ref[0, 3:4, :], mod_ref[0, 4:5, :]).astype(BF16)
        yield
        a = jnp.dot(h, wg_ref[...], preferred_element_type=F32)
        u = jnp.dot(h, wu_ref[...], preferred_element_type=F32)
        yield
        act = ((a * jax.nn.sigmoid(a)) * u).astype(BF16)
        yield
        o_ref[rs, :] = x + mod_ref[0, 5:6, :] * jnp.dot(act, wd_ref[...], preferred_element_type=F32)

    half = x_ref.shape[0] // 2
    _interleave(rows(slice(0, half)), rows(slice(half, 2 * half)))


def _ffn(x2, gain, mod, wg, wu, wd, rows):
    m, d = x2.shape
    dff = wg.shape[1]
    tm = _row_tile(rows, 512)
    per = rows // tm
    many = mod.shape[0] > 1
    once = pl.Buffered(1)
    return pl.pallas_call(
        _ffn_kernel,
        grid=(m // tm,),
        in_specs=[pl.BlockSpec((tm, d), lambda i: (i, 0)),
                  pl.BlockSpec((1, d), lambda i: (0, 0)),
                  pl.BlockSpec((1, 6, d), lambda i: ((i // per) if many else 0, 0, 0)),
                  pl.BlockSpec((d, dff), lambda i: (0, 0), pipeline_mode=once),
                  pl.BlockSpec((d, dff), lambda i: (0, 0), pipeline_mode=once),
                  pl.BlockSpec((dff, d), lambda i: (0, 0), pipeline_mode=once)],
        out_specs=pl.BlockSpec((tm, d), lambda i: (i, 0)),
        out_shape=jax.ShapeDtypeStruct((m, d), F32),
        compiler_params=_cparams("parallel"),
        name="ffn_swiglu",
    )(x2, gain.reshape(1, d), mod, wg, wu, wd)


def _dma_pieces(count, sizes, make):
    for k, size in enumerate(sizes):
        @pl.when((count & size) != 0)
        def _(k=k, size=size):
            make(k, pl.multiple_of(count & ~(2 * size - 1), PACK), size)


def _moe_gather_kernel(pos_ref, cnt_ref, h_ref, rankt_ref, zero_ref, out_ref, hs_sc, sem, *, ne, nsteps, sizes):
    del zero_ref
    i = pl.program_id(0)
    ex = pl.program_id(1)
    step = i * ne + ex
    slot = step % 2
    tt = h_ref.shape[0]

    def copies(st, sl, act):
        base = pos_ref[st]

        def make(k, off, size):
            act(pltpu.make_async_copy(hs_sc.at[sl, pl.ds(off, size)],
                                      out_ref.at[pl.ds(pl.multiple_of(base + off, PACK), size)], sem.at[sl, k]))

        _dma_pieces(cnt_ref[st], sizes, make)

    @pl.when(step >= 2)
    def _():
        copies(step - 2, slot, lambda cp: cp.wait())

    rrow = rankt_ref[0, pl.ds(ex, 1), :]
    rowid = lax.broadcasted_iota(jnp.int32, (MOE_BLK, tt), 0).astype(F32)

    def gather(b, carry):
        sel = ((rrow - (b * MOE_BLK).astype(F32)) == rowid).astype(BF16)
        r0 = pl.multiple_of(b * MOE_BLK, MOE_BLK)
        hs_sc[slot, pl.ds(r0, MOE_BLK), :] = jnp.dot(sel, h_ref[...], preferred_element_type=F32).astype(BF16)
        return carry

    lax.fori_loop(0, (cnt_ref[step] + MOE_BLK - 1) // MOE_BLK, gather, 0)
    copies(step, slot, lambda cp: cp.start())

    @pl.when(step == nsteps - 1)
    def _():
        if nsteps >= 2:
            copies(step - 1, 1 - slot, lambda cp: cp.wait())
        copies(step, slot, lambda cp: cp.wait())


def _moe_ffn_kernel(te_ref, tm_ref, tv_ref, hs_ref, wg_ref, wu_ref, wd_ref, os_ref):
    del te_ref, tm_ref
    valid = tv_ref[pl.program_id(0)] != 0

    @pl.when(valid)
    def _():
        def rows(rs):
            hb = hs_ref[rs, :]
            a = jnp.dot(hb, wg_ref[0], preferred_element_type=F32)
            u = jnp.dot(hb, wu_ref[0], preferred_element_type=F32)
            yield
            act = ((a * jax.nn.sigmoid(a)) * u).astype(BF16)
            yield
            os_ref[rs, :] = jnp.dot(act, wd_ref[0], preferred_element_type=F32).astype(os_ref.dtype)

        half = hs_ref.shape[0] // 2
        _interleave(rows(slice(0, half)), rows(slice(half, 2 * half)))

    @pl.when(jnp.logical_not(valid))
    def _():
        os_ref[...] = jnp.zeros_like(os_ref)


def _moe_combine_kernel(pos_ref, cnt_ref, x_ref, we_ref, rank_ref, mod_ref, os_ref, o_ref, acc_sc, ob_sc, sem,
                        *, ne, nsteps, sizes):
    i = pl.program_id(0)
    ex = pl.program_id(1)
    step = i * ne + ex
    slot = step % 2
    tt = x_ref.shape[0]
    sblk = min(2 * MOE_BLK, tt)
    lane = lax.broadcasted_iota(jnp.int32, (tt, LANES), 1)

    def copies(st, sl, act):
        base = pos_ref[st]

        def make(k, off, size):
            act(pltpu.make_async_copy(os_ref.at[pl.ds(pl.multiple_of(base + off, PACK), size)],
                                      ob_sc.at[sl, pl.ds(off, size)], sem.at[sl, k]))

        _dma_pieces(cnt_ref[st], sizes, make)

    @pl.when(step == 0)
    def _():
        ob_sc[...] = jnp.zeros_like(ob_sc)
        copies(0, 0, lambda cp: cp.start())

    @pl.when(step + 1 < nsteps)
    def _():
        copies(step + 1, 1 - slot, lambda cp: cp.start())

    copies(step, slot, lambda cp: cp.wait())

    @pl.when(ex == 0)
    def _():
        acc_sc[...] = jnp.zeros_like(acc_sc)

    rcol = jnp.sum(jnp.where(lane == ex, rank_ref[...], 0.0), axis=1, keepdims=True)
    wcol = jnp.sum(jnp.where(lane == ex, we_ref[...], 0.0), axis=1, keepdims=True)
    colid = lax.broadcasted_iota(jnp.int32, (tt, sblk), 1).astype(F32)

    def scatter(b, carry):
        sel = ((rcol - (b * sblk).astype(F32)) == colid).astype(BF16)
        r0 = pl.multiple_of(b * sblk, sblk)
        acc_sc[...] += wcol * jnp.dot(sel, ob_sc[slot, pl.ds(r0, sblk), :], preferred_element_type=F32)
        return carry

    lax.fori_loop(0, (cnt_ref[step] + sblk - 1) // sblk, scatter, 0)

    @pl.when(ex == ne - 1)
    def _():
        o_ref[...] = x_ref[...] + mod_ref[0, 5:6, :] * acc_sc[...]


def _route_kernel(x_ref, g_ref, mod_ref, rth_ref, rtl_ref, h_ref, we_ref, rank_ref, rankt_ref, cnt_ref, *, ne):
    tt = x_ref.shape[0]
    h = _norm_mod(x_ref[...], g_ref[...], mod_ref[0, 3:4, :], mod_ref[0, 4:5, :])
    h_ref[...] = h.astype(BF16)
    lane = lax.broadcasted_iota(jnp.int32, (tt, LANES), 1)
    logits = _dot3(h, rth_ref[...], rtl_ref[...])
    lg = jnp.where(lane < ne, logits, NEG_INF)
    t1 = jnp.max(lg, axis=1, keepdims=True)
    i1 = jnp.min(jnp.where(lg == t1, lane, LANES), axis=1, keepdims=True)
    lg2 = jnp.where(lane == i1, NEG_INF, lg)
    t2 = jnp.max(lg2, axis=1, keepdims=True)
    i2 = jnp.min(jnp.where(lg2 == t2, lane, LANES), axis=1, keepdims=True)
    e2 = jnp.exp(t2 - t1)
    den = 1.0 + e2
    we_ref[...] = jnp.where(lane == i1, 1.0 / den, 0.0) + jnp.where(lane == i2, e2 / den, 0.0)
    sel = (lane == i1) | (lane == i2)
    self32 = sel.astype(F32)
    before = (lax.broadcasted_iota(jnp.int32, (tt, tt), 1) < lax.broadcasted_iota(jnp.int32, (tt, tt), 0))
    rank = jnp.dot(before.astype(BF16), sel.astype(BF16), preferred_element_type=F32)
    rank = jnp.where(sel, rank, -1.0)
    rank_ref[...] = rank
    rankt_ref[0] = rank.T[0:SUBLANES, :]
    cnt_ref[0] = jnp.broadcast_to(jnp.sum(self32, axis=0, keepdims=True), (SUBLANES, LANES))


def _moe(x2, gain, mod, router_pad, wg, wu, wd, rows):
    m, d = x2.shape
    ne, _, dff = wg.shape
    assert ne <= SUBLANES
    tt = _row_tile(rows, MOE_TILE)
    nti = m // tt
    per = rows // tt
    many = mod.shape[0] > 1
    h, we, rank, rankt, cnt = pl.pallas_call(
        functools.partial(_route_kernel, ne=ne),
        grid=(nti,),
        in_specs=[pl.BlockSpec((tt, d), lambda i: (i, 0)),
                  pl.BlockSpec((1, d), lambda i: (0, 0)),
                  pl.BlockSpec((1, 6, d), lambda i: ((i // per) if many else 0, 0, 0)),
                  pl.BlockSpec((d, LANES), lambda i: (0, 0)),
                  pl.BlockSpec((d, LANES), lambda i: (0, 0))],
        out_specs=[pl.BlockSpec((tt, d), lambda i: (i, 0)),
                   pl.BlockSpec((tt, LANES), lambda i: (i, 0)),
                   pl.BlockSpec((tt, LANES), lambda i: (i, 0)),
                   pl.BlockSpec((1, SUBLANES, tt), lambda i: (i, 0, 0)),
                   pl.BlockSpec((1, SUBLANES, LANES), lambda i: (i, 0, 0))],
        out_shape=[jax.ShapeDtypeStruct((m, d), BF16),
                   jax.ShapeDtypeStruct((m, LANES), F32),
                   jax.ShapeDtypeStruct((m, LANES), F32),
                   jax.ShapeDtypeStruct((nti, SUBLANES, tt), F32),
                   jax.ShapeDtypeStruct((nti, SUBLANES, LANES), F32)],
        compiler_params=_cparams("parallel"),
        name="moe_route",
    )(x2, gain.reshape(1, d), mod, *_hi_lo(router_pad))
    c16 = (cnt[:, 0, :ne].astype(jnp.int32) + (PACK - 1)) // PACK * PACK
    region = (jnp.sum(c16, axis=0) + (MOE_FT - 1)) // MOE_FT * MOE_FT
    e_off = jnp.cumsum(region) - region
    pos = (e_off[None, :] + jnp.cumsum(c16, axis=0) - c16).reshape(-1).astype(jnp.int32)
    c16 = c16.reshape(-1)
    rmax = -(-(2 * m + nti * ne * PACK + ne * MOE_FT) // MOE_FT) * MOE_FT
    nft = rmax // MOE_FT
    tiles_e = jnp.cumsum(region // MOE_FT)
    tid = jnp.arange(nft, dtype=jnp.int32)
    tile_valid = (tid < tiles_e[-1]).astype(jnp.int32)
    tile_map = jnp.minimum(tid, tiles_e[-1] - 1).astype(jnp.int32)
    tile_exp = jnp.minimum(jnp.sum(tile_map[:, None] >= tiles_e[None, :], axis=1), ne - 1).astype(jnp.int32)

    sizes = tuple(tt >> k for k in range(tt.bit_length()) if (tt >> k) >= PACK)
    nsteps = nti * ne
    hs = pl.pallas_call(
        functools.partial(_moe_gather_kernel, ne=ne, nsteps=nsteps, sizes=sizes),
        grid_spec=pltpu.PrefetchScalarGridSpec(
            num_scalar_prefetch=2,
            grid=(nti, ne),
            in_specs=[pl.BlockSpec((tt, d), lambda i, e, p_, c_: (i, 0)),
                      pl.BlockSpec((1, SUBLANES, tt), lambda i, e, p_, c_: (i, 0, 0)),
                      pl.BlockSpec(memory_space=pl.ANY)],
            out_specs=pl.BlockSpec(memory_space=pl.ANY),
            scratch_shapes=[pltpu.VMEM((2, tt, d), BF16), pltpu.SemaphoreType.DMA((2, len(sizes)))]),
        out_shape=jax.ShapeDtypeStruct((rmax, d), BF16),
        input_output_aliases={4: 0},
        compiler_params=_cparams("arbitrary", "arbitrary"),
        name="moe_gather",
    )(pos, c16, h, rankt, jnp.zeros((rmax, d), BF16))

    once = pl.Buffered(1)
    os_sorted = pl.pallas_call(
        _moe_ffn_kernel,
        grid_spec=pltpu.PrefetchScalarGridSpec(
            num_scalar_prefetch=3,
            grid=(nft,),
            in_specs=[pl.BlockSpec((MOE_FT, d), lambda t, te, tm, tv: (tm[t], 0)),
                      pl.BlockSpec((1, d, dff), lambda t, te, tm, tv: (te[t], 0, 0), pipeline_mode=once),
                      pl.BlockSpec((1, d, dff), lambda t, te, tm, tv: (te[t], 0, 0), pipeline_mode=once),
                      pl.BlockSpec((1, dff, d), lambda t, te, tm, tv: (te[t], 0, 0), pipeline_mode=once)],
            out_specs=pl.BlockSpec((MOE_FT, d), lambda t, te, tm, tv: (t, 0))),
        out_shape=jax.ShapeDtypeStruct((rmax, d), BF16),
        compiler_params=pltpu.CompilerParams(dimension_semantics=("parallel",),
                                             vmem_limit_bytes=58 * 1024 * 1024),
        name="moe_experts",
    )(tile_exp, tile_map, tile_valid, hs, wg, wu, wd)

    return pl.pallas_call(
        functools.partial(_moe_combine_kernel, ne=ne, nsteps=nsteps, sizes=sizes),
        grid_spec=pltpu.PrefetchScalarGridSpec(
            num_scalar_prefetch=2,
            grid=(nti, ne),
            in_specs=[pl.BlockSpec((tt, d), lambda i, e, p_, c_: (i, 0)),
                      pl.BlockSpec((tt, LANES), lambda i, e, p_, c_: (i, 0)),
                      pl.BlockSpec((tt, LANES), lambda i, e, p_, c_: (i, 0)),
                      pl.BlockSpec((1, 6, d), lambda i, e, p_, c_: ((i // per) if many else 0, 0, 0)),
                      pl.BlockSpec(memory_space=pl.ANY)],
            out_specs=pl.BlockSpec((tt, d), lambda i, e, p_, c_: (i, 0)),
            scratch_shapes=[pltpu.VMEM((tt, d), F32), pltpu.VMEM((2, tt, d), BF16),
                            pltpu.SemaphoreType.DMA((2, len(sizes)))]),
        out_shape=jax.ShapeDtypeStruct((m, d), F32),
        compiler_params=_cparams("arbitrary", "arbitrary"),
        name="moe_combine",
    )(pos, c16, x2, we, rank, mod, os_sorted)


def _rope_table(t):
    rows = t // GRID_W
    row = jnp.repeat(jnp.arange(rows, dtype=F32), GRID_W)
    col = jnp.tile(jnp.arange(GRID_W, dtype=F32), rows)
    half = HEAD_DIM // 2
    inv_freq = ROPE_THETA ** (-jnp.arange(0, half, 2, dtype=F32) / half)
    ar = row[:, None] * inv_freq
    ac = col[:, None] * inv_freq
    ang = jnp.concatenate([ar, ar, ac, ac] * 2, axis=-1)
    return jnp.stack([jnp.cos(ang), jnp.sin(ang)])


def _pad_rows(a, rows):
    return jnp.zeros((rows,) + a.shape[1:], a.dtype).at[:a.shape[0]].set(a)


def _hi_lo(w):
    hi = w.astype(BF16)
    return hi, (w - hi.astype(F32)).astype(BF16)


def kernel(x, c, ctx, c_ctx, norm1, norm2, w_mod, b_mod, w_in, tshift, w0, w_decay_up, a0, w_iclr_up, v0, w_vres_down, w_vres_up, w_gate_up, k_k, k_a, r_k, lnx_g, lnx_b, q_norm, k_norm, sink, w_br_rwkv, w_br_attn, w_out, ffn_gate, ffn_up, ffn_down, router, moe_gate, moe_up, moe_down):
    nb, t, d = x.shape
    cn = ctx.shape[1]
    depth = w_in.shape[0]
    mods = _modulation(c, c_ctx, w_mod, b_mod)
    cs = _rope_table(t)
    e128 = jnp.kron(jnp.eye(2, dtype=F32), jnp.ones((HEAD_DIM, HEAD_DIM), F32)).astype(BF16)
    xl = x.reshape(nb * t, d)
    xc = ctx.reshape(nb * cn, d)
    vf_l = vf_c = None
    for layer in range(depth):
        last = layer == depth - 1
        mod_l = mods[layer, :nb].reshape(nb, 6, d)
        mod_c = mods[layer, nb:nb + 1].reshape(1, 6, d)
        w_in_b = w_in[layer].astype(BF16)
        vd_b = None
        prm = {"tshift": tshift[layer], "e128": e128, "wg": w_gate_up[layer].astype(BF16)}
        prm["pd"] = jnp.stack(
            [_pad_rows(jnp.stack([w0[layer, dd], a0[layer, dd]]), SUBLANES) for dd in range(2)])
        shared = [k_k[layer], k_a[layer], r_k[layer].reshape(RW_DIM)]
        if layer > 0:
            shared.append(v0[layer - 1])
            vd_b = _pad_rows(w_vres_down[layer - 1].T, LANES).T.astype(BF16)
            prm["vu_hi"], prm["vu_lo"] = _hi_lo(_pad_rows(w_vres_up[layer - 1], LANES))
        prm["ps"] = _pad_rows(jnp.stack(shared), SUBLANES)
        wla = jnp.zeros((2, LANES, 2 * RW_DIM), F32)
        wla = wla.at[:, :HEAD_DIM, :RW_DIM].set(w_decay_up[layer]).at[:, HEAD_DIM:, RW_DIM:].set(w_iclr_up[layer])
        prm["wla"] = wla.astype(BF16)

        p_l, hv_l = _inproj(xl, norm1[layer], mod_l, w_in_b, vd_b, t)
        p_c, hv_c = _inproj(xc, norm1[layer], mod_c, w_in_b, vd_b, cn)

        s0 = jnp.zeros((nb, N_PAIRS, LANES, LANES), F32)
        y_c, aux_c, s_ctx = _rwkv_both(p_c, vf_c, hv_c, [s0, s0], prm, nb, cn)
        y_l, aux_l, _ = _rwkv_both(p_l, vf_l, hv_l, s_ctx, prm, nb, t)
        if layer == 0:
            vf_l, vf_c = aux_l[1], aux_c[1]

        qg = jnp.tile(q_norm[layer], 2).reshape(1, LANES)
        kg = jnp.tile(k_norm[layer], 2).reshape(1, LANES)
        qn_l, kn_l, vb_l = _qk_prep(p_l, cs, qg, kg, e128, t, True)
        qn_c, kn_c, vb_c = _qk_prep(p_c, None, qg, kg, e128, cn, False)
        sk = sink[layer].reshape(1, -1)
        ya_l = _attention(qn_l, kn_l, vb_l, kn_c, vb_c, sk, nb, t, cn, True)

        ln = jnp.stack([lnx_g[layer], lnx_b[layer]])
        wbr = w_br_rwkv[layer].astype(BF16)
        wba = w_br_attn[layer].astype(BF16)
        wo = w_out[layer].astype(BF16)
        xl = _merge(xl, y_l, aux_l, ya_l, p_l, ln, e128, wbr, wba, wo, mod_l, t)
        if not last:
            ya_c = _attention(qn_c, None, None, kn_c, vb_c, sk, nb, cn, cn, False)
            xc = _merge(xc, y_c, aux_c, ya_c, p_c, ln, e128, wbr, wba, wo, mod_c, cn)

        i = layer // 2
        if layer % 2 == 0:
            fw = (ffn_gate[i].astype(BF16), ffn_up[i].astype(BF16), ffn_down[i].astype(BF16))
            xl = _ffn(xl, norm2[layer], mod_l, *fw, t)
            if not last:
                xc = _ffn(xc, norm2[layer], mod_c, *fw, cn)
        else:
            rt = _pad_rows(router[i].T, LANES).T
            mw = (moe_gate[i].astype(BF16), moe_up[i].astype(BF16), moe_down[i].astype(BF16))
            xl = _moe(xl, norm2[layer], mod_l, rt, *mw, t)
            if not last:
                xc = _moe(xc, norm2[layer], mod_c, rt, *mw, cn)
    return xl.reshape(nb, t, d)
```

```python
import functools

import jax
import jax.numpy as jnp
from jax import lax
from jax.experimental import pallas as pl
from jax.experimental.pallas import tpu as pltpu

F32 = jnp.float32
BF16 = jnp.bfloat16
HIGHEST = lax.Precision.HIGHEST

LANES = 128
SUBLANES = 8
PACK = 16
ACT = jnp.bfloat16
HEAD_DIM = 64
RW_DIM = 512
ATT_DIM = 512
KV_DIM = 128
RW_IN = 1792
ATT_OFF = RW_IN
GATE_OFF = RW_IN + ATT_DIM + 2 * KV_DIM
N_EXPERTS = 8
CHUNK = 64
N_PAIRS = RW_DIM // LANES
MOE_TILE = 1024
MOE_BLK = 128
MOE_FT = 512
SCAN_GROUP = 4
ATT_SCALE = HEAD_DIM ** -0.5
ROPE_THETA = 10000.0
GRID_W = 64
RMS_EPS = 1e-6
GN_EPS = 64e-5
NEG_INF = -1e30
DECAY_SCALE = 0.6065306597126334
VMEM_LIMIT = 48 * 1024 * 1024


def _cparams(*sem):
    return pltpu.CompilerParams(dimension_semantics=sem, vmem_limit_bytes=VMEM_LIMIT)


def _row_tile(rows, cap):
    t = cap
    while rows % t:
        t //= 2
    return t


def _bdot(a, b):
    return jnp.dot(a.astype(BF16), b.astype(BF16), preferred_element_type=F32)


def _bdot_nt(a, b):
    return lax.dot_general(a.astype(BF16), b.astype(BF16), (((1,), (1,)), ((), ())),
                           preferred_element_type=F32)


def _bdot_tn(a, b):
    return lax.dot_general(a.astype(BF16), b.astype(BF16), (((0,), (0,)), ((), ())),
                           preferred_element_type=F32)


def _split2(x):
    hi = x.astype(BF16)
    lo = (x - hi.astype(F32)).astype(BF16)
    return hi, lo


def _dot3(a, w_hi, w_lo):
    a_hi, a_lo = _split2(a)
    return (jnp.dot(a_hi, w_hi, preferred_element_type=F32)
            + (jnp.dot(a_lo, w_hi, preferred_element_type=F32)
               + jnp.dot(a_hi, w_lo, preferred_element_type=F32)))


def _segsum(x, e, exact=True):
    outs = []
    for s in range(x.shape[1] // LANES):
        hi, lo = _split2(x[:, s * LANES:(s + 1) * LANES])
        acc = jnp.dot(hi, e, preferred_element_type=F32)
        outs.append(acc + jnp.dot(lo, e, preferred_element_type=F32) if exact else acc)
    return outs[0] if len(outs) == 1 else jnp.concatenate(outs, axis=1)


def _interleave(*stages):
    live = list(stages)
    while live:
        for g in list(live):
            if next(g, live) is live:
                live.remove(g)


def _norm_mod(x, gain, shift, scale):
    y = x * lax.rsqrt(jnp.mean(x * x, axis=-1, keepdims=True) + RMS_EPS) * gain
    return y * (1.0 + scale) + shift


def _mod_kernel(c_ref, w_ref, b_ref, o_ref):
    c = c_ref[...]
    s = c * jax.nn.sigmoid(c)
    o_ref[0] = jnp.dot(s, w_ref[0], precision=HIGHEST, preferred_element_type=F32) + b_ref[0]


def _modulation(c, c_ctx, w_mod, b_mod):
    depth, d, n6 = w_mod.shape
    nb = c.shape[0]
    rm = -(-(nb + 1) // SUBLANES) * SUBLANES
    cc = jnp.zeros((rm, d), F32).at[:nb].set(c).at[nb].set(c_ctx)
    tn = _row_tile(n6, 1536)
    return pl.pallas_call(
        _mod_kernel,
        grid=(depth, n6 // tn),
        in_specs=[pl.BlockSpec((rm, d), lambda l, n: (0, 0)),
                  pl.BlockSpec((1, d, tn), lambda l, n: (l, 0, n)),
                  pl.BlockSpec((1, 1, tn), lambda l, n: (l, 0, n))],
        out_specs=pl.BlockSpec((1, rm, tn), lambda l, n: (l, 0, n)),
        out_shape=jax.ShapeDtypeStruct((depth, rm, n6), F32),
        compiler_params=_cparams("parallel", "parallel"),
        name="adaln_mod",
    )(cc, w_mod, b_mod.reshape(depth, 1, n6))


def _inproj_kernel(*refs, vres, nsub):
    if vres:
        x_ref, g_ref, mod_ref, w_ref, vd_ref, o_ref, hv_ref = refs
    else:
        x_ref, g_ref, mod_ref, w_ref, o_ref = refs
    sub = x_ref.shape[0] // nsub
    for s in range(nsub):
        rs = slice(s * sub, (s + 1) * sub)
        h = _norm_mod(x_ref[rs, :], g_ref[...], mod_ref[0, 0:1, :], mod_ref[0, 1:2, :]).astype(BF16)
        if vres:
            hv_ref[rs, :] = jnp.dot(h, vd_ref[...], preferred_element_type=F32)
        o_ref[rs, :] = jnp.dot(h, w_ref[...], preferred_element_type=F32).astype(o_ref.dtype)


def _inproj(x2, gain, mod, w_b, vd_b, rows):
    m, d = x2.shape
    n = w_b.shape[1]
    tm = _row_tile(rows, 1024)
    per = rows // tm
    many = mod.shape[0] > 1
    vres = vd_b is not None
    once = pl.Buffered(1)
    in_specs = [pl.BlockSpec((tm, d), lambda i: (i, 0)),
                pl.BlockSpec((1, d), lambda i: (0, 0)),
                pl.BlockSpec((1, 6, d), lambda i: ((i // per) if many else 0, 0, 0)),
                pl.BlockSpec((d, n), lambda i: (0, 0), pipeline_mode=once)]
    args = [x2, gain.reshape(1, d), mod, w_b]
    out_specs = [pl.BlockSpec((tm, n), lambda i: (i, 0))]
    out_shape = [jax.ShapeDtypeStruct((m, n), ACT)]
    if vres:
        in_specs.append(pl.BlockSpec((d, LANES), lambda i: (0, 0)))
        args.append(vd_b)
        out_specs.append(pl.BlockSpec((tm, LANES), lambda i: (i, 0)))
        out_shape.append(jax.ShapeDtypeStruct((m, LANES), F32))
    outs = pl.pallas_call(
        functools.partial(_inproj_kernel, vres=vres, nsub=max(1, tm // 256)),
        grid=(m // tm,),
        in_specs=in_specs, out_specs=out_specs, out_shape=out_shape,
        compiler_params=_cparams("parallel"),
        name="in_proj",
    )(*args)
    return (outs[0], outs[1]) if vres else (outs[0], None)


def _qkprep_kernel(*refs, rope):
    if rope:
        q0_ref, q1_ref, k_ref, v_ref, cs_ref, qg_ref, kg_ref, e_ref, qn_ref, kn_ref, vb_ref = refs
    else:
        q0_ref, q1_ref, k_ref, v_ref, qg_ref, kg_ref, e_ref, qn_ref, kn_ref, vb_ref = refs
    e = e_ref[...]
    lane = lax.broadcasted_iota(jnp.int32, (1, LANES), 1)
    first = (lane & 31) < 16

    def norm_rope(u, gain):
        y = u * lax.rsqrt(_segsum(u * u, e) * (1.0 / HEAD_DIM) + RMS_EPS) * gain
        if rope:
            rot = jnp.where(first, -pltpu.roll(y, LANES - 16, 1), pltpu.roll(y, 16, 1))
            y = y * cs_ref[0] + rot * cs_ref[1]
        return y

    qg = qg_ref[...]
    for s, ref in enumerate((q0_ref, q1_ref)):
        for t in range(2):
            u = ref[:, t * LANES:(t + 1) * LANES].astype(F32)
            c0 = (2 * s + t) * LANES
            qn_ref[:, c0:c0 + LANES] = (norm_rope(u, qg) * ATT_SCALE).astype(BF16)
    low = lane < HEAD_DIM
    for src_val, dst in ((norm_rope(k_ref[...].astype(F32), kg_ref[...]), kn_ref), (v_ref[...].astype(F32), vb_ref)):
        swapped = pltpu.roll(src_val, HEAD_DIM, 1)
        dst[:, 0:LANES] = jnp.where(low, src_val, swapped).astype(BF16)
        dst[:, LANES:2 * LANES] = jnp.where(low, swapped, src_val).astype(BF16)


def _qk_prep(p, cs, qg, kg, e128, rows, rope):
    m = p.shape[0]
    tm = _row_tile(rows, 512)
    per = rows // tm
    qb = ATT_OFF // 256
    kb = (ATT_OFF + ATT_DIM) // LANES
    in_specs = [pl.BlockSpec((tm, 256), lambda i: (i, qb)),
                pl.BlockSpec((tm, 256), lambda i: (i, qb + 1)),
                pl.BlockSpec((tm, LANES), lambda i: (i, kb)),
                pl.BlockSpec((tm, LANES), lambda i: (i, kb + 1))]
    args = [p, p, p, p]
    if rope:
        in_specs.append(pl.BlockSpec((2, tm, LANES), lambda i: (0, i % per, 0)))
        args.append(cs)
    in_specs += [pl.BlockSpec((1, LANES), lambda i: (0, 0)),
                 pl.BlockSpec((1, LANES), lambda i: (0, 0)),
                 pl.BlockSpec((LANES, LANES), lambda i: (0, 0))]
    args += [qg, kg, e128]
    return pl.pallas_call(
        functools.partial(_qkprep_kernel, rope=rope),
        grid=(m // tm,),
        in_specs=in_specs,
        out_specs=[pl.BlockSpec((tm, ATT_DIM), lambda i: (i, 0)),
                   pl.BlockSpec((tm, 2 * KV_DIM), lambda i: (i, 0)),
                   pl.BlockSpec((tm, 2 * KV_DIM), lambda i: (i, 0))],
        out_shape=[jax.ShapeDtypeStruct((m, ATT_DIM), BF16),
                   jax.ShapeDtypeStruct((m, 2 * KV_DIM), BF16),
                   jax.ShapeDtypeStruct((m, 2 * KV_DIM), BF16)],
        compiler_params=_cparams("parallel"),
        name="qk_prep",
    )(*args)


def _attn_kernel(*refs, nq, local):
    if local:
        sink_ref, q_ref, kp_ref, kc_ref, kn_ref, vp_ref, vc_ref, vn_ref, kx_ref, vx_ref, wm_ref, o_ref = refs
        k = jnp.concatenate([kp_ref[...], kc_ref[...], kn_ref[...], kx_ref[...]], axis=0)
        v = jnp.concatenate([vp_ref[...], vc_ref[...], vn_ref[...], vx_ref[...]], axis=0)
    else:
        sink_ref, q_ref, kx_ref, vx_ref, o_ref = refs
        k = kx_ref[...]
        v = vx_ref[...]
    i = pl.program_id(1)
    bq = q_ref.shape[0]
    low = lax.broadcasted_iota(jnp.int32, (1, LANES), 1) < HEAD_DIM
    row = lax.broadcasted_iota(jnp.int32, (2 * bq, 1), 0)
    if local:
        kblk = lax.broadcasted_iota(jnp.int32, (1, 3 * bq), 1) // bq
        off_end = ((kblk == 0) & (i == 0)) | ((kblk == 2) & (i == nq - 1))
        bias = wm_ref[...] + jnp.where(off_end, NEG_INF, 0.0)
    def head_pair(a):
        g = a // 2
        kb = k[:, g * LANES:(g + 1) * LANES]
        vb = v[:, g * LANES:(g + 1) * LANES]
        qa = q_ref[:, a * LANES:(a + 1) * LANES].astype(F32)
        qs = jnp.concatenate([jnp.where(low, qa, 0.0), jnp.where(low, 0.0, qa)], axis=0).astype(BF16)
        s = lax.dot_general(qs, kb, (((1,), (1,)), ((), ())), preferred_element_type=F32)
        yield
        if local:
            s = jnp.concatenate([s[:, :3 * bq] + bias, s[:, 3 * bq:]], axis=1)
        sk = jnp.where(row < bq, sink_ref[0, 2 * a], sink_ref[0, 2 * a + 1])
        mx = jnp.maximum(jnp.max(s, axis=1, keepdims=True), sk)
        yield
        ex = jnp.exp(s - mx)
        den = jnp.sum(ex, axis=1, keepdims=True) + jnp.exp(sk - mx)
        yield
        o = jnp.dot(ex.astype(BF16), vb, preferred_element_type=F32) / den
        o_ref[:, a * LANES:(a + 1) * LANES] = jnp.where(low, o[0:bq], o[bq:2 * bq]).astype(o_ref.dtype)

    _interleave(*[head_pair(a) for a in range(4)])


def _attention(qn, kn, vb, kctx, vctx, sink, nb, rows, cn, local):
    m = qn.shape[0]
    bq = 128
    nq = rows // bq
    smem = pl.BlockSpec(memory_space=pltpu.SMEM)
    qspec = pl.BlockSpec((bq, ATT_DIM), lambda b, i: (b * nq + i, 0))
    cspec = pl.BlockSpec((cn, 2 * KV_DIM), lambda b, i: (b, 0))
    if local:
        prv = pl.BlockSpec((bq, 2 * KV_DIM), lambda b, i: (b * nq + jnp.maximum(i - 1, 0), 0))
        cur = pl.BlockSpec((bq, 2 * KV_DIM), lambda b, i: (b * nq + i, 0))
        nxt = pl.BlockSpec((bq, 2 * KV_DIM), lambda b, i: (b * nq + jnp.minimum(i + 1, nq - 1), 0))
        qo = jnp.arange(2 * bq)[:, None] % bq
        kcol = jnp.arange(3 * bq)[None, :]
        inwin = ((kcol // bq == 1) | ((kcol // bq == 0) & (kcol % bq >= qo)) | ((kcol // bq == 2) & (kcol % bq <= qo)))
        wmask = jnp.where(inwin, 0.0, NEG_INF).astype(F32)
        in_specs = [smem, qspec, prv, cur, nxt, prv, cur, nxt, cspec, cspec,
                    pl.BlockSpec((2 * bq, 3 * bq), lambda b, i: (0, 0))]
        args = [sink, qn, kn, kn, kn, vb, vb, vb, kctx, vctx, wmask]
    else:
        in_specs = [smem, qspec, cspec, cspec]
        args = [sink, qn, kctx, vctx]
    return pl.pallas_call(
        functools.partial(_attn_kernel, nq=nq, local=local),
        grid=(nb, nq),
        in_specs=in_specs,
        out_specs=pl.BlockSpec((bq, ATT_DIM), lambda b, i: (b * nq + i, 0)),
        out_shape=jax.ShapeDtypeStruct((m, ATT_DIM), BF16),
        compiler_params=_cparams("parallel", "parallel"),
        name="window_attn" if local else "ctx_attn",
    )(*args)


def _rwkv_kernel(*refs, nt, tq, vres, drn, tps):
    it = iter(refs)
    n_in = 5 if vres else 3
    first = [next(it) for _ in range(n_in)] if tps == 2 else None
    tiles = [[next(it) for _ in range(n_in)] for _ in range(tps)]
    ts_ref, pd_ref, ps_ref, wla_ref, wg_ref = (next(it) for _ in range(5))
    if vres:
        vuh_ref, vul_ref = next(it), next(it)
    e_ref, sin_ref = next(it), next(it)
    y_ref, aux_ref, sout_ref = next(it), next(it), next(it)
    ops = [next(it) for _ in range(tps)]
    auxs = [next(it) for _ in range(tps)]
    s_sc = next(it)

    j = pl.program_id(1)
    sgn = 1 - 2 * drn
    nch = tq // CHUNK
    nsteps = nt // tps
    e = e_ref[...]
    lane = lax.broadcasted_iota(jnp.int32, (1, LANES), 1)
    low = lane < HEAD_DIM

    def natural(k):
        return k if drn == 0 else nt - 1 - k

    def prepare(tile, in_refs, ops_sc, aux_sc):
        p_ref, pv_ref, nx_ref = in_refs[:3]
        rw = p_ref[...].astype(F32)
        row = lax.broadcasted_iota(jnp.int32, (tq, 1), 0)
        prev_row = jnp.where(tile > 0, pv_ref[PACK - 1:PACK, :].astype(F32), 0.0)
        next_row = jnp.where(tile < nt - 1, nx_ref[0:1, :].astype(F32), 0.0)
        prv = jnp.where(row == 0, prev_row, pltpu.roll(rw, 1, 0))
        yield
        nxt = jnp.where(row == tq - 1, next_row, pltpu.roll(rw, tq - 1, 0))
        xs = rw + ts_ref[0:1, :] * (prv - rw) + ts_ref[1:2, :] * (nxt - rw)
        yield

        r = xs[:, 0:RW_DIM]
        k = xs[:, RW_DIM:2 * RW_DIM]
        v = xs[:, 2 * RW_DIM:3 * RW_DIM]
        la = xs[:, 3 * RW_DIM:3 * RW_DIM + LANES]
        la = jnp.where(low, jnp.tanh(la), la)
        la_hi, la_lo = _split2(la)
        lora = (jnp.dot(la_hi, wla_ref[0], preferred_element_type=F32)
                + jnp.dot(la_lo, wla_ref[0], preferred_element_type=F32))
        yield
        logw = -DECAY_SCALE * jax.nn.sigmoid(pd_ref[0, 0:1, :] + lora[:, 0:RW_DIM])
        iclr = jax.nn.sigmoid(pd_ref[0, 1:2, :] + lora[:, RW_DIM:2 * RW_DIM])
        if vres:
            vf_ref, hv_ref = in_refs[3:]
            mix = jax.nn.sigmoid(ps_ref[3:4, :] + _dot3(hv_ref[...], vuh_ref[...], vul_ref[...]))
            v = v + (vf_ref[...].astype(F32) - v) * mix
        yield
        kk = k * ps_ref[0:1, :]
        kk = kk * lax.rsqrt(_segsum(kk * kk, e, exact=False) + 1e-12)
        yield
        kh = k * (1.0 + (iclr - 1.0) * ps_ref[1:2, :])
        bb = kk * iclr
        aux_sc[:, 0:RW_DIM] = (_segsum(r * kh * ps_ref[2:3, :], e, exact=False) * v).astype(aux_sc.dtype)
        if drn == 0:
            glo = xs[:, 3 * RW_DIM + LANES:RW_IN]
            aux_sc[:, RW_DIM:2 * RW_DIM] = _bdot(jax.nn.sigmoid(glo), wg_ref[...]).astype(aux_sc.dtype)
        else:
            aux_sc[:, RW_DIM:2 * RW_DIM] = v.astype(aux_sc.dtype)

        yield
        ti = lax.broadcasted_iota(jnp.int32, (tq, tq), 0)
        si = lax.broadcasted_iota(jnp.int32, (tq, tq), 1)
        tri = (((ti // CHUNK) == (si // CHUNK)) & (((si - ti) * sgn) <= 0)).astype(BF16)
        lw_hi, lw_lo = _split2(logw)
        cin = jnp.dot(tri, lw_hi, preferred_element_type=F32) + jnp.dot(tri, lw_lo, preferred_element_type=F32)
        cin3 = cin.reshape(nch, CHUNK, RW_DIM)
        last = CHUNK - 1 if drn == 0 else 0
        ctot = jnp.broadcast_to(cin3[:, last:last + 1, :], cin3.shape).reshape(tq, RW_DIM)
        yield
        e_neg = jnp.exp(-cin)
        e_rem = jnp.exp(ctot - cin)
        ops_sc[0] = kk * jnp.exp(cin - logw)
        ops_sc[1] = r * jnp.exp(cin)
        ops_sc[2] = bb * e_neg
        ops_sc[3] = kh * e_neg
        yield
        ops_sc[4] = kh * e_rem
        ops_sc[5] = bb * e_rem
        ops_sc[6] = v
        ops_sc[7] = jnp.exp(ctot)

    tr = lax.broadcasted_iota(jnp.int32, (CHUNK, LANES), 0)
    sc = lax.broadcasted_iota(jnp.int32, (CHUNK, LANES), 1) & (CHUNK - 1)
    rel = (sc - tr) * sgn
    strict = rel < 0
    incl = rel <= 0
    eye = (sc == tr).astype(F32)
    same_head = ((lax.broadcasted_iota(jnp.int32, (LANES, LANES), 0) // HEAD_DIM)
                 == (lax.broadcasted_iota(jnp.int32, (LANES, LANES), 1) // HEAD_DIM))

    def stack(x):
        return jnp.concatenate([jnp.where(low, x, 0.0), jnp.where(low, 0.0, x)], axis=0)

    def hdot(x, y):
        return _bdot(x, stack(y))

    def scan_tile(ops_sc, row0):
        st = [s_sc[pr] for pr in range(N_PAIRS)]
        for g0 in range(0, nch, SCAN_GROUP):
            yield from scan_chunks(ops_sc, row0, st, range(g0, g0 + SCAN_GROUP))
        for pr in range(N_PAIRS):
            s_sc[pr] = st[pr]

    def scan_chunks(ops_sc, row0, st, chunk_ids):
        probs = []
        for ii in chunk_ids:
            off = (ii if drn == 0 else nch - 1 - ii) * CHUNK
            for pr in range(N_PAIRS):
                ls = slice(pr * LANES, (pr + 1) * LANES)
                q = {"off": off, "ls": ls, "pr": pr}
                for n, name in enumerate(("kkt", "rt", "bh", "kh", "kp", "bp", "v")):
                    q[name] = ops_sc[n, off:off + CHUNK, ls]
                q["wl"] = ops_sc[7, off:off + 1, ls]
                probs.append(q)
        for q in probs:
            g = _bdot_nt(jnp.concatenate([q["kkt"], q["rt"]], axis=0),
                         jnp.concatenate([stack(q["bh"]), stack(q["kh"])], axis=0))
            q["nn"] = jnp.where(strict, -g[0:CHUNK, 0:LANES], 0.0)
            q["avk"] = jnp.where(strict, g[0:CHUNK, LANES:], 0.0)
            q["arb"] = jnp.where(incl, g[CHUNK:, 0:LANES], 0.0)
            q["ark"] = jnp.where(incl, g[CHUNK:, LANES:], 0.0)
            q["t"] = eye + q["nn"]
        yield
        for q in probs:
            q["q"] = hdot(q["nn"], q["nn"])
        for _ in range(4):
            for q in probs:
                tq2 = _bdot(jnp.concatenate([q["t"], q["q"]], axis=0), stack(q["q"]))
                q["t"] = q["t"] + tq2[:CHUNK]
                q["q"] = tq2[CHUNK:]
            yield
        for q in probs:
            q["t"] = q["t"] + hdot(q["t"], q["q"])
        for q in probs:
            av = _bdot(jnp.concatenate([q["avk"], q["ark"]], axis=0), stack(q["v"]))
            q["av"], q["arkv"] = av[:CHUNK], av[CHUNK:]
        yield
        for q in probs:
            tt2 = _bdot(q["t"], jnp.concatenate([stack(q["kkt"]), stack(q["av"])], axis=1))
            q["tk"], q["tav"] = tt2[:, :LANES], tt2[:, LANES:]
        yield
        for q in probs:
            tn = _bdot_tn(jnp.concatenate([q["tk"], q["tav"]], axis=1), q["bp"])
            q["mb"] = jnp.where(same_head, tn[:LANES], 0.0)
            q["nm"] = jnp.where(same_head, _bdot_tn(q["v"], q["kp"]) - tn[LANES:], 0.0)
            z = _bdot(q["arb"], jnp.concatenate([stack(q["tk"]), stack(q["tav"])], axis=1))
            q["r2"] = q["rt"] - z[:, :LANES]
            q["y0"] = q["arkv"] - z[:, LANES:]
        yield

        outs = []
        for q in probs:
            sm = st[q["pr"]]
            outs.append(_bdot_nt(q["r2"], sm) + q["y0"])
            st[q["pr"]] = sm * q["wl"] - _bdot(sm, q["mb"]) + q["nm"]
        for q, yo in zip(probs, outs):
            y_ref[row0 + q["off"]:row0 + q["off"] + CHUNK, q["ls"]] = yo.astype(y_ref.dtype)

    run = _interleave

    @pl.when(j == 0)
    def _():
        s_sc[...] = sin_ref[0]

    if tps == 1:
        run(prepare(natural(j), tiles[0], ops[0], auxs[0]))
        aux_ref[...] = auxs[0][...]
        run(scan_tile(ops[0], 0))
    else:
        @pl.when((j == 0) & (pl.program_id(0) == 0))
        def _():
            run(prepare(natural(0), first, ops[0], auxs[0]))

        rows = [0, tq] if drn == 0 else [tq, 0]
        aux_ref[rows[0]:rows[0] + tq, :] = auxs[0][...]
        run(scan_tile(ops[0], rows[0]), prepare(natural(2 * j + 1), tiles[0], ops[1], auxs[1]))
        aux_ref[rows[1]:rows[1] + tq, :] = auxs[1][...]
        run(scan_tile(ops[1], rows[1]),
            prepare(natural(jnp.where(2 * j + 2 >= nt, 0, 2 * j + 2)), tiles[1], ops[0], auxs[0]))

    @pl.when(j == nsteps - 1)
    def _():
        sout_ref[0] = s_sc[...]


def _rwkv(p, vf_src, hv, state_in, prm, nb, rows, drn):
    m = p.shape[0]
    tq = _row_tile(rows, 256)
    nt = rows // tq
    tps = 2 if nt % 2 == 0 else 1
    nsteps = nt // tps
    rb = tq // PACK
    vres = vf_src is not None

    def tile_specs(scan_pos):
        def tile(b, j):
            k = scan_pos(j)
            over = k >= nt
            row = jnp.where(over, jnp.minimum(b + 1, nb - 1), b)
            k = jnp.where(over, 0, k)
            return row * nt + (k if drn == 0 else nt - 1 - k)

        specs = [pl.BlockSpec((tq, RW_IN), lambda b, j: (tile(b, j), 0)),
                 pl.BlockSpec((PACK, RW_IN), lambda b, j: (jnp.maximum(tile(b, j) * rb - 1, 0), 0)),
                 pl.BlockSpec((PACK, RW_IN), lambda b, j: (jnp.minimum((tile(b, j) + 1) * rb, m // PACK - 1), 0))]
        args = [p, p, p]
        if vres:
            specs += [pl.BlockSpec((tq, RW_DIM), lambda b, j: (tile(b, j), 1)),
                      pl.BlockSpec((tq, LANES), lambda b, j: (tile(b, j), 0))]
            args += [vf_src, hv]
        return specs, args

    if tps == 2:
        positions = [lambda j: 0 * j, lambda j: 2 * j + 1, lambda j: 2 * j + 2]
    else:
        positions = [lambda j: j]
    in_specs, args = [], []
    for pos in positions:
        s_, a_ = tile_specs(pos)
        in_specs += s_
        args += a_

    const2 = lambda b, j: (0, 0)
    dir3 = lambda b, j: (drn, 0, 0)
    in_specs += [pl.BlockSpec((2, RW_IN), const2),
                 pl.BlockSpec((1, SUBLANES, RW_DIM), dir3),
                 pl.BlockSpec((SUBLANES, RW_DIM), const2),
                 pl.BlockSpec((1, LANES, 2 * RW_DIM), dir3),
                 pl.BlockSpec((LANES, RW_DIM), const2)]
    args += [prm["tshift"], prm["pd"], prm["ps"], prm["wla"], prm["wg"]]
    if vres:
        in_specs += [pl.BlockSpec((LANES, RW_DIM), const2), pl.BlockSpec((LANES, RW_DIM), const2)]
        args += [prm["vu_hi"], prm["vu_lo"]]
    sspec = pl.BlockSpec((1, N_PAIRS, LANES, LANES), lambda b, j: (b, 0, 0, 0))
    in_specs += [pl.BlockSpec((LANES, LANES), const2), sspec]
    args += [prm["e128"], state_in]

    def out_block(b, j):
        return (b * nsteps + (j if drn == 0 else nsteps - 1 - j), 0)

    return pl.pallas_call(
        functools.partial(_rwkv_kernel, nt=nt, tq=tq, vres=vres, drn=drn, tps=tps),
        grid=(nb, nsteps),
        in_specs=in_specs,
        out_specs=[pl.BlockSpec((tps * tq, RW_DIM), out_block),
                   pl.BlockSpec((tps * tq, 2 * RW_DIM), out_block),
                   sspec],
        out_shape=[jax.ShapeDtypeStruct((m, RW_DIM), ACT),
                   jax.ShapeDtypeStruct((m, 2 * RW_DIM), ACT),
                   jax.ShapeDtypeStruct((nb, N_PAIRS, LANES, LANES), F32)],
        scratch_shapes=([pltpu.VMEM((8, tq, RW_DIM), F32)] * tps + [pltpu.VMEM((tq, 2 * RW_DIM), ACT)] * tps
                        + [pltpu.VMEM((N_PAIRS, LANES, LANES), F32)]),
        compiler_params=_cparams("arbitrary", "arbitrary"),
        name="rwkv7_fwd" if drn == 0 else "rwkv7_bwd",
    )(*args)


def _rwkv_both(p, vf_src, hv, states, prm, nb, rows):
    outs = [_rwkv(p, vf_src, hv, states[d], prm, nb, rows, d) for d in range(2)]
    return [o[0] for o in outs], [o[1] for o in outs], [o[2] for o in outs]


def _merge_kernel(x_ref, y0_ref, y1_ref, a0_ref, a1_ref, ya_ref, g0_ref, g1_ref, g2_ref, g3_ref, ln_ref, e_ref,
                  wbr_ref, wba_ref, wo_ref, mod_ref, o_ref):
    e = e_ref[...]
    y = y0_ref[...].astype(F32) + y1_ref[...].astype(F32)
    mu = _segsum(y, e) * (1.0 / HEAD_DIM)
    yc = y - mu
    var = _segsum(yc * yc, e) * (1.0 / HEAD_DIM)
    yn = yc * lax.rsqrt(var + GN_EPS) * ln_ref[0:1, :] + ln_ref[1:2, :]
    cv = a0_ref[:, 0:RW_DIM].astype(F32) + a1_ref[:, 0:RW_DIM].astype(F32)
    yrw = (yn + cv) * a0_ref[:, RW_DIM:2 * RW_DIM].astype(F32)
    a = _bdot(yrw, wbr_ref[...])
    b = jnp.dot(ya_ref[...], wba_ref[...], preferred_element_type=F32)
    h = a.shape[1] // 2
    mrg = jnp.concatenate(
        [jax.nn.sigmoid(g0_ref[...].astype(F32)) * a[:, :h] + jax.nn.sigmoid(g2_ref[...].astype(F32)) * b[:, :h],
         jax.nn.sigmoid(g1_ref[...].astype(F32)) * a[:, h:] + jax.nn.sigmoid(g3_ref[...].astype(F32)) * b[:, h:]],
        axis=1)
    o_ref[...] = x_ref[...] + mod_ref[0, 2:3, :] * _bdot(mrg, wo_ref[...])


def _merge(x2, ys, auxs, ya, p, ln, e128, wbr, wba, wo, mod, rows):
    m, d = x2.shape
    tm = _row_tile(rows, 512)
    per = rows // tm
    many = mod.shape[0] > 1
    gb = GATE_OFF // 512
    const2 = lambda i: (0, 0)
    in_specs = [pl.BlockSpec((tm, d), lambda i: (i, 0)),
                pl.BlockSpec((tm, RW_DIM), lambda i: (i, 0)),
                pl.BlockSpec((tm, RW_DIM), lambda i: (i, 0)),
                pl.BlockSpec((tm, 2 * RW_DIM), lambda i: (i, 0)),
                pl.BlockSpec((tm, 2 * RW_DIM), lambda i: (i, 0)),
                pl.BlockSpec((tm, ATT_DIM), lambda i: (i, 0))]
    in_specs += [pl.BlockSpec((tm, 512), (lambda i, c=c: (i, gb + c))) for c in range(4)]
    in_specs += [pl.BlockSpec((2, RW_DIM), const2),
                 pl.BlockSpec((LANES, LANES), const2),
                 pl.BlockSpec(wbr.shape, const2),
                 pl.BlockSpec(wba.shape, const2),
                 pl.BlockSpec(wo.shape, const2),
                 pl.BlockSpec((1, 6, d), lambda i: ((i // per) if many else 0, 0, 0))]
    return pl.pallas_call(
        _merge_kernel,
        grid=(m // tm,),
        in_specs=in_specs,
        out_specs=pl.BlockSpec((tm, d), lambda i: (i, 0)),
        out_shape=jax.ShapeDtypeStruct((m, d), F32),
        compiler_params=_cparams("parallel"),
        name="branch_merge",
    )(x2, ys[0], ys[1], auxs[0], auxs[1], ya, p, p, p, p, ln, e128, wbr, wba, wo, mod)


def _ffn_kernel(x_ref, g_ref, mod_ref, wg_ref, wu_ref, wd_ref, o_ref):
    def rows(rs):
        x = x_ref[rs, :]
        h = _norm_mod(x, g_ref[...], mod_ref[0, 3:4, :], mod_ref[0, 4:5, :]).astype(BF16)
        yield
        a = jnp.dot(h, wg_ref[...], preferred_element_type=F32)
        u = jnp.dot(h, wu_ref[...], preferred_element_type=F32)
        yield
        act = ((a * jax.nn.sigmoid(a)) * u).astype(BF16)
        yield
        o_ref[rs, :] = x + mod_ref[0, 5:6, :] * jnp.dot(act, wd_ref[...], preferred_element_type=F32)

    half = x_ref.shape[0] // 2
    _interleave(rows(slice(0, half)), rows(slice(half, 2 * half)))


def _ffn(x2, gain, mod, wg, wu, wd, rows):
    m, d = x2.shape
    dff = wg.shape[1]
    tm = _row_tile(rows, 512)
    per = rows // tm
    many = mod.shape[0] > 1
    once = pl.Buffered(1)
    return pl.pallas_call(
        _ffn_kernel,
        grid=(m // tm,),
        in_specs=[pl.BlockSpec((tm, d), lambda i: (i, 0)),
                  pl.BlockSpec((1, d), lambda i: (0, 0)),
                  pl.BlockSpec((1, 6, d), lambda i: ((i // per) if many else 0, 0, 0)),
                  pl.BlockSpec((d, dff), lambda i: (0, 0), pipeline_mode=once),
                  pl.BlockSpec((d, dff), lambda i: (0, 0), pipeline_mode=once),
                  pl.BlockSpec((dff, d), lambda i: (0, 0), pipeline_mode=once)],
        out_specs=pl.BlockSpec((tm, d), lambda i: (i, 0)),
        out_shape=jax.ShapeDtypeStruct((m, d), F32),
        compiler_params=_cparams("parallel"),
        name="ffn_swiglu",
    )(x2, gain.reshape(1, d), mod, wg, wu, wd)


def _dma_pieces(count, sizes, make):
    for k, size in enumerate(sizes):
        @pl.when((count & size) != 0)
        def _(k=k, size=size):
            make(k, pl.multiple_of(count & ~(2 * size - 1), PACK), size)


def _moe_gather_kernel(pos_ref, cnt_ref, h_ref, rankt_ref, zero_ref, out_ref, hs_sc, sem, *, ne, nsteps, sizes):
    del zero_ref
    i = pl.program_id(0)
    ex = pl.program_id(1)
    step = i * ne + ex
    slot = step % 2
    tt = h_ref.shape[0]

    def copies(st, sl, act):
        base = pos_ref[st]

        def make(k, off, size):
            act(pltpu.make_async_copy(hs_sc.at[sl, pl.ds(off, size)],
                                      out_ref.at[pl.ds(pl.multiple_of(base + off, PACK), size)], sem.at[sl, k]))

        _dma_pieces(cnt_ref[st], sizes, make)

    @pl.when(step >= 2)
    def _():
        copies(step - 2, slot, lambda cp: cp.wait())

    rrow = rankt_ref[0, pl.ds(ex, 1), :]
    rowid = lax.broadcasted_iota(jnp.int32, (MOE_BLK, tt), 0).astype(F32)

    def gather(b, carry):
        sel = ((rrow - (b * MOE_BLK).astype(F32)) == rowid).astype(BF16)
        r0 = pl.multiple_of(b * MOE_BLK, MOE_BLK)
        hs_sc[slot, pl.ds(r0, MOE_BLK), :] = jnp.dot(sel, h_ref[...], preferred_element_type=F32).astype(BF16)
        return carry

    lax.fori_loop(0, (cnt_ref[step] + MOE_BLK - 1) // MOE_BLK, gather, 0)
    copies(step, slot, lambda cp: cp.start())

    @pl.when(step == nsteps - 1)
    def _():
        if nsteps >= 2:
            copies(step - 1, 1 - slot, lambda cp: cp.wait())
        copies(step, slot, lambda cp: cp.wait())


def _moe_ffn_kernel(te_ref, tm_ref, tv_ref, hs_ref, wg_ref, wu_ref, wd_ref, os_ref):
    del te_ref, tm_ref
    valid = tv_ref[pl.program_id(0)] != 0

    @pl.when(valid)
    def _():
        def rows(rs):
            hb = hs_ref[rs, :]
            a = jnp.dot(hb, wg_ref[0], preferred_element_type=F32)
            u = jnp.dot(hb, wu_ref[0], preferred_element_type=F32)
            yield
            act = ((a * jax.nn.sigmoid(a)) * u).astype(BF16)
            yield
            os_ref[rs, :] = jnp.dot(act, wd_ref[0], preferred_element_type=F32).astype(os_ref.dtype)

        half = hs_ref.shape[0] // 2
        _interleave(rows(slice(0, half)), rows(slice(half, 2 * half)))

    @pl.when(jnp.logical_not(valid))
    def _():
        os_ref[...] = jnp.zeros_like(os_ref)


def _moe_combine_kernel(pos_ref, cnt_ref, x_ref, we_ref, rank_ref, mod_ref, os_ref, o_ref, acc_sc, ob_sc, sem,
                        *, ne, nsteps, sizes):
    i = pl.program_id(0)
    ex = pl.program_id(1)
    step = i * ne + ex
    slot = step % 2
    tt = x_ref.shape[0]
    sblk = min(2 * MOE_BLK, tt)
    lane = lax.broadcasted_iota(jnp.int32, (tt, LANES), 1)

    def copies(st, sl, act):
        base = pos_ref[st]

        def make(k, off, size):
            act(pltpu.make_async_copy(os_ref.at[pl.ds(pl.multiple_of(base + off, PACK), size)],
                                      ob_sc.at[sl, pl.ds(off, size)], sem.at[sl, k]))

        _dma_pieces(cnt_ref[st], sizes, make)

    @pl.when(step == 0)
    def _():
        ob_sc[...] = jnp.zeros_like(ob_sc)
        copies(0, 0, lambda cp: cp.start())

    @pl.when(step + 1 < nsteps)
    def _():
        copies(step + 1, 1 - slot, lambda cp: cp.start())

    copies(step, slot, lambda cp: cp.wait())

    @pl.when(ex == 0)
    def _():
        acc_sc[...] = jnp.zeros_like(acc_sc)

    rcol = jnp.sum(jnp.where(lane == ex, rank_ref[...], 0.0), axis=1, keepdims=True)
    wcol = jnp.sum(jnp.where(lane == ex, we_ref[...], 0.0), axis=1, keepdims=True)
    colid = lax.broadcasted_iota(jnp.int32, (tt, sblk), 1).astype(F32)

    def scatter(b, carry):
        sel = ((rcol - (b * sblk).astype(F32)) == colid).astype(BF16)
        r0 = pl.multiple_of(b * sblk, sblk)
        acc_sc[...] += wcol * jnp.dot(sel, ob_sc[slot, pl.ds(r0, sblk), :], preferred_element_type=F32)
        return carry

    lax.fori_loop(0, (cnt_ref[step] + sblk - 1) // sblk, scatter, 0)

    @pl.when(ex == ne - 1)
    def _():
        o_ref[...] = x_ref[...] + mod_ref[0, 5:6, :] * acc_sc[...]


def _route_kernel(x_ref, g_ref, mod_ref, rth_ref, rtl_ref, h_ref, we_ref, rank_ref, rankt_ref, cnt_ref, *, ne):
    tt = x_ref.shape[0]
    h = _norm_mod(x_ref[...], g_ref[...], mod_ref[0, 3:4, :], mod_ref[0, 4:5, :])
    h_ref[...] = h.astype(BF16)
    lane = lax.broadcasted_iota(jnp.int32, (tt, LANES), 1)
    logits = _dot3(h, rth_ref[...], rtl_ref[...])
    lg = jnp.where(lane < ne, logits, NEG_INF)
    t1 = jnp.max(lg, axis=1, keepdims=True)
    i1 = jnp.min(jnp.where(lg == t1, lane, LANES), axis=1, keepdims=True)
    lg2 = jnp.where(lane == i1, NEG_INF, lg)
    t2 = jnp.max(lg2, axis=1, keepdims=True)
    i2 = jnp.min(jnp.where(lg2 == t2, lane, LANES), axis=1, keepdims=True)
    e2 = jnp.exp(t2 - t1)
    den = 1.0 + e2
    we_ref[...] = jnp.where(lane == i1, 1.0 / den, 0.0) + jnp.where(lane == i2, e2 / den, 0.0)
    sel = (lane == i1) | (lane == i2)
    self32 = sel.astype(F32)
    before = (lax.broadcasted_iota(jnp.int32, (tt, tt), 1) < lax.broadcasted_iota(jnp.int32, (tt, tt), 0))
    rank = jnp.dot(before.astype(BF16), sel.astype(BF16), preferred_element_type=F32)
    rank = jnp.where(sel, rank, -1.0)
    rank_ref[...] = rank
    rankt_ref[0] = rank.T[0:SUBLANES, :]
    cnt_ref[0] = jnp.broadcast_to(jnp.sum(self32, axis=0, keepdims=True), (SUBLANES, LANES))


def _moe(x2, gain, mod, router_pad, wg, wu, wd, rows):
    m, d = x2.shape
    ne, _, dff = wg.shape
    assert ne <= SUBLANES
    tt = _row_tile(rows, MOE_TILE)
    nti = m // tt
    per = rows // tt
    many = mod.shape[0] > 1
    h, we, rank, rankt, cnt = pl.pallas_call(
        functools.partial(_route_kernel, ne=ne),
        grid=(nti,),
        in_specs=[pl.BlockSpec((tt, d), lambda i: (i, 0)),
                  pl.BlockSpec((1, d), lambda i: (0, 0)),
                  pl.BlockSpec((1, 6, d), lambda i: ((i // per) if many else 0, 0, 0)),
                  pl.BlockSpec((d, LANES), lambda i: (0, 0)),
                  pl.BlockSpec((d, LANES), lambda i: (0, 0))],
        out_specs=[pl.BlockSpec((tt, d), lambda i: (i, 0)),
                   pl.BlockSpec((tt, LANES), lambda i: (i, 0)),
                   pl.BlockSpec((tt, LANES), lambda i: (i, 0)),
                   pl.BlockSpec((1, SUBLANES, tt), lambda i: (i, 0, 0)),
                   pl.BlockSpec((1, SUBLANES, LANES), lambda i: (i, 0, 0))],
        out_shape=[jax.ShapeDtypeStruct((m, d), BF16),
                   jax.ShapeDtypeStruct((m, LANES), F32),
                   jax.ShapeDtypeStruct((m, LANES), F32),
                   jax.ShapeDtypeStruct((nti, SUBLANES, tt), F32),
                   jax.ShapeDtypeStruct((nti, SUBLANES, LANES), F32)],
        compiler_params=_cparams("parallel"),
        name="moe_route",
    )(x2, gain.reshape(1, d), mod, *_hi_lo(router_pad))
    c16 = (cnt[:, 0, :ne].astype(jnp.int32) + (PACK - 1)) // PACK * PACK
    region = (jnp.sum(c16, axis=0) + (MOE_FT - 1)) // MOE_FT * MOE_FT
    e_off = jnp.cumsum(region) - region
    pos = (e_off[None, :] + jnp.cumsum(c16, axis=0) - c16).reshape(-1).astype(jnp.int32)
    c16 = c16.reshape(-1)
    rmax = -(-(2 * m + nti * ne * PACK + ne * MOE_FT) // MOE_FT) * MOE_FT
    nft = rmax // MOE_FT
    tiles_e = jnp.cumsum(region // MOE_FT)
    tid = jnp.arange(nft, dtype=jnp.int32)
    tile_valid = (tid < tiles_e[-1]).astype(jnp.int32)
    tile_map = jnp.minimum(tid, tiles_e[-1] - 1).astype(jnp.int32)
    tile_exp = jnp.minimum(jnp.sum(tile_map[:, None] >= tiles_e[None, :], axis=1), ne - 1).astype(jnp.int32)

    sizes = tuple(tt >> k for k in range(tt.bit_length()) if (tt >> k) >= PACK)
    nsteps = nti * ne
    hs = pl.pallas_call(
        functools.partial(_moe_gather_kernel, ne=ne, nsteps=nsteps, sizes=sizes),
        grid_spec=pltpu.PrefetchScalarGridSpec(
            num_scalar_prefetch=2,
            grid=(nti, ne),
            in_specs=[pl.BlockSpec((tt, d), lambda i, e, p_, c_: (i, 0)),
                      pl.BlockSpec((1, SUBLANES, tt), lambda i, e, p_, c_: (i, 0, 0)),
                      pl.BlockSpec(memory_space=pl.ANY)],
            out_specs=pl.BlockSpec(memory_space=pl.ANY),
            scratch_shapes=[pltpu.VMEM((2, tt, d), BF16), pltpu.SemaphoreType.DMA((2, len(sizes)))]),
        out_shape=jax.ShapeDtypeStruct((rmax, d), BF16),
        input_output_aliases={4: 0},
        compiler_params=_cparams("arbitrary", "arbitrary"),
        name="moe_gather",
    )(pos, c16, h, rankt, jnp.zeros((rmax, d), BF16))

    once = pl.Buffered(1)
    os_sorted = pl.pallas_call(
        _moe_ffn_kernel,
        grid_spec=pltpu.PrefetchScalarGridSpec(
            num_scalar_prefetch=3,
            grid=(nft,),
            in_specs=[pl.BlockSpec((MOE_FT, d), lambda t, te, tm, tv: (tm[t], 0)),
                      pl.BlockSpec((1, d, dff), lambda t, te, tm, tv: (te[t], 0, 0), pipeline_mode=once),
                      pl.BlockSpec((1, d, dff), lambda t, te, tm, tv: (te[t], 0, 0), pipeline_mode=once),
                      pl.BlockSpec((1, dff, d), lambda t, te, tm, tv: (te[t], 0, 0), pipeline_mode=once)],
            out_specs=pl.BlockSpec((MOE_FT, d), lambda t, te, tm, tv: (t, 0))),
        out_shape=jax.ShapeDtypeStruct((rmax, d), BF16),
        compiler_params=pltpu.CompilerParams(dimension_semantics=("parallel",),
                                             vmem_limit_bytes=58 * 1024 * 1024),
        name="moe_experts",
    )(tile_exp, tile_map, tile_valid, hs, wg, wu, wd)

    return pl.pallas_call(
        functools.partial(_moe_combine_kernel, ne=ne, nsteps=nsteps, sizes=sizes),
        grid_spec=pltpu.PrefetchScalarGridSpec(
            num_scalar_prefetch=2,
            grid=(nti, ne),
            in_specs=[pl.BlockSpec((tt, d), lambda i, e, p_, c_: (i, 0)),
                      pl.BlockSpec((tt, LANES), lambda i, e, p_, c_: (i, 0)),
                      pl.BlockSpec((tt, LANES), lambda i, e, p_, c_: (i, 0)),
                      pl.BlockSpec((1, 6, d), lambda i, e, p_, c_: ((i // per) if many else 0, 0, 0)),
                      pl.BlockSpec(memory_space=pl.ANY)],
            out_specs=pl.BlockSpec((tt, d), lambda i, e, p_, c_: (i, 0)),
            scratch_shapes=[pltpu.VMEM((tt, d), F32), pltpu.VMEM((2, tt, d), BF16),
                            pltpu.SemaphoreType.DMA((2, len(sizes)))]),
        out_shape=jax.ShapeDtypeStruct((m, d), F32),
        compiler_params=_cparams("arbitrary", "arbitrary"),
        name="moe_combine",
    )(pos, c16, x2, we, rank, mod, os_sorted)


def _rope_table(t):
    rows = t // GRID_W
    row = jnp.repeat(jnp.arange(rows, dtype=F32), GRID_W)
    col = jnp.tile(jnp.arange(GRID_W, dtype=F32), rows)
    half = HEAD_DIM // 2
    inv_freq = ROPE_THETA ** (-jnp.arange(0, half, 2, dtype=F32) / half)
    ar = row[:, None] * inv_freq
    ac = col[:, None] * inv_freq
    ang = jnp.concatenate([ar, ar, ac, ac] * 2, axis=-1)
    return jnp.stack([jnp.cos(ang), jnp.sin(ang)])


def _pad_rows(a, rows):
    return jnp.zeros((rows,) + a.shape[1:], a.dtype).at[:a.shape[0]].set(a)


def _hi_lo(w):
    hi = w.astype(BF16)
    return hi, (w - hi.astype(F32)).astype(BF16)


def kernel(x, c, ctx, c_ctx, norm1, norm2, w_mod, b_mod, w_in, tshift, w0, w_decay_up, a0, w_iclr_up, v0, w_vres_down, w_vres_up, w_gate_up, k_k, k_a, r_k, lnx_g, lnx_b, q_norm, k_norm, sink, w_br_rwkv, w_br_attn, w_out, ffn_gate, ffn_up, ffn_down, router, moe_gate, moe_up, moe_down):
    nb, t, d = x.shape
    cn = ctx.shape[1]
    depth = w_in.shape[0]
    mods = _modulation(c, c_ctx, w_mod, b_mod)
    cs = _rope_table(t)
    e128 = jnp.kron(jnp.eye(2, dtype=F32), jnp.ones((HEAD_DIM, HEAD_DIM), F32)).astype(BF16)
    xl = x.reshape(nb * t, d)
    xc = ctx.reshape(nb * cn, d)
    vf_l = vf_c = None
    for layer in range(depth):
        last = layer == depth - 1
        mod_l = mods[layer, :nb].reshape(nb, 6, d)
        mod_c = mods[layer, nb:nb + 1].reshape(1, 6, d)
        w_in_b = w_in[layer].astype(BF16)
        vd_b = None
        prm = {"tshift": tshift[layer], "e128": e128, "wg": w_gate_up[layer].astype(BF16)}
        prm["pd"] = jnp.stack(
            [_pad_rows(jnp.stack([w0[layer, dd], a0[layer, dd]]), SUBLANES) for dd in range(2)])
        shared = [k_k[layer], k_a[layer], r_k[layer].reshape(RW_DIM)]
        if layer > 0:
            shared.append(v0[layer - 1])
            vd_b = _pad_rows(w_vres_down[layer - 1].T, LANES).T.astype(BF16)
            prm["vu_hi"], prm["vu_lo"] = _hi_lo(_pad_rows(w_vres_up[layer - 1], LANES))
        prm["ps"] = _pad_rows(jnp.stack(shared), SUBLANES)
        wla = jnp.zeros((2, LANES, 2 * RW_DIM), F32)
        wla = wla.at[:, :HEAD_DIM, :RW_DIM].set(w_decay_up[layer]).at[:, HEAD_DIM:, RW_DIM:].set(w_iclr_up[layer])
        prm["wla"] = wla.astype(BF16)

        p_l, hv_l = _inproj(xl, norm1[layer], mod_l, w_in_b, vd_b, t)
        p_c, hv_c = _inproj(xc, norm1[layer], mod_c, w_in_b, vd_b, cn)

        s0 = jnp.zeros((nb, N_PAIRS, LANES, LANES), F32)
        y_c, aux_c, s_ctx = _rwkv_both(p_c, vf_c, hv_c, [s0, s0], prm, nb, cn)
        y_l, aux_l, _ = _rwkv_both(p_l, vf_l, hv_l, s_ctx, prm, nb, t)
        if layer == 0:
            vf_l, vf_c = aux_l[1], aux_c[1]

        qg = jnp.tile(q_norm[layer], 2).reshape(1, LANES)
        kg = jnp.tile(k_norm[layer], 2).reshape(1, LANES)
        qn_l, kn_l, vb_l = _qk_prep(p_l, cs, qg, kg, e128, t, True)
        qn_c, kn_c, vb_c = _qk_prep(p_c, None, qg, kg, e128, cn, False)
        sk = sink[layer].reshape(1, -1)
        ya_l = _attention(qn_l, kn_l, vb_l, kn_c, vb_c, sk, nb, t, cn, True)

        ln = jnp.stack([lnx_g[layer], lnx_b[layer]])
        wbr = w_br_rwkv[layer].astype(BF16)
        wba = w_br_attn[layer].astype(BF16)
        wo = w_out[layer].astype(BF16)
        xl = _merge(xl, y_l, aux_l, ya_l, p_l, ln, e128, wbr, wba, wo, mod_l, t)
        if not last:
            ya_c = _attention(qn_c, None, None, kn_c, vb_c, sk, nb, cn, cn, False)
            xc = _merge(xc, y_c, aux_c, ya_c, p_c, ln, e128, wbr, wba, wo, mod_c, cn)

        i = layer // 2
        if layer % 2 == 0:
            fw = (ffn_gate[i].astype(BF16), ffn_up[i].astype(BF16), ffn_down[i].astype(BF16))
            xl = _ffn(xl, norm2[layer], mod_l, *fw, t)
            if not last:
                xc = _ffn(xc, norm2[layer], mod_c, *fw, cn)
        else:
            rt = _pad_rows(router[i].T, LANES).T
            mw = (moe_gate[i].astype(BF16), moe_up[i].astype(BF16), moe_down[i].astype(BF16))
            xl = _moe(xl, norm2[layer], mod_l, rt, *mw, t)
            if not last:
                xc = _moe(xc, norm2[layer], mod_c, rt, *mw, cn)
    return xl.reshape(nb, t, d)
```
